```python
import math
import jax, jax.numpy as jnp
from jax import lax
import numpy as np

D_MODEL = 1024
BATCH = 8
SEQ = 2048
DEPTH = 1
DEC_BATCH = 32
DEC_SEQ = 4
PAST_LEN = 16384
PAGE_SIZE = 128

RET_HEADS = 4
RET_QK_DIM = D_MODEL // 8
RET_V_DIM = D_MODEL // 4
RET_CHUNK = 128
RET_THETA = 10000.0
RET_QK = RET_HEADS * RET_QK_DIM
RET_V = RET_HEADS * RET_V_DIM
ATT_GROUPS = ((128, 1), (512, 4), (2048, 16))
N_GROUPS = 3
ATT_HEADS = 4
ATT_HEAD_DIM = 64
ROPE_DIM = ATT_HEAD_DIM // 4
ROPE_THETA = 500000.0
BAND_BLOCK = 128
ATT_W = N_GROUPS * ATT_HEADS * ATT_HEAD_DIM
ATT_OUT = ATT_HEADS * ATT_HEAD_DIM
D_FF = 2816
CONV_W = 3
EPS = 1e-6
IN_WIDTHS = (RET_QK, RET_QK, RET_V, RET_V, ATT_W, ATT_W, ATT_W, D_MODEL, D_MODEL)
D_IN = 2 * RET_QK + 2 * RET_V + 3 * ATT_W + 2 * D_MODEL

kernel_name = 'hybrid_retention_dilated_attn_convffn_step'

F32 = jnp.float32


def rms_norm(x, g):
    xf = x.astype(F32)
    y = xf * lax.rsqrt(jnp.mean(xf * xf, axis=-1, keepdims=True) + EPS)
    return (y * g.astype(F32)).astype(x.dtype)


def rotary(x, pos, inv_freq):
    half = inv_freq.shape[0]
    rot = 2 * half
    ang = pos.astype(F32)[:, None] * inv_freq[None, :]
    cos = jnp.cos(ang)[:, None, :]
    sin = jnp.sin(ang)[:, None, :]
    xf = x[..., :rot].astype(F32)
    x1, x2 = xf[..., :half], xf[..., half:]
    r = jnp.concatenate([x1 * cos - x2 * sin, x2 * cos + x1 * sin], axis=-1).astype(x.dtype)
    return jnp.concatenate([r, x[..., rot:]], axis=-1)


def ret_inv_freq():
    return RET_THETA ** (-jnp.linspace(0.0, 1.0, RET_QK_DIM // 2, dtype=F32))


def att_inv_freq():
    half = ROPE_DIM // 2
    return ROPE_THETA ** (-jnp.arange(half, dtype=F32) / half)


def retention_log_decay():
    return jnp.log1p(-jnp.exp2(-5.0 - jnp.arange(RET_HEADS, dtype=F32)))


def split_columns(t, widths):
    out, start = [], 0
    for wd in widths:
        out.append(t[..., start:start + wd])
        start += wd
    return out


def mixer_projections(h, w_in, pos):
    B, T, _ = h.shape
    rq, rk, rv, rg, aq, ak, av, gr, ga = split_columns(h @ w_in, IN_WIDTHS)
    rf = ret_inv_freq()
    rq = rotary(rq.reshape(B, T, RET_HEADS, RET_QK_DIM), pos, rf)
    rk = rotary(rk.reshape(B, T, RET_HEADS, RET_QK_DIM), pos, rf) * RET_QK_DIM ** -0.5
    rv = rv.reshape(B, T, RET_HEADS, RET_V_DIM)
    af = att_inv_freq()
    na = N_GROUPS * ATT_HEADS
    gshape = (B, T, N_GROUPS, ATT_HEADS, ATT_HEAD_DIM)
    aq = rotary(aq.reshape(B, T, na, ATT_HEAD_DIM), pos, af).reshape(gshape) * ATT_HEAD_DIM ** -0.5
    ak = rotary(ak.reshape(B, T, na, ATT_HEAD_DIM), pos, af).reshape(gshape)
    av = av.reshape(gshape)
    return rq, rk, rv, rg, aq, ak, av, gr, ga


def retention_chunk(q, k, v, s0, log_gamma):
    C = q.shape[1]
    qf, kf, vf = q.astype(F32), k.astype(F32), v.astype(F32)
    idx = jnp.arange(C, dtype=F32)
    diff = idx[:, None] - idx[None, :]
    decay = jnp.where(diff[None] >= 0, jnp.exp(log_gamma[:, None, None] * jnp.maximum(diff, 0.0)[None]), 0.0)
    scores = jnp.einsum('bihd,bjhd->bhij', qf, kf) * decay[None]
    o = jnp.einsum('bhij,bjhe->bihe', scores, vf)
    q_dec = jnp.exp(log_gamma[None, :] * (idx[:, None] + 1.0))
    o = o + jnp.einsum('bihd,bhde->bihe', qf, s0) * q_dec[None, :, :, None]
    k_dec = jnp.exp(log_gamma[None, :] * (C - 1.0 - idx)[:, None])
    s1 = s0 * jnp.exp(log_gamma * C)[None, :, None, None] + jnp.einsum('bjhd,bjhe->bhde', kf * k_dec[None, :, :, None], vf)
    return o, s1


def retention_prompt(q, k, v, log_gamma):
    B, S, H, dk = q.shape
    dv = v.shape[-1]
    C = min(RET_CHUNK, S)
    n = S // C

    def chunks(t):
        return t.reshape(B, n, C, H, t.shape[-1]).swapaxes(0, 1)

    def step(s, qkv):
        o, s_new = retention_chunk(qkv[0], qkv[1], qkv[2], s, log_gamma)
        return s_new, o

    s_fin, o = lax.scan(step, jnp.zeros((B, H, dk, dv), F32), (chunks(q), chunks(k), chunks(v)))
    return o.swapaxes(0, 1).reshape(B, S, H, dv), s_fin


def band_attention(q, k, v, span):
    N, L, H, Dh = q.shape
    blk = BAND_BLOCK
    nb = -(-L // blk)
    Lp = nb * blk
    qb = jnp.pad(q.astype(F32), ((0, 0), (0, Lp - L), (0, 0), (0, 0))).reshape(N, nb, blk, H, Dh)

    def kv_blocks(t):
        tp = jnp.pad(t.astype(F32), ((0, 0), (blk, Lp - L), (0, 0), (0, 0))).reshape(N, nb + 1, blk, H, Dh)
        return jnp.concatenate([tp[:, :-1], tp[:, 1:]], axis=2)

    kb, vb = kv_blocks(k), kv_blocks(v)
    s = jnp.einsum('nbqhd,nbkhd->nbhqk', qb, kb)
    qi = jnp.arange(blk)[:, None]
    kj = jnp.arange(2 * blk)[None, :]
    dist = qi - kj + blk
    kpos = jnp.arange(nb)[:, None, None] * blk + kj[None] - blk
    mask = (dist[None] >= 0) & (dist[None] <= span) & (kpos >= 0)
    s = jnp.where(mask[None, :, None], s, -jnp.inf)
    m = jnp.max(s, axis=-1, keepdims=True)
    p = jnp.exp(s - m)
    den = jnp.sum(p, axis=-1, keepdims=True)
    o = jnp.einsum('nbhqk,nbkhd->nbqhd', p / den, vb).reshape(N, Lp, H, Dh)[:, :L]
    lse = (m + jnp.log(den))[..., 0].transpose(0, 1, 3, 2).reshape(N, Lp, H)[:, :L]
    return o, lse


def dilated_prompt(q, k, v, window, dil):
    B, S, H, Dh = q.shape
    L = S // dil

    def split(t):
        return t.reshape(B, L, dil, H, Dh).transpose(0, 2, 1, 3, 4).reshape(B * dil, L, H, Dh)

    o, lse = band_attention(split(q), split(k), split(v), window // dil)
    o = o.reshape(B, dil, L, H, Dh).transpose(0, 2, 1, 3, 4).reshape(B, S, H, Dh)
    lse = lse.reshape(B, dil, L, H).transpose(0, 2, 1, 3).reshape(B, S, H)
    return o, lse


def dilated_sample(q, k_new, v_new, buf, window, dil):
    T = q.shape[1]
    Wb = buf.shape[1]
    buf = buf.astype(k_new.dtype)
    kf = jnp.concatenate([buf[:, :, 0], k_new], axis=1)
    vf = jnp.concatenate([buf[:, :, 1], v_new], axis=1)
    nk = window // dil + 1
    idx = Wb + jnp.arange(T)[:, None] - dil * jnp.arange(nk)[None, :]
    valid = idx >= 0
    idxc = jnp.maximum(idx, 0)
    kg = kf[:, idxc].astype(F32)
    vg = vf[:, idxc].astype(F32)
    s = jnp.einsum('bthd,btkhd->bthk', q.astype(F32), kg)
    s = jnp.where(valid[None, :, None, :], s, -jnp.inf)
    m = jnp.max(s, axis=-1, keepdims=True)
    p = jnp.exp(s - m)
    den = jnp.sum(p, axis=-1, keepdims=True)
    o = jnp.einsum('bthk,btkhd->bthd', p / den, vg)
    lse = (m + jnp.log(den))[..., 0]
    new_buf = jnp.concatenate([buf, jnp.stack([k_new, v_new], axis=2)], axis=1)[:, -Wb:]
    return o, lse, new_buf


def merge_branches(ret_o, ret_gate, att_o, gate_r, gate_a, w_branch_ret, w_branch_attn, w_out, dtype):
    B, T = ret_o.shape[:2]
    of = ret_o.astype(F32)
    on = of * lax.rsqrt(jnp.mean(of * of, axis=-1, keepdims=True) + EPS)
    r = on.reshape(B, T, RET_V).astype(dtype) * jax.nn.silu(ret_gate)
    br = r @ w_branch_ret
    ba = att_o.reshape(B, T, ATT_OUT).astype(dtype) @ w_branch_attn
    mix = jax.nn.sigmoid(gate_r) * br + jax.nn.sigmoid(gate_a) * ba
    return mix @ w_out


def conv_ffn(h, conv_state, w_ffn_up, conv_w, conv_b, w_ffn_down):
    u = h @ w_ffn_up
    T = u.shape[1]
    ext = jnp.concatenate([conv_state.astype(u.dtype), u], axis=1)
    c = conv_b + sum(conv_w[i] * ext[:, i:i + T] for i in range(CONV_W))
    g, val = c[..., :D_FF], c[..., D_FF:]
    f = (jax.nn.gelu(g, approximate=True) * val) @ w_ffn_down
    return f, ext[:, -(CONV_W - 1):]


def trunk_layer(x, pos, state, norm_mix_pre, w_in, w_branch_ret, w_branch_attn, w_out, norm_mix_post,
                norm_ffn_pre, w_ffn_up, conv_w, conv_b, w_ffn_down, norm_ffn_post):
    B, T, _ = x.shape
    h = rms_norm(x, norm_mix_pre)
    rq, rk, rv, rg, aq, ak, av, gr, ga = mixer_projections(h, w_in, pos)
    log_gamma = retention_log_decay()
    if state is None:
        ret_o, ret_state = retention_prompt(rq, rk, rv, log_gamma)
        conv_state = jnp.zeros((B, CONV_W - 1, 2 * D_FF), x.dtype)
        kv_bufs = None
    else:
        kv_bufs = state[:3]
        ret_o, ret_state = retention_chunk(rq, rk, rv, state[3].astype(F32), log_gamma)
        conv_state = state[4]
    outs, lses, new_bufs = [], [], []
    for g, (win, dil) in enumerate(ATT_GROUPS):
        q_g, k_g, v_g = aq[:, :, g], ak[:, :, g], av[:, :, g]
        if state is None:
            o_g, l_g = dilated_prompt(q_g, k_g, v_g, win, dil)
            buf = jnp.stack([k_g, v_g], axis=2)[:, -min(win, T):]
        else:
            o_g, l_g, buf = dilated_sample(q_g, k_g, v_g, kv_bufs[g], win, dil)
        outs.append(o_g)
        lses.append(l_g)
        new_bufs.append(buf)
    wts = jax.nn.softmax(jnp.stack(lses, axis=0), axis=0)
    att_o = jnp.einsum('gbth,gbthd->bthd', wts, jnp.stack(outs, axis=0))
    mix = merge_branches(ret_o, rg, att_o, gr, ga, w_branch_ret, w_branch_attn, w_out, x.dtype)
    x = x + rms_norm(mix, norm_mix_post)
    f, conv_new = conv_ffn(rms_norm(x, norm_ffn_pre), conv_state, w_ffn_up, conv_w, conv_b, w_ffn_down)
    x = x + rms_norm(f, norm_ffn_post)
    return x, (new_bufs[0], new_bufs[1], new_bufs[2], ret_state.astype(x.dtype), conv_new)


def setup_inputs(seed: int = 0) -> dict:
    key = jax.random.key(seed)
    ks = jax.random.split(key, 24)
    nrm = jax.random.normal
    L = DEPTH

    def gain(k):
        return 1.0 + 0.02 * nrm(k, (L, D_MODEL), F32)

    def win_len(w):
        return min(w, PAST_LEN)

    return {
        'x_prompt': nrm(ks[0], (BATCH, SEQ, D_MODEL), F32),
        'x_sample': nrm(ks[1], (DEC_BATCH, DEC_SEQ, D_MODEL), F32),
        'cache_kv_g0': nrm(ks[2], (L, DEC_BATCH, win_len(ATT_GROUPS[0][0]), 2, ATT_HEADS, ATT_HEAD_DIM), F32),
        'cache_kv_g1': nrm(ks[3], (L, DEC_BATCH, win_len(ATT_GROUPS[1][0]), 2, ATT_HEADS, ATT_HEAD_DIM), F32),
        'cache_kv_g2': nrm(ks[4], (L, DEC_BATCH, win_len(ATT_GROUPS[2][0]), 2, ATT_HEADS, ATT_HEAD_DIM), F32),
        'state_ret': 0.5 * nrm(ks[5], (L, DEC_BATCH, RET_HEADS, RET_QK_DIM, RET_V_DIM), F32),
        'state_conv': nrm(ks[6], (L, DEC_BATCH, CONV_W - 1, 2 * D_FF), F32),
        'norm_mix_pre': gain(ks[7]),
        'w_in': nrm(ks[8], (L, D_MODEL, D_IN), F32) * D_MODEL ** -0.5,
        'w_branch_ret': nrm(ks[9], (L, RET_V, D_MODEL), F32) * RET_V ** -0.5,
        'w_branch_attn': nrm(ks[10], (L, ATT_OUT, D_MODEL), F32) * ATT_OUT ** -0.5,
        'w_out': nrm(ks[11], (L, D_MODEL, D_MODEL), F32) * D_MODEL ** -0.5,
        'norm_mix_post': gain(ks[12]),
        'norm_ffn_pre': gain(ks[13]),
        'w_ffn_up': nrm(ks[14], (L, D_MODEL, 2 * D_FF), F32) * D_MODEL ** -0.5,
        'conv_w': nrm(ks[15], (L, CONV_W, 2 * D_FF), F32) * CONV_W ** -0.5,
        'conv_b': 0.02 * nrm(ks[16], (L, 2 * D_FF), F32),
        'w_ffn_down': nrm(ks[17], (L, D_FF, D_MODEL), F32) * D_FF ** -0.5,
        'norm_ffn_post': gain(ks[18]),
    }


def reference(x_prompt, x_sample, cache_kv_g0, cache_kv_g1, cache_kv_g2, state_ret, state_conv,
              norm_mix_pre, w_in, w_branch_ret, w_branch_attn, w_out, norm_mix_post,
              norm_ffn_pre, w_ffn_up, conv_w, conv_b, w_ffn_down, norm_ffn_post):
    pos_p = jnp.arange(x_prompt.shape[1])
    pos_s = PAST_LEN + jnp.arange(x_sample.shape[1])
    hp, hs = x_prompt, x_sample
    new_p = [[], [], [], [], []]
    new_s = [[], [], [], [], []]
    for l in range(DEPTH):
        w = (norm_mix_pre[l], w_in[l], w_branch_ret[l], w_branch_attn[l], w_out[l], norm_mix_post[l],
             norm_ffn_pre[l], w_ffn_up[l], conv_w[l], conv_b[l], w_ffn_down[l], norm_ffn_post[l])
        hp, st_p = trunk_layer(hp, pos_p, None, *w)
        st_in = (cache_kv_g0[l], cache_kv_g1[l], cache_kv_g2[l], state_ret[l], state_conv[l])
        hs, st_s = trunk_layer(hs, pos_s, st_in, *w)
        for lst, a in zip(new_p, st_p):
            lst.append(a)
        for lst, a in zip(new_s, st_s):
            lst.append(a)
    p_kv_g0, p_kv_g1, p_kv_g2, p_ret, p_conv = [jnp.stack(a, axis=0) for a in new_p]
    s_kv_g0, s_kv_g1, s_kv_g2, s_ret, s_conv = [jnp.stack(a, axis=0) for a in new_s]
    return (hp, hs, p_kv_g0, p_kv_g1, p_kv_g2, p_ret, p_conv, s_kv_g0, s_kv_g1, s_kv_g2, s_ret, s_conv)
```

```python
import functools
import math

import jax
import jax.numpy as jnp
from jax import lax
from jax.experimental import pallas as pl
from jax.experimental.pallas import tpu as pltpu

F32 = jnp.float32
BF16 = jnp.bfloat16

D_MODEL = 1024
PAST_LEN = 16384
RET_HEADS = 4
RET_QK_DIM = 128
RET_V_DIM = 256
RET_CHUNK = 128
RET_THETA = 10000.0
RET_QK = RET_HEADS * RET_QK_DIM
RET_V = RET_HEADS * RET_V_DIM
ATT_GROUPS = ((128, 1), (512, 4), (2048, 16))
N_GROUPS = 3
ATT_HEADS = 4
ATT_HEAD_DIM = 64
ROPE_DIM = ATT_HEAD_DIM // 4
ROPE_THETA = 500000.0
BAND_BLOCK = 128
ATT_OUT = ATT_HEADS * ATT_HEAD_DIM
ATT_W = N_GROUPS * ATT_OUT
D_FF = 2816
CONV_W = 3
EPS = 1e-6

LANES = 128
SUBLANES = 8
MXU_N = 256
VMEM_LIMIT = 56 * 1024 * 1024
NEG = -1e30
FF_CHUNK = MXU_N
QKV_COLS = 2 * RET_QK + RET_V + 3 * ATT_W


def _rms(x, g):
    return x * lax.rsqrt(jnp.mean(x * x, axis=-1, keepdims=True) + EPS) * g


def _dot(a, b):
    return jnp.dot(a, b, preferred_element_type=F32)


def _dot_nt(a, b):
    return lax.dot_general(a, b, (((1,), (1,)), ((), ())), preferred_element_type=F32)


def _resident(shape):
    return pl.BlockSpec(shape, lambda *_: (0,) * len(shape), pipeline_mode=pl.Buffered(1))


def _log2(n):
    assert n > 0 and n & (n - 1) == 0, n
    return n.bit_length() - 1


def _params(n_axes=1):
    return pltpu.CompilerParams(dimension_semantics=("arbitrary",) * n_axes, vmem_limit_bytes=VMEM_LIMIT)


def _proj_kernel(x_ref, g_ref, w_ref, crq_ref, srq_ref, crk_ref, srk_ref, ca_ref, sa1_ref, sa2_ref,
                 rq_ref, rk_ref, rv_ref, aq_ref, kv0_ref, kv1_ref, kv2_ref):
    h = _rms(x_ref[...], g_ref[...]).astype(BF16)

    def mm(c0, width=MXU_N):
        return _dot(h, w_ref[:, c0:c0 + width])

    def rot_ret(t, c, s):
        return t * c + pltpu.roll(t, RET_QK_DIM // 2, 1) * s

    ca, sa1, sa2 = ca_ref[...], sa1_ref[...], sa2_ref[...]

    def rot_att(t):
        return t * ca + pltpu.roll(t, ROPE_DIM // 2, 1) * sa1 + pltpu.roll(t, LANES - ROPE_DIM // 2, 1) * sa2

    for c in range(RET_QK // MXU_N):
        q2 = mm(c * MXU_N)
        k2 = mm(RET_QK + c * MXU_N)
        for half in range(MXU_N // LANES):
            lo = half * LANES
            col = c * MXU_N + lo
            rq_ref[:, col:col + LANES] = rot_ret(q2[:, lo:lo + LANES], crq_ref[...], srq_ref[...]).astype(BF16)
            rk_ref[:, col:col + LANES] = rot_ret(k2[:, lo:lo + LANES], crk_ref[...], srk_ref[...]).astype(BF16)
    for c in range(RET_V // MXU_N):
        rv_ref[:, c * MXU_N:(c + 1) * MXU_N] = mm(2 * RET_QK + c * MXU_N).astype(BF16)
    base = 2 * RET_QK + RET_V
    kv_refs = (kv0_ref, kv1_ref, kv2_ref)
    for g in range(N_GROUPS):
        q2 = mm(base + g * ATT_OUT)
        k2 = mm(base + ATT_W + g * ATT_OUT)
        v2 = mm(base + 2 * ATT_W + g * ATT_OUT)
        for half in range(MXU_N // LANES):
            lo = half * LANES
            aq_ref[:, g * ATT_OUT + lo:g * ATT_OUT + lo + LANES] = rot_att(q2[:, lo:lo + LANES]).astype(BF16)
            kv_refs[g][:, lo:lo + LANES] = rot_att(k2[:, lo:lo + LANES])
        kv_refs[g][:, ATT_OUT:2 * ATT_OUT] = v2


def _proj(x2d, g_pre, w_qkv, tabs, tm):
    m = x2d.shape[0]
    n_tab = tabs[0].shape[0] // tm
    row = lambda i: (i, 0)
    tab = lambda i: (i % n_tab, 0)
    in_specs = [pl.BlockSpec((tm, D_MODEL), row), _resident((1, D_MODEL)),
                _resident((D_MODEL, QKV_COLS))] + [pl.BlockSpec((tm, LANES), tab)] * 7
    widths = (RET_QK, RET_QK, RET_V, ATT_W, 2 * ATT_OUT, 2 * ATT_OUT, 2 * ATT_OUT)
    dtypes = (BF16, BF16, BF16, BF16, F32, F32, F32)
    return pl.pallas_call(
        _proj_kernel,
        grid=(m // tm,),
        in_specs=in_specs,
        out_specs=[pl.BlockSpec((tm, w), row) for w in widths],
        out_shape=[jax.ShapeDtypeStruct((m, w), d) for w, d in zip(widths, dtypes)],
        compiler_params=_params(),
        name="proj",
    )(x2d, g_pre, w_qkv, *tabs)


def _ret_kernel(q_ref, k_ref, v_ref, dec_ref, qd_ref, kd_ref, gc_ref, o_ref, st_ref, s_scr, *, n_chunks):
    s_scr[...] = jnp.zeros_like(s_scr)

    def body(c, carry):
        r0 = pl.multiple_of(c * RET_CHUNK, RET_CHUNK)
        rows = pl.ds(r0, RET_CHUNK)
        for h in range(RET_HEADS):
            qk = slice(h * RET_QK_DIM, (h + 1) * RET_QK_DIM)
            vv = slice(h * RET_V_DIM, (h + 1) * RET_V_DIM)
            q = q_ref[rows, qk]
            k = k_ref[rows, qk]
            v = v_ref[rows, vv]
            s0 = s_scr[h]
            sc = _dot_nt(q, k) * dec_ref[h]
            o = _dot(sc.astype(BF16), v) + _dot(q, s0.astype(BF16)) * qd_ref[h]
            kd_t = (k.astype(F32) * kd_ref[h]).T.astype(BF16)
            s_scr[h] = s0 * gc_ref[h] + _dot(kd_t, v)
            o_ref[rows, vv] = o * lax.rsqrt(jnp.mean(o * o, axis=-1, keepdims=True) + EPS)
        return carry

    lax.fori_loop(0, n_chunks, body, 0)
    st_ref[0] = s_scr[...]


def _ret_tables(chunk):
    lg = jnp.log1p(-jnp.exp2(-5.0 - jnp.arange(RET_HEADS, dtype=F32)))
    idx = jnp.arange(RET_CHUNK, dtype=F32)
    diff = idx[:, None] - idx[None, :]
    dec = jnp.where(diff[None] >= 0, jnp.exp(lg[:, None, None] * jnp.maximum(diff, 0.0)[None]), 0.0)
    qd = jnp.exp(lg[:, None] * (idx[None, :] + 1.0))
    kd = jnp.exp(lg[:, None] * (chunk - 1.0 - idx)[None, :])
    gc = jnp.exp(lg * chunk)
    qd = jnp.broadcast_to(qd[:, :, None], (RET_HEADS, RET_CHUNK, RET_V_DIM))
    kd = jnp.broadcast_to(kd[:, :, None], (RET_HEADS, RET_CHUNK, RET_QK_DIM))
    gc = jnp.broadcast_to(gc[:, None, None], (RET_HEADS, RET_QK_DIM, RET_V_DIM))
    return dec, qd, kd, gc


def _ret_prompt(rq, rk, rv, batch, seq):
    dec, qd, kd, gc = _ret_tables(RET_CHUNK)
    row = lambda b: (b, 0)
    return pl.pallas_call(
        functools.partial(_ret_kernel, n_chunks=seq // RET_CHUNK),
        grid=(batch,),
        in_specs=[pl.BlockSpec((seq, RET_QK), row), pl.BlockSpec((seq, RET_QK), row), pl.BlockSpec((seq, RET_V), row),
                  _resident(dec.shape), _resident(qd.shape), _resident(kd.shape), _resident(gc.shape)],
        out_specs=[pl.BlockSpec((seq, RET_V), row),
                   pl.BlockSpec((1, RET_HEADS, RET_QK_DIM, RET_V_DIM), lambda b: (b, 0, 0, 0))],
        out_shape=[jax.ShapeDtypeStruct((batch * seq, RET_V), F32),
                   jax.ShapeDtypeStruct((batch, RET_HEADS, RET_QK_DIM, RET_V_DIM), F32)],
        scratch_shapes=[pltpu.VMEM((RET_HEADS, RET_QK_DIM, RET_V_DIM), F32)],
        compiler_params=_params(),
        name="ret_prompt",
    )(rq, rk, rv, dec, qd, kd, gc)


def _ret_sample_kernel(q_ref, k_ref, kt_ref, v_ref, s_ref, dec_ref, qd_ref, kd_ref, gc_ref, o_ref, st_ref,
                       *, n_tok, n_b):
    for b in range(n_b):
        for h in range(RET_HEADS):
            qk = slice(h * RET_QK_DIM, (h + 1) * RET_QK_DIM)
            vv = slice(h * RET_V_DIM, (h + 1) * RET_V_DIM)
            q = q_ref[b, :, qk]
            k = k_ref[b, :, qk]
            v = v_ref[b, :, vv]
            kt = kt_ref[b, h]
            s0 = s_ref[b, h]
            o = _dot(q.astype(BF16), s0.astype(BF16)) * qd_ref[h]
            s1 = s0 * gc_ref[h]
            for i in range(n_tok):
                sc_i = jnp.sum(q * k[i:i + 1, :], axis=-1, keepdims=True) * dec_ref[h][:, i:i + 1]
                o = o + sc_i * v[i:i + 1, :]
                s1 = s1 + (kt[:, i:i + 1] * kd_ref[h][:, i:i + 1]) * v[i:i + 1, :]
            st_ref[b, h] = s1
            o_ref[b, :, vv] = o * lax.rsqrt(jnp.mean(o * o, axis=-1, keepdims=True) + EPS)


def _ret_sample(rq, rk, rv, state, n_tok):
    db = state.shape[0]
    pad = SUBLANES - n_tok
    n_b = 8

    def pad_rows(t):
        t = t.astype(F32).reshape(db, n_tok, t.shape[-1])
        return jnp.pad(t, ((0, 0), (0, pad), (0, 0)))

    q, k, v = pad_rows(rq), pad_rows(rk), pad_rows(rv)
    kt = k.reshape(db, SUBLANES, RET_HEADS, RET_QK_DIM).transpose(0, 2, 3, 1)
    lg = jnp.log1p(-jnp.exp2(-5.0 - jnp.arange(RET_HEADS, dtype=F32)))
    idx = jnp.arange(SUBLANES, dtype=F32)
    diff = idx[:, None] - idx[None, :]
    dec = jnp.where(diff[None] >= 0, jnp.exp(lg[:, None, None] * jnp.maximum(diff, 0.0)[None]), 0.0)
    qd = jnp.broadcast_to(jnp.exp(lg[:, None] * (idx[None, :] + 1.0))[:, :, None], (RET_HEADS, SUBLANES, RET_V_DIM))
    kd = jnp.broadcast_to(jnp.exp(lg[:, None] * (n_tok - 1.0 - idx)[None, :])[:, None, :],
                          (RET_HEADS, RET_QK_DIM, SUBLANES))
    gc = jnp.broadcast_to(jnp.exp(lg * n_tok)[:, None, None], (RET_HEADS, RET_QK_DIM, RET_V_DIM))
    b3 = lambda i: (i, 0, 0)
    b4 = lambda i: (i, 0, 0, 0)
    o, st = pl.pallas_call(
        functools.partial(_ret_sample_kernel, n_tok=n_tok, n_b=n_b),
        grid=(db // n_b,),
        in_specs=[pl.BlockSpec((n_b, SUBLANES, RET_QK), b3), pl.BlockSpec((n_b, SUBLANES, RET_QK), b3),
                  pl.BlockSpec((n_b, RET_HEADS, RET_QK_DIM, SUBLANES), b4),
                  pl.BlockSpec((n_b, SUBLANES, RET_V), b3),
                  pl.BlockSpec((n_b, RET_HEADS, RET_QK_DIM, RET_V_DIM), b4),
                  _resident(dec.shape), _resident(qd.shape), _resident(kd.shape), _resident(gc.shape)],
        out_specs=[pl.BlockSpec((n_b, SUBLANES, RET_V), b3),
                   pl.BlockSpec((n_b, RET_HEADS, RET_QK_DIM, RET_V_DIM), b4)],
        out_shape=[jax.ShapeDtypeStruct((db, SUBLANES, RET_V), F32),
                   jax.ShapeDtypeStruct(state.shape, F32)],
        compiler_params=_params(),
        name="ret_sample",
    )(q, k, kt, v, state, dec, qd, kd, gc)
    return o[:, :n_tok].reshape(db * n_tok, RET_V), st


def _attn_kernel(q_ref, kp_ref, kc_ref, vp_ref, vc_ref, o_ref, l_ref):
    j = pl.program_id(2)
    q = q_ref[0]
    kp, kc = kp_ref[0].astype(BF16), kc_ref[0].astype(BF16)
    vp, vc = vp_ref[0].astype(BF16), vc_ref[0].astype(BF16)
    qi = lax.broadcasted_iota(jnp.int32, (BAND_BLOCK, BAND_BLOCK), 0)
    ki = lax.broadcasted_iota(jnp.int32, (BAND_BLOCK, BAND_BLOCK), 1)
    mask_p = ki >= qi + jnp.where(j > 0, 0, BAND_BLOCK)
    mask_c = ki <= qi
    for h in range(ATT_HEADS):
        sl = slice(h * ATT_HEAD_DIM, (h + 1) * ATT_HEAD_DIM)
        sp = jnp.where(mask_p, _dot_nt(q[:, sl], kp[:, sl]), NEG)
        sc = jnp.where(mask_c, _dot_nt(q[:, sl], kc[:, sl]), NEG)
        m = jnp.maximum(jnp.max(sp, axis=-1, keepdims=True), jnp.max(sc, axis=-1, keepdims=True))
        pp = jnp.exp(sp - m)
        pc = jnp.exp(sc - m)
        den = jnp.sum(pp, axis=-1, keepdims=True) + jnp.sum(pc, axis=-1, keepdims=True)
        o = (_dot(pp.astype(BF16), vp[:, sl]) + _dot(pc.astype(BF16), vc[:, sl])) / den
        o_ref[0, :, sl] = o
        l_ref[0, :, sl] = jnp.broadcast_to(m + jnp.log(den), (BAND_BLOCK, ATT_HEAD_DIM))


def _attn_prompt(aq, kv, g, batch, seq):
    _, dil = ATT_GROUPS[g]
    length = seq // dil
    nb = length // BAND_BLOCK
    qv = aq.reshape(batch, length, dil * ATT_W)
    kvv = kv.reshape(batch, length, dil * 2 * ATT_OUT)
    blk = (1, BAND_BLOCK, ATT_OUT)
    n_q = ATT_W // ATT_OUT
    out = pl.pallas_call(
        _attn_kernel,
        grid=(batch, dil, nb),
        in_specs=[pl.BlockSpec(blk, lambda b, r, j: (b, j, r * n_q + g)),
                  pl.BlockSpec(blk, lambda b, r, j: (b, jnp.maximum(j - 1, 0), 2 * r)),
                  pl.BlockSpec(blk, lambda b, r, j: (b, j, 2 * r)),
                  pl.BlockSpec(blk, lambda b, r, j: (b, jnp.maximum(j - 1, 0), 2 * r + 1)),
                  pl.BlockSpec(blk, lambda b, r, j: (b, j, 2 * r + 1))],
        out_specs=[pl.BlockSpec(blk, lambda b, r, j: (b, j, r))] * 2,
        out_shape=[jax.ShapeDtypeStruct((batch, length, dil * ATT_OUT), F32)] * 2,
        compiler_params=_params(3),
        name=f"attn_prompt_g{g}",
    )(qv, kvv, kvv, kvv, kvv)
    return [t.reshape(batch * seq, ATT_OUT) for t in out]


def _attn_sample_kernel(q_ref, c_ref, n_ref, o_ref, l_ref, nc_ref, *, g, n_tok, wb):
    win, dil = ATT_GROUPS[g]
    nk = win // dil
    _log2(dil)
    n_col = ATT_HEADS * n_tok
    nc_ref[0, 0:wb - n_tok, :] = c_ref[0, n_tok:wb, :]
    nc_ref[0, wb - n_tok:wb, :] = n_ref[0, 0:n_tok, :]

    q = q_ref[0][:, g * ATT_OUT:(g + 1) * ATT_OUT]
    row = lax.broadcasted_iota(jnp.int32, (n_col, ATT_OUT), 0)
    lane = lax.broadcasted_iota(jnp.int32, (n_col, ATT_OUT), 1)
    head_sel = (lane >> _log2(ATT_HEAD_DIM)) == (row >> _log2(n_tok))
    qbd = jnp.where(head_sel, q, jnp.zeros_like(q))

    kc = c_ref[0, :, 0:ATT_OUT].astype(BF16)
    vc = c_ref[0, :, ATT_OUT:2 * ATT_OUT].astype(BF16)
    pad = jnp.zeros((LANES - SUBLANES, ATT_OUT), F32)
    kn = jnp.concatenate([n_ref[0, :, 0:ATT_OUT], pad], axis=0).astype(BF16)
    vn = jnp.concatenate([n_ref[0, :, ATT_OUT:2 * ATT_OUT], pad], axis=0).astype(BF16)

    s_c = _dot_nt(qbd, kc)
    s_n = _dot_nt(qbd, kn)
    t_c = lax.broadcasted_iota(jnp.int32, (n_col, wb), 0) & (n_tok - 1)
    d_c = wb + t_c - lax.broadcasted_iota(jnp.int32, (n_col, wb), 1)
    ok_c = ((d_c & (dil - 1)) == 0) & (d_c <= nk * dil)
    t_n = lax.broadcasted_iota(jnp.int32, (n_col, LANES), 0) & (n_tok - 1)
    d_n = t_n - lax.broadcasted_iota(jnp.int32, (n_col, LANES), 1)
    ok_n = (d_n >= 0) & ((d_n & (dil - 1)) == 0) & (d_n <= nk * dil)
    s_c = jnp.where(ok_c, s_c, NEG)
    s_n = jnp.where(ok_n, s_n, NEG)
    m = jnp.maximum(jnp.max(s_c, axis=-1, keepdims=True), jnp.max(s_n, axis=-1, keepdims=True))
    p_c = jnp.exp(s_c - m)
    p_n = jnp.exp(s_n - m)
    den = jnp.sum(p_c, axis=-1, keepdims=True) + jnp.sum(p_n, axis=-1, keepdims=True)
    o = (_dot(p_c.astype(BF16), vc) + _dot(p_n.astype(BF16), vn)) / den
    lse = jnp.broadcast_to(m + jnp.log(den), (n_col, ATT_OUT))
    o = jnp.where(head_sel, o, 0.0)
    lse = jnp.where(head_sel, lse, 0.0)
    o_t = o[0:n_tok]
    l_t = lse[0:n_tok]
    for h in range(1, ATT_HEADS):
        o_t = o_t + o[h * n_tok:(h + 1) * n_tok]
        l_t = l_t + lse[h * n_tok:(h + 1) * n_tok]
    o_ref[0] = o_t
    l_ref[0] = l_t


def _attn_sample(q_tiled, cache, kv_new, g, n_tok):
    db, wb, _ = cache.shape
    b3 = lambda i: (i, 0, 0)
    return pl.pallas_call(
        functools.partial(_attn_sample_kernel, g=g, n_tok=n_tok, wb=wb),
        grid=(db,),
        in_specs=[pl.BlockSpec((1, ATT_HEADS * n_tok, ATT_W), b3),
                  pl.BlockSpec((1, wb, 2 * ATT_OUT), b3),
                  pl.BlockSpec((1, SUBLANES, 2 * ATT_OUT), b3)],
        out_specs=[pl.BlockSpec((1, n_tok, ATT_OUT), b3), pl.BlockSpec((1, n_tok, ATT_OUT), b3),
                   pl.BlockSpec((1, wb, 2 * ATT_OUT), b3)],
        out_shape=[jax.ShapeDtypeStruct((db, n_tok, ATT_OUT), F32), jax.ShapeDtypeStruct((db, n_tok, ATT_OUT), F32),
                   jax.ShapeDtypeStruct(cache.shape, F32)],
        compiler_params=_params(),
        name=f"attn_sample_g{g}",
    )(q_tiled, cache, kv_new)


def _merge_kernel(x_ref, on_ref, o0_ref, o1_ref, o2_ref, l0_ref, l1_ref, l2_ref, gpre_ref, wg_ref, wbr_ref,
                  wba_ref, wout_ref, gpost_ref, gffn_ref, x1_ref, h2_ref):
    x = x_ref[...]
    h = _rms(x, gpre_ref[...]).astype(BF16)
    rg = _dot(h, wg_ref[:, 0:RET_V])
    r = (on_ref[...] * (rg * jax.nn.sigmoid(rg))).astype(BF16)
    br = _dot(r, wbr_ref[...])
    l0, l1, l2 = l0_ref[...], l1_ref[...], l2_ref[...]
    mx = jnp.maximum(jnp.maximum(l0, l1), l2)
    e0, e1, e2 = jnp.exp(l0 - mx), jnp.exp(l1 - mx), jnp.exp(l2 - mx)
    att = (e0 * o0_ref[...] + e1 * o1_ref[...] + e2 * o2_ref[...]) / (e0 + e1 + e2)
    ba = _dot(att.astype(BF16), wba_ref[...])
    gr = _dot(h, wg_ref[:, RET_V:RET_V + D_MODEL])
    ga = _dot(h, wg_ref[:, RET_V + D_MODEL:RET_V + 2 * D_MODEL])
    mix = jax.nn.sigmoid(gr) * br + jax.nn.sigmoid(ga) * ba
    mo = _dot(mix.astype(BF16), wout_ref[...])
    x1 = x + _rms(mo, gpost_ref[...])
    x1_ref[...] = x1
    h2_ref[...] = _rms(x1, gffn_ref[...]).astype(BF16)


def _merge(x2d, on, outs, lses, g_pre, w_gates, w_br, w_ba, w_out, g_post, g_ffn, tm):
    m = x2d.shape[0]
    row = lambda i: (i, 0)
    vec = _resident((1, D_MODEL))
    att_spec = pl.BlockSpec((tm, ATT_OUT), row)
    return pl.pallas_call(
        _merge_kernel,
        grid=(m // tm,),
        in_specs=[pl.BlockSpec((tm, D_MODEL), row), pl.BlockSpec((tm, RET_V), row)] + [att_spec] * 6 + [
            vec, _resident(w_gates.shape), _resident(w_br.shape), _resident(w_ba.shape), _resident(w_out.shape),
            vec, vec],
        out_specs=[pl.BlockSpec((tm, D_MODEL), row), pl.BlockSpec((tm, D_MODEL), row)],
        out_shape=[jax.ShapeDtypeStruct((m, D_MODEL), F32), jax.ShapeDtypeStruct((m, D_MODEL), BF16)],
        compiler_params=_params(),
        name="merge",
    )(x2d, on, *outs, *lses, g_pre, w_gates, w_br, w_ba, w_out, g_post, g_ffn)


def _ffn_kernel(*refs, tm, tiles_per_seq, seq_tok):
    if seq_tok is None:
        h2_ref, x1_ref, wup_ref, wdn_ref, cw_ref, cb_ref, g_ref, y_ref, cs_ref, carry_ref = refs
    else:
        h2_ref, x1_ref, p1_ref, p2_ref, wup_ref, wdn_ref, cw_ref, cb_ref, g_ref, y_ref, u_ref = refs
    h2 = h2_ref[...]
    t = lax.broadcasted_iota(jnp.int32, (tm, FF_CHUNK), 0)
    if seq_tok is None:
        @pl.when(pl.program_id(0) % tiles_per_seq == 0)
        def _():
            carry_ref[...] = jnp.zeros_like(carry_ref)
    else:
        t = t & (seq_tok - 1)
        _log2(seq_tok)

    def conv(col):
        cols = slice(col, col + FF_CHUNK)
        u = _dot(h2, wup_ref[:, cols])
        if seq_tok is None:
            prev = carry_ref[:, cols]
            last1 = jnp.broadcast_to(prev[SUBLANES - 1:SUBLANES], u.shape)
            last2 = jnp.broadcast_to(prev[SUBLANES - 2:SUBLANES - 1], u.shape)
            u1 = jnp.where(t == 0, last1, pltpu.roll(u, 1, 0))
            u2 = jnp.where(t == 0, last2, jnp.where(t == 1, last1, pltpu.roll(u, 2, 0)))
            carry_ref[:, cols] = u[tm - SUBLANES:tm]
        else:
            u1 = jnp.where(t == 0, p1_ref[:, cols], pltpu.roll(u, 1, 0))
            u2 = jnp.where(t <= 1, p2_ref[:, cols], pltpu.roll(u, 2, 0))
            u_ref[:, cols] = u
        cw = cw_ref[:, cols]
        return cb_ref[:, cols] + cw[0:1] * u2 + cw[1:2] * u1 + cw[2:3] * u

    acc = jnp.zeros((tm, D_MODEL), F32)
    for j in range(D_FF // FF_CHUNK):
        cg = conv(j * FF_CHUNK)
        cv = conv(D_FF + j * FF_CHUNK)
        f = (jax.nn.gelu(cg, approximate=True) * cv).astype(BF16)
        acc = acc + _dot(f, wdn_ref[j * FF_CHUNK:(j + 1) * FF_CHUNK, :])
    y_ref[...] = x1_ref[...] + _rms(acc, g_ref[...])
    if seq_tok is None:
        @pl.when(pl.program_id(0) % tiles_per_seq == tiles_per_seq - 1)
        def _():
            cs_ref[0] = carry_ref[SUBLANES - (CONV_W - 1):SUBLANES, :]


def _ffn_prompt(h2, x1, w_up, w_dn, conv_w, conv_b, g_post, batch, seq, tm):
    m = h2.shape[0]
    tps = seq // tm
    row = lambda i: (i, 0)
    return pl.pallas_call(
        functools.partial(_ffn_kernel, tm=tm, tiles_per_seq=tps, seq_tok=None),
        grid=(m // tm,),
        in_specs=[pl.BlockSpec((tm, D_MODEL), row), pl.BlockSpec((tm, D_MODEL), row),
                  _resident(w_up.shape), _resident(w_dn.shape), _resident(conv_w.shape), _resident(conv_b.shape),
                  _resident((1, D_MODEL))],
        out_specs=[pl.BlockSpec((tm, D_MODEL), row),
                   pl.BlockSpec((1, CONV_W - 1, 2 * D_FF), lambda i: (i // tps, 0, 0))],
        out_shape=[jax.ShapeDtypeStruct((m, D_MODEL), F32),
                   jax.ShapeDtypeStruct((batch, CONV_W - 1, 2 * D_FF), F32)],
        scratch_shapes=[pltpu.VMEM((SUBLANES, 2 * D_FF), F32)],
        compiler_params=_params(),
        name="ffn_prompt",
    )(h2, x1, w_up, w_dn, conv_w, conv_b, g_post)


def _ffn_sample(h2, x1, p1, p2, w_up, w_dn, conv_w, conv_b, g_post, n_tok):
    m = h2.shape[0]
    const = lambda i: (0, 0)
    full = lambda a: _resident(a.shape)
    return pl.pallas_call(
        functools.partial(_ffn_kernel, tm=m, tiles_per_seq=1, seq_tok=n_tok),
        grid=(1,),
        in_specs=[full(h2), full(x1), full(p1), full(p2), full(w_up), full(w_dn), full(conv_w), full(conv_b),
                  _resident((1, D_MODEL))],
        out_specs=[pl.BlockSpec((m, D_MODEL), const), pl.BlockSpec((m, 2 * D_FF), const)],
        out_shape=[jax.ShapeDtypeStruct((m, D_MODEL), F32), jax.ShapeDtypeStruct((m, 2 * D_FF), F32)],
        compiler_params=_params(),
        name="ffn_sample",
    )(h2, x1, p1, p2, w_up, w_dn, conv_w, conv_b, g_post)


def _rot_tables(pos):
    pos = pos.astype(F32)
    rf = RET_THETA ** (-jnp.linspace(0.0, 1.0, RET_QK_DIM // 2, dtype=F32))
    ang = pos[:, None] * rf[None, :]
    cos, sin = jnp.cos(ang), jnp.sin(ang)
    cr = jnp.concatenate([cos, cos], axis=-1)
    sr = jnp.concatenate([-sin, sin], axis=-1)
    k_scale = RET_QK_DIM ** -0.5
    half = ROPE_DIM // 2
    af = ROPE_THETA ** (-jnp.arange(half, dtype=F32) / half)
    ang = pos[:, None] * af[None, :]
    cos, sin = jnp.cos(ang), jnp.sin(ang)
    n = pos.shape[0]
    rest = ATT_HEAD_DIM - ROPE_DIM
    zh = jnp.zeros((n, half), F32)
    ca = jnp.concatenate([cos, cos, jnp.ones((n, rest), F32)], axis=-1)
    sa1 = jnp.concatenate([zh, sin, jnp.zeros((n, rest), F32)], axis=-1)
    sa2 = jnp.concatenate([-sin, zh, jnp.zeros((n, rest), F32)], axis=-1)
    rep = LANES // ATT_HEAD_DIM
    ca, sa1, sa2 = (jnp.tile(t, (1, rep)) for t in (ca, sa1, sa2))
    return [cr, sr, cr * k_scale, sr * k_scale, ca, sa1, sa2]


def kernel(x_prompt, x_sample, cache_kv_g0, cache_kv_g1, cache_kv_g2, state_ret, state_conv, norm_mix_pre, w_in,
           w_branch_ret, w_branch_attn, w_out, norm_mix_post, norm_ffn_pre, w_ffn_up, conv_w, conv_b, w_ffn_down,
           norm_ffn_post):
    batch, seq, _ = x_prompt.shape
    db, n_tok, _ = x_sample.shape
    depth = w_in.shape[0]
    assert depth == 1 and seq % (ATT_GROUPS[-1][1] * BAND_BLOCK) == 0 and n_tok <= SUBLANES
    caches = (cache_kv_g0, cache_kv_g1, cache_kv_g2)

    wi = w_in[0]
    o_rg = 2 * RET_QK + RET_V
    o_aq = o_rg + RET_V
    o_gr = o_aq + 3 * ATT_W
    w_qkv = jnp.concatenate([wi[:, :o_rg], wi[:, o_aq:o_aq + ATT_W] * ATT_HEAD_DIM ** -0.5,
                             wi[:, o_aq + ATT_W:o_gr]], axis=1).astype(BF16)
    w_gates = jnp.concatenate([wi[:, o_rg:o_aq], wi[:, o_gr:]], axis=1).astype(BF16)
    w_br, w_ba, w_o = w_branch_ret[0].astype(BF16), w_branch_attn[0].astype(BF16), w_out[0].astype(BF16)
    w_up, w_dn = w_ffn_up[0].astype(BF16), w_ffn_down[0].astype(BF16)
    cb = conv_b[0][None, :]
    cw = conv_w[0]
    g_pre, g_post, g_ffn, g_post2 = (t[0][None, :] for t in (norm_mix_pre, norm_mix_post, norm_ffn_pre, norm_ffn_post))

    tm = 512
    xp = x_prompt.reshape(batch * seq, D_MODEL)
    rq, rk, rv, aq, kv0, kv1, kv2 = _proj(xp, g_pre, w_qkv, _rot_tables(jnp.arange(seq)), tm)
    kvs = (kv0, kv1, kv2)
    on, p_ret = _ret_prompt(rq, rk, rv, batch, seq)
    outs, lses = [], []
    for g in range(N_GROUPS):
        o_g, l_g = _attn_prompt(aq, kvs[g], g, batch, seq)
        outs.append(o_g)
        lses.append(l_g)
    x1, h2 = _merge(xp, on, outs, lses, g_pre, w_gates, w_br, w_ba, w_o, g_post, g_ffn, tm)
    y_p, p_conv = _ffn_prompt(h2, x1, w_up, w_dn, cw, cb, g_post2, batch, seq, tm)
    p_kv = [kvs[g].reshape(batch, seq, 2, ATT_HEADS, ATT_HEAD_DIM)[:, seq - min(w, seq):][None]
            for g, (w, _) in enumerate(ATT_GROUPS)]

    ms = db * n_tok
    xs = x_sample.reshape(ms, D_MODEL)
    pos_s = jnp.tile(PAST_LEN + jnp.arange(n_tok), db)
    rq, rk, rv, aq, kv0, kv1, kv2 = _proj(xs, g_pre, w_qkv, _rot_tables(pos_s), ms)
    kvs = (kv0, kv1, kv2)
    on, s_ret = _ret_sample(rq, rk, rv, state_ret[0], n_tok)
    q_tiled = jnp.tile(aq.reshape(db, n_tok, ATT_W), (1, ATT_HEADS, 1))
    outs, lses, s_kv = [], [], []
    for g in range(N_GROUPS):
        cache = caches[g][0]
        wb = cache.shape[1]
        kv_new = jnp.pad(kvs[g].reshape(db, n_tok, 2 * ATT_OUT), ((0, 0), (0, SUBLANES - n_tok), (0, 0)))
        o_g, l_g, nc = _attn_sample(q_tiled, cache.reshape(db, wb, 2 * ATT_OUT), kv_new, g, n_tok)
        outs.append(o_g.reshape(ms, ATT_OUT))
        lses.append(l_g.reshape(ms, ATT_OUT))
        s_kv.append(nc.reshape(cache.shape)[None])
    x1, h2 = _merge(xs, on, outs, lses, g_pre, w_gates, w_br, w_ba, w_o, g_post, g_ffn, ms)
    st = state_conv[0]
    zeros = jnp.zeros((db, n_tok - 1, 2 * D_FF), F32)
    p1 = jnp.concatenate([st[:, 1:2], zeros], axis=1).reshape(ms, 2 * D_FF)
    p2 = jnp.concatenate([st, zeros[:, :n_tok - 2]], axis=1).reshape(ms, 2 * D_FF)
    y_s, u_s = _ffn_sample(h2, x1, p1, p2, w_up, w_dn, cw, cb, g_post2, n_tok)
    s_conv = u_s.reshape(db, n_tok, 2 * D_FF)[:, n_tok - (CONV_W - 1):][None]

    return (y_p.reshape(x_prompt.shape), y_s.reshape(x_sample.shape), p_kv[0], p_kv[1], p_kv[2], p_ret[None],
            p_conv[None], s_kv[0], s_kv[1], s_kv[2], s_ret[None], s_conv)
```

```python
import functools
import math

import jax
import jax.numpy as jnp
from jax import lax
from jax.experimental import pallas as pl
from jax.experimental.pallas import tpu as pltpu

F32 = jnp.float32
BF16 = jnp.bfloat16

D_MODEL = 1024
PAST_LEN = 16384
RET_HEADS = 4
RET_QK_DIM = 128
RET_V_DIM = 256
RET_CHUNK = 128
RET_THETA = 10000.0
RET_QK = RET_HEADS * RET_QK_DIM
RET_V = RET_HEADS * RET_V_DIM
ATT_GROUPS = ((128, 1), (512, 4), (2048, 16))
N_GROUPS = 3
ATT_HEADS = 4
ATT_HEAD_DIM = 64
ROPE_DIM = ATT_HEAD_DIM // 4
ROPE_THETA = 500000.0
BAND_BLOCK = 128
ATT_OUT = ATT_HEADS * ATT_HEAD_DIM
ATT_W = N_GROUPS * ATT_OUT
D_FF = 2816
CONV_W = 3
EPS = 1e-6

LANES = 128
SUBLANES = 8
MXU_N = 256
VMEM_LIMIT = 56 * 1024 * 1024
NEG = -1e30
FF_CHUNK = MXU_N
ATTN_UNROLL = 3
QKV_COLS = 2 * RET_QK + RET_V + 3 * ATT_W


def _rms(x, g):
    return x * lax.rsqrt(jnp.mean(x * x, axis=-1, keepdims=True) + EPS) * g


def _dot(a, b):
    return jnp.dot(a, b, preferred_element_type=F32)


def _dot_nt(a, b):
    return lax.dot_general(a, b, (((1,), (1,)), ((), ())), preferred_element_type=F32)


def _resident(shape):
    return pl.BlockSpec(shape, lambda *_: (0,) * len(shape), pipeline_mode=pl.Buffered(1))


def _log2(n):
    assert n > 0 and n & (n - 1) == 0, n
    return n.bit_length() - 1


def _params(n_axes=1):
    return pltpu.CompilerParams(dimension_semantics=("arbitrary",) * n_axes, vmem_limit_bytes=VMEM_LIMIT)


def _proj_kernel(x_ref, g_ref, w_ref, crq_ref, srq_ref, crk_ref, srk_ref, ca_ref, sa1_ref, sa2_ref, *rest,
                 tm, by_residue):
    rq_ref, rk_ref, rv_ref = rest[:3]
    if by_residue:
        res_refs = rest[3:3 + 3 * N_GROUPS]
        kv_refs = rest[3 + 3 * N_GROUPS:3 + 4 * N_GROUPS]
        scr = rest[3 + 4 * N_GROUPS]
    else:
        aq_ref = rest[3]
        kv_refs = rest[4:4 + N_GROUPS]
    h = _rms(x_ref[...], g_ref[...]).astype(BF16)

    def mm(c0, width=MXU_N):
        return _dot(h, w_ref[:, c0:c0 + width])

    def rot_ret(t, c, s):
        return t * c + pltpu.roll(t, RET_QK_DIM // 2, 1) * s

    ca, sa1, sa2 = ca_ref[...], sa1_ref[...], sa2_ref[...]

    def rot_att(t):
        return t * ca + pltpu.roll(t, ROPE_DIM // 2, 1) * sa1 + pltpu.roll(t, MXU_N - ROPE_DIM // 2, 1) * sa2

    def put_by_residue(ref, val, dil):
        if dil == 1:
            ref[0, 0] = val.astype(BF16)
            return
        for half in range(MXU_N // LANES):
            lanes = slice(half * LANES, (half + 1) * LANES)
            scr[half] = val[:, lanes]
            for r in range(dil):
                ref[0, r, :, lanes] = scr[half, pl.ds(r, tm // dil, stride=dil), :].astype(BF16)

    for c in range(RET_QK // MXU_N):
        q2 = mm(c * MXU_N)
        k2 = mm(RET_QK + c * MXU_N)
        for half in range(MXU_N // LANES):
            lo = half * LANES
            col = c * MXU_N + lo
            rq_ref[:, col:col + LANES] = rot_ret(q2[:, lo:lo + LANES], crq_ref[...], srq_ref[...]).astype(BF16)
            rk_ref[:, col:col + LANES] = rot_ret(k2[:, lo:lo + LANES], crk_ref[...], srk_ref[...]).astype(BF16)
    for c in range(RET_V // MXU_N):
        rv_ref[:, c * MXU_N:(c + 1) * MXU_N] = mm(2 * RET_QK + c * MXU_N).astype(BF16)
    base = 2 * RET_QK + RET_V
    for g in range(N_GROUPS):
        q2 = rot_att(mm(base + g * ATT_OUT))
        k2 = rot_att(mm(base + ATT_W + g * ATT_OUT))
        v2 = mm(base + 2 * ATT_W + g * ATT_OUT)
        if by_residue:
            dil = ATT_GROUPS[g][1]
            put_by_residue(res_refs[3 * g], q2, dil)
            put_by_residue(res_refs[3 * g + 1], k2, dil)
            put_by_residue(res_refs[3 * g + 2], v2, dil)
            rows = kv_refs[g].shape[1]
            kv_refs[g][0, :, 0:ATT_OUT] = k2[tm - rows:tm]
            kv_refs[g][0, :, ATT_OUT:2 * ATT_OUT] = v2[tm - rows:tm]
        else:
            aq_ref[:, g * ATT_OUT:(g + 1) * ATT_OUT] = q2.astype(BF16)
            kv_refs[g][:, 0:ATT_OUT] = k2
            kv_refs[g][:, ATT_OUT:2 * ATT_OUT] = v2


def _proj(x2d, g_pre, w_qkv, tabs, tm, seq=None):
    m = x2d.shape[0]
    n_tab = tabs[0].shape[0] // tm
    row = lambda i: (i, 0)
    tab = lambda i: (i % n_tab, 0)
    in_specs = [pl.BlockSpec((tm, D_MODEL), row), _resident((1, D_MODEL)), _resident((D_MODEL, QKV_COLS))]
    in_specs += [pl.BlockSpec((tm, LANES), tab)] * 4 + [pl.BlockSpec((tm, MXU_N), tab)] * 3
    out_specs = [pl.BlockSpec((tm, w), row) for w in (RET_QK, RET_QK, RET_V)]
    out_shape = [jax.ShapeDtypeStruct((m, w), BF16) for w in (RET_QK, RET_QK, RET_V)]
    scratch = []
    if seq is None:
        widths = (ATT_W,) + (2 * ATT_OUT,) * N_GROUPS
        out_specs += [pl.BlockSpec((tm, w), row) for w in widths]
        out_shape += [jax.ShapeDtypeStruct((m, w), d) for w, d in zip(widths, (BF16,) + (F32,) * N_GROUPS)]
    else:
        batch, tps = m // seq, seq // tm
        for _, dil in ATT_GROUPS:
            assert tm % (dil * 2 * SUBLANES) == 0
            out_specs += [pl.BlockSpec((1, dil, tm // dil, ATT_OUT), lambda i: (i // tps, 0, i % tps, 0))] * 3
            out_shape += [jax.ShapeDtypeStruct((batch, dil, seq // dil, ATT_OUT), BF16)] * 3
        for win, _ in ATT_GROUPS:
            win = min(win, seq)
            if win == seq:
                out_specs.append(pl.BlockSpec((1, tm, 2 * ATT_OUT), lambda i: (i // tps, i % tps, 0)))
            else:
                assert win <= tm
                out_specs.append(pl.BlockSpec((1, win, 2 * ATT_OUT), lambda i: (i // tps, 0, 0)))
            out_shape.append(jax.ShapeDtypeStruct((batch, win, 2 * ATT_OUT), F32))
        scratch = [pltpu.VMEM((MXU_N // LANES, tm, LANES), F32)]
    return pl.pallas_call(
        functools.partial(_proj_kernel, tm=tm, by_residue=seq is not None),
        grid=(m // tm,),
        in_specs=in_specs,
        out_specs=out_specs,
        out_shape=out_shape,
        scratch_shapes=scratch,
        compiler_params=_params(),
        name="proj",
    )(x2d, g_pre, w_qkv, *tabs)


def _ret_kernel(q_ref, k_ref, v_ref, dec_ref, qd_ref, kd_ref, gc_ref, o_ref, st_ref, s_scr, *, n_chunks):
    s_scr[...] = jnp.zeros_like(s_scr)

    def body(c, carry):
        r0 = pl.multiple_of(c * RET_CHUNK, RET_CHUNK)
        rows = pl.ds(r0, RET_CHUNK)
        for h in range(RET_HEADS):
            qk = slice(h * RET_QK_DIM, (h + 1) * RET_QK_DIM)
            vv = slice(h * RET_V_DIM, (h + 1) * RET_V_DIM)
            q = q_ref[rows, qk]
            k = k_ref[rows, qk]
            v = v_ref[rows, vv]
            s0 = s_scr[h]
            sc = _dot_nt(q, k) * dec_ref[h]
            o = _dot(sc.astype(BF16), v) + _dot(q, s0.astype(BF16)) * qd_ref[h]
            kd_t = (k.astype(F32) * kd_ref[h]).T.astype(BF16)
            s_scr[h] = s0 * gc_ref[h] + _dot(kd_t, v)
            o_ref[rows, vv] = o * lax.rsqrt(jnp.mean(o * o, axis=-1, keepdims=True) + EPS)
        return carry

    lax.fori_loop(0, n_chunks, body, 0)
    st_ref[0] = s_scr[...]


def _ret_tables(chunk):
    lg = jnp.log1p(-jnp.exp2(-5.0 - jnp.arange(RET_HEADS, dtype=F32)))
    idx = jnp.arange(RET_CHUNK, dtype=F32)
    diff = idx[:, None] - idx[None, :]
    dec = jnp.where(diff[None] >= 0, jnp.exp(lg[:, None, None] * jnp.maximum(diff, 0.0)[None]), 0.0)
    qd = jnp.exp(lg[:, None] * (idx[None, :] + 1.0))
    kd = jnp.exp(lg[:, None] * (chunk - 1.0 - idx)[None, :])
    gc = jnp.exp(lg * chunk)
    qd = jnp.broadcast_to(qd[:, :, None], (RET_HEADS, RET_CHUNK, RET_V_DIM))
    kd = jnp.broadcast_to(kd[:, :, None], (RET_HEADS, RET_CHUNK, RET_QK_DIM))
    gc = jnp.broadcast_to(gc[:, None, None], (RET_HEADS, RET_QK_DIM, RET_V_DIM))
    return dec, qd, kd, gc


def _ret_prompt(rq, rk, rv, batch, seq):
    dec, qd, kd, gc = _ret_tables(RET_CHUNK)
    row = lambda b: (b, 0)
    return pl.pallas_call(
        functools.partial(_ret_kernel, n_chunks=seq // RET_CHUNK),
        grid=(batch,),
        in_specs=[pl.BlockSpec((seq, RET_QK), row), pl.BlockSpec((seq, RET_QK), row), pl.BlockSpec((seq, RET_V), row),
                  _resident(dec.shape), _resident(qd.shape), _resident(kd.shape), _resident(gc.shape)],
        out_specs=[pl.BlockSpec((seq, RET_V), row),
                   pl.BlockSpec((1, RET_HEADS, RET_QK_DIM, RET_V_DIM), lambda b: (b, 0, 0, 0))],
        out_shape=[jax.ShapeDtypeStruct((batch * seq, RET_V), F32),
                   jax.ShapeDtypeStruct((batch, RET_HEADS, RET_QK_DIM, RET_V_DIM), F32)],
        scratch_shapes=[pltpu.VMEM((RET_HEADS, RET_QK_DIM, RET_V_DIM), F32)],
        compiler_params=_params(),
        name="ret_prompt",
    )(rq, rk, rv, dec, qd, kd, gc)


def _ret_sample_kernel(q_ref, k_ref, kt_ref, v_ref, s_ref, dec_ref, qd_ref, kd_ref, gc_ref, o_ref, st_ref,
                       *, n_tok, n_b):
    for b in range(n_b):
        for h in range(RET_HEADS):
            qk = slice(h * RET_QK_DIM, (h + 1) * RET_QK_DIM)
            vv = slice(h * RET_V_DIM, (h + 1) * RET_V_DIM)
            q = q_ref[b, :, qk]
            k = k_ref[b, :, qk]
            v = v_ref[b, :, vv]
            kt = kt_ref[b, h]
            s0 = s_ref[b, h]
            o = _dot(q.astype(BF16), s0.astype(BF16)) * qd_ref[h]
            s1 = s0 * gc_ref[h]
            for i in range(n_tok):
                sc_i = jnp.sum(q * k[i:i + 1, :], axis=-1, keepdims=True) * dec_ref[h][:, i:i + 1]
                o = o + sc_i * v[i:i + 1, :]
                s1 = s1 + (kt[:, i:i + 1] * kd_ref[h][:, i:i + 1]) * v[i:i + 1, :]
            st_ref[b, h] = s1
            o_ref[b, :, vv] = o * lax.rsqrt(jnp.mean(o * o, axis=-1, keepdims=True) + EPS)


def _ret_sample(rq, rk, rv, state, n_tok):
    db = state.shape[0]
    pad = SUBLANES - n_tok
    n_b = 8

    def pad_rows(t):
        t = t.astype(F32).reshape(db, n_tok, t.shape[-1])
        return jnp.pad(t, ((0, 0), (0, pad), (0, 0)))

    q, k, v = pad_rows(rq), pad_rows(rk), pad_rows(rv)
    kt = k.reshape(db, SUBLANES, RET_HEADS, RET_QK_DIM).transpose(0, 2, 3, 1)
    lg = jnp.log1p(-jnp.exp2(-5.0 - jnp.arange(RET_HEADS, dtype=F32)))
    idx = jnp.arange(SUBLANES, dtype=F32)
    diff = idx[:, None] - idx[None, :]
    dec = jnp.where(diff[None] >= 0, jnp.exp(lg[:, None, None] * jnp.maximum(diff, 0.0)[None]), 0.0)
    qd = jnp.broadcast_to(jnp.exp(lg[:, None] * (idx[None, :] + 1.0))[:, :, None], (RET_HEADS, SUBLANES, RET_V_DIM))
    kd = jnp.broadcast_to(jnp.exp(lg[:, None] * (n_tok - 1.0 - idx)[None, :])[:, None, :],
                          (RET_HEADS, RET_QK_DIM, SUBLANES))
    gc = jnp.broadcast_to(jnp.exp(lg * n_tok)[:, None, None], (RET_HEADS, RET_QK_DIM, RET_V_DIM))
    b3 = lambda i: (i, 0, 0)
    b4 = lambda i: (i, 0, 0, 0)
    o, st = pl.pallas_call(
        functools.partial(_ret_sample_kernel, n_tok=n_tok, n_b=n_b),
        grid=(db // n_b,),
        in_specs=[pl.BlockSpec((n_b, SUBLANES, RET_QK), b3), pl.BlockSpec((n_b, SUBLANES, RET_QK), b3),
                  pl.BlockSpec((n_b, RET_HEADS, RET_QK_DIM, SUBLANES), b4),
                  pl.BlockSpec((n_b, SUBLANES, RET_V), b3),
                  pl.BlockSpec((n_b, RET_HEADS, RET_QK_DIM, RET_V_DIM), b4),
                  _resident(dec.shape), _resident(qd.shape), _resident(kd.shape), _resident(gc.shape)],
        out_specs=[pl.BlockSpec((n_b, SUBLANES, RET_V), b3),
                   pl.BlockSpec((n_b, RET_HEADS, RET_QK_DIM, RET_V_DIM), b4)],
        out_shape=[jax.ShapeDtypeStruct((db, SUBLANES, RET_V), F32),
                   jax.ShapeDtypeStruct(state.shape, F32)],
        compiler_params=_params(),
        name="ret_sample",
    )(q, k, kt, v, state, dec, qd, kd, gc)
    return o[:, :n_tok].reshape(db * n_tok, RET_V), st


def _attn_kernel(q_ref, k_ref, v_ref, o_ref, l_ref, vt_scr, bias_scr, *staging, dil, nb):
    n_blk = dil * nb
    length = nb * BAND_BLOCK
    blk = BAND_BLOCK
    o_dst, l_dst = staging if dil > 1 else (o_ref, l_ref)

    kk = lax.broadcasted_iota(jnp.int32, (2 * blk, 2 * LANES), 0)
    ql = lax.broadcasted_iota(jnp.int32, (2 * blk, 2 * LANES), 1) & (blk - 1)
    in_span = kk <= ql + blk
    bias_scr[...] = jnp.where(in_span & (kk >= ql), 0.0, NEG)

    def vt_body(c, carry):
        r0 = pl.multiple_of(c * blk, blk)
        vt_scr[c] = v_ref[0, pl.ds(r0, blk), :].astype(F32).T.astype(BF16)
        return carry

    lax.fori_loop(0, n_blk, vt_body, 0)

    lane = lax.broadcasted_iota(jnp.int32, (blk, LANES), 1)

    def scores(c, p):
        lanes = slice(p * LANES, (p + 1) * LANES)
        k0 = 0 if c % nb == 0 else blk
        q128 = q_ref[0, c * blk:(c + 1) * blk, lanes]
        zero = jnp.zeros_like(q128)
        q_pair = jnp.concatenate([jnp.where(lane < ATT_HEAD_DIM, q128, zero),
                                  jnp.where(lane >= ATT_HEAD_DIM, q128, zero)], axis=0)
        keys = k_ref[0, c * blk - k0:(c + 1) * blk, lanes]
        return _dot_nt(keys, q_pair) + bias_scr[blk - k0:2 * blk, :]

    def weights_values(c, p, s_t):
        lanes = slice(p * LANES, (p + 1) * LANES)
        m = jnp.max(s_t, axis=0, keepdims=True)
        p_t = jnp.exp(s_t - m).astype(BF16)
        v_t = vt_scr[c, lanes, :]
        if c % nb:
            v_t = jnp.concatenate([vt_scr[c - 1, lanes, :], v_t], axis=1)
        v_t = jnp.concatenate([v_t, jnp.ones((2 * SUBLANES, v_t.shape[1]), BF16)], axis=0)
        return _dot(v_t, p_t), m

    def finish(c, p, o_t, m):
        den = o_t[LANES:LANES + 1, :]
        lse = m + jnp.log(den)
        o_pair = jnp.concatenate([o_t[0:ATT_HEAD_DIM, 0:blk] / den[:, 0:blk],
                                  o_t[ATT_HEAD_DIM:LANES, blk:2 * blk] / den[:, blk:2 * blk]], axis=0)
        l_pair = jnp.concatenate([jnp.broadcast_to(lse[:, 0:blk], (ATT_HEAD_DIM, blk)),
                                  jnp.broadcast_to(lse[:, blk:2 * blk], (ATT_HEAD_DIM, blk))], axis=0)
        o_dst[p, c * blk:(c + 1) * blk, :] = o_pair.T
        l_dst[p, c * blk:(c + 1) * blk, :] = l_pair.T

    chains = [(c, p) for c in range(n_blk) for p in range(ATT_OUT // LANES)]
    s_next = scores(*chains[0])
    pending = None
    for i, chain in enumerate(chains):
        s_cur = s_next
        if i + 1 < len(chains):
            s_next = scores(*chains[i + 1])
        o_m = weights_values(*chain, s_cur)
        if pending is not None:
            finish(*pending)
        pending = chain + o_m
    finish(*pending)

    if dil > 1:
        for p in range(ATT_OUT // LANES):
            for r in range(dil):
                o_ref[p, pl.ds(r, length, stride=dil), :] = o_dst[p, r * length:(r + 1) * length, :]
                l_ref[p, pl.ds(r, length, stride=dil), :] = l_dst[p, r * length:(r + 1) * length, :]


def _attn_prompt(q, k, v, g, batch, seq):
    _, dil = ATT_GROUPS[g]
    nb = seq // dil // BAND_BLOCK
    _log2(nb)
    q, k, v = (t.reshape(batch, seq, ATT_OUT) for t in (q, k, v))
    in_spec = pl.BlockSpec((1, seq, ATT_OUT), lambda b: (b, 0, 0))
    n_pair = ATT_OUT // LANES
    out_spec = pl.BlockSpec((n_pair, seq, LANES), lambda b: (0, b, 0))
    scratch = [pltpu.VMEM((dil * nb, ATT_OUT, BAND_BLOCK), BF16), pltpu.VMEM((2 * BAND_BLOCK, 2 * LANES), F32)]
    if dil > 1:
        scratch += [pltpu.VMEM((n_pair, seq, LANES), F32)] * 2
    return pl.pallas_call(
        functools.partial(_attn_kernel, dil=dil, nb=nb),
        grid=(batch,),
        in_specs=[in_spec] * 3,
        out_specs=[out_spec] * 2,
        out_shape=[jax.ShapeDtypeStruct((n_pair, batch * seq, LANES), F32)] * 2,
        scratch_shapes=scratch,
        compiler_params=_params(),
        name=f"attn_prompt_g{g}",
    )(q, k, v)


def _attn_sample_kernel(q_ref, c_ref, n_ref, o_ref, l_ref, nc_ref, *, g, n_tok, wb):
    win, dil = ATT_GROUPS[g]
    nk = win // dil
    _log2(dil)
    n_col = ATT_HEADS * n_tok
    nc_ref[0, 0:wb - n_tok, :] = c_ref[0, n_tok:wb, :]
    nc_ref[0, wb - n_tok:wb, :] = n_ref[0, 0:n_tok, :]

    q = q_ref[0][:, g * ATT_OUT:(g + 1) * ATT_OUT]
    row = lax.broadcasted_iota(jnp.int32, (n_col, ATT_OUT), 0)
    lane = lax.broadcasted_iota(jnp.int32, (n_col, ATT_OUT), 1)
    head_sel = (lane >> _log2(ATT_HEAD_DIM)) == (row >> _log2(n_tok))
    qbd = jnp.where(head_sel, q, jnp.zeros_like(q))

    kc = c_ref[0, :, 0:ATT_OUT].astype(BF16)
    vc = c_ref[0, :, ATT_OUT:2 * ATT_OUT].astype(BF16)
    pad = jnp.zeros((LANES - SUBLANES, ATT_OUT), F32)
    kn = jnp.concatenate([n_ref[0, :, 0:ATT_OUT], pad], axis=0).astype(BF16)
    vn = jnp.concatenate([n_ref[0, :, ATT_OUT:2 * ATT_OUT], pad], axis=0).astype(BF16)

    s_c = _dot_nt(qbd, kc)
    s_n = _dot_nt(qbd, kn)
    t_c = lax.broadcasted_iota(jnp.int32, (n_col, wb), 0) & (n_tok - 1)
    d_c = wb + t_c - lax.broadcasted_iota(jnp.int32, (n_col, wb), 1)
    ok_c = ((d_c & (dil - 1)) == 0) & (d_c <= nk * dil)
    t_n = lax.broadcasted_iota(jnp.int32, (n_col, LANES), 0) & (n_tok - 1)
    d_n = t_n - lax.broadcasted_iota(jnp.int32, (n_col, LANES), 1)
    ok_n = (d_n >= 0) & ((d_n & (dil - 1)) == 0) & (d_n <= nk * dil)
    s_c = jnp.where(ok_c, s_c, NEG)
    s_n = jnp.where(ok_n, s_n, NEG)
    m = jnp.maximum(jnp.max(s_c, axis=-1, keepdims=True), jnp.max(s_n, axis=-1, keepdims=True))
    p_c = jnp.exp(s_c - m)
    p_n = jnp.exp(s_n - m)
    den = jnp.sum(p_c, axis=-1, keepdims=True) + jnp.sum(p_n, axis=-1, keepdims=True)
    o = (_dot(p_c.astype(BF16), vc) + _dot(p_n.astype(BF16), vn)) / den
    lse = jnp.broadcast_to(m + jnp.log(den), (n_col, ATT_OUT))
    o = jnp.where(head_sel, o, 0.0)
    lse = jnp.where(head_sel, lse, 0.0)
    o_t = o[0:n_tok]
    l_t = lse[0:n_tok]
    for h in range(1, ATT_HEADS):
        o_t = o_t + o[h * n_tok:(h + 1) * n_tok]
        l_t = l_t + lse[h * n_tok:(h + 1) * n_tok]
    o_ref[0] = o_t
    l_ref[0] = l_t


def _attn_sample(q_tiled, cache, kv_new, g, n_tok):
    db, wb, _ = cache.shape
    b3 = lambda i: (i, 0, 0)
    return pl.pallas_call(
        functools.partial(_attn_sample_kernel, g=g, n_tok=n_tok, wb=wb),
        grid=(db,),
        in_specs=[pl.BlockSpec((1, ATT_HEADS * n_tok, ATT_W), b3),
                  pl.BlockSpec((1, wb, 2 * ATT_OUT), b3),
                  pl.BlockSpec((1, SUBLANES, 2 * ATT_OUT), b3)],
        out_specs=[pl.BlockSpec((1, n_tok, ATT_OUT), b3), pl.BlockSpec((1, n_tok, ATT_OUT), b3),
                   pl.BlockSpec((1, wb, 2 * ATT_OUT), b3)],
        out_shape=[jax.ShapeDtypeStruct((db, n_tok, ATT_OUT), F32), jax.ShapeDtypeStruct((db, n_tok, ATT_OUT), F32),
                   jax.ShapeDtypeStruct(cache.shape, F32)],
        compiler_params=_params(),
        name=f"attn_sample_g{g}",
    )(q_tiled, cache, kv_new)


def _merge_kernel(x_ref, on_ref, o0_ref, o1_ref, o2_ref, l0_ref, l1_ref, l2_ref, gpre_ref, wg_ref, wbr_ref,
                  wba_ref, wout_ref, gpost_ref, gffn_ref, x1_ref, h2_ref):
    x = x_ref[...]
    h = _rms(x, gpre_ref[...]).astype(BF16)
    rg = _dot(h, wg_ref[:, 0:RET_V])
    r = (on_ref[...] * (rg * jax.nn.sigmoid(rg))).astype(BF16)
    br = _dot(r, wbr_ref[...])
    ba = None
    for p in range(ATT_OUT // LANES):
        l0, l1, l2 = l0_ref[p], l1_ref[p], l2_ref[p]
        mx = jnp.maximum(jnp.maximum(l0, l1), l2)
        e0, e1, e2 = jnp.exp(l0 - mx), jnp.exp(l1 - mx), jnp.exp(l2 - mx)
        att = (e0 * o0_ref[p] + e1 * o1_ref[p] + e2 * o2_ref[p]) / (e0 + e1 + e2)
        part = _dot(att.astype(BF16), wba_ref[p * LANES:(p + 1) * LANES, :])
        ba = part if ba is None else ba + part
    gr = _dot(h, wg_ref[:, RET_V:RET_V + D_MODEL])
    ga = _dot(h, wg_ref[:, RET_V + D_MODEL:RET_V + 2 * D_MODEL])
    mix = jax.nn.sigmoid(gr) * br + jax.nn.sigmoid(ga) * ba
    mo = _dot(mix.astype(BF16), wout_ref[...])
    x1 = x + _rms(mo, gpost_ref[...])
    x1_ref[...] = x1
    h2_ref[...] = _rms(x1, gffn_ref[...]).astype(BF16)


def _merge(x2d, on, outs, lses, g_pre, w_gates, w_br, w_ba, w_out, g_post, g_ffn, tm):
    m = x2d.shape[0]
    row = lambda i: (i, 0)
    vec = _resident((1, D_MODEL))
    att_spec = pl.BlockSpec((ATT_OUT // LANES, tm, LANES), lambda i: (0, i, 0))
    return pl.pallas_call(
        _merge_kernel,
        grid=(m // tm,),
        in_specs=[pl.BlockSpec((tm, D_MODEL), row), pl.BlockSpec((tm, RET_V), row)] + [att_spec] * 6 + [
            vec, _resident(w_gates.shape), _resident(w_br.shape), _resident(w_ba.shape), _resident(w_out.shape),
            vec, vec],
        out_specs=[pl.BlockSpec((tm, D_MODEL), row), pl.BlockSpec((tm, D_MODEL), row)],
        out_shape=[jax.ShapeDtypeStruct((m, D_MODEL), F32), jax.ShapeDtypeStruct((m, D_MODEL), BF16)],
        compiler_params=_params(),
        name="merge",
    )(x2d, on, *outs, *lses, g_pre, w_gates, w_br, w_ba, w_out, g_post, g_ffn)


def _ffn_kernel(*refs, tm, tiles_per_seq, seq_tok):
    if seq_tok is None:
        h2_ref, x1_ref, wup_ref, wdn_ref, cw_ref, cb_ref, g_ref, y_ref, cs_ref, carry_ref = refs
    else:
        h2_ref, x1_ref, p1_ref, p2_ref, wup_ref, wdn_ref, cw_ref, cb_ref, g_ref, y_ref, u_ref = refs
    h2 = h2_ref[...]
    t = lax.broadcasted_iota(jnp.int32, (tm, FF_CHUNK), 0)
    if seq_tok is None:
        @pl.when(pl.program_id(0) % tiles_per_seq == 0)
        def _():
            carry_ref[...] = jnp.zeros_like(carry_ref)
    else:
        t = t & (seq_tok - 1)
        _log2(seq_tok)

    def conv(col):
        cols = slice(col, col + FF_CHUNK)
        u = _dot(h2, wup_ref[:, cols])
        if seq_tok is None:
            prev = carry_ref[:, cols]
            last1 = jnp.broadcast_to(prev[SUBLANES - 1:SUBLANES], u.shape)
            last2 = jnp.broadcast_to(prev[SUBLANES - 2:SUBLANES - 1], u.shape)
            u1 = jnp.where(t == 0, last1, pltpu.roll(u, 1, 0))
            u2 = jnp.where(t == 0, last2, jnp.where(t == 1, last1, pltpu.roll(u, 2, 0)))
            carry_ref[:, cols] = u[tm - SUBLANES:tm]
        else:
            u1 = jnp.where(t == 0, p1_ref[:, cols], pltpu.roll(u, 1, 0))
            u2 = jnp.where(t <= 1, p2_ref[:, cols], pltpu.roll(u, 2, 0))
            u_ref[:, cols] = u
        cw = cw_ref[:, cols]
        return cb_ref[:, cols] + cw[0:1] * u2 + cw[1:2] * u1 + cw[2:3] * u

    acc = jnp.zeros((tm, D_MODEL), F32)
    for j in range(D_FF // FF_CHUNK):
        cg = conv(j * FF_CHUNK)
        cv = conv(D_FF + j * FF_CHUNK)
        f = (jax.nn.gelu(cg, approximate=True) * cv).astype(BF16)
        acc = acc + _dot(f, wdn_ref[j * FF_CHUNK:(j + 1) * FF_CHUNK, :])
    y_ref[...] = x1_ref[...] + _rms(acc, g_ref[...])
    if seq_tok is None:
        @pl.when(pl.program_id(0) % tiles_per_seq == tiles_per_seq - 1)
        def _():
            cs_ref[0] = carry_ref[SUBLANES - (CONV_W - 1):SUBLANES, :]


def _ffn_prompt(h2, x1, w_up, w_dn, conv_w, conv_b, g_post, batch, seq, tm):
    m = h2.shape[0]
    tps = seq // tm
    row = lambda i: (i, 0)
    return pl.pallas_call(
        functools.partial(_ffn_kernel, tm=tm, tiles_per_seq=tps, seq_tok=None),
        grid=(m // tm,),
        in_specs=[pl.BlockSpec((tm, D_MODEL), row), pl.BlockSpec((tm, D_MODEL), row),
                  _resident(w_up.shape), _resident(w_dn.shape), _resident(conv_w.shape), _resident(conv_b.shape),
                  _resident((1, D_MODEL))],
        out_specs=[pl.BlockSpec((tm, D_MODEL), row),
                   pl.BlockSpec((1, CONV_W - 1, 2 * D_FF), lambda i: (i // tps, 0, 0))],
        out_shape=[jax.ShapeDtypeStruct((m, D_MODEL), F32),
                   jax.ShapeDtypeStruct((batch, CONV_W - 1, 2 * D_FF), F32)],
        scratch_shapes=[pltpu.VMEM((SUBLANES, 2 * D_FF), F32)],
        compiler_params=_params(),
        name="ffn_prompt",
    )(h2, x1, w_up, w_dn, conv_w, conv_b, g_post)


def _ffn_sample(h2, x1, p1, p2, w_up, w_dn, conv_w, conv_b, g_post, n_tok):
    m = h2.shape[0]
    const = lambda i: (0, 0)
    full = lambda a: _resident(a.shape)
    return pl.pallas_call(
        functools.partial(_ffn_kernel, tm=m, tiles_per_seq=1, seq_tok=n_tok),
        grid=(1,),
        in_specs=[full(h2), full(x1), full(p1), full(p2), full(w_up), full(w_dn), full(conv_w), full(conv_b),
                  _resident((1, D_MODEL))],
        out_specs=[pl.BlockSpec((m, D_MODEL), const), pl.BlockSpec((m, 2 * D_FF), const)],
        out_shape=[jax.ShapeDtypeStruct((m, D_MODEL), F32), jax.ShapeDtypeStruct((m, 2 * D_FF), F32)],
        compiler_params=_params(),
        name="ffn_sample",
    )(h2, x1, p1, p2, w_up, w_dn, conv_w, conv_b, g_post)


def _rot_tables(pos):
    pos = pos.astype(F32)
    rf = RET_THETA ** (-jnp.linspace(0.0, 1.0, RET_QK_DIM // 2, dtype=F32))
    ang = pos[:, None] * rf[None, :]
    cos, sin = jnp.cos(ang), jnp.sin(ang)
    cr = jnp.concatenate([cos, cos], axis=-1)
    sr = jnp.concatenate([-sin, sin], axis=-1)
    k_scale = RET_QK_DIM ** -0.5
    half = ROPE_DIM // 2
    af = ROPE_THETA ** (-jnp.arange(half, dtype=F32) / half)
    ang = pos[:, None] * af[None, :]
    cos, sin = jnp.cos(ang), jnp.sin(ang)
    n = pos.shape[0]
    rest = ATT_HEAD_DIM - ROPE_DIM
    zh = jnp.zeros((n, half), F32)
    ca = jnp.concatenate([cos, cos, jnp.ones((n, rest), F32)], axis=-1)
    sa1 = jnp.concatenate([zh, sin, jnp.zeros((n, rest), F32)], axis=-1)
    sa2 = jnp.concatenate([-sin, zh, jnp.zeros((n, rest), F32)], axis=-1)
    rep = MXU_N // ATT_HEAD_DIM
    ca, sa1, sa2 = (jnp.tile(t, (1, rep)) for t in (ca, sa1, sa2))
    return [cr, sr, cr * k_scale, sr * k_scale, ca, sa1, sa2]


def kernel(x_prompt, x_sample, cache_kv_g0, cache_kv_g1, cache_kv_g2, state_ret, state_conv, norm_mix_pre, w_in,
           w_branch_ret, w_branch_attn, w_out, norm_mix_post, norm_ffn_pre, w_ffn_up, conv_w, conv_b, w_ffn_down,
           norm_ffn_post):
    batch, seq, _ = x_prompt.shape
    db, n_tok, _ = x_sample.shape
    depth = w_in.shape[0]
    assert depth == 1 and seq % (ATT_GROUPS[-1][1] * BAND_BLOCK) == 0 and n_tok <= SUBLANES
    caches = (cache_kv_g0, cache_kv_g1, cache_kv_g2)

    wi = w_in[0]
    o_rg = 2 * RET_QK + RET_V
    o_aq = o_rg + RET_V
    o_gr = o_aq + 3 * ATT_W
    w_qkv = jnp.concatenate([wi[:, :o_rg], wi[:, o_aq:o_aq + ATT_W] * ATT_HEAD_DIM ** -0.5,
                             wi[:, o_aq + ATT_W:o_gr]], axis=1).astype(BF16)
    w_gates = jnp.concatenate([wi[:, o_rg:o_aq], wi[:, o_gr:]], axis=1).astype(BF16)
    w_br, w_ba, w_o = w_branch_ret[0].astype(BF16), w_branch_attn[0].astype(BF16), w_out[0].astype(BF16)
    w_up, w_dn = w_ffn_up[0].astype(BF16), w_ffn_down[0].astype(BF16)
    cb = conv_b[0][None, :]
    cw = conv_w[0]
    g_pre, g_post, g_ffn, g_post2 = (t[0][None, :] for t in (norm_mix_pre, norm_mix_post, norm_ffn_pre, norm_ffn_post))

    tm = 512
    xp = x_prompt.reshape(batch * seq, D_MODEL)
    proj_out = _proj(xp, g_pre, w_qkv, _rot_tables(jnp.arange(seq)), tm, seq=seq)
    rq, rk, rv = proj_out[:3]
    qkv_res = proj_out[3:3 + 3 * N_GROUPS]
    kv_last = proj_out[3 + 3 * N_GROUPS:]
    on, p_ret = _ret_prompt(rq, rk, rv, batch, seq)
    outs, lses = [], []
    for g in range(N_GROUPS):
        o_g, l_g = _attn_prompt(*qkv_res[3 * g:3 * g + 3], g, batch, seq)
        outs.append(o_g)
        lses.append(l_g)
    x1, h2 = _merge(xp, on, outs, lses, g_pre, w_gates, w_br, w_ba, w_o, g_post, g_ffn, tm)
    y_p, p_conv = _ffn_prompt(h2, x1, w_up, w_dn, cw, cb, g_post2, batch, seq, tm)
    p_kv = [t.reshape(1, batch, t.shape[1], 2, ATT_HEADS, ATT_HEAD_DIM) for t in kv_last]

    ms = db * n_tok
    xs = x_sample.reshape(ms, D_MODEL)
    pos_s = jnp.tile(PAST_LEN + jnp.arange(n_tok), db)
    rq, rk, rv, aq, kv0, kv1, kv2 = _proj(xs, g_pre, w_qkv, _rot_tables(pos_s), ms)
    kvs = (kv0, kv1, kv2)
    on, s_ret = _ret_sample(rq, rk, rv, state_ret[0], n_tok)
    q_tiled = jnp.tile(aq.reshape(db, n_tok, ATT_W), (1, ATT_HEADS, 1))
    outs, lses, s_kv = [], [], []
    for g in range(N_GROUPS):
        cache = caches[g][0]
        wb = cache.shape[1]
        kv_new = jnp.pad(kvs[g].reshape(db, n_tok, 2 * ATT_OUT), ((0, 0), (0, SUBLANES - n_tok), (0, 0)))
        o_g, l_g, nc = _attn_sample(q_tiled, cache.reshape(db, wb, 2 * ATT_OUT), kv_new, g, n_tok)
        outs.append(o_g.reshape(ms, ATT_OUT // LANES, LANES).transpose(1, 0, 2))
        lses.append(l_g.reshape(ms, ATT_OUT // LANES, LANES).transpose(1, 0, 2))
        s_kv.append(nc.reshape(cache.shape)[None])
    x1, h2 = _merge(xs, on, outs, lses, g_pre, w_gates, w_br, w_ba, w_o, g_post, g_ffn, ms)
    st = state_conv[0]
    zeros = jnp.zeros((db, n_tok - 1, 2 * D_FF), F32)
    p1 = jnp.concatenate([st[:, 1:2], zeros], axis=1).reshape(ms, 2 * D_FF)
    p2 = jnp.concatenate([st, zeros[:, :n_tok - 2]], axis=1).reshape(ms, 2 * D_FF)
    y_s, u_s = _ffn_sample(h2, x1, p1, p2, w_up, w_dn, cw, cb, g_post2, n_tok)
    s_conv = u_s.reshape(db, n_tok, 2 * D_FF)[:, n_tok - (CONV_W - 1):][None]

    return (y_p.reshape(x_prompt.shape), y_s.reshape(x_sample.shape), p_kv[0], p_kv[1], p_kv[2], p_ret[None],
            p_conv[None], s_kv[0], s_kv[1], s_kv[2], s_ret[None], s_conv)
```

```python
import functools
import math

import jax
import jax.numpy as jnp
from jax import lax
from jax.experimental import pallas as pl
from jax.experimental.pallas import tpu as pltpu

F32 = jnp.float32
BF16 = jnp.bfloat16

D_MODEL = 1024
PAST_LEN = 16384
RET_HEADS = 4
RET_QK_DIM = 128
RET_V_DIM = 256
RET_CHUNK = 128
RET_THETA = 10000.0
RET_QK = RET_HEADS * RET_QK_DIM
RET_V = RET_HEADS * RET_V_DIM
ATT_GROUPS = ((128, 1), (512, 4), (2048, 16))
N_GROUPS = 3
ATT_HEADS = 4
ATT_HEAD_DIM = 64
ROPE_DIM = ATT_HEAD_DIM // 4
ROPE_THETA = 500000.0
BAND_BLOCK = 128
ATT_OUT = ATT_HEADS * ATT_HEAD_DIM
ATT_W = N_GROUPS * ATT_OUT
D_FF = 2816
CONV_W = 3
EPS = 1e-6

LANES = 128
SUBLANES = 8
MXU_N = 256
VMEM_LIMIT = 56 * 1024 * 1024
NEG = -1e30
FF_CHUNK = MXU_N
ATTN_UNROLL = 3
QKV_COLS = 2 * RET_QK + RET_V + 3 * ATT_W


def _rms(x, g):
    return x * lax.rsqrt(jnp.mean(x * x, axis=-1, keepdims=True) + EPS) * g


def _dot(a, b):
    return jnp.dot(a, b, preferred_element_type=F32)


def _dot_nt(a, b):
    return lax.dot_general(a, b, (((1,), (1,)), ((), ())), preferred_element_type=F32)


def _resident(shape):
    return pl.BlockSpec(shape, lambda *_: (0,) * len(shape), pipeline_mode=pl.Buffered(1))


def _log2(n):
    assert n > 0 and n & (n - 1) == 0, n
    return n.bit_length() - 1


def _params(n_axes=1):
    return pltpu.CompilerParams(dimension_semantics=("arbitrary",) * n_axes, vmem_limit_bytes=VMEM_LIMIT)


def _proj_kernel(x_ref, g_ref, w_ref, crq_ref, srq_ref, crk_ref, srk_ref, ca_ref, sa1_ref, sa2_ref, *rest,
                 tm, by_residue):
    rq_ref, rk_ref, rv_ref = rest[:3]
    if by_residue:
        res_refs = rest[3:3 + 3 * N_GROUPS]
        kv_refs = rest[3 + 3 * N_GROUPS:3 + 4 * N_GROUPS]
        scr = rest[3 + 4 * N_GROUPS]
    else:
        aq_ref = rest[3]
        kv_refs = rest[4:4 + N_GROUPS]
    h = _rms(x_ref[...], g_ref[...]).astype(BF16)

    def mm(c0, width=MXU_N):
        return _dot(h, w_ref[:, c0:c0 + width])

    def rot_ret(t, c, s):
        return t * c + pltpu.roll(t, RET_QK_DIM // 2, 1) * s

    ca, sa1, sa2 = ca_ref[...], sa1_ref[...], sa2_ref[...]

    def rot_att(t):
        return t * ca + pltpu.roll(t, ROPE_DIM // 2, 1) * sa1 + pltpu.roll(t, MXU_N - ROPE_DIM // 2, 1) * sa2

    def put_by_residue(ref, val, dil):
        if dil == 1:
            ref[0, 0] = val.astype(BF16)
            return
        for half in range(MXU_N // LANES):
            lanes = slice(half * LANES, (half + 1) * LANES)
            scr[half] = val[:, lanes]
            for r in range(dil):
                ref[0, r, :, lanes] = scr[half, pl.ds(r, tm // dil, stride=dil), :].astype(BF16)

    for c in range(RET_QK // MXU_N):
        q2 = mm(c * MXU_N)
        k2 = mm(RET_QK + c * MXU_N)
        for half in range(MXU_N // LANES):
            lo = half * LANES
            col = c * MXU_N + lo
            rq_ref[:, col:col + LANES] = rot_ret(q2[:, lo:lo + LANES], crq_ref[...], srq_ref[...]).astype(BF16)
            rk_ref[:, col:col + LANES] = rot_ret(k2[:, lo:lo + LANES], crk_ref[...], srk_ref[...]).astype(BF16)
    for c in range(RET_V // MXU_N):
        rv_ref[:, c * MXU_N:(c + 1) * MXU_N] = mm(2 * RET_QK + c * MXU_N).astype(BF16)
    base = 2 * RET_QK + RET_V
    for g in range(N_GROUPS):
        q2 = rot_att(mm(base + g * ATT_OUT))
        k2 = rot_att(mm(base + ATT_W + g * ATT_OUT))
        v2 = mm(base + 2 * ATT_W + g * ATT_OUT)
        if by_residue:
            dil = ATT_GROUPS[g][1]
            put_by_residue(res_refs[3 * g], q2, dil)
            put_by_residue(res_refs[3 * g + 1], k2, dil)
            put_by_residue(res_refs[3 * g + 2], v2, dil)
            rows = kv_refs[g].shape[1]
            kv_refs[g][0, :, 0:ATT_OUT] = k2[tm - rows:tm]
            kv_refs[g][0, :, ATT_OUT:2 * ATT_OUT] = v2[tm - rows:tm]
        else:
            aq_ref[:, g * ATT_OUT:(g + 1) * ATT_OUT] = q2.astype(BF16)
            kv_refs[g][:, 0:ATT_OUT] = k2
            kv_refs[g][:, ATT_OUT:2 * ATT_OUT] = v2


def _proj(x2d, g_pre, w_qkv, tabs, tm, seq=None):
    m = x2d.shape[0]
    n_tab = tabs[0].shape[0] // tm
    row = lambda i: (i, 0)
    tab = lambda i: (i % n_tab, 0)
    in_specs = [pl.BlockSpec((tm, D_MODEL), row), _resident((1, D_MODEL)), _resident((D_MODEL, QKV_COLS))]
    in_specs += [pl.BlockSpec((tm, LANES), tab)] * 4 + [pl.BlockSpec((tm, MXU_N), tab)] * 3
    out_specs = [pl.BlockSpec((tm, w), row) for w in (RET_QK, RET_QK, RET_V)]
    out_shape = [jax.ShapeDtypeStruct((m, w), BF16) for w in (RET_QK, RET_QK, RET_V)]
    scratch = []
    if seq is None:
        widths = (ATT_W,) + (2 * ATT_OUT,) * N_GROUPS
        out_specs += [pl.BlockSpec((tm, w), row) for w in widths]
        out_shape += [jax.ShapeDtypeStruct((m, w), d) for w, d in zip(widths, (BF16,) + (F32,) * N_GROUPS)]
    else:
        batch, tps = m // seq, seq // tm
        for _, dil in ATT_GROUPS:
            assert tm % (dil * 2 * SUBLANES) == 0
            out_specs += [pl.BlockSpec((1, dil, tm // dil, ATT_OUT), lambda i: (i // tps, 0, i % tps, 0))] * 3
            out_shape += [jax.ShapeDtypeStruct((batch, dil, seq // dil, ATT_OUT), BF16)] * 3
        for win, _ in ATT_GROUPS:
            win = min(win, seq)
            if win == seq:
                out_specs.append(pl.BlockSpec((1, tm, 2 * ATT_OUT), lambda i: (i // tps, i % tps, 0)))
            else:
                assert win <= tm
                out_specs.append(pl.BlockSpec((1, win, 2 * ATT_OUT), lambda i: (i // tps, 0, 0)))
            out_shape.append(jax.ShapeDtypeStruct((batch, win, 2 * ATT_OUT), F32))
        scratch = [pltpu.VMEM((MXU_N // LANES, tm, LANES), F32)]
    return pl.pallas_call(
        functools.partial(_proj_kernel, tm=tm, by_residue=seq is not None),
        grid=(m // tm,),
        in_specs=in_specs,
        out_specs=out_specs,
        out_shape=out_shape,
        scratch_shapes=scratch,
        compiler_params=_params(),
        name="proj",
    )(x2d, g_pre, w_qkv, *tabs)


def _ret_kernel(q_ref, k_ref, v_ref, dec_ref, qd_ref, kd_ref, gc_ref, o_ref, st_ref, s_scr, *, n_chunks):
    s_scr[...] = jnp.zeros_like(s_scr)

    def body(c, carry):
        r0 = pl.multiple_of(c * RET_CHUNK, RET_CHUNK)
        rows = pl.ds(r0, RET_CHUNK)
        for h in range(RET_HEADS):
            qk = slice(h * RET_QK_DIM, (h + 1) * RET_QK_DIM)
            vv = slice(h * RET_V_DIM, (h + 1) * RET_V_DIM)
            q = q_ref[rows, qk]
            k = k_ref[rows, qk]
            v = v_ref[rows, vv]
            s0 = s_scr[h]
            sc = _dot_nt(q, k) * dec_ref[h]
            o = _dot(sc.astype(BF16), v) + _dot(q, s0.astype(BF16)) * qd_ref[h]
            kd_t = (k.astype(F32) * kd_ref[h]).T.astype(BF16)
            s_scr[h] = s0 * gc_ref[h] + _dot(kd_t, v)
            o_ref[rows, vv] = o * lax.rsqrt(jnp.mean(o * o, axis=-1, keepdims=True) + EPS)
        return carry

    lax.fori_loop(0, n_chunks, body, 0)
    st_ref[0] = s_scr[...]


def _ret_tables(chunk):
    lg = jnp.log1p(-jnp.exp2(-5.0 - jnp.arange(RET_HEADS, dtype=F32)))
    idx = jnp.arange(RET_CHUNK, dtype=F32)
    diff = idx[:, None] - idx[None, :]
    dec = jnp.where(diff[None] >= 0, jnp.exp(lg[:, None, None] * jnp.maximum(diff, 0.0)[None]), 0.0)
    qd = jnp.exp(lg[:, None] * (idx[None, :] + 1.0))
    kd = jnp.exp(lg[:, None] * (chunk - 1.0 - idx)[None, :])
    gc = jnp.exp(lg * chunk)
    qd = jnp.broadcast_to(qd[:, :, None], (RET_HEADS, RET_CHUNK, RET_V_DIM))
    kd = jnp.broadcast_to(kd[:, :, None], (RET_HEADS, RET_CHUNK, RET_QK_DIM))
    gc = jnp.broadcast_to(gc[:, None, None], (RET_HEADS, RET_QK_DIM, RET_V_DIM))
    return dec, qd, kd, gc


def _ret_prompt(rq, rk, rv, batch, seq):
    dec, qd, kd, gc = _ret_tables(RET_CHUNK)
    row = lambda b: (b, 0)
    return pl.pallas_call(
        functools.partial(_ret_kernel, n_chunks=seq // RET_CHUNK),
        grid=(batch,),
        in_specs=[pl.BlockSpec((seq, RET_QK), row), pl.BlockSpec((seq, RET_QK), row), pl.BlockSpec((seq, RET_V), row),
                  _resident(dec.shape), _resident(qd.shape), _resident(kd.shape), _resident(gc.shape)],
        out_specs=[pl.BlockSpec((seq, RET_V), row),
                   pl.BlockSpec((1, RET_HEADS, RET_QK_DIM, RET_V_DIM), lambda b: (b, 0, 0, 0))],
        out_shape=[jax.ShapeDtypeStruct((batch * seq, RET_V), F32),
                   jax.ShapeDtypeStruct((batch, RET_HEADS, RET_QK_DIM, RET_V_DIM), F32)],
        scratch_shapes=[pltpu.VMEM((RET_HEADS, RET_QK_DIM, RET_V_DIM), F32)],
        compiler_params=_params(),
        name="ret_prompt",
    )(rq, rk, rv, dec, qd, kd, gc)


def _ret_sample_kernel(q_ref, k_ref, kt_ref, v_ref, s_ref, dec_ref, qd_ref, kd_ref, gc_ref, o_ref, st_ref,
                       *, n_tok, n_b):
    for b in range(n_b):
        for h in range(RET_HEADS):
            qk = slice(h * RET_QK_DIM, (h + 1) * RET_QK_DIM)
            vv = slice(h * RET_V_DIM, (h + 1) * RET_V_DIM)
            q = q_ref[b, :, qk]
            k = k_ref[b, :, qk]
            v = v_ref[b, :, vv]
            kt = kt_ref[b, h]
            s0 = s_ref[b, h]
            o = _dot(q.astype(BF16), s0.astype(BF16)) * qd_ref[h]
            s1 = s0 * gc_ref[h]
            for i in range(n_tok):
                sc_i = jnp.sum(q * k[i:i + 1, :], axis=-1, keepdims=True) * dec_ref[h][:, i:i + 1]
                o = o + sc_i * v[i:i + 1, :]
                s1 = s1 + (kt[:, i:i + 1] * kd_ref[h][:, i:i + 1]) * v[i:i + 1, :]
            st_ref[b, h] = s1
            o_ref[b, :, vv] = o * lax.rsqrt(jnp.mean(o * o, axis=-1, keepdims=True) + EPS)


def _ret_sample(rq, rk, rv, state, n_tok):
    db = state.shape[0]
    pad = SUBLANES - n_tok
    n_b = 8

    def pad_rows(t):
        t = t.astype(F32).reshape(db, n_tok, t.shape[-1])
        return jnp.pad(t, ((0, 0), (0, pad), (0, 0)))

    q, k, v = pad_rows(rq), pad_rows(rk), pad_rows(rv)
    kt = k.reshape(db, SUBLANES, RET_HEADS, RET_QK_DIM).transpose(0, 2, 3, 1)
    lg = jnp.log1p(-jnp.exp2(-5.0 - jnp.arange(RET_HEADS, dtype=F32)))
    idx = jnp.arange(SUBLANES, dtype=F32)
    diff = idx[:, None] - idx[None, :]
    dec = jnp.where(diff[None] >= 0, jnp.exp(lg[:, None, None] * jnp.maximum(diff, 0.0)[None]), 0.0)
    qd = jnp.broadcast_to(jnp.exp(lg[:, None] * (idx[None, :] + 1.0))[:, :, None], (RET_HEADS, SUBLANES, RET_V_DIM))
    kd = jnp.broadcast_to(jnp.exp(lg[:, None] * (n_tok - 1.0 - idx)[None, :])[:, None, :],
                          (RET_HEADS, RET_QK_DIM, SUBLANES))
    gc = jnp.broadcast_to(jnp.exp(lg * n_tok)[:, None, None], (RET_HEADS, RET_QK_DIM, RET_V_DIM))
    b3 = lambda i: (i, 0, 0)
    b4 = lambda i: (i, 0, 0, 0)
    o, st = pl.pallas_call(
        functools.partial(_ret_sample_kernel, n_tok=n_tok, n_b=n_b),
        grid=(db // n_b,),
        in_specs=[pl.BlockSpec((n_b, SUBLANES, RET_QK), b3), pl.BlockSpec((n_b, SUBLANES, RET_QK), b3),
                  pl.BlockSpec((n_b, RET_HEADS, RET_QK_DIM, SUBLANES), b4),
                  pl.BlockSpec((n_b, SUBLANES, RET_V), b3),
                  pl.BlockSpec((n_b, RET_HEADS, RET_QK_DIM, RET_V_DIM), b4),
                  _resident(dec.shape), _resident(qd.shape), _resident(kd.shape), _resident(gc.shape)],
        out_specs=[pl.BlockSpec((n_b, SUBLANES, RET_V), b3),
                   pl.BlockSpec((n_b, RET_HEADS, RET_QK_DIM, RET_V_DIM), b4)],
        out_shape=[jax.ShapeDtypeStruct((db, SUBLANES, RET_V), F32),
                   jax.ShapeDtypeStruct(state.shape, F32)],
        compiler_params=_params(),
        name="ret_sample",
    )(q, k, kt, v, state, dec, qd, kd, gc)
    return o[:, :n_tok].reshape(db * n_tok, RET_V), st


def _attn_kernel(q_ref, k_ref, v_ref, o_ref, l_ref, vt_scr, bias_scr, *staging, dil, nb):
    n_blk = dil * nb
    length = nb * BAND_BLOCK
    blk = BAND_BLOCK
    o_dst, l_dst = staging if dil > 1 else (o_ref, l_ref)

    kk = lax.broadcasted_iota(jnp.int32, (2 * blk, 2 * LANES), 0)
    ql = lax.broadcasted_iota(jnp.int32, (2 * blk, 2 * LANES), 1) & (blk - 1)
    in_span = kk <= ql + blk
    bias_scr[...] = jnp.where(in_span & (kk >= ql), 0.0, NEG)

    def vt_body(c, carry):
        r0 = pl.multiple_of(c * blk, blk)
        vt_scr[c] = v_ref[0, pl.ds(r0, blk), :].astype(F32).T.astype(BF16)
        return carry

    lax.fori_loop(0, n_blk, vt_body, 0)

    lane = lax.broadcasted_iota(jnp.int32, (blk, LANES), 1)

    def scores(c, p):
        lanes = slice(p * LANES, (p + 1) * LANES)
        k0 = 0 if c % nb == 0 else blk
        q128 = q_ref[0, c * blk:(c + 1) * blk, lanes]
        zero = jnp.zeros_like(q128)
        q_pair = jnp.concatenate([jnp.where(lane < ATT_HEAD_DIM, q128, zero),
                                  jnp.where(lane >= ATT_HEAD_DIM, q128, zero)], axis=0)
        keys = k_ref[0, c * blk - k0:(c + 1) * blk, lanes]
        return _dot_nt(keys, q_pair) + bias_scr[blk - k0:2 * blk, :]

    def weights_values(c, p, s_t):
        lanes = slice(p * LANES, (p + 1) * LANES)
        m = jnp.max(s_t, axis=0, keepdims=True)
        p_t = jnp.exp(s_t - m).astype(BF16)
        v_t = vt_scr[c, lanes, :]
        if c % nb:
            v_t = jnp.concatenate([vt_scr[c - 1, lanes, :], v_t], axis=1)
        v_t = jnp.concatenate([v_t, jnp.ones((2 * SUBLANES, v_t.shape[1]), BF16)], axis=0)
        return _dot(v_t, p_t), m

    def finish(c, p, o_t, m):
        den = o_t[LANES:LANES + 1, :]
        lse = m + jnp.log(den)
        o_pair = jnp.concatenate([o_t[0:ATT_HEAD_DIM, 0:blk] / den[:, 0:blk],
                                  o_t[ATT_HEAD_DIM:LANES, blk:2 * blk] / den[:, blk:2 * blk]], axis=0)
        l_pair = jnp.concatenate([jnp.broadcast_to(lse[:, 0:blk], (ATT_HEAD_DIM, blk)),
                                  jnp.broadcast_to(lse[:, blk:2 * blk], (ATT_HEAD_DIM, blk))], axis=0)
        o_dst[p, c * blk:(c + 1) * blk, :] = o_pair.T
        l_dst[p, c * blk:(c + 1) * blk, :] = l_pair.T

    chains = [(c, p) for c in range(n_blk) for p in range(ATT_OUT // LANES)]
    s_next = scores(*chains[0])
    pending = None
    for i, chain in enumerate(chains):
        s_cur = s_next
        if i + 1 < len(chains):
            s_next = scores(*chains[i + 1])
        o_m = weights_values(*chain, s_cur)
        if pending is not None:
            finish(*pending)
        pending = chain + o_m
    finish(*pending)

    if dil > 1:
        for p in range(ATT_OUT // LANES):
            for r in range(dil):
                o_ref[p, pl.ds(r, length, stride=dil), :] = o_dst[p, r * length:(r + 1) * length, :]
                l_ref[p, pl.ds(r, length, stride=dil), :] = l_dst[p, r * length:(r + 1) * length, :]


def _attn_prompt(q, k, v, g, batch, seq):
    _, dil = ATT_GROUPS[g]
    nb = seq // dil // BAND_BLOCK
    _log2(nb)
    q, k, v = (t.reshape(batch, seq, ATT_OUT) for t in (q, k, v))
    in_spec = pl.BlockSpec((1, seq, ATT_OUT), lambda b: (b, 0, 0))
    n_pair = ATT_OUT // LANES
    out_spec = pl.BlockSpec((n_pair, seq, LANES), lambda b: (0, b, 0))
    scratch = [pltpu.VMEM((dil * nb, ATT_OUT, BAND_BLOCK), BF16), pltpu.VMEM((2 * BAND_BLOCK, 2 * LANES), F32)]
    if dil > 1:
        scratch += [pltpu.VMEM((n_pair, seq, LANES), F32)] * 2
    return pl.pallas_call(
        functools.partial(_attn_kernel, dil=dil, nb=nb),
        grid=(batch,),
        in_specs=[in_spec] * 3,
        out_specs=[out_spec] * 2,
        out_shape=[jax.ShapeDtypeStruct((n_pair, batch * seq, LANES), F32)] * 2,
        scratch_shapes=scratch,
        compiler_params=_params(),
        name=f"attn_prompt_g{g}",
    )(q, k, v)


def _attn_sample_kernel(q_ref, c_ref, n_ref, o_ref, l_ref, nc_ref, *, g, n_tok, wb):
    win, dil = ATT_GROUPS[g]
    nk = win // dil
    _log2(dil)
    n_col = ATT_HEADS * n_tok
    nc_ref[0, 0:wb - n_tok, :] = c_ref[0, n_tok:wb, :]
    nc_ref[0, wb - n_tok:wb, :] = n_ref[0, 0:n_tok, :]

    q = q_ref[0][:, g * ATT_OUT:(g + 1) * ATT_OUT]
    row = lax.broadcasted_iota(jnp.int32, (n_col, ATT_OUT), 0)
    lane = lax.broadcasted_iota(jnp.int32, (n_col, ATT_OUT), 1)
    head_sel = (lane >> _log2(ATT_HEAD_DIM)) == (row >> _log2(n_tok))
    qbd = jnp.where(head_sel, q, jnp.zeros_like(q))

    kc = c_ref[0, :, 0:ATT_OUT].astype(BF16)
    vc = c_ref[0, :, ATT_OUT:2 * ATT_OUT].astype(BF16)
    pad = jnp.zeros((LANES - SUBLANES, ATT_OUT), F32)
    kn = jnp.concatenate([n_ref[0, :, 0:ATT_OUT], pad], axis=0).astype(BF16)
    vn = jnp.concatenate([n_ref[0, :, ATT_OUT:2 * ATT_OUT], pad], axis=0).astype(BF16)

    s_c = _dot_nt(qbd, kc)
    s_n = _dot_nt(qbd, kn)
    t_c = lax.broadcasted_iota(jnp.int32, (n_col, wb), 0) & (n_tok - 1)
    d_c = wb + t_c - lax.broadcasted_iota(jnp.int32, (n_col, wb), 1)
    ok_c = ((d_c & (dil - 1)) == 0) & (d_c <= nk * dil)
    t_n = lax.broadcasted_iota(jnp.int32, (n_col, LANES), 0) & (n_tok - 1)
    d_n = t_n - lax.broadcasted_iota(jnp.int32, (n_col, LANES), 1)
    ok_n = (d_n >= 0) & ((d_n & (dil - 1)) == 0) & (d_n <= nk * dil)
    s_c = jnp.where(ok_c, s_c, NEG)
    s_n = jnp.where(ok_n, s_n, NEG)
    m = jnp.maximum(jnp.max(s_c, axis=-1, keepdims=True), jnp.max(s_n, axis=-1, keepdims=True))
    p_c = jnp.exp(s_c - m)
    p_n = jnp.exp(s_n - m)
    den = jnp.sum(p_c, axis=-1, keepdims=True) + jnp.sum(p_n, axis=-1, keepdims=True)
    o = (_dot(p_c.astype(BF16), vc) + _dot(p_n.astype(BF16), vn)) / den
    lse = jnp.broadcast_to(m + jnp.log(den), (n_col, ATT_OUT))
    o = jnp.where(head_sel, o, 0.0)
    lse = jnp.where(head_sel, lse, 0.0)
    o_t = o[0:n_tok]
    l_t = lse[0:n_tok]
    for h in range(1, ATT_HEADS):
        o_t = o_t + o[h * n_tok:(h + 1) * n_tok]
        l_t = l_t + lse[h * n_tok:(h + 1) * n_tok]
    o_ref[0] = o_t
    l_ref[0] = l_t


def _attn_sample(q_tiled, cache, kv_new, g, n_tok):
    db, wb, _ = cache.shape
    b3 = lambda i: (i, 0, 0)
    return pl.pallas_call(
        functools.partial(_attn_sample_kernel, g=g, n_tok=n_tok, wb=wb),
        grid=(db,),
        in_specs=[pl.BlockSpec((1, ATT_HEADS * n_tok, ATT_W), b3),
                  pl.BlockSpec((1, wb, 2 * ATT_OUT), b3),
                  pl.BlockSpec((1, SUBLANES, 2 * ATT_OUT), b3)],
        out_specs=[pl.BlockSpec((1, n_tok, ATT_OUT), b3), pl.BlockSpec((1, n_tok, ATT_OUT), b3),
                   pl.BlockSpec((1, wb, 2 * ATT_OUT), b3)],
        out_shape=[jax.ShapeDtypeStruct((db, n_tok, ATT_OUT), F32), jax.ShapeDtypeStruct((db, n_tok, ATT_OUT), F32),
                   jax.ShapeDtypeStruct(cache.shape, F32)],
        compiler_params=_params(),
        name=f"attn_sample_g{g}",
    )(q_tiled, cache, kv_new)


def _merge_kernel(x_ref, on_ref, o0_ref, o1_ref, o2_ref, l0_ref, l1_ref, l2_ref, gpre_ref, wg_ref, wbr_ref,
                  wba_ref, wout_ref, gpost_ref, gffn_ref, x1_ref, h2_ref, *perm, tm, n_split, permute):
    rows_of = [slice(i * tm // n_split, (i + 1) * tm // n_split) for i in range(n_split)]
    n_pair = ATT_OUT // LANES

    def gates(rows):
        x = x_ref[rows, :]
        h = _rms(x, gpre_ref[...]).astype(BF16)
        return [_dot(h, wg_ref[:, c:c + D_MODEL]) for c in (0, RET_V, RET_V + D_MODEL)]

    def branches(rows, rg):
        r = (on_ref[rows, :] * (rg * jax.nn.sigmoid(rg))).astype(BF16)
        atts = []
        for p in range(n_pair):
            l0, l1, l2 = l0_ref[p, rows, :], l1_ref[p, rows, :], l2_ref[p, rows, :]
            mx = jnp.maximum(jnp.maximum(l0, l1), l2)
            e0, e1, e2 = jnp.exp(l0 - mx), jnp.exp(l1 - mx), jnp.exp(l2 - mx)
            att = (e0 * o0_ref[p, rows, :] + e1 * o1_ref[p, rows, :] + e2 * o2_ref[p, rows, :]) / (e0 + e1 + e2)
            atts.append(att.astype(BF16))
        return _dot(r, wbr_ref[...]), _dot(jnp.concatenate(atts, axis=1), wba_ref[...])

    def mixed(gr, ga, br, ba):
        mix = jax.nn.sigmoid(gr) * br + jax.nn.sigmoid(ga) * ba
        return _dot(mix.astype(BF16), wout_ref[...])

    def finish(rows, mo):
        x1 = x_ref[rows, :] + _rms(mo, gpost_ref[...])
        x1_ref[rows, :] = x1
        return _rms(x1, gffn_ref[...])

    g = [gates(rows) for rows in rows_of]
    b = [branches(rows, gi[0]) for rows, gi in zip(rows_of, g)]
    mo = [mixed(gi[1], gi[2], *bi) for gi, bi in zip(g, b)]
    h2 = [finish(rows, moi) for rows, moi in zip(rows_of, mo)]
    if not permute:
        for rows, h2i in zip(rows_of, h2):
            h2_ref[rows, :] = h2i.astype(BF16)
        return
    perm_ref, = perm
    nv = tm // SUBLANES
    assert nv % (tm // n_split) == 0 or (tm // n_split) % nv == 0
    for lc in range(D_MODEL // LANES):
        lanes = slice(lc * LANES, (lc + 1) * LANES)
        for s in range(SUBLANES):
            tok0 = s * nv
            i, off = divmod(tok0, tm // n_split)
            perm_ref[lc, pl.ds(s, nv, stride=SUBLANES), :] = h2[i][off:off + nv, lanes]
        h2_ref[:, lanes] = perm_ref[lc].astype(BF16)


def _merge(x2d, on, outs, lses, g_pre, w_gates, w_br, w_ba, w_out, g_post, g_ffn, tm, permute):
    m = x2d.shape[0]
    n_split = 2 if tm >= 2 * MXU_N else 1
    scratch = [pltpu.VMEM((D_MODEL // LANES, tm, LANES), F32)] if permute else []
    row = lambda i: (i, 0)
    vec = _resident((1, D_MODEL))
    att_spec = pl.BlockSpec((ATT_OUT // LANES, tm, LANES), lambda i: (0, i, 0))
    return pl.pallas_call(
        functools.partial(_merge_kernel, tm=tm, n_split=n_split, permute=permute),
        grid=(m // tm,),
        in_specs=[pl.BlockSpec((tm, D_MODEL), row), pl.BlockSpec((tm, RET_V), row)] + [att_spec] * 6 + [
            vec, _resident(w_gates.shape), _resident(w_br.shape), _resident(w_ba.shape), _resident(w_out.shape),
            vec, vec],
        out_specs=[pl.BlockSpec((tm, D_MODEL), row), pl.BlockSpec((tm, D_MODEL), row)],
        out_shape=[jax.ShapeDtypeStruct((m, D_MODEL), F32), jax.ShapeDtypeStruct((m, D_MODEL), BF16)],
        scratch_shapes=scratch,
        compiler_params=_params(),
        name="merge",
    )(x2d, on, *outs, *lses, g_pre, w_gates, w_br, w_ba, w_out, g_post, g_ffn)


def _ffn_kernel(*refs, tm, tiles_per_seq, seq_tok):
    if seq_tok is None:
        h2_ref, x1_ref, wup_ref, wdn_ref, cw_ref, cb_ref, g_ref, y_ref, cs_ref, carry_ref, f_ref, perm_ref = refs
    else:
        h2_ref, x1_ref, p1_ref, p2_ref, wup_ref, wdn_ref, cw_ref, cb_ref, g_ref, y_ref, u_ref, f_ref = refs
    h2 = h2_ref[...]
    if seq_tok is None:
        sub = lax.broadcasted_iota(jnp.int32, (SUBLANES, FF_CHUNK), 0)

        @pl.when(pl.program_id(0) % tiles_per_seq == 0)
        def _():
            carry_ref[...] = jnp.zeros_like(carry_ref)
    else:
        t = lax.broadcasted_iota(jnp.int32, (tm, FF_CHUNK), 0) & (seq_tok - 1)
        _log2(seq_tok)

    def up(j):
        return [_dot(h2, wup_ref[:, c:c + FF_CHUNK]) for c in (j * FF_CHUNK, D_FF + j * FF_CHUNK)]

    def conv(u, col):
        cols = slice(col, col + FF_CHUNK)
        if seq_tok is None:
            prev = carry_ref[:, cols]
            last2 = jnp.where(sub == 0, prev[SUBLANES - 1:SUBLANES], pltpu.roll(u[tm - 2 * SUBLANES:tm - SUBLANES], 1, 0))
            last1 = jnp.where(sub == 0, prev[2 * SUBLANES - 1:2 * SUBLANES], pltpu.roll(u[tm - SUBLANES:tm], 1, 0))
            u1 = jnp.concatenate([last1, u[0:tm - SUBLANES]], axis=0)
            u2 = jnp.concatenate([last2, last1, u[0:tm - 2 * SUBLANES]], axis=0)
            carry_ref[:, cols] = u[tm - 2 * SUBLANES:tm]
        else:
            u1 = jnp.where(t == 0, p1_ref[:, cols], pltpu.roll(u, 1, 0))
            u2 = jnp.where(t <= 1, p2_ref[:, cols], pltpu.roll(u, 2, 0))
            u_ref[:, cols] = u
        cw = cw_ref[:, cols]
        return cb_ref[:, cols] + cw[0:1] * u2 + cw[1:2] * u1 + cw[2:3] * u

    def gate(j, ug, uv):
        cg = conv(ug, j * FF_CHUNK)
        cv = conv(uv, D_FF + j * FF_CHUNK)
        f_ref[:, j * FF_CHUNK:(j + 1) * FF_CHUNK] = (jax.nn.gelu(cg, approximate=True) * cv).astype(BF16)

    n_chunks = D_FF // FF_CHUNK
    u_next = up(0)
    for j in range(n_chunks):
        u_cur = u_next
        if j + 1 < n_chunks:
            u_next = up(j + 1)
        gate(j, *u_cur)
    out = _rms(_dot(f_ref[...], wdn_ref[...]), g_ref[...])
    if seq_tok is None:
        nv = tm // SUBLANES
        for lc in range(D_MODEL // LANES):
            lanes = slice(lc * LANES, (lc + 1) * LANES)
            perm_ref[lc] = out[:, lanes]
            for s in range(SUBLANES):
                rows = slice(s * nv, (s + 1) * nv)
                y_ref[rows, lanes] = x1_ref[rows, lanes] + perm_ref[lc, pl.ds(s, nv, stride=SUBLANES), :]

        @pl.when(pl.program_id(0) % tiles_per_seq == tiles_per_seq - 1)
        def _():
            cs_ref[0, 0:1, :] = carry_ref[SUBLANES - 1:SUBLANES, :]
            cs_ref[0, 1:2, :] = carry_ref[2 * SUBLANES - 1:2 * SUBLANES, :]
    else:
        y_ref[...] = x1_ref[...] + out


def _ffn_prompt(h2, x1, w_up, w_dn, conv_w, conv_b, g_post, batch, seq, tm):
    m = h2.shape[0]
    tps = seq // tm
    row = lambda i: (i, 0)
    return pl.pallas_call(
        functools.partial(_ffn_kernel, tm=tm, tiles_per_seq=tps, seq_tok=None),
        grid=(m // tm,),
        in_specs=[pl.BlockSpec((tm, D_MODEL), row), pl.BlockSpec((tm, D_MODEL), row),
                  _resident(w_up.shape), _resident(w_dn.shape), _resident(conv_w.shape), _resident(conv_b.shape),
                  _resident((1, D_MODEL))],
        out_specs=[pl.BlockSpec((tm, D_MODEL), row),
                   pl.BlockSpec((1, CONV_W - 1, 2 * D_FF), lambda i: (i // tps, 0, 0))],
        out_shape=[jax.ShapeDtypeStruct((m, D_MODEL), F32),
                   jax.ShapeDtypeStruct((batch, CONV_W - 1, 2 * D_FF), F32)],
        scratch_shapes=[pltpu.VMEM((2 * SUBLANES, 2 * D_FF), F32), pltpu.VMEM((tm, D_FF), BF16),
                        pltpu.VMEM((D_MODEL // LANES, tm, LANES), F32)],
        compiler_params=_params(),
        name="ffn_prompt",
    )(h2, x1, w_up, w_dn, conv_w, conv_b, g_post)


def _ffn_sample(h2, x1, p1, p2, w_up, w_dn, conv_w, conv_b, g_post, n_tok):
    m = h2.shape[0]
    const = lambda i: (0, 0)
    full = lambda a: _resident(a.shape)
    return pl.pallas_call(
        functools.partial(_ffn_kernel, tm=m, tiles_per_seq=1, seq_tok=n_tok),
        grid=(1,),
        in_specs=[full(h2), full(x1), full(p1), full(p2), full(w_up), full(w_dn), full(conv_w), full(conv_b),
                  _resident((1, D_MODEL))],
        out_specs=[pl.BlockSpec((m, D_MODEL), const), pl.BlockSpec((m, 2 * D_FF), const)],
        out_shape=[jax.ShapeDtypeStruct((m, D_MODEL), F32), jax.ShapeDtypeStruct((m, 2 * D_FF), F32)],
        scratch_shapes=[pltpu.VMEM((m, D_FF), BF16)],
        compiler_params=_params(),
        name="ffn_sample",
    )(h2, x1, p1, p2, w_up, w_dn, conv_w, conv_b, g_post)


def _rot_tables(pos):
    pos = pos.astype(F32)
    rf = RET_THETA ** (-jnp.linspace(0.0, 1.0, RET_QK_DIM // 2, dtype=F32))
    ang = pos[:, None] * rf[None, :]
    cos, sin = jnp.cos(ang), jnp.sin(ang)
    cr = jnp.concatenate([cos, cos], axis=-1)
    sr = jnp.concatenate([-sin, sin], axis=-1)
    k_scale = RET_QK_DIM ** -0.5
    half = ROPE_DIM // 2
    af = ROPE_THETA ** (-jnp.arange(half, dtype=F32) / half)
    ang = pos[:, None] * af[None, :]
    cos, sin = jnp.cos(ang), jnp.sin(ang)
    n = pos.shape[0]
    rest = ATT_HEAD_DIM - ROPE_DIM
    zh = jnp.zeros((n, half), F32)
    ca = jnp.concatenate([cos, cos, jnp.ones((n, rest), F32)], axis=-1)
    sa1 = jnp.concatenate([zh, sin, jnp.zeros((n, rest), F32)], axis=-1)
    sa2 = jnp.concatenate([-sin, zh, jnp.zeros((n, rest), F32)], axis=-1)
    rep = MXU_N // ATT_HEAD_DIM
    ca, sa1, sa2 = (jnp.tile(t, (1, rep)) for t in (ca, sa1, sa2))
    return [cr, sr, cr * k_scale, sr * k_scale, ca, sa1, sa2]


def kernel(x_prompt, x_sample, cache_kv_g0, cache_kv_g1, cache_kv_g2, state_ret, state_conv, norm_mix_pre, w_in,
           w_branch_ret, w_branch_attn, w_out, norm_mix_post, norm_ffn_pre, w_ffn_up, conv_w, conv_b, w_ffn_down,
           norm_ffn_post):
    batch, seq, _ = x_prompt.shape
    db, n_tok, _ = x_sample.shape
    depth = w_in.shape[0]
    assert depth == 1 and seq % (ATT_GROUPS[-1][1] * BAND_BLOCK) == 0 and n_tok <= SUBLANES
    caches = (cache_kv_g0, cache_kv_g1, cache_kv_g2)

    wi = w_in[0]
    o_rg = 2 * RET_QK + RET_V
    o_aq = o_rg + RET_V
    o_gr = o_aq + 3 * ATT_W
    w_qkv = jnp.concatenate([wi[:, :o_rg], wi[:, o_aq:o_aq + ATT_W] * ATT_HEAD_DIM ** -0.5,
                             wi[:, o_aq + ATT_W:o_gr]], axis=1).astype(BF16)
    w_gates = jnp.concatenate([wi[:, o_rg:o_aq], wi[:, o_gr:]], axis=1).astype(BF16)
    w_br, w_ba, w_o = w_branch_ret[0].astype(BF16), w_branch_attn[0].astype(BF16), w_out[0].astype(BF16)
    w_up, w_dn = w_ffn_up[0].astype(BF16), w_ffn_down[0].astype(BF16)
    cb = conv_b[0][None, :]
    cw = conv_w[0]
    g_pre, g_post, g_ffn, g_post2 = (t[0][None, :] for t in (norm_mix_pre, norm_mix_post, norm_ffn_pre, norm_ffn_post))

    tm = 512
    xp = x_prompt.reshape(batch * seq, D_MODEL)
    proj_out = _proj(xp, g_pre, w_qkv, _rot_tables(jnp.arange(seq)), tm, seq=seq)
    rq, rk, rv = proj_out[:3]
    qkv_res = proj_out[3:3 + 3 * N_GROUPS]
    kv_last = proj_out[3 + 3 * N_GROUPS:]
    on, p_ret = _ret_prompt(rq, rk, rv, batch, seq)
    outs, lses = [], []
    for g in range(N_GROUPS):
        o_g, l_g = _attn_prompt(*qkv_res[3 * g:3 * g + 3], g, batch, seq)
        outs.append(o_g)
        lses.append(l_g)
    x1, h2 = _merge(xp, on, outs, lses, g_pre, w_gates, w_br, w_ba, w_o, g_post, g_ffn, tm, permute=True)
    y_p, p_conv = _ffn_prompt(h2, x1, w_up, w_dn, cw, cb, g_post2, batch, seq, tm)
    p_kv = [t.reshape(1, batch, t.shape[1], 2, ATT_HEADS, ATT_HEAD_DIM) for t in kv_last]

    ms = db * n_tok
    xs = x_sample.reshape(ms, D_MODEL)
    pos_s = jnp.tile(PAST_LEN + jnp.arange(n_tok), db)
    rq, rk, rv, aq, kv0, kv1, kv2 = _proj(xs, g_pre, w_qkv, _rot_tables(pos_s), ms)
    kvs = (kv0, kv1, kv2)
    on, s_ret = _ret_sample(rq, rk, rv, state_ret[0], n_tok)
    q_tiled = jnp.tile(aq.reshape(db, n_tok, ATT_W), (1, ATT_HEADS, 1))
    outs, lses, s_kv = [], [], []
    for g in range(N_GROUPS):
        cache = caches[g][0]
        wb = cache.shape[1]
        kv_new = jnp.pad(kvs[g].reshape(db, n_tok, 2 * ATT_OUT), ((0, 0), (0, SUBLANES - n_tok), (0, 0)))
        o_g, l_g, nc = _attn_sample(q_tiled, cache.reshape(db, wb, 2 * ATT_OUT), kv_new, g, n_tok)
        outs.append(o_g.reshape(ms, ATT_OUT // LANES, LANES).transpose(1, 0, 2))
        lses.append(l_g.reshape(ms, ATT_OUT // LANES, LANES).transpose(1, 0, 2))
        s_kv.append(nc.reshape(cache.shape)[None])
    x1, h2 = _merge(xs, on, outs, lses, g_pre, w_gates, w_br, w_ba, w_o, g_post, g_ffn, ms, permute=False)
    st = state_conv[0]
    zeros = jnp.zeros((db, n_tok - 1, 2 * D_FF), F32)
    p1 = jnp.concatenate([st[:, 1:2], zeros], axis=1).reshape(ms, 2 * D_FF)
    p2 = jnp.concatenate([st, zeros[:, :n_tok - 2]], axis=1).reshape(ms, 2 * D_FF)
    y_s, u_s = _ffn_sample(h2, x1, p1, p2, w_up, w_dn, cw, cb, g_post2, n_tok)
    s_conv = u_s.reshape(db, n_tok, 2 * D_FF)[:, n_tok - (CONV_W - 1):][None]

    return (y_p.reshape(x_prompt.shape), y_s.reshape(x_sample.shape), p_kv[0], p_kv[1], p_kv[2], p_ret[None],
            p_conv[None], s_kv[0], s_kv[1], s_kv[2], s_ret[None], s_conv)
```

```python
import functools
import math

import jax
import jax.numpy as jnp
from jax import lax
from jax.experimental import pallas as pl
from jax.experimental.pallas import tpu as pltpu

F32 = jnp.float32
BF16 = jnp.bfloat16

D_MODEL = 1024
PAST_LEN = 16384
RET_HEADS = 4
RET_QK_DIM = 128
RET_V_DIM = 256
RET_CHUNK = 128
RET_THETA = 10000.0
RET_QK = RET_HEADS * RET_QK_DIM
RET_V = RET_HEADS * RET_V_DIM
ATT_GROUPS = ((128, 1), (512, 4), (2048, 16))
N_GROUPS = 3
ATT_HEADS = 4
ATT_HEAD_DIM = 64
ROPE_DIM = ATT_HEAD_DIM // 4
ROPE_THETA = 500000.0
BAND_BLOCK = 128
ATT_OUT = ATT_HEADS * ATT_HEAD_DIM
ATT_W = N_GROUPS * ATT_OUT
D_FF = 2816
CONV_W = 3
EPS = 1e-6

LANES = 128
SUBLANES = 8
MXU_N = 256
VMEM_LIMIT = 56 * 1024 * 1024
NEG = -1e30
FF_CHUNK = MXU_N
ATTN_UNROLL = 3
QKV_COLS = 2 * RET_QK + RET_V + 3 * ATT_W


def _rms(x, g):
    return x * lax.rsqrt(jnp.mean(x * x, axis=-1, keepdims=True) + EPS) * g


def _dot(a, b):
    return jnp.dot(a, b, preferred_element_type=F32)


def _dot_nt(a, b):
    return lax.dot_general(a, b, (((1,), (1,)), ((), ())), preferred_element_type=F32)


def _resident(shape):
    return pl.BlockSpec(shape, lambda *_: (0,) * len(shape), pipeline_mode=pl.Buffered(1))


def _log2(n):
    assert n > 0 and n & (n - 1) == 0, n
    return n.bit_length() - 1


def _params(n_axes=1):
    return pltpu.CompilerParams(dimension_semantics=("arbitrary",) * n_axes, vmem_limit_bytes=VMEM_LIMIT)


def _proj_kernel(x_ref, g_ref, w_ref, crq_ref, srq_ref, crk_ref, srk_ref, ca_ref, sa1_ref, sa2_ref, *rest,
                 tm, by_residue, tiles_per_seq):
    rq_ref, rk_ref, rv_ref = rest[:3]
    if by_residue:
        res_refs = rest[3:3 + 3 * N_GROUPS]
        kv_refs = rest[3 + 3 * N_GROUPS:3 + 4 * N_GROUPS]
        scr = rest[3 + 4 * N_GROUPS]
    else:
        aq_ref = rest[3]
        kv_refs = rest[4:4 + N_GROUPS]
    h = _rms(x_ref[...], g_ref[...]).astype(BF16)

    def mm(c0, width=MXU_N):
        return _dot(h, w_ref[:, c0:c0 + width])

    def rot_ret(t, c, s):
        return t * c + pltpu.roll(t, RET_QK_DIM // 2, 1) * s

    ca, sa1, sa2 = ca_ref[...], sa1_ref[...], sa2_ref[...]

    def rot_att(t):
        return t * ca + pltpu.roll(t, ROPE_DIM // 2, 1) * sa1 + pltpu.roll(t, MXU_N - ROPE_DIM // 2, 1) * sa2

    def put_by_residue(ref, val, dil):
        if dil == 1:
            ref[0, 0] = val.astype(BF16)
            return
        for half in range(MXU_N // LANES):
            lanes = slice(half * LANES, (half + 1) * LANES)
            scr[half] = val[:, lanes]
            for r in range(dil):
                ref[0, r, :, lanes] = scr[half, pl.ds(r, tm // dil, stride=dil), :].astype(BF16)

    for c in range(RET_QK // MXU_N):
        q2 = mm(c * MXU_N)
        k2 = mm(RET_QK + c * MXU_N)
        for half in range(MXU_N // LANES):
            lo = half * LANES
            col = c * MXU_N + lo
            rq_ref[:, col:col + LANES] = rot_ret(q2[:, lo:lo + LANES], crq_ref[...], srq_ref[...]).astype(BF16)
            rk_ref[:, col:col + LANES] = rot_ret(k2[:, lo:lo + LANES], crk_ref[...], srk_ref[...]).astype(BF16)
    for c in range(RET_V // MXU_N):
        rv_ref[:, c * MXU_N:(c + 1) * MXU_N] = mm(2 * RET_QK + c * MXU_N).astype(BF16)
    base = 2 * RET_QK + RET_V
    for g in range(N_GROUPS):
        q2 = rot_att(mm(base + g * ATT_OUT))
        k2 = rot_att(mm(base + ATT_W + g * ATT_OUT))
        v2 = mm(base + 2 * ATT_W + g * ATT_OUT)
        if by_residue:
            dil = ATT_GROUPS[g][1]
            put_by_residue(res_refs[3 * g], q2, dil)
            put_by_residue(res_refs[3 * g + 1], k2, dil)
            put_by_residue(res_refs[3 * g + 2], v2, dil)
            rows = kv_refs[g].shape[3]

            def put_window(ref=kv_refs[g], k2=k2, v2=v2, rows=rows):
                ref[0, 0] = k2[tm - rows:tm].T
                ref[0, 1] = v2[tm - rows:tm].T

            if ATT_GROUPS[g][0] >= tiles_per_seq * tm:
                put_window()
            else:
                pl.when(pl.program_id(0) % tiles_per_seq == tiles_per_seq - 1)(put_window)
        else:
            aq_ref[:, g * ATT_OUT:(g + 1) * ATT_OUT] = q2.astype(BF16)
            kv_refs[g][:, 0:ATT_OUT] = k2
            kv_refs[g][:, ATT_OUT:2 * ATT_OUT] = v2


def _proj(x2d, g_pre, w_qkv, tabs, tm, seq=None):
    m = x2d.shape[0]
    n_tab = tabs[0].shape[0] // tm
    row = lambda i: (i, 0)
    tab = lambda i: (i % n_tab, 0)
    in_specs = [pl.BlockSpec((tm, D_MODEL), row), _resident((1, D_MODEL)), _resident((D_MODEL, QKV_COLS))]
    in_specs += [pl.BlockSpec((tm, LANES), tab)] * 4 + [pl.BlockSpec((tm, MXU_N), tab)] * 3
    out_specs = [pl.BlockSpec((tm, w), row) for w in (RET_QK, RET_QK, RET_V)]
    out_shape = [jax.ShapeDtypeStruct((m, w), BF16) for w in (RET_QK, RET_QK, RET_V)]
    scratch = []
    if seq is None:
        widths = (ATT_W,) + (2 * ATT_OUT,) * N_GROUPS
        out_specs += [pl.BlockSpec((tm, w), row) for w in widths]
        out_shape += [jax.ShapeDtypeStruct((m, w), d) for w, d in zip(widths, (BF16,) + (F32,) * N_GROUPS)]
    else:
        batch, tps = m // seq, seq // tm
        for _, dil in ATT_GROUPS:
            assert tm % (dil * 2 * SUBLANES) == 0
            out_specs += [pl.BlockSpec((1, dil, tm // dil, ATT_OUT), lambda i: (i // tps, 0, i % tps, 0))] * 3
            out_shape += [jax.ShapeDtypeStruct((batch, dil, seq // dil, ATT_OUT), BF16)] * 3
        for win, _ in ATT_GROUPS:
            win = min(win, seq)
            if win == seq:
                out_specs.append(pl.BlockSpec((1, 2, ATT_OUT, tm), lambda i: (i // tps, 0, 0, i % tps)))
            else:
                assert win <= tm and win % LANES == 0
                out_specs.append(pl.BlockSpec((1, 2, ATT_OUT, win), lambda i: (i // tps, 0, 0, 0)))
            out_shape.append(jax.ShapeDtypeStruct((batch, 2, ATT_OUT, win), F32))
        scratch = [pltpu.VMEM((MXU_N // LANES, tm, LANES), F32)]
    return pl.pallas_call(
        functools.partial(_proj_kernel, tm=tm, by_residue=seq is not None,
                          tiles_per_seq=None if seq is None else seq // tm),
        grid=(m // tm,),
        in_specs=in_specs,
        out_specs=out_specs,
        out_shape=out_shape,
        scratch_shapes=scratch,
        compiler_params=_params(),
        name="proj",
    )(x2d, g_pre, w_qkv, *tabs)


def _ret_kernel(q_ref, k_ref, v_ref, dec_ref, qd_ref, kd_ref, gc_ref, o_ref, st_ref, s_scr, *, n_chunks):
    s_scr[...] = jnp.zeros_like(s_scr)

    def body(c, carry):
        r0 = pl.multiple_of(c * RET_CHUNK, RET_CHUNK)
        rows = pl.ds(r0, RET_CHUNK)
        for h in range(RET_HEADS):
            qk = slice(h * RET_QK_DIM, (h + 1) * RET_QK_DIM)
            vv = slice(h * RET_V_DIM, (h + 1) * RET_V_DIM)
            q = q_ref[rows, qk]
            k = k_ref[rows, qk]
            v = v_ref[rows, vv]
            s0 = s_scr[h]
            sc = _dot_nt(q, k) * dec_ref[h]
            o = _dot(sc.astype(BF16), v) + _dot(q, s0.astype(BF16)) * qd_ref[h]
            kd_t = (k.astype(F32) * kd_ref[h]).T.astype(BF16)
            s_scr[h] = s0 * gc_ref[h] + _dot(kd_t, v)
            o_ref[rows, vv] = o * lax.rsqrt(jnp.mean(o * o, axis=-1, keepdims=True) + EPS)
        return carry

    lax.fori_loop(0, n_chunks, body, 0)
    st_ref[0] = s_scr[...]


def _ret_tables(chunk):
    lg = jnp.log1p(-jnp.exp2(-5.0 - jnp.arange(RET_HEADS, dtype=F32)))
    idx = jnp.arange(RET_CHUNK, dtype=F32)
    diff = idx[:, None] - idx[None, :]
    dec = jnp.where(diff[None] >= 0, jnp.exp(lg[:, None, None] * jnp.maximum(diff, 0.0)[None]), 0.0)
    qd = jnp.exp(lg[:, None] * (idx[None, :] + 1.0))
    kd = jnp.exp(lg[:, None] * (chunk - 1.0 - idx)[None, :])
    gc = jnp.exp(lg * chunk)
    qd = jnp.broadcast_to(qd[:, :, None], (RET_HEADS, RET_CHUNK, RET_V_DIM))
    kd = jnp.broadcast_to(kd[:, :, None], (RET_HEADS, RET_CHUNK, RET_QK_DIM))
    gc = jnp.broadcast_to(gc[:, None, None], (RET_HEADS, RET_QK_DIM, RET_V_DIM))
    return dec, qd, kd, gc


def _ret_prompt(rq, rk, rv, batch, seq):
    dec, qd, kd, gc = _ret_tables(RET_CHUNK)
    row = lambda b: (b, 0)
    return pl.pallas_call(
        functools.partial(_ret_kernel, n_chunks=seq // RET_CHUNK),
        grid=(batch,),
        in_specs=[pl.BlockSpec((seq, RET_QK), row), pl.BlockSpec((seq, RET_QK), row), pl.BlockSpec((seq, RET_V), row),
                  _resident(dec.shape), _resident(qd.shape), _resident(kd.shape), _resident(gc.shape)],
        out_specs=[pl.BlockSpec((seq, RET_V), row),
                   pl.BlockSpec((1, RET_HEADS, RET_QK_DIM, RET_V_DIM), lambda b: (b, 0, 0, 0))],
        out_shape=[jax.ShapeDtypeStruct((batch * seq, RET_V), F32),
                   jax.ShapeDtypeStruct((batch, RET_HEADS, RET_QK_DIM, RET_V_DIM), F32)],
        scratch_shapes=[pltpu.VMEM((RET_HEADS, RET_QK_DIM, RET_V_DIM), F32)],
        compiler_params=_params(),
        name="ret_prompt",
    )(rq, rk, rv, dec, qd, kd, gc)


def _ret_sample_kernel(q_ref, k_ref, kt_ref, v_ref, s_ref, dec_ref, qd_ref, kd_ref, gc_ref, o_ref, st_ref,
                       *, n_tok, n_b):
    for b in range(n_b):
        for h in range(RET_HEADS):
            qk = slice(h * RET_QK_DIM, (h + 1) * RET_QK_DIM)
            vv = slice(h * RET_V_DIM, (h + 1) * RET_V_DIM)
            q = q_ref[b, :, qk]
            k = k_ref[b, :, qk]
            v = v_ref[b, :, vv]
            kt = kt_ref[b, h]
            s0 = s_ref[b, h]
            o = _dot(q.astype(BF16), s0.astype(BF16)) * qd_ref[h]
            s1 = s0 * gc_ref[h]
            for i in range(n_tok):
                sc_i = jnp.sum(q * k[i:i + 1, :], axis=-1, keepdims=True) * dec_ref[h][:, i:i + 1]
                o = o + sc_i * v[i:i + 1, :]
                s1 = s1 + (kt[:, i:i + 1] * kd_ref[h][:, i:i + 1]) * v[i:i + 1, :]
            st_ref[b, h] = s1
            o_ref[b, :, vv] = o * lax.rsqrt(jnp.mean(o * o, axis=-1, keepdims=True) + EPS)


def _ret_sample(rq, rk, rv, state, n_tok):
    db = state.shape[0]
    pad = SUBLANES - n_tok
    n_b = 8

    def pad_rows(t):
        t = t.astype(F32).reshape(db, n_tok, t.shape[-1])
        return jnp.pad(t, ((0, 0), (0, pad), (0, 0)))

    q, k, v = pad_rows(rq), pad_rows(rk), pad_rows(rv)
    kt = k.reshape(db, SUBLANES, RET_HEADS, RET_QK_DIM).transpose(0, 2, 3, 1)
    lg = jnp.log1p(-jnp.exp2(-5.0 - jnp.arange(RET_HEADS, dtype=F32)))
    idx = jnp.arange(SUBLANES, dtype=F32)
    diff = idx[:, None] - idx[None, :]
    dec = jnp.where(diff[None] >= 0, jnp.exp(lg[:, None, None] * jnp.maximum(diff, 0.0)[None]), 0.0)
    qd = jnp.broadcast_to(jnp.exp(lg[:, None] * (idx[None, :] + 1.0))[:, :, None], (RET_HEADS, SUBLANES, RET_V_DIM))
    kd = jnp.broadcast_to(jnp.exp(lg[:, None] * (n_tok - 1.0 - idx)[None, :])[:, None, :],
                          (RET_HEADS, RET_QK_DIM, SUBLANES))
    gc = jnp.broadcast_to(jnp.exp(lg * n_tok)[:, None, None], (RET_HEADS, RET_QK_DIM, RET_V_DIM))
    b3 = lambda i: (i, 0, 0)
    b4 = lambda i: (i, 0, 0, 0)
    o, st = pl.pallas_call(
        functools.partial(_ret_sample_kernel, n_tok=n_tok, n_b=n_b),
        grid=(db // n_b,),
        in_specs=[pl.BlockSpec((n_b, SUBLANES, RET_QK), b3), pl.BlockSpec((n_b, SUBLANES, RET_QK), b3),
                  pl.BlockSpec((n_b, RET_HEADS, RET_QK_DIM, SUBLANES), b4),
                  pl.BlockSpec((n_b, SUBLANES, RET_V), b3),
                  pl.BlockSpec((n_b, RET_HEADS, RET_QK_DIM, RET_V_DIM), b4),
                  _resident(dec.shape), _resident(qd.shape), _resident(kd.shape), _resident(gc.shape)],
        out_specs=[pl.BlockSpec((n_b, SUBLANES, RET_V), b3),
                   pl.BlockSpec((n_b, RET_HEADS, RET_QK_DIM, RET_V_DIM), b4)],
        out_shape=[jax.ShapeDtypeStruct((db, SUBLANES, RET_V), F32),
                   jax.ShapeDtypeStruct(state.shape, F32)],
        compiler_params=_params(),
        name="ret_sample",
    )(q, k, kt, v, state, dec, qd, kd, gc)
    return o[:, :n_tok].reshape(db * n_tok, RET_V), st


def _attn_kernel(q_ref, k_ref, v_ref, o_ref, l_ref, vt_scr, bias_scr, *staging, dil, nb):
    n_blk = dil * nb
    length = nb * BAND_BLOCK
    blk = BAND_BLOCK
    o_dst, l_dst = staging if dil > 1 else (o_ref, l_ref)

    kk = lax.broadcasted_iota(jnp.int32, (2 * blk, 2 * LANES), 0)
    ql = lax.broadcasted_iota(jnp.int32, (2 * blk, 2 * LANES), 1) & (blk - 1)
    in_span = kk <= ql + blk
    bias_scr[...] = jnp.where(in_span & (kk >= ql), 0.0, NEG)

    def vt_body(c, carry):
        r0 = pl.multiple_of(c * blk, blk)
        vt_scr[c] = v_ref[0, pl.ds(r0, blk), :].astype(F32).T.astype(BF16)
        return carry

    lax.fori_loop(0, n_blk, vt_body, 0)

    lane = lax.broadcasted_iota(jnp.int32, (blk, LANES), 1)

    def scores(c, p):
        lanes = slice(p * LANES, (p + 1) * LANES)
        k0 = 0 if c % nb == 0 else blk
        q128 = q_ref[0, c * blk:(c + 1) * blk, lanes]
        zero = jnp.zeros_like(q128)
        q_pair = jnp.concatenate([jnp.where(lane < ATT_HEAD_DIM, q128, zero),
                                  jnp.where(lane >= ATT_HEAD_DIM, q128, zero)], axis=0)
        keys = k_ref[0, c * blk - k0:(c + 1) * blk, lanes]
        return _dot_nt(keys, q_pair) + bias_scr[blk - k0:2 * blk, :]

    def weights_values(c, p, s_t):
        lanes = slice(p * LANES, (p + 1) * LANES)
        m = jnp.max(s_t, axis=0, keepdims=True)
        p_t = jnp.exp(s_t - m).astype(BF16)
        v_t = vt_scr[c, lanes, :]
        if c % nb:
            v_t = jnp.concatenate([vt_scr[c - 1, lanes, :], v_t], axis=1)
        v_t = jnp.concatenate([v_t, jnp.ones((2 * SUBLANES, v_t.shape[1]), BF16)], axis=0)
        return _dot(v_t, p_t), m

    def finish(c, p, o_t, m):
        den = o_t[LANES:LANES + 1, :]
        lse = m + jnp.log(den)
        o_pair = jnp.concatenate([o_t[0:ATT_HEAD_DIM, 0:blk] / den[:, 0:blk],
                                  o_t[ATT_HEAD_DIM:LANES, blk:2 * blk] / den[:, blk:2 * blk]], axis=0)
        l_pair = jnp.concatenate([jnp.broadcast_to(lse[:, 0:blk], (ATT_HEAD_DIM, blk)),
                                  jnp.broadcast_to(lse[:, blk:2 * blk], (ATT_HEAD_DIM, blk))], axis=0)
        o_dst[p, c * blk:(c + 1) * blk, :] = o_pair.T
        l_dst[p, c * blk:(c + 1) * blk, :] = l_pair.T

    chains = [(c, p) for c in range(n_blk) for p in range(ATT_OUT // LANES)]
    s_next = scores(*chains[0])
    pending = None
    for i, chain in enumerate(chains):
        s_cur = s_next
        if i + 1 < len(chains):
            s_next = scores(*chains[i + 1])
        o_m = weights_values(*chain, s_cur)
        if pending is not None:
            finish(*pending)
        pending = chain + o_m
    finish(*pending)

    if dil > 1:
        for p in range(ATT_OUT // LANES):
            for r in range(dil):
                o_ref[p, pl.ds(r, length, stride=dil), :] = o_dst[p, r * length:(r + 1) * length, :]
                l_ref[p, pl.ds(r, length, stride=dil), :] = l_dst[p, r * length:(r + 1) * length, :]


def _attn_prompt(q, k, v, g, batch, seq):
    _, dil = ATT_GROUPS[g]
    nb = seq // dil // BAND_BLOCK
    _log2(nb)
    q, k, v = (t.reshape(batch, seq, ATT_OUT) for t in (q, k, v))
    in_spec = pl.BlockSpec((1, seq, ATT_OUT), lambda b: (b, 0, 0))
    n_pair = ATT_OUT // LANES
    out_spec = pl.BlockSpec((n_pair, seq, LANES), lambda b: (0, b, 0))
    scratch = [pltpu.VMEM((dil * nb, ATT_OUT, BAND_BLOCK), BF16), pltpu.VMEM((2 * BAND_BLOCK, 2 * LANES), F32)]
    if dil > 1:
        scratch += [pltpu.VMEM((n_pair, seq, LANES), F32)] * 2
    return pl.pallas_call(
        functools.partial(_attn_kernel, dil=dil, nb=nb),
        grid=(batch,),
        in_specs=[in_spec] * 3,
        out_specs=[out_spec] * 2,
        out_shape=[jax.ShapeDtypeStruct((n_pair, batch * seq, LANES), F32)] * 2,
        scratch_shapes=scratch,
        compiler_params=_params(),
        name=f"attn_prompt_g{g}",
    )(q, k, v)


def _attn_sample_kernel(q_ref, c_ref, n_ref, o_ref, l_ref, nc_ref, *, g, n_tok, wb):
    win, dil = ATT_GROUPS[g]
    nk = win // dil
    _log2(dil)
    n_col = ATT_HEADS * n_tok
    first_new = LANES - n_tok
    lane_t = lax.broadcasted_iota(jnp.int32, (ATT_OUT, LANES), 1)
    for kv in range(2):
        shifted = pltpu.roll(c_ref[0, kv], wb - n_tok, 1)
        if wb > LANES:
            nc_ref[0, kv, :, 0:wb - LANES] = shifted[:, 0:wb - LANES]
        nc_ref[0, kv, :, wb - LANES:wb] = jnp.where(lane_t < first_new, shifted[:, wb - LANES:wb], n_ref[0, kv])

    q = q_ref[0][:, g * ATT_OUT:(g + 1) * ATT_OUT]
    row = lax.broadcasted_iota(jnp.int32, (n_col, ATT_OUT), 0)
    lane = lax.broadcasted_iota(jnp.int32, (n_col, ATT_OUT), 1)
    head_sel = (lane >> _log2(ATT_HEAD_DIM)) == (row >> _log2(n_tok))
    qbd = jnp.where(head_sel, q, jnp.zeros_like(q))

    s_c = _dot(qbd, c_ref[0, 0].astype(BF16))
    s_n = _dot(qbd, n_ref[0, 0].astype(BF16))
    t_c = lax.broadcasted_iota(jnp.int32, (n_col, wb), 0) & (n_tok - 1)
    d_c = wb + t_c - lax.broadcasted_iota(jnp.int32, (n_col, wb), 1)
    ok_c = ((d_c & (dil - 1)) == 0) & (d_c <= nk * dil)
    t_n = lax.broadcasted_iota(jnp.int32, (n_col, LANES), 0) & (n_tok - 1)
    new_idx = lax.broadcasted_iota(jnp.int32, (n_col, LANES), 1) - first_new
    d_n = t_n - new_idx
    ok_n = (new_idx >= 0) & (d_n >= 0) & ((d_n & (dil - 1)) == 0) & (d_n <= nk * dil)
    s_c = jnp.where(ok_c, s_c, NEG)
    s_n = jnp.where(ok_n, s_n, NEG)
    m = jnp.maximum(jnp.max(s_c, axis=-1, keepdims=True), jnp.max(s_n, axis=-1, keepdims=True))
    p_c = jnp.exp(s_c - m)
    p_n = jnp.exp(s_n - m)
    den = jnp.sum(p_c, axis=-1, keepdims=True) + jnp.sum(p_n, axis=-1, keepdims=True)
    o = (_dot_nt(p_c.astype(BF16), c_ref[0, 1].astype(BF16))
         + _dot_nt(p_n.astype(BF16), n_ref[0, 1].astype(BF16))) / den
    lse = jnp.broadcast_to(m + jnp.log(den), (n_col, ATT_OUT))
    o = jnp.where(head_sel, o, 0.0)
    lse = jnp.where(head_sel, lse, 0.0)
    o_t = o[0:n_tok]
    l_t = lse[0:n_tok]
    for h in range(1, ATT_HEADS):
        o_t = o_t + o[h * n_tok:(h + 1) * n_tok]
        l_t = l_t + lse[h * n_tok:(h + 1) * n_tok]
    o_ref[0] = o_t
    l_ref[0] = l_t


def _attn_sample(q_tiled, cache, kv_new, g, n_tok):
    db, _, _, wb = cache.shape
    assert wb % LANES == 0
    b3 = lambda i: (i, 0, 0)
    b4 = lambda i: (i, 0, 0, 0)
    return pl.pallas_call(
        functools.partial(_attn_sample_kernel, g=g, n_tok=n_tok, wb=wb),
        grid=(db,),
        in_specs=[pl.BlockSpec((1, ATT_HEADS * n_tok, ATT_W), b3),
                  pl.BlockSpec((1, 2, ATT_OUT, wb), b4),
                  pl.BlockSpec((1, 2, ATT_OUT, LANES), b4)],
        out_specs=[pl.BlockSpec((1, n_tok, ATT_OUT), b3), pl.BlockSpec((1, n_tok, ATT_OUT), b3),
                   pl.BlockSpec((1, 2, ATT_OUT, wb), b4)],
        out_shape=[jax.ShapeDtypeStruct((db, n_tok, ATT_OUT), F32), jax.ShapeDtypeStruct((db, n_tok, ATT_OUT), F32),
                   jax.ShapeDtypeStruct(cache.shape, F32)],
        compiler_params=_params(),
        name=f"attn_sample_g{g}",
    )(q_tiled, cache, kv_new)


def _merge_kernel(x_ref, on_ref, o0_ref, o1_ref, o2_ref, l0_ref, l1_ref, l2_ref, gpre_ref, wg_ref, wbr_ref,
                  wba_ref, wout_ref, gpost_ref, gffn_ref, x1_ref, h2_ref, *perm, tm, n_split, permute):
    rows_of = [slice(i * tm // n_split, (i + 1) * tm // n_split) for i in range(n_split)]
    n_pair = ATT_OUT // LANES

    def gates(rows):
        x = x_ref[rows, :]
        h = _rms(x, gpre_ref[...]).astype(BF16)
        return [_dot(h, wg_ref[:, c:c + D_MODEL]) for c in (0, RET_V, RET_V + D_MODEL)]

    def branches(rows, rg):
        r = (on_ref[rows, :] * (rg * jax.nn.sigmoid(rg))).astype(BF16)
        atts = []
        for p in range(n_pair):
            l0, l1, l2 = l0_ref[p, rows, :], l1_ref[p, rows, :], l2_ref[p, rows, :]
            mx = jnp.maximum(jnp.maximum(l0, l1), l2)
            e0, e1, e2 = jnp.exp(l0 - mx), jnp.exp(l1 - mx), jnp.exp(l2 - mx)
            att = (e0 * o0_ref[p, rows, :] + e1 * o1_ref[p, rows, :] + e2 * o2_ref[p, rows, :]) / (e0 + e1 + e2)
            atts.append(att.astype(BF16))
        return _dot(r, wbr_ref[...]), _dot(jnp.concatenate(atts, axis=1), wba_ref[...])

    def mixed(gr, ga, br, ba):
        mix = jax.nn.sigmoid(gr) * br + jax.nn.sigmoid(ga) * ba
        return _dot(mix.astype(BF16), wout_ref[...])

    def finish(rows, mo):
        x1 = x_ref[rows, :] + _rms(mo, gpost_ref[...])
        x1_ref[rows, :] = x1
        return _rms(x1, gffn_ref[...])

    g = [gates(rows) for rows in rows_of]
    b = [branches(rows, gi[0]) for rows, gi in zip(rows_of, g)]
    mo = [mixed(gi[1], gi[2], *bi) for gi, bi in zip(g, b)]
    h2 = [finish(rows, moi) for rows, moi in zip(rows_of, mo)]
    if not permute:
        for rows, h2i in zip(rows_of, h2):
            h2_ref[rows, :] = h2i.astype(BF16)
        return
    perm_ref, = perm
    nv = tm // SUBLANES
    assert nv % (tm // n_split) == 0 or (tm // n_split) % nv == 0
    for lc in range(D_MODEL // LANES):
        lanes = slice(lc * LANES, (lc + 1) * LANES)
        for s in range(SUBLANES):
            tok0 = s * nv
            i, off = divmod(tok0, tm // n_split)
            perm_ref[lc, pl.ds(s, nv, stride=SUBLANES), :] = h2[i][off:off + nv, lanes]
        h2_ref[:, lanes] = perm_ref[lc].astype(BF16)


def _merge(x2d, on, outs, lses, g_pre, w_gates, w_br, w_ba, w_out, g_post, g_ffn, tm, permute):
    m = x2d.shape[0]
    n_split = 2 if tm >= 2 * MXU_N else 1
    scratch = [pltpu.VMEM((D_MODEL // LANES, tm, LANES), F32)] if permute else []
    row = lambda i: (i, 0)
    vec = _resident((1, D_MODEL))
    att_spec = pl.BlockSpec((ATT_OUT // LANES, tm, LANES), lambda i: (0, i, 0))
    return pl.pallas_call(
        functools.partial(_merge_kernel, tm=tm, n_split=n_split, permute=permute),
        grid=(m // tm,),
        in_specs=[pl.BlockSpec((tm, D_MODEL), row), pl.BlockSpec((tm, RET_V), row)] + [att_spec] * 6 + [
            vec, _resident(w_gates.shape), _resident(w_br.shape), _resident(w_ba.shape), _resident(w_out.shape),
            vec, vec],
        out_specs=[pl.BlockSpec((tm, D_MODEL), row), pl.BlockSpec((tm, D_MODEL), row)],
        out_shape=[jax.ShapeDtypeStruct((m, D_MODEL), F32), jax.ShapeDtypeStruct((m, D_MODEL), BF16)],
        scratch_shapes=scratch,
        compiler_params=_params(),
        name="merge",
    )(x2d, on, *outs, *lses, g_pre, w_gates, w_br, w_ba, w_out, g_post, g_ffn)


def _ffn_kernel(*refs, tm, tiles_per_seq, seq_tok):
    if seq_tok is None:
        h2_ref, x1_ref, wup_ref, wdn_ref, cw_ref, cb_ref, g_ref, y_ref, cs_ref, carry_ref, f_ref, perm_ref = refs
    else:
        h2_ref, x1_ref, p1_ref, p2_ref, wup_ref, wdn_ref, cw_ref, cb_ref, g_ref, y_ref, u_ref, f_ref = refs
    h2 = h2_ref[...]
    if seq_tok is None:
        sub = lax.broadcasted_iota(jnp.int32, (SUBLANES, FF_CHUNK), 0)

        @pl.when(pl.program_id(0) % tiles_per_seq == 0)
        def _():
            carry_ref[...] = jnp.zeros_like(carry_ref)
    else:
        t = lax.broadcasted_iota(jnp.int32, (tm, FF_CHUNK), 0) & (seq_tok - 1)
        _log2(seq_tok)

    def up(j):
        return [_dot(h2, wup_ref[:, c:c + FF_CHUNK]) for c in (j * FF_CHUNK, D_FF + j * FF_CHUNK)]

    def conv(u, col):
        cols = slice(col, col + FF_CHUNK)
        if seq_tok is None:
            prev = carry_ref[:, cols]
            last2 = jnp.where(sub == 0, prev[SUBLANES - 1:SUBLANES], pltpu.roll(u[tm - 2 * SUBLANES:tm - SUBLANES], 1, 0))
            last1 = jnp.where(sub == 0, prev[2 * SUBLANES - 1:2 * SUBLANES], pltpu.roll(u[tm - SUBLANES:tm], 1, 0))
            u1 = jnp.concatenate([last1, u[0:tm - SUBLANES]], axis=0)
            u2 = jnp.concatenate([last2, last1, u[0:tm - 2 * SUBLANES]], axis=0)
            carry_ref[:, cols] = u[tm - 2 * SUBLANES:tm]
        else:
            u1 = jnp.where(t == 0, p1_ref[:, cols], pltpu.roll(u, 1, 0))
            u2 = jnp.where(t <= 1, p2_ref[:, cols], pltpu.roll(u, 2, 0))
            u_ref[:, cols] = u
        cw = cw_ref[:, cols]
        return cb_ref[:, cols] + cw[0:1] * u2 + cw[1:2] * u1 + cw[2:3] * u

    def gate(j, ug, uv):
        cg = conv(ug, j * FF_CHUNK)
        cv = conv(uv, D_FF + j * FF_CHUNK)
        f_ref[:, j * FF_CHUNK:(j + 1) * FF_CHUNK] = (jax.nn.gelu(cg, approximate=True) * cv).astype(BF16)

    n_chunks = D_FF // FF_CHUNK
    u_next = up(0)
    for j in range(n_chunks):
        u_cur = u_next
        if j + 1 < n_chunks:
            u_next = up(j + 1)
        gate(j, *u_cur)
    out = _rms(_dot(f_ref[...], wdn_ref[...]), g_ref[...])
    if seq_tok is None:
        nv = tm // SUBLANES
        for lc in range(D_MODEL // LANES):
            lanes = slice(lc * LANES, (lc + 1) * LANES)
            perm_ref[lc] = out[:, lanes]
            for s in range(SUBLANES):
                rows = slice(s * nv, (s + 1) * nv)
                y_ref[rows, lanes] = x1_ref[rows, lanes] + perm_ref[lc, pl.ds(s, nv, stride=SUBLANES), :]

        @pl.when(pl.program_id(0) % tiles_per_seq == tiles_per_seq - 1)
        def _():
            cs_ref[0, 0:1, :] = carry_ref[SUBLANES - 1:SUBLANES, :]
            cs_ref[0, 1:2, :] = carry_ref[2 * SUBLANES - 1:2 * SUBLANES, :]
    else:
        y_ref[...] = x1_ref[...] + out


def _ffn_prompt(h2, x1, w_up, w_dn, conv_w, conv_b, g_post, batch, seq, tm):
    m = h2.shape[0]
    tps = seq // tm
    row = lambda i: (i, 0)
    return pl.pallas_call(
        functools.partial(_ffn_kernel, tm=tm, tiles_per_seq=tps, seq_tok=None),
        grid=(m // tm,),
        in_specs=[pl.BlockSpec((tm, D_MODEL), row), pl.BlockSpec((tm, D_MODEL), row),
                  _resident(w_up.shape), _resident(w_dn.shape), _resident(conv_w.shape), _resident(conv_b.shape),
                  _resident((1, D_MODEL))],
        out_specs=[pl.BlockSpec((tm, D_MODEL), row),
                   pl.BlockSpec((1, CONV_W - 1, 2 * D_FF), lambda i: (i // tps, 0, 0))],
        out_shape=[jax.ShapeDtypeStruct((m, D_MODEL), F32),
                   jax.ShapeDtypeStruct((batch, CONV_W - 1, 2 * D_FF), F32)],
        scratch_shapes=[pltpu.VMEM((2 * SUBLANES, 2 * D_FF), F32), pltpu.VMEM((tm, D_FF), BF16),
                        pltpu.VMEM((D_MODEL // LANES, tm, LANES), F32)],
        compiler_params=_params(),
        name="ffn_prompt",
    )(h2, x1, w_up, w_dn, conv_w, conv_b, g_post)


def _ffn_sample(h2, x1, p1, p2, w_up, w_dn, conv_w, conv_b, g_post, n_tok):
    m = h2.shape[0]
    const = lambda i: (0, 0)
    full = lambda a: _resident(a.shape)
    return pl.pallas_call(
        functools.partial(_ffn_kernel, tm=m, tiles_per_seq=1, seq_tok=n_tok),
        grid=(1,),
        in_specs=[full(h2), full(x1), full(p1), full(p2), full(w_up), full(w_dn), full(conv_w), full(conv_b),
                  _resident((1, D_MODEL))],
        out_specs=[pl.BlockSpec((m, D_MODEL), const), pl.BlockSpec((m, 2 * D_FF), const)],
        out_shape=[jax.ShapeDtypeStruct((m, D_MODEL), F32), jax.ShapeDtypeStruct((m, 2 * D_FF), F32)],
        scratch_shapes=[pltpu.VMEM((m, D_FF), BF16)],
        compiler_params=_params(),
        name="ffn_sample",
    )(h2, x1, p1, p2, w_up, w_dn, conv_w, conv_b, g_post)


def _rot_tables(pos):
    pos = pos.astype(F32)
    rf = RET_THETA ** (-jnp.linspace(0.0, 1.0, RET_QK_DIM // 2, dtype=F32))
    ang = pos[:, None] * rf[None, :]
    cos, sin = jnp.cos(ang), jnp.sin(ang)
    cr = jnp.concatenate([cos, cos], axis=-1)
    sr = jnp.concatenate([-sin, sin], axis=-1)
    k_scale = RET_QK_DIM ** -0.5
    half = ROPE_DIM // 2
    af = ROPE_THETA ** (-jnp.arange(half, dtype=F32) / half)
    ang = pos[:, None] * af[None, :]
    cos, sin = jnp.cos(ang), jnp.sin(ang)
    n = pos.shape[0]
    rest = ATT_HEAD_DIM - ROPE_DIM
    zh = jnp.zeros((n, half), F32)
    ca = jnp.concatenate([cos, cos, jnp.ones((n, rest), F32)], axis=-1)
    sa1 = jnp.concatenate([zh, sin, jnp.zeros((n, rest), F32)], axis=-1)
    sa2 = jnp.concatenate([-sin, zh, jnp.zeros((n, rest), F32)], axis=-1)
    rep = MXU_N // ATT_HEAD_DIM
    ca, sa1, sa2 = (jnp.tile(t, (1, rep)) for t in (ca, sa1, sa2))
    return [cr, sr, cr * k_scale, sr * k_scale, ca, sa1, sa2]


def kernel(x_prompt, x_sample, cache_kv_g0, cache_kv_g1, cache_kv_g2, state_ret, state_conv, norm_mix_pre, w_in,
           w_branch_ret, w_branch_attn, w_out, norm_mix_post, norm_ffn_pre, w_ffn_up, conv_w, conv_b, w_ffn_down,
           norm_ffn_post):
    batch, seq, _ = x_prompt.shape
    db, n_tok, _ = x_sample.shape
    depth = w_in.shape[0]
    assert depth == 1 and seq % (ATT_GROUPS[-1][1] * BAND_BLOCK) == 0 and n_tok <= SUBLANES
    caches = (cache_kv_g0, cache_kv_g1, cache_kv_g2)

    wi = w_in[0]
    o_rg = 2 * RET_QK + RET_V
    o_aq = o_rg + RET_V
    o_gr = o_aq + 3 * ATT_W
    w_qkv = jnp.concatenate([wi[:, :o_rg], wi[:, o_aq:o_aq + ATT_W] * ATT_HEAD_DIM ** -0.5,
                             wi[:, o_aq + ATT_W:o_gr]], axis=1).astype(BF16)
    w_gates = jnp.concatenate([wi[:, o_rg:o_aq], wi[:, o_gr:]], axis=1).astype(BF16)
    w_br, w_ba, w_o = w_branch_ret[0].astype(BF16), w_branch_attn[0].astype(BF16), w_out[0].astype(BF16)
    w_up, w_dn = w_ffn_up[0].astype(BF16), w_ffn_down[0].astype(BF16)
    cb = conv_b[0][None, :]
    cw = conv_w[0]
    g_pre, g_post, g_ffn, g_post2 = (t[0][None, :] for t in (norm_mix_pre, norm_mix_post, norm_ffn_pre, norm_ffn_post))

    tm = 512
    xp = x_prompt.reshape(batch * seq, D_MODEL)
    proj_out = _proj(xp, g_pre, w_qkv, _rot_tables(jnp.arange(seq)), tm, seq=seq)
    rq, rk, rv = proj_out[:3]
    qkv_res = proj_out[3:3 + 3 * N_GROUPS]
    kv_last = proj_out[3 + 3 * N_GROUPS:]
    on, p_ret = _ret_prompt(rq, rk, rv, batch, seq)
    outs, lses = [], []
    for g in range(N_GROUPS):
        o_g, l_g = _attn_prompt(*qkv_res[3 * g:3 * g + 3], g, batch, seq)
        outs.append(o_g)
        lses.append(l_g)
    x1, h2 = _merge(xp, on, outs, lses, g_pre, w_gates, w_br, w_ba, w_o, g_post, g_ffn, tm, permute=True)
    y_p, p_conv = _ffn_prompt(h2, x1, w_up, w_dn, cw, cb, g_post2, batch, seq, tm)
    p_kv = [t.reshape(batch, 2, ATT_HEADS, ATT_HEAD_DIM, t.shape[3]).transpose(0, 4, 1, 2, 3)[None] for t in kv_last]

    ms = db * n_tok
    xs = x_sample.reshape(ms, D_MODEL)
    pos_s = jnp.tile(PAST_LEN + jnp.arange(n_tok), db)
    rq, rk, rv, aq, kv0, kv1, kv2 = _proj(xs, g_pre, w_qkv, _rot_tables(pos_s), ms)
    kvs = (kv0, kv1, kv2)
    on, s_ret = _ret_sample(rq, rk, rv, state_ret[0], n_tok)
    q_tiled = jnp.tile(aq.reshape(db, n_tok, ATT_W), (1, ATT_HEADS, 1))
    outs, lses, s_kv = [], [], []
    for g in range(N_GROUPS):
        cache = caches[g][0]
        wb = cache.shape[1]
        cache_t = cache.transpose(0, 2, 3, 4, 1).reshape(db, 2, ATT_OUT, wb)
        kv_new = kvs[g].reshape(db, n_tok, 2, ATT_OUT).transpose(0, 2, 3, 1)
        kv_new = jnp.pad(kv_new, ((0, 0), (0, 0), (0, 0), (LANES - n_tok, 0)))
        o_g, l_g, nc = _attn_sample(q_tiled, cache_t, kv_new, g, n_tok)
        outs.append(o_g.reshape(ms, ATT_OUT // LANES, LANES).transpose(1, 0, 2))
        lses.append(l_g.reshape(ms, ATT_OUT // LANES, LANES).transpose(1, 0, 2))
        s_kv.append(nc.reshape(db, 2, ATT_HEADS, ATT_HEAD_DIM, wb).transpose(0, 4, 1, 2, 3)[None])
    x1, h2 = _merge(xs, on, outs, lses, g_pre, w_gates, w_br, w_ba, w_o, g_post, g_ffn, ms, permute=False)
    st = state_conv[0]
    zeros = jnp.zeros((db, n_tok - 1, 2 * D_FF), F32)
    p1 = jnp.concatenate([st[:, 1:2], zeros], axis=1).reshape(ms, 2 * D_FF)
    p2 = jnp.concatenate([st, zeros[:, :n_tok - 2]], axis=1).reshape(ms, 2 * D_FF)
    y_s, u_s = _ffn_sample(h2, x1, p1, p2, w_up, w_dn, cw, cb, g_post2, n_tok)
    s_conv = u_s.reshape(db, n_tok, 2 * D_FF)[:, n_tok - (CONV_W - 1):][None]

    return (y_p.reshape(x_prompt.shape), y_s.reshape(x_sample.shape), p_kv[0], p_kv[1], p_kv[2], p_ret[None],
            p_conv[None], s_kv[0], s_kv[1], s_kv[2], s_ret[None], s_conv)
```

```python
import functools
import math

import jax
import jax.numpy as jnp
from jax import lax
from jax.experimental import pallas as pl
from jax.experimental.pallas import tpu as pltpu

F32 = jnp.float32
BF16 = jnp.bfloat16

D_MODEL = 1024
PAST_LEN = 16384
RET_HEADS = 4
RET_QK_DIM = 128
RET_V_DIM = 256
RET_CHUNK = 128
RET_THETA = 10000.0
RET_QK = RET_HEADS * RET_QK_DIM
RET_V = RET_HEADS * RET_V_DIM
ATT_GROUPS = ((128, 1), (512, 4), (2048, 16))
N_GROUPS = 3
ATT_HEADS = 4
ATT_HEAD_DIM = 64
ROPE_DIM = ATT_HEAD_DIM // 4
ROPE_THETA = 500000.0
BAND_BLOCK = 128
ATT_OUT = ATT_HEADS * ATT_HEAD_DIM
ATT_W = N_GROUPS * ATT_OUT
D_FF = 2816
CONV_W = 3
EPS = 1e-6

LANES = 128
SUBLANES = 8
MXU_N = 256
VMEM_LIMIT = 56 * 1024 * 1024
NEG = -1e30
FF_CHUNK = MXU_N
ATTN_UNROLL = 3
QKV_COLS = 2 * RET_QK + RET_V + 3 * ATT_W


def _rms(x, g):
    return x * lax.rsqrt(jnp.mean(x * x, axis=-1, keepdims=True) + EPS) * g


def _dot(a, b):
    return jnp.dot(a, b, preferred_element_type=F32)


def _dot_nt(a, b):
    return lax.dot_general(a, b, (((1,), (1,)), ((), ())), preferred_element_type=F32)


def _resident(shape):
    return pl.BlockSpec(shape, lambda *_: (0,) * len(shape), pipeline_mode=pl.Buffered(1))


def _log2(n):
    assert n > 0 and n & (n - 1) == 0, n
    return n.bit_length() - 1


def _params(n_axes=1):
    return pltpu.CompilerParams(dimension_semantics=("arbitrary",) * n_axes, vmem_limit_bytes=VMEM_LIMIT)


def _proj_kernel(x_ref, g_ref, w_ref, crq_ref, srq_ref, crk_ref, srk_ref, ca_ref, sa1_ref, sa2_ref, *rest,
                 tm, by_residue, tiles_per_seq):
    rq_ref, rk_ref, rv_ref = rest[:3]
    if by_residue:
        res_refs = rest[3:3 + 3 * N_GROUPS]
        kv_refs = rest[3 + 3 * N_GROUPS:3 + 4 * N_GROUPS]
        scr = rest[3 + 4 * N_GROUPS]
    else:
        aq_ref = rest[3]
        kv_refs = rest[4:4 + N_GROUPS]
    h = _rms(x_ref[...], g_ref[...]).astype(BF16)

    def mm(c0, width=MXU_N):
        return _dot(h, w_ref[:, c0:c0 + width])

    def rot_ret(t, c, s):
        return t * c + pltpu.roll(t, RET_QK_DIM // 2, 1) * s

    ca, sa1, sa2 = ca_ref[...], sa1_ref[...], sa2_ref[...]

    def rot_att(t):
        return t * ca + pltpu.roll(t, ROPE_DIM // 2, 1) * sa1 + pltpu.roll(t, MXU_N - ROPE_DIM // 2, 1) * sa2

    def put_by_residue(ref, val, dil):
        if dil == 1:
            ref[0, 0] = val.astype(BF16)
            return
        for half in range(MXU_N // LANES):
            lanes = slice(half * LANES, (half + 1) * LANES)
            scr[half] = val[:, lanes]
            for r in range(dil):
                ref[0, r, :, lanes] = scr[half, pl.ds(r, tm // dil, stride=dil), :].astype(BF16)

    def put_ret(ref, c, c_ref, s_ref):
        def put(t):
            for half in range(MXU_N // LANES):
                lo = half * LANES
                col = c * MXU_N + lo
                ref[:, col:col + LANES] = rot_ret(t[:, lo:lo + LANES], c_ref[...], s_ref[...]).astype(BF16)
        return put

    def put_rv(c):
        def put(t):
            rv_ref[:, c * MXU_N:(c + 1) * MXU_N] = t.astype(BF16)
        return put

    def put_att(g, which):
        def put(t):
            if which < 2:
                t = rot_att(t)
            if not by_residue:
                if which == 0:
                    aq_ref[:, g * ATT_OUT:(g + 1) * ATT_OUT] = t.astype(BF16)
                else:
                    kv_refs[g][:, (which - 1) * ATT_OUT:which * ATT_OUT] = t
                return
            put_by_residue(res_refs[3 * g + which], t, ATT_GROUPS[g][1])
            if which == 0:
                return
            rows = kv_refs[g].shape[3]

            def put_window():
                kv_refs[g][0, which - 1] = t[tm - rows:tm].T

            if ATT_GROUPS[g][0] >= tiles_per_seq * tm:
                put_window()
            else:
                pl.when(pl.program_id(0) % tiles_per_seq == tiles_per_seq - 1)(put_window)
        return put

    base = 2 * RET_QK + RET_V
    groups = []
    for c in range(RET_QK // MXU_N):
        groups.append([(c * MXU_N, put_ret(rq_ref, c, crq_ref, srq_ref)),
                       (RET_QK + c * MXU_N, put_ret(rk_ref, c, crk_ref, srk_ref))])
    for c in range(RET_V // MXU_N):
        groups.append([(2 * RET_QK + c * MXU_N, put_rv(c))])
    for g in range(N_GROUPS):
        groups.append([(base + which * ATT_W + g * ATT_OUT, put_att(g, which)) for which in range(3)])
    for group in groups:
        results = [mm(col) for col, _ in group]
        for (_, epilogue), t in zip(group, results):
            epilogue(t)


def _proj(x2d, g_pre, w_qkv, tabs, tm, seq=None):
    m = x2d.shape[0]
    n_tab = tabs[0].shape[0] // tm
    row = lambda i: (i, 0)
    tab = lambda i: (i % n_tab, 0)
    in_specs = [pl.BlockSpec((tm, D_MODEL), row), _resident((1, D_MODEL)), _resident((D_MODEL, QKV_COLS))]
    in_specs += [pl.BlockSpec((tm, LANES), tab)] * 4 + [pl.BlockSpec((tm, MXU_N), tab)] * 3
    out_specs = [pl.BlockSpec((tm, w), row) for w in (RET_QK, RET_QK, RET_V)]
    out_shape = [jax.ShapeDtypeStruct((m, w), BF16) for w in (RET_QK, RET_QK, RET_V)]
    scratch = []
    if seq is None:
        widths = (ATT_W,) + (2 * ATT_OUT,) * N_GROUPS
        out_specs += [pl.BlockSpec((tm, w), row) for w in widths]
        out_shape += [jax.ShapeDtypeStruct((m, w), d) for w, d in zip(widths, (BF16,) + (F32,) * N_GROUPS)]
    else:
        batch, tps = m // seq, seq // tm
        for _, dil in ATT_GROUPS:
            assert tm % (dil * 2 * SUBLANES) == 0
            out_specs += [pl.BlockSpec((1, dil, tm // dil, ATT_OUT), lambda i: (i // tps, 0, i % tps, 0))] * 3
            out_shape += [jax.ShapeDtypeStruct((batch, dil, seq // dil, ATT_OUT), BF16)] * 3
        for win, _ in ATT_GROUPS:
            win = min(win, seq)
            if win == seq:
                out_specs.append(pl.BlockSpec((1, 2, ATT_OUT, tm), lambda i: (i // tps, 0, 0, i % tps)))
            else:
                assert win <= tm and win % LANES == 0
                out_specs.append(pl.BlockSpec((1, 2, ATT_OUT, win), lambda i: (i // tps, 0, 0, 0)))
            out_shape.append(jax.ShapeDtypeStruct((batch, 2, ATT_OUT, win), F32))
        scratch = [pltpu.VMEM((MXU_N // LANES, tm, LANES), F32)]
    return pl.pallas_call(
        functools.partial(_proj_kernel, tm=tm, by_residue=seq is not None,
                          tiles_per_seq=None if seq is None else seq // tm),
        grid=(m // tm,),
        in_specs=in_specs,
        out_specs=out_specs,
        out_shape=out_shape,
        scratch_shapes=scratch,
        compiler_params=_params(),
        name="proj",
    )(x2d, g_pre, w_qkv, *tabs)


def _ret_kernel(q_ref, k_ref, v_ref, dec_ref, qd_ref, kd_ref, gc_ref, o_ref, st_ref, s_scr, *, n_chunks):
    s_scr[...] = jnp.zeros_like(s_scr)

    def body(c, carry):
        r0 = pl.multiple_of(c * RET_CHUNK, RET_CHUNK)
        rows = pl.ds(r0, RET_CHUNK)
        first = []
        for h in range(RET_HEADS):
            qk = slice(h * RET_QK_DIM, (h + 1) * RET_QK_DIM)
            vv = slice(h * RET_V_DIM, (h + 1) * RET_V_DIM)
            q = q_ref[rows, qk]
            k = k_ref[rows, qk]
            v = v_ref[rows, vv]
            s0 = s_scr[h]
            sc = _dot_nt(q, k)
            from_state = _dot(q, s0.astype(BF16))
            kd_t = (k.astype(F32) * kd_ref[h]).T.astype(BF16)
            first.append((sc, from_state, _dot(kd_t, v), s0, v))
        for h, (sc, from_state, inc, s0, v) in enumerate(first):
            vv = slice(h * RET_V_DIM, (h + 1) * RET_V_DIM)
            o = _dot((sc * dec_ref[h]).astype(BF16), v) + from_state * qd_ref[h]
            s_scr[h] = s0 * gc_ref[h] + inc
            o_ref[rows, vv] = o * lax.rsqrt(jnp.mean(o * o, axis=-1, keepdims=True) + EPS)
        return carry

    lax.fori_loop(0, n_chunks, body, 0, unroll=2)
    st_ref[0] = s_scr[...]


def _ret_tables(chunk):
    lg = jnp.log1p(-jnp.exp2(-5.0 - jnp.arange(RET_HEADS, dtype=F32)))
    idx = jnp.arange(RET_CHUNK, dtype=F32)
    diff = idx[:, None] - idx[None, :]
    dec = jnp.where(diff[None] >= 0, jnp.exp(lg[:, None, None] * jnp.maximum(diff, 0.0)[None]), 0.0)
    qd = jnp.exp(lg[:, None] * (idx[None, :] + 1.0))
    kd = jnp.exp(lg[:, None] * (chunk - 1.0 - idx)[None, :])
    gc = jnp.exp(lg * chunk)
    qd = jnp.broadcast_to(qd[:, :, None], (RET_HEADS, RET_CHUNK, RET_V_DIM))
    kd = jnp.broadcast_to(kd[:, :, None], (RET_HEADS, RET_CHUNK, RET_QK_DIM))
    gc = jnp.broadcast_to(gc[:, None, None], (RET_HEADS, RET_QK_DIM, RET_V_DIM))
    return dec, qd, kd, gc


def _ret_prompt(rq, rk, rv, batch, seq):
    dec, qd, kd, gc = _ret_tables(RET_CHUNK)
    row = lambda b: (b, 0)
    return pl.pallas_call(
        functools.partial(_ret_kernel, n_chunks=seq // RET_CHUNK),
        grid=(batch,),
        in_specs=[pl.BlockSpec((seq, RET_QK), row), pl.BlockSpec((seq, RET_QK), row), pl.BlockSpec((seq, RET_V), row),
                  _resident(dec.shape), _resident(qd.shape), _resident(kd.shape), _resident(gc.shape)],
        out_specs=[pl.BlockSpec((seq, RET_V), row),
                   pl.BlockSpec((1, RET_HEADS, RET_QK_DIM, RET_V_DIM), lambda b: (b, 0, 0, 0))],
        out_shape=[jax.ShapeDtypeStruct((batch * seq, RET_V), F32),
                   jax.ShapeDtypeStruct((batch, RET_HEADS, RET_QK_DIM, RET_V_DIM), F32)],
        scratch_shapes=[pltpu.VMEM((RET_HEADS, RET_QK_DIM, RET_V_DIM), F32)],
        compiler_params=_params(),
        name="ret_prompt",
    )(rq, rk, rv, dec, qd, kd, gc)


def _ret_sample_kernel(q_ref, k_ref, kt_ref, v_ref, s_ref, dec_ref, qd_ref, kd_ref, gc_ref, o_ref, st_ref,
                       *, n_tok, n_b):
    for b in range(n_b):
        for h in range(RET_HEADS):
            qk = slice(h * RET_QK_DIM, (h + 1) * RET_QK_DIM)
            vv = slice(h * RET_V_DIM, (h + 1) * RET_V_DIM)
            q = q_ref[b, :, qk]
            k = k_ref[b, :, qk]
            v = v_ref[b, :, vv]
            kt = kt_ref[b, h]
            s0 = s_ref[b, h]
            o = _dot(q.astype(BF16), s0.astype(BF16)) * qd_ref[h]
            s1 = s0 * gc_ref[h]
            for i in range(n_tok):
                sc_i = jnp.sum(q * k[i:i + 1, :], axis=-1, keepdims=True) * dec_ref[h][:, i:i + 1]
                o = o + sc_i * v[i:i + 1, :]
                s1 = s1 + (kt[:, i:i + 1] * kd_ref[h][:, i:i + 1]) * v[i:i + 1, :]
            st_ref[b, h] = s1
            o_ref[b, :, vv] = o * lax.rsqrt(jnp.mean(o * o, axis=-1, keepdims=True) + EPS)


def _ret_sample(rq, rk, rv, state, n_tok):
    db = state.shape[0]
    pad = SUBLANES - n_tok
    n_b = 8

    def pad_rows(t):
        t = t.astype(F32).reshape(db, n_tok, t.shape[-1])
        return jnp.pad(t, ((0, 0), (0, pad), (0, 0)))

    q, k, v = pad_rows(rq), pad_rows(rk), pad_rows(rv)
    kt = k.reshape(db, SUBLANES, RET_HEADS, RET_QK_DIM).transpose(0, 2, 3, 1)
    lg = jnp.log1p(-jnp.exp2(-5.0 - jnp.arange(RET_HEADS, dtype=F32)))
    idx = jnp.arange(SUBLANES, dtype=F32)
    diff = idx[:, None] - idx[None, :]
    dec = jnp.where(diff[None] >= 0, jnp.exp(lg[:, None, None] * jnp.maximum(diff, 0.0)[None]), 0.0)
    qd = jnp.broadcast_to(jnp.exp(lg[:, None] * (idx[None, :] + 1.0))[:, :, None], (RET_HEADS, SUBLANES, RET_V_DIM))
    kd = jnp.broadcast_to(jnp.exp(lg[:, None] * (n_tok - 1.0 - idx)[None, :])[:, None, :],
                          (RET_HEADS, RET_QK_DIM, SUBLANES))
    gc = jnp.broadcast_to(jnp.exp(lg * n_tok)[:, None, None], (RET_HEADS, RET_QK_DIM, RET_V_DIM))
    b3 = lambda i: (i, 0, 0)
    b4 = lambda i: (i, 0, 0, 0)
    o, st = pl.pallas_call(
        functools.partial(_ret_sample_kernel, n_tok=n_tok, n_b=n_b),
        grid=(db // n_b,),
        in_specs=[pl.BlockSpec((n_b, SUBLANES, RET_QK), b3), pl.BlockSpec((n_b, SUBLANES, RET_QK), b3),
                  pl.BlockSpec((n_b, RET_HEADS, RET_QK_DIM, SUBLANES), b4),
                  pl.BlockSpec((n_b, SUBLANES, RET_V), b3),
                  pl.BlockSpec((n_b, RET_HEADS, RET_QK_DIM, RET_V_DIM), b4),
                  _resident(dec.shape), _resident(qd.shape), _resident(kd.shape), _resident(gc.shape)],
        out_specs=[pl.BlockSpec((n_b, SUBLANES, RET_V), b3),
                   pl.BlockSpec((n_b, RET_HEADS, RET_QK_DIM, RET_V_DIM), b4)],
        out_shape=[jax.ShapeDtypeStruct((db, SUBLANES, RET_V), F32),
                   jax.ShapeDtypeStruct(state.shape, F32)],
        compiler_params=_params(),
        name="ret_sample",
    )(q, k, kt, v, state, dec, qd, kd, gc)
    return o[:, :n_tok].reshape(db * n_tok, RET_V), st


def _attn_kernel(q_ref, k_ref, v_ref, o_ref, l_ref, vt_scr, bias_scr, *staging, dil, nb):
    n_blk = dil * nb
    length = nb * BAND_BLOCK
    blk = BAND_BLOCK
    o_dst, l_dst = staging if dil > 1 else (o_ref, l_ref)

    kk = lax.broadcasted_iota(jnp.int32, (2 * blk, 2 * LANES), 0)
    ql = lax.broadcasted_iota(jnp.int32, (2 * blk, 2 * LANES), 1) & (blk - 1)
    in_span = kk <= ql + blk
    bias_scr[...] = jnp.where(in_span & (kk >= ql), 0.0, NEG)

    def vt_body(c, carry):
        r0 = pl.multiple_of(c * blk, blk)
        vt_scr[c] = v_ref[0, pl.ds(r0, blk), :].astype(F32).T.astype(BF16)
        return carry

    lax.fori_loop(0, n_blk, vt_body, 0)

    lane = lax.broadcasted_iota(jnp.int32, (blk, LANES), 1)

    def scores(c, p):
        lanes = slice(p * LANES, (p + 1) * LANES)
        k0 = 0 if c % nb == 0 else blk
        q128 = q_ref[0, c * blk:(c + 1) * blk, lanes]
        zero = jnp.zeros_like(q128)
        q_pair = jnp.concatenate([jnp.where(lane < ATT_HEAD_DIM, q128, zero),
                                  jnp.where(lane >= ATT_HEAD_DIM, q128, zero)], axis=0)
        keys = k_ref[0, c * blk - k0:(c + 1) * blk, lanes]
        return _dot_nt(keys, q_pair) + bias_scr[blk - k0:2 * blk, :]

    def weights_values(c, p, s_t):
        lanes = slice(p * LANES, (p + 1) * LANES)
        m = jnp.max(s_t, axis=0, keepdims=True)
        p_t = jnp.exp(s_t - m).astype(BF16)
        v_t = vt_scr[c, lanes, :]
        if c % nb:
            v_t = jnp.concatenate([vt_scr[c - 1, lanes, :], v_t], axis=1)
        v_t = jnp.concatenate([v_t, jnp.ones((2 * SUBLANES, v_t.shape[1]), BF16)], axis=0)
        return _dot(v_t, p_t), m

    def finish(c, p, o_t, m):
        den = o_t[LANES:LANES + 1, :]
        lse = m + jnp.log(den)
        o_pair = jnp.concatenate([o_t[0:ATT_HEAD_DIM, 0:blk] / den[:, 0:blk],
                                  o_t[ATT_HEAD_DIM:LANES, blk:2 * blk] / den[:, blk:2 * blk]], axis=0)
        l_pair = jnp.concatenate([jnp.broadcast_to(lse[:, 0:blk], (ATT_HEAD_DIM, blk)),
                                  jnp.broadcast_to(lse[:, blk:2 * blk], (ATT_HEAD_DIM, blk))], axis=0)
        o_dst[p, c * blk:(c + 1) * blk, :] = o_pair.T
        l_dst[p, c * blk:(c + 1) * blk, :] = l_pair.T

    chains = [(c, p) for c in range(n_blk) for p in range(ATT_OUT // LANES)]
    s_next = scores(*chains[0])
    pending = None
    for i, chain in enumerate(chains):
        s_cur = s_next
        if i + 1 < len(chains):
            s_next = scores(*chains[i + 1])
        o_m = weights_values(*chain, s_cur)
        if pending is not None:
            finish(*pending)
        pending = chain + o_m
    finish(*pending)

    if dil > 1:
        for p in range(ATT_OUT // LANES):
            for r in range(dil):
                o_ref[p, pl.ds(r, length, stride=dil), :] = o_dst[p, r * length:(r + 1) * length, :]
                l_ref[p, pl.ds(r, length, stride=dil), :] = l_dst[p, r * length:(r + 1) * length, :]


def _attn_prompt(q, k, v, g, batch, seq):
    _, dil = ATT_GROUPS[g]
    nb = seq // dil // BAND_BLOCK
    _log2(nb)
    q, k, v = (t.reshape(batch, seq, ATT_OUT) for t in (q, k, v))
    in_spec = pl.BlockSpec((1, seq, ATT_OUT), lambda b: (b, 0, 0))
    n_pair = ATT_OUT // LANES
    out_spec = pl.BlockSpec((n_pair, seq, LANES), lambda b: (0, b, 0))
    scratch = [pltpu.VMEM((dil * nb, ATT_OUT, BAND_BLOCK), BF16), pltpu.VMEM((2 * BAND_BLOCK, 2 * LANES), F32)]
    if dil > 1:
        scratch += [pltpu.VMEM((n_pair, seq, LANES), F32)] * 2
    return pl.pallas_call(
        functools.partial(_attn_kernel, dil=dil, nb=nb),
        grid=(batch,),
        in_specs=[in_spec] * 3,
        out_specs=[out_spec] * 2,
        out_shape=[jax.ShapeDtypeStruct((n_pair, batch * seq, LANES), F32)] * 2,
        scratch_shapes=scratch,
        compiler_params=_params(),
        name=f"attn_prompt_g{g}",
    )(q, k, v)


def _attn_sample_kernel(q_ref, c_ref, n_ref, o_ref, l_ref, nc_ref, *, g, n_tok, wb, n_b):
    for bb in range(n_b):
        _attn_sample_one(q_ref, c_ref, n_ref, o_ref, l_ref, nc_ref, bb, g=g, n_tok=n_tok, wb=wb, n_b=n_b)


def _attn_sample_one(q_ref, c_ref, n_ref, o_ref, l_ref, nc_ref, bb, *, g, n_tok, wb, n_b):
    win, dil = ATT_GROUPS[g]
    nk = win // dil
    _log2(dil)
    n_col = ATT_HEADS * n_tok
    first_new = LANES - n_tok
    seq_lane = ((pl.program_id(0) * n_b + bb) * n_tok) % LANES
    new = [pltpu.roll(n_ref[kv], (first_new + LANES - seq_lane) % LANES, 1) for kv in range(2)]
    lane_t = lax.broadcasted_iota(jnp.int32, (ATT_OUT, LANES), 1)
    for kv in range(2):
        shifted = pltpu.roll(c_ref[bb, kv], wb - n_tok, 1)
        if wb > LANES:
            nc_ref[bb, kv, :, 0:wb - LANES] = shifted[:, 0:wb - LANES]
        nc_ref[bb, kv, :, wb - LANES:wb] = jnp.where(lane_t < first_new, shifted[:, wb - LANES:wb], new[kv])

    q = q_ref[bb][:, g * ATT_OUT:(g + 1) * ATT_OUT]
    row = lax.broadcasted_iota(jnp.int32, (n_col, ATT_OUT), 0)
    lane = lax.broadcasted_iota(jnp.int32, (n_col, ATT_OUT), 1)
    head_sel = (lane >> _log2(ATT_HEAD_DIM)) == (row >> _log2(n_tok))
    qbd = jnp.where(head_sel, q, jnp.zeros_like(q))

    s_c = _dot(qbd, c_ref[bb, 0].astype(BF16))
    s_n = _dot(qbd, new[0].astype(BF16))
    t_c = lax.broadcasted_iota(jnp.int32, (n_col, wb), 0) & (n_tok - 1)
    d_c = wb + t_c - lax.broadcasted_iota(jnp.int32, (n_col, wb), 1)
    ok_c = ((d_c & (dil - 1)) == 0) & (d_c <= nk * dil)
    t_n = lax.broadcasted_iota(jnp.int32, (n_col, LANES), 0) & (n_tok - 1)
    new_idx = lax.broadcasted_iota(jnp.int32, (n_col, LANES), 1) - first_new
    d_n = t_n - new_idx
    ok_n = (new_idx >= 0) & (d_n >= 0) & ((d_n & (dil - 1)) == 0) & (d_n <= nk * dil)
    s_c = jnp.where(ok_c, s_c, NEG)
    s_n = jnp.where(ok_n, s_n, NEG)
    m = jnp.maximum(jnp.max(s_c, axis=-1, keepdims=True), jnp.max(s_n, axis=-1, keepdims=True))
    p_c = jnp.exp(s_c - m)
    p_n = jnp.exp(s_n - m)
    den = jnp.sum(p_c, axis=-1, keepdims=True) + jnp.sum(p_n, axis=-1, keepdims=True)
    o = (_dot_nt(p_c.astype(BF16), c_ref[bb, 1].astype(BF16))
         + _dot_nt(p_n.astype(BF16), new[1].astype(BF16))) / den
    lse = jnp.broadcast_to(m + jnp.log(den), (n_col, ATT_OUT))
    o = jnp.where(head_sel, o, 0.0)
    lse = jnp.where(head_sel, lse, 0.0)
    o_t = o[0:n_tok]
    l_t = lse[0:n_tok]
    for h in range(1, ATT_HEADS):
        o_t = o_t + o[h * n_tok:(h + 1) * n_tok]
        l_t = l_t + lse[h * n_tok:(h + 1) * n_tok]
    o_ref[bb] = o_t
    l_ref[bb] = l_t


def _attn_sample(q_tiled, cache, kv_new, g, n_tok):
    db, _, _, wb = cache.shape
    n_b = max(1, min(SUBLANES, ATT_GROUPS[-1][0] // wb))
    assert wb % LANES == 0 and db % n_b == 0 and LANES % (n_b * n_tok) == 0 and kv_new.shape[2] % LANES == 0
    b3 = lambda i: (i, 0, 0)
    b4 = lambda i: (i, 0, 0, 0)
    return pl.pallas_call(
        functools.partial(_attn_sample_kernel, g=g, n_tok=n_tok, wb=wb, n_b=n_b),
        grid=(db // n_b,),
        in_specs=[pl.BlockSpec((n_b, ATT_HEADS * n_tok, ATT_W), b3),
                  pl.BlockSpec((n_b, 2, ATT_OUT, wb), b4),
                  pl.BlockSpec((2, ATT_OUT, LANES), lambda i: (0, 0, i * n_b * n_tok // LANES))],
        out_specs=[pl.BlockSpec((n_b, n_tok, ATT_OUT), b3), pl.BlockSpec((n_b, n_tok, ATT_OUT), b3),
                   pl.BlockSpec((n_b, 2, ATT_OUT, wb), b4)],
        out_shape=[jax.ShapeDtypeStruct((db, n_tok, ATT_OUT), F32), jax.ShapeDtypeStruct((db, n_tok, ATT_OUT), F32),
                   jax.ShapeDtypeStruct(cache.shape, F32)],
        compiler_params=_params(),
        name=f"attn_sample_g{g}",
    )(q_tiled, cache, kv_new)


def _merge_kernel(x_ref, on_ref, o0_ref, o1_ref, o2_ref, l0_ref, l1_ref, l2_ref, gpre_ref, wg_ref, wbr_ref,
                  wba_ref, wout_ref, gpost_ref, gffn_ref, x1_ref, h2_ref, *perm, tm, n_split, permute):
    rows_of = [slice(i * tm // n_split, (i + 1) * tm // n_split) for i in range(n_split)]
    n_pair = ATT_OUT // LANES

    def gates(rows):
        x = x_ref[rows, :]
        h = _rms(x, gpre_ref[...]).astype(BF16)
        return [_dot(h, wg_ref[:, c:c + D_MODEL]) for c in (0, RET_V, RET_V + D_MODEL)]

    def branches(rows, rg):
        r = (on_ref[rows, :] * (rg * jax.nn.sigmoid(rg))).astype(BF16)
        atts = []
        for p in range(n_pair):
            l0, l1, l2 = l0_ref[p, rows, :], l1_ref[p, rows, :], l2_ref[p, rows, :]
            mx = jnp.maximum(jnp.maximum(l0, l1), l2)
            e0, e1, e2 = jnp.exp(l0 - mx), jnp.exp(l1 - mx), jnp.exp(l2 - mx)
            att = (e0 * o0_ref[p, rows, :] + e1 * o1_ref[p, rows, :] + e2 * o2_ref[p, rows, :]) / (e0 + e1 + e2)
            atts.append(att.astype(BF16))
        return _dot(r, wbr_ref[...]), _dot(jnp.concatenate(atts, axis=1), wba_ref[...])

    def mixed(gr, ga, br, ba):
        mix = jax.nn.sigmoid(gr) * br + jax.nn.sigmoid(ga) * ba
        return _dot(mix.astype(BF16), wout_ref[...])

    def finish(rows, mo):
        x1 = x_ref[rows, :] + _rms(mo, gpost_ref[...])
        x1_ref[rows, :] = x1
        return _rms(x1, gffn_ref[...])

    g = [gates(rows) for rows in rows_of]
    b = [branches(rows, gi[0]) for rows, gi in zip(rows_of, g)]
    mo = [mixed(gi[1], gi[2], *bi) for gi, bi in zip(g, b)]
    h2 = [finish(rows, moi) for rows, moi in zip(rows_of, mo)]
    if not permute:
        for rows, h2i in zip(rows_of, h2):
            h2_ref[rows, :] = h2i.astype(BF16)
        return
    perm_ref, = perm
    nv = tm // SUBLANES
    assert nv % (tm // n_split) == 0 or (tm // n_split) % nv == 0
    for lc in range(D_MODEL // LANES):
        lanes = slice(lc * LANES, (lc + 1) * LANES)
        for s in range(SUBLANES):
            tok0 = s * nv
            i, off = divmod(tok0, tm // n_split)
            perm_ref[lc, pl.ds(s, nv, stride=SUBLANES), :] = h2[i][off:off + nv, lanes]
        h2_ref[:, lanes] = perm_ref[lc].astype(BF16)


def _merge(x2d, on, outs, lses, g_pre, w_gates, w_br, w_ba, w_out, g_post, g_ffn, tm, permute):
    m = x2d.shape[0]
    n_split = 2 if tm >= 2 * MXU_N else 1
    scratch = [pltpu.VMEM((D_MODEL // LANES, tm, LANES), F32)] if permute else []
    row = lambda i: (i, 0)
    vec = _resident((1, D_MODEL))
    att_spec = pl.BlockSpec((ATT_OUT // LANES, tm, LANES), lambda i: (0, i, 0))
    return pl.pallas_call(
        functools.partial(_merge_kernel, tm=tm, n_split=n_split, permute=permute),
        grid=(m // tm,),
        in_specs=[pl.BlockSpec((tm, D_MODEL), row), pl.BlockSpec((tm, RET_V), row)] + [att_spec] * 6 + [
            vec, _resident(w_gates.shape), _resident(w_br.shape), _resident(w_ba.shape), _resident(w_out.shape),
            vec, vec],
        out_specs=[pl.BlockSpec((tm, D_MODEL), row), pl.BlockSpec((tm, D_MODEL), row)],
        out_shape=[jax.ShapeDtypeStruct((m, D_MODEL), F32), jax.ShapeDtypeStruct((m, D_MODEL), BF16)],
        scratch_shapes=scratch,
        compiler_params=_params(),
        name="merge",
    )(x2d, on, *outs, *lses, g_pre, w_gates, w_br, w_ba, w_out, g_post, g_ffn)


def _ffn_kernel(*refs, tm, tiles_per_seq, seq_tok):
    if seq_tok is None:
        h2_ref, x1_ref, wup_ref, wdn_ref, cw_ref, cb_ref, g_ref, y_ref, cs_ref, carry_ref, f_ref, perm_ref = refs
    else:
        h2_ref, x1_ref, p1_ref, p2_ref, wup_ref, wdn_ref, cw_ref, cb_ref, g_ref, y_ref, u_ref, f_ref = refs
    h2 = h2_ref[...]
    if seq_tok is None:
        sub = lax.broadcasted_iota(jnp.int32, (SUBLANES, FF_CHUNK), 0)

        @pl.when(pl.program_id(0) % tiles_per_seq == 0)
        def _():
            carry_ref[...] = jnp.zeros_like(carry_ref)
    else:
        t = lax.broadcasted_iota(jnp.int32, (tm, FF_CHUNK), 0) & (seq_tok - 1)
        _log2(seq_tok)

    def up(j):
        return [_dot(h2, wup_ref[:, c:c + FF_CHUNK]) for c in (j * FF_CHUNK, D_FF + j * FF_CHUNK)]

    def conv(u, col):
        cols = slice(col, col + FF_CHUNK)
        if seq_tok is None:
            prev = carry_ref[:, cols]
            last2 = jnp.where(sub == 0, prev[SUBLANES - 1:SUBLANES], pltpu.roll(u[tm - 2 * SUBLANES:tm - SUBLANES], 1, 0))
            last1 = jnp.where(sub == 0, prev[2 * SUBLANES - 1:2 * SUBLANES], pltpu.roll(u[tm - SUBLANES:tm], 1, 0))
            u1 = jnp.concatenate([last1, u[0:tm - SUBLANES]], axis=0)
            u2 = jnp.concatenate([last2, last1, u[0:tm - 2 * SUBLANES]], axis=0)
            carry_ref[:, cols] = u[tm - 2 * SUBLANES:tm]
        else:
            u1 = jnp.where(t == 0, p1_ref[:, cols], pltpu.roll(u, 1, 0))
            u2 = jnp.where(t <= 1, p2_ref[:, cols], pltpu.roll(u, 2, 0))
            u_ref[:, cols] = u
        cw = cw_ref[:, cols]
        return cb_ref[:, cols] + cw[0:1] * u2 + cw[1:2] * u1 + cw[2:3] * u

    def gate(j, ug, uv):
        cg = conv(ug, j * FF_CHUNK)
        cv = conv(uv, D_FF + j * FF_CHUNK)
        f_ref[:, j * FF_CHUNK:(j + 1) * FF_CHUNK] = (jax.nn.gelu(cg, approximate=True) * cv).astype(BF16)

    n_chunks = D_FF // FF_CHUNK
    u_next = up(0)
    for j in range(n_chunks):
        u_cur = u_next
        if j + 1 < n_chunks:
            u_next = up(j + 1)
        gate(j, *u_cur)
    out = _rms(_dot(f_ref[...], wdn_ref[...]), g_ref[...])
    if seq_tok is None:
        nv = tm // SUBLANES
        for lc in range(D_MODEL // LANES):
            lanes = slice(lc * LANES, (lc + 1) * LANES)
            perm_ref[lc] = out[:, lanes]
            for s in range(SUBLANES):
                rows = slice(s * nv, (s + 1) * nv)
                y_ref[rows, lanes] = x1_ref[rows, lanes] + perm_ref[lc, pl.ds(s, nv, stride=SUBLANES), :]

        @pl.when(pl.program_id(0) % tiles_per_seq == tiles_per_seq - 1)
        def _():
            cs_ref[0, 0:1, :] = carry_ref[SUBLANES - 1:SUBLANES, :]
            cs_ref[0, 1:2, :] = carry_ref[2 * SUBLANES - 1:2 * SUBLANES, :]
    else:
        y_ref[...] = x1_ref[...] + out


def _ffn_prompt(h2, x1, w_up, w_dn, conv_w, conv_b, g_post, batch, seq, tm):
    m = h2.shape[0]
    tps = seq // tm
    row = lambda i: (i, 0)
    return pl.pallas_call(
        functools.partial(_ffn_kernel, tm=tm, tiles_per_seq=tps, seq_tok=None),
        grid=(m // tm,),
        in_specs=[pl.BlockSpec((tm, D_MODEL), row), pl.BlockSpec((tm, D_MODEL), row),
                  _resident(w_up.shape), _resident(w_dn.shape), _resident(conv_w.shape), _resident(conv_b.shape),
                  _resident((1, D_MODEL))],
        out_specs=[pl.BlockSpec((tm, D_MODEL), row),
                   pl.BlockSpec((1, CONV_W - 1, 2 * D_FF), lambda i: (i // tps, 0, 0))],
        out_shape=[jax.ShapeDtypeStruct((m, D_MODEL), F32),
                   jax.ShapeDtypeStruct((batch, CONV_W - 1, 2 * D_FF), F32)],
        scratch_shapes=[pltpu.VMEM((2 * SUBLANES, 2 * D_FF), F32), pltpu.VMEM((tm, D_FF), BF16),
                        pltpu.VMEM((D_MODEL // LANES, tm, LANES), F32)],
        compiler_params=_params(),
        name="ffn_prompt",
    )(h2, x1, w_up, w_dn, conv_w, conv_b, g_post)


def _ffn_sample(h2, x1, p1, p2, w_up, w_dn, conv_w, conv_b, g_post, n_tok):
    m = h2.shape[0]
    const = lambda i: (0, 0)
    full = lambda a: _resident(a.shape)
    return pl.pallas_call(
        functools.partial(_ffn_kernel, tm=m, tiles_per_seq=1, seq_tok=n_tok),
        grid=(1,),
        in_specs=[full(h2), full(x1), full(p1), full(p2), full(w_up), full(w_dn), full(conv_w), full(conv_b),
                  _resident((1, D_MODEL))],
        out_specs=[pl.BlockSpec((m, D_MODEL), const), pl.BlockSpec((m, 2 * D_FF), const)],
        out_shape=[jax.ShapeDtypeStruct((m, D_MODEL), F32), jax.ShapeDtypeStruct((m, 2 * D_FF), F32)],
        scratch_shapes=[pltpu.VMEM((m, D_FF), BF16)],
        compiler_params=_params(),
        name="ffn_sample",
    )(h2, x1, p1, p2, w_up, w_dn, conv_w, conv_b, g_post)


def _rot_tables(pos):
    pos = pos.astype(F32)
    rf = RET_THETA ** (-jnp.linspace(0.0, 1.0, RET_QK_DIM // 2, dtype=F32))
    ang = pos[:, None] * rf[None, :]
    cos, sin = jnp.cos(ang), jnp.sin(ang)
    cr = jnp.concatenate([cos, cos], axis=-1)
    sr = jnp.concatenate([-sin, sin], axis=-1)
    k_scale = RET_QK_DIM ** -0.5
    half = ROPE_DIM // 2
    af = ROPE_THETA ** (-jnp.arange(half, dtype=F32) / half)
    ang = pos[:, None] * af[None, :]
    cos, sin = jnp.cos(ang), jnp.sin(ang)
    n = pos.shape[0]
    rest = ATT_HEAD_DIM - ROPE_DIM
    zh = jnp.zeros((n, half), F32)
    ca = jnp.concatenate([cos, cos, jnp.ones((n, rest), F32)], axis=-1)
    sa1 = jnp.concatenate([zh, sin, jnp.zeros((n, rest), F32)], axis=-1)
    sa2 = jnp.concatenate([-sin, zh, jnp.zeros((n, rest), F32)], axis=-1)
    rep = MXU_N // ATT_HEAD_DIM
    ca, sa1, sa2 = (jnp.tile(t, (1, rep)) for t in (ca, sa1, sa2))
    return [cr, sr, cr * k_scale, sr * k_scale, ca, sa1, sa2]


def kernel(x_prompt, x_sample, cache_kv_g0, cache_kv_g1, cache_kv_g2, state_ret, state_conv, norm_mix_pre, w_in,
           w_branch_ret, w_branch_attn, w_out, norm_mix_post, norm_ffn_pre, w_ffn_up, conv_w, conv_b, w_ffn_down,
           norm_ffn_post):
    batch, seq, _ = x_prompt.shape
    db, n_tok, _ = x_sample.shape
    depth = w_in.shape[0]
    assert depth == 1 and seq % (ATT_GROUPS[-1][1] * BAND_BLOCK) == 0 and n_tok <= SUBLANES
    caches = (cache_kv_g0, cache_kv_g1, cache_kv_g2)

    wi = w_in[0]
    o_rg = 2 * RET_QK + RET_V
    o_aq = o_rg + RET_V
    o_gr = o_aq + 3 * ATT_W
    w_qkv = jnp.concatenate([wi[:, :o_rg], wi[:, o_aq:o_aq + ATT_W] * ATT_HEAD_DIM ** -0.5,
                             wi[:, o_aq + ATT_W:o_gr]], axis=1).astype(BF16)
    w_gates = jnp.concatenate([wi[:, o_rg:o_aq], wi[:, o_gr:]], axis=1).astype(BF16)
    w_br, w_ba, w_o = w_branch_ret[0].astype(BF16), w_branch_attn[0].astype(BF16), w_out[0].astype(BF16)
    w_up, w_dn = w_ffn_up[0].astype(BF16), w_ffn_down[0].astype(BF16)
    cb = conv_b[0][None, :]
    cw = conv_w[0]
    g_pre, g_post, g_ffn, g_post2 = (t[0][None, :] for t in (norm_mix_pre, norm_mix_post, norm_ffn_pre, norm_ffn_post))

    tm = 512
    xp = x_prompt.reshape(batch * seq, D_MODEL)
    proj_out = _proj(xp, g_pre, w_qkv, _rot_tables(jnp.arange(seq)), tm, seq=seq)
    rq, rk, rv = proj_out[:3]
    qkv_res = proj_out[3:3 + 3 * N_GROUPS]
    kv_last = proj_out[3 + 3 * N_GROUPS:]
    on, p_ret = _ret_prompt(rq, rk, rv, batch, seq)
    outs, lses = [], []
    for g in range(N_GROUPS):
        o_g, l_g = _attn_prompt(*qkv_res[3 * g:3 * g + 3], g, batch, seq)
        outs.append(o_g)
        lses.append(l_g)
    x1, h2 = _merge(xp, on, outs, lses, g_pre, w_gates, w_br, w_ba, w_o, g_post, g_ffn, tm, permute=True)
    y_p, p_conv = _ffn_prompt(h2, x1, w_up, w_dn, cw, cb, g_post2, batch, seq, tm)
    p_kv = [t.reshape(batch, 2, ATT_HEADS, ATT_HEAD_DIM, t.shape[3]).transpose(0, 4, 1, 2, 3)[None] for t in kv_last]

    ms = db * n_tok
    xs = x_sample.reshape(ms, D_MODEL)
    pos_s = jnp.tile(PAST_LEN + jnp.arange(n_tok), db)
    rq, rk, rv, aq, kv0, kv1, kv2 = _proj(xs, g_pre, w_qkv, _rot_tables(pos_s), ms)
    kvs = (kv0, kv1, kv2)
    on, s_ret = _ret_sample(rq, rk, rv, state_ret[0], n_tok)
    q_tiled = jnp.tile(aq.reshape(db, n_tok, ATT_W), (1, ATT_HEADS, 1))
    outs, lses, s_kv = [], [], []
    for g in range(N_GROUPS):
        cache = caches[g][0]
        wb = cache.shape[1]
        cache_t = cache.transpose(0, 2, 3, 4, 1).reshape(db, 2, ATT_OUT, wb)
        kv_new = kvs[g].reshape(ms, 2, ATT_OUT).transpose(1, 2, 0)
        o_g, l_g, nc = _attn_sample(q_tiled, cache_t, kv_new, g, n_tok)
        outs.append(o_g.reshape(ms, ATT_OUT // LANES, LANES).transpose(1, 0, 2))
        lses.append(l_g.reshape(ms, ATT_OUT // LANES, LANES).transpose(1, 0, 2))
        s_kv.append(nc.reshape(db, 2, ATT_HEADS, ATT_HEAD_DIM, wb).transpose(0, 4, 1, 2, 3)[None])
    x1, h2 = _merge(xs, on, outs, lses, g_pre, w_gates, w_br, w_ba, w_o, g_post, g_ffn, ms, permute=False)
    st = state_conv[0]
    zeros = jnp.zeros((db, n_tok - 1, 2 * D_FF), F32)
    p1 = jnp.concatenate([st[:, 1:2], zeros], axis=1).reshape(ms, 2 * D_FF)
    p2 = jnp.concatenate([st, zeros[:, :n_tok - 2]], axis=1).reshape(ms, 2 * D_FF)
    y_s, u_s = _ffn_sample(h2, x1, p1, p2, w_up, w_dn, cw, cb, g_post2, n_tok)
    s_conv = u_s.reshape(db, n_tok, 2 * D_FF)[:, n_tok - (CONV_W - 1):][None]

    return (y_p.reshape(x_prompt.shape), y_s.reshape(x_sample.shape), p_kv[0], p_kv[1], p_kv[2], p_ret[None],
            p_conv[None], s_kv[0], s_kv[1], s_kv[2], s_ret[None], s_conv)
```

```python
import functools
import math

import jax
import jax.numpy as jnp
from jax import lax
from jax.experimental import pallas as pl
from jax.experimental.pallas import tpu as pltpu

F32 = jnp.float32
BF16 = jnp.bfloat16

D_MODEL = 1024
PAST_LEN = 16384
RET_HEADS = 4
RET_QK_DIM = 128
RET_V_DIM = 256
RET_CHUNK = 128
RET_THETA = 10000.0
RET_QK = RET_HEADS * RET_QK_DIM
RET_V = RET_HEADS * RET_V_DIM
ATT_GROUPS = ((128, 1), (512, 4), (2048, 16))
N_GROUPS = 3
ATT_HEADS = 4
ATT_HEAD_DIM = 64
ROPE_DIM = ATT_HEAD_DIM // 4
ROPE_THETA = 500000.0
BAND_BLOCK = 128
ATT_OUT = ATT_HEADS * ATT_HEAD_DIM
ATT_W = N_GROUPS * ATT_OUT
D_FF = 2816
CONV_W = 3
EPS = 1e-6

LANES = 128
SUBLANES = 8
MXU_N = 256
VMEM_LIMIT = 56 * 1024 * 1024
NEG = -1e30
FF_CHUNK = MXU_N
RET_PAD = 2 * SUBLANES
QKV_COLS = 2 * RET_QK + RET_V + 3 * ATT_W


def _rms(x, g):
    return x * lax.rsqrt(jnp.mean(x * x, axis=-1, keepdims=True) + EPS) * g


def _dot(a, b):
    return jnp.dot(a, b, preferred_element_type=F32)


def _dot_nt(a, b):
    return lax.dot_general(a, b, (((1,), (1,)), ((), ())), preferred_element_type=F32)


def _resident(shape):
    return pl.BlockSpec(shape, lambda *_: (0,) * len(shape), pipeline_mode=pl.Buffered(1))


def _log2(n):
    assert n > 0 and n & (n - 1) == 0, n
    return n.bit_length() - 1


def _params(n_axes=1):
    return pltpu.CompilerParams(dimension_semantics=("arbitrary",) * n_axes, vmem_limit_bytes=VMEM_LIMIT)


def _proj_kernel(x_ref, g_ref, w_ref, crq_ref, srq_ref, crk_ref, srk_ref, ca_ref, sa1_ref, sa2_ref, *rest,
                 tm, by_residue, tiles_per_seq):
    rq_ref, rk_ref, rv_ref = rest[:3]
    if by_residue:
        res_refs = rest[3:3 + 3 * N_GROUPS]
        kv_refs = rest[3 + 3 * N_GROUPS:3 + 4 * N_GROUPS]
        scr = rest[3 + 4 * N_GROUPS]
    else:
        aq_ref = rest[3]
        kv_refs = rest[4:4 + N_GROUPS]
    h = _rms(x_ref[...], g_ref[...]).astype(BF16)

    def mm(c0, width=MXU_N):
        return _dot(h, w_ref[:, c0:c0 + width])

    def rot_ret(t, c, s):
        return t * c + pltpu.roll(t, RET_QK_DIM // 2, 1) * s

    ca, sa1, sa2 = ca_ref[...], sa1_ref[...], sa2_ref[...]

    def rot_att(t):
        return t * ca + pltpu.roll(t, ROPE_DIM // 2, 1) * sa1 + pltpu.roll(t, MXU_N - ROPE_DIM // 2, 1) * sa2

    def put_by_residue(ref, val, dil):
        if dil == 1:
            ref[0, 0] = val.astype(BF16)
            return
        for half in range(MXU_N // LANES):
            lanes = slice(half * LANES, (half + 1) * LANES)
            scr[half] = val[:, lanes]
            for r in range(dil):
                ref[0, r, :, lanes] = scr[half, pl.ds(r, tm // dil, stride=dil), :].astype(BF16)

    def put_ret(ref, c, c_ref, s_ref):
        def put(t):
            for half in range(MXU_N // LANES):
                lo = half * LANES
                col = c * MXU_N + lo
                ref[:, col:col + LANES] = rot_ret(t[:, lo:lo + LANES], c_ref[...], s_ref[...]).astype(BF16)
        return put

    def put_rv(c):
        def put(t):
            rv_ref[:, c * MXU_N:(c + 1) * MXU_N] = t.astype(BF16)
        return put

    def put_att(g, which):
        def put(t):
            if which < 2:
                t = rot_att(t)
            if not by_residue:
                if which == 0:
                    aq_ref[:, g * ATT_OUT:(g + 1) * ATT_OUT] = t.astype(BF16)
                else:
                    kv_refs[g][:, (which - 1) * ATT_OUT:which * ATT_OUT] = t
                return
            put_by_residue(res_refs[3 * g + which], t, ATT_GROUPS[g][1])
            if which == 0:
                return
            rows = kv_refs[g].shape[3]

            def put_window():
                kv_refs[g][0, which - 1] = t[tm - rows:tm].T

            if ATT_GROUPS[g][0] >= tiles_per_seq * tm:
                put_window()
            else:
                pl.when(pl.program_id(0) % tiles_per_seq == tiles_per_seq - 1)(put_window)
        return put

    base = 2 * RET_QK + RET_V
    groups = []
    for c in range(RET_QK // MXU_N):
        groups.append([(c * MXU_N, put_ret(rq_ref, c, crq_ref, srq_ref)),
                       (RET_QK + c * MXU_N, put_ret(rk_ref, c, crk_ref, srk_ref))])
    for c in range(RET_V // MXU_N):
        groups.append([(2 * RET_QK + c * MXU_N, put_rv(c))])
    for g in range(N_GROUPS):
        groups.append([(base + which * ATT_W + g * ATT_OUT, put_att(g, which)) for which in range(3)])
    for group in groups:
        results = [mm(col) for col, _ in group]
        for (_, epilogue), t in zip(group, results):
            epilogue(t)


def _proj(x2d, g_pre, w_qkv, tabs, tm, seq=None):
    m = x2d.shape[0]
    n_tab = tabs[0].shape[0] // tm
    row = lambda i: (i, 0)
    tab = lambda i: (i % n_tab, 0)
    in_specs = [pl.BlockSpec((tm, D_MODEL), row), _resident((1, D_MODEL)), _resident((D_MODEL, QKV_COLS))]
    in_specs += [pl.BlockSpec((tm, LANES), tab)] * 4 + [pl.BlockSpec((tm, MXU_N), tab)] * 3
    out_specs = [pl.BlockSpec((tm, w), row) for w in (RET_QK, RET_QK, RET_V)]
    out_shape = [jax.ShapeDtypeStruct((m, w), BF16) for w in (RET_QK, RET_QK, RET_V)]
    scratch = []
    if seq is None:
        widths = (ATT_W,) + (2 * ATT_OUT,) * N_GROUPS
        out_specs += [pl.BlockSpec((tm, w), row) for w in widths]
        out_shape += [jax.ShapeDtypeStruct((m, w), d) for w, d in zip(widths, (BF16,) + (F32,) * N_GROUPS)]
    else:
        batch, tps = m // seq, seq // tm
        for _, dil in ATT_GROUPS:
            assert tm % (dil * 2 * SUBLANES) == 0
            out_specs += [pl.BlockSpec((1, dil, tm // dil, ATT_OUT), lambda i: (i // tps, 0, i % tps, 0))] * 3
            out_shape += [jax.ShapeDtypeStruct((batch, dil, seq // dil, ATT_OUT), BF16)] * 3
        for win, _ in ATT_GROUPS:
            win = min(win, seq)
            if win == seq:
                out_specs.append(pl.BlockSpec((1, 2, ATT_OUT, tm), lambda i: (i // tps, 0, 0, i % tps)))
            else:
                assert win <= tm and win % LANES == 0
                out_specs.append(pl.BlockSpec((1, 2, ATT_OUT, win), lambda i: (i // tps, 0, 0, 0)))
            out_shape.append(jax.ShapeDtypeStruct((batch, 2, ATT_OUT, win), F32))
        scratch = [pltpu.VMEM((MXU_N // LANES, tm, LANES), F32)]
    return pl.pallas_call(
        functools.partial(_proj_kernel, tm=tm, by_residue=seq is not None,
                          tiles_per_seq=None if seq is None else seq // tm),
        grid=(m // tm,),
        in_specs=in_specs,
        out_specs=out_specs,
        out_shape=out_shape,
        scratch_shapes=scratch,
        compiler_params=_params(),
        name="proj",
    )(x2d, g_pre, w_qkv, *tabs)


def _ret_kernel(q_ref, k_ref, v_ref, dec_ref, qd_ref, kd_ref, gc_ref, o_ref, st_ref, s_scr, *, n_chunks):
    s_scr[...] = jnp.zeros_like(s_scr)

    def body(c, carry):
        r0 = pl.multiple_of(c * RET_CHUNK, RET_CHUNK)
        rows = pl.ds(r0, RET_CHUNK)
        first = []
        for h in range(RET_HEADS):
            qk = slice(h * RET_QK_DIM, (h + 1) * RET_QK_DIM)
            vv = slice(h * RET_V_DIM, (h + 1) * RET_V_DIM)
            q = q_ref[rows, qk]
            k = k_ref[rows, qk]
            v = v_ref[rows, vv]
            s0 = s_scr[h]
            sc = _dot_nt(q, k)
            from_state = _dot(q, s0.astype(BF16))
            kd_t = (k.astype(F32) * kd_ref[h]).T.astype(BF16)
            first.append((sc, from_state, _dot(kd_t, v), s0, v))
        for h, (sc, from_state, inc, s0, v) in enumerate(first):
            vv = slice(h * RET_V_DIM, (h + 1) * RET_V_DIM)
            o = _dot((sc * dec_ref[h]).astype(BF16), v) + from_state * qd_ref[h]
            s_scr[h] = s0 * gc_ref[h] + inc
            o_ref[rows, vv] = o * lax.rsqrt(jnp.mean(o * o, axis=-1, keepdims=True) + EPS)
        return carry

    lax.fori_loop(0, n_chunks, body, 0, unroll=2)
    st_ref[0] = s_scr[...]


def _ret_tables(chunk):
    lg = jnp.log1p(-jnp.exp2(-5.0 - jnp.arange(RET_HEADS, dtype=F32)))
    idx = jnp.arange(RET_CHUNK, dtype=F32)
    diff = idx[:, None] - idx[None, :]
    dec = jnp.where(diff[None] >= 0, jnp.exp(lg[:, None, None] * jnp.maximum(diff, 0.0)[None]), 0.0)
    qd = jnp.exp(lg[:, None] * (idx[None, :] + 1.0))
    kd = jnp.exp(lg[:, None] * (chunk - 1.0 - idx)[None, :])
    gc = jnp.exp(lg * chunk)
    qd = jnp.broadcast_to(qd[:, :, None], (RET_HEADS, RET_CHUNK, RET_V_DIM))
    kd = jnp.broadcast_to(kd[:, :, None], (RET_HEADS, RET_CHUNK, RET_QK_DIM))
    gc = jnp.broadcast_to(gc[:, None, None], (RET_HEADS, RET_QK_DIM, RET_V_DIM))
    return dec, qd, kd, gc


def _ret_prompt(rq, rk, rv, batch, seq):
    dec, qd, kd, gc = _ret_tables(RET_CHUNK)
    row = lambda b: (b, 0)
    return pl.pallas_call(
        functools.partial(_ret_kernel, n_chunks=seq // RET_CHUNK),
        grid=(batch,),
        in_specs=[pl.BlockSpec((seq, RET_QK), row), pl.BlockSpec((seq, RET_QK), row), pl.BlockSpec((seq, RET_V), row),
                  _resident(dec.shape), _resident(qd.shape), _resident(kd.shape), _resident(gc.shape)],
        out_specs=[pl.BlockSpec((seq, RET_V), row),
                   pl.BlockSpec((1, RET_HEADS, RET_QK_DIM, RET_V_DIM), lambda b: (b, 0, 0, 0))],
        out_shape=[jax.ShapeDtypeStruct((batch * seq, RET_V), F32),
                   jax.ShapeDtypeStruct((batch, RET_HEADS, RET_QK_DIM, RET_V_DIM), F32)],
        scratch_shapes=[pltpu.VMEM((RET_HEADS, RET_QK_DIM, RET_V_DIM), F32)],
        compiler_params=_params(),
        name="ret_prompt",
    )(rq, rk, rv, dec, qd, kd, gc)


def _ret_sample_kernel(q_ref, k_ref, kt_ref, v_ref, s_ref, dec_ref, qd_ref, kd_ref, gc_ref, o_ref, st_ref,
                       *, n_tok, n_b):
    pairs = [(b, h) for b in range(n_b) for h in range(RET_HEADS)]
    qk_of = lambda h: slice(h * RET_QK_DIM, (h + 1) * RET_QK_DIM)
    vv_of = lambda h: slice(h * RET_V_DIM, (h + 1) * RET_V_DIM)
    zeros = jnp.zeros((LANES - RET_PAD, RET_V_DIM), BF16)
    first = []
    for b, h in pairs:
        q = q_ref[b, :, qk_of(h)]
        k = k_ref[b, :, qk_of(h)]
        v = v_ref[b, :, vv_of(h)]
        s0 = s_ref[b, h]
        from_state = _dot(q.astype(BF16), s0.astype(BF16))
        k_dec_t = (kt_ref[b, h] * kd_ref[h]).astype(BF16)
        v_rows = jnp.concatenate([v.astype(BF16), zeros], axis=0)
        st_ref[b, h] = s0 * gc_ref[h] + _dot(k_dec_t, v_rows)
        scores = [jnp.sum(q * k[i:i + 1, :], axis=-1, keepdims=True) for i in range(n_tok)]
        first.append((from_state, scores))
    second = []
    for (b, h), (from_state, scores) in zip(pairs, first):
        v = v_ref[b, :, vv_of(h)]
        o = from_state * qd_ref[h]
        for i, sc_i in enumerate(scores):
            o = o + (sc_i * dec_ref[h][:, i:i + 1]) * v[i:i + 1, :]
        second.append((o, jnp.mean(o * o, axis=-1, keepdims=True)))
    for (b, h), (o, ms) in zip(pairs, second):
        o_ref[b, :, vv_of(h)] = o * lax.rsqrt(ms + EPS)


def _ret_sample(rq, rk, rv, state, n_tok):
    db = state.shape[0]
    pad = RET_PAD - n_tok
    n_b = 8

    def pad_rows(t):
        t = t.astype(F32).reshape(db, n_tok, t.shape[-1])
        return jnp.pad(t, ((0, 0), (0, pad), (0, 0)))

    q, k, v = pad_rows(rq), pad_rows(rk), pad_rows(rv)
    kt = jnp.pad(k, ((0, 0), (0, LANES - RET_PAD), (0, 0)))
    kt = kt.reshape(db, LANES, RET_HEADS, RET_QK_DIM).transpose(0, 2, 3, 1)
    lg = jnp.log1p(-jnp.exp2(-5.0 - jnp.arange(RET_HEADS, dtype=F32)))
    idx = jnp.arange(RET_PAD, dtype=F32)
    diff = idx[:, None] - idx[None, :]
    dec = jnp.where(diff[None] >= 0, jnp.exp(lg[:, None, None] * jnp.maximum(diff, 0.0)[None]), 0.0)
    qd = jnp.broadcast_to(jnp.exp(lg[:, None] * (idx[None, :] + 1.0))[:, :, None], (RET_HEADS, RET_PAD, RET_V_DIM))
    kd = jnp.exp(lg[:, None] * (n_tok - 1.0 - jnp.arange(LANES, dtype=F32))[None, :])
    kd = jnp.broadcast_to(kd[:, None, :], (RET_HEADS, RET_QK_DIM, LANES))
    gc = jnp.broadcast_to(jnp.exp(lg * n_tok)[:, None, None], (RET_HEADS, RET_QK_DIM, RET_V_DIM))
    b3 = lambda i: (i, 0, 0)
    b4 = lambda i: (i, 0, 0, 0)
    o, st = pl.pallas_call(
        functools.partial(_ret_sample_kernel, n_tok=n_tok, n_b=n_b),
        grid=(db // n_b,),
        in_specs=[pl.BlockSpec((n_b, RET_PAD, RET_QK), b3), pl.BlockSpec((n_b, RET_PAD, RET_QK), b3),
                  pl.BlockSpec((n_b, RET_HEADS, RET_QK_DIM, LANES), b4),
                  pl.BlockSpec((n_b, RET_PAD, RET_V), b3),
                  pl.BlockSpec((n_b, RET_HEADS, RET_QK_DIM, RET_V_DIM), b4),
                  _resident(dec.shape), _resident(qd.shape), _resident(kd.shape), _resident(gc.shape)],
        out_specs=[pl.BlockSpec((n_b, RET_PAD, RET_V), b3),
                   pl.BlockSpec((n_b, RET_HEADS, RET_QK_DIM, RET_V_DIM), b4)],
        out_shape=[jax.ShapeDtypeStruct((db, RET_PAD, RET_V), F32),
                   jax.ShapeDtypeStruct(state.shape, F32)],
        compiler_params=_params(),
        name="ret_sample",
    )(q, k, kt, v, state, dec, qd, kd, gc)
    return o[:, :n_tok].reshape(db * n_tok, RET_V), st


def _attn_kernel(q_ref, k_ref, v_ref, o_ref, l_ref, vt_scr, bias_scr, *staging, dil, nb):
    n_blk = dil * nb
    length = nb * BAND_BLOCK
    blk = BAND_BLOCK
    o_dst, l_dst = staging if dil > 1 else (o_ref, l_ref)

    kk = lax.broadcasted_iota(jnp.int32, (2 * blk, 2 * LANES), 0)
    ql = lax.broadcasted_iota(jnp.int32, (2 * blk, 2 * LANES), 1) & (blk - 1)
    in_span = kk <= ql + blk
    bias_scr[...] = jnp.where(in_span & (kk >= ql), 0.0, NEG)

    def transpose_values(c):
        vt_scr[c] = v_ref[0, c * blk:(c + 1) * blk, :].astype(F32).T.astype(BF16)

    lane = lax.broadcasted_iota(jnp.int32, (blk, LANES), 1)

    def scores(c, p):
        lanes = slice(p * LANES, (p + 1) * LANES)
        k0 = 0 if c % nb == 0 else blk
        q128 = q_ref[0, c * blk:(c + 1) * blk, lanes]
        zero = jnp.zeros_like(q128)
        q_pair = jnp.concatenate([jnp.where(lane < ATT_HEAD_DIM, q128, zero),
                                  jnp.where(lane >= ATT_HEAD_DIM, q128, zero)], axis=0)
        keys = k_ref[0, c * blk - k0:(c + 1) * blk, lanes]
        return _dot_nt(keys, q_pair) + bias_scr[blk - k0:2 * blk, :]

    def weights_values(c, p, s_t):
        lanes = slice(p * LANES, (p + 1) * LANES)
        m = jnp.max(s_t, axis=0, keepdims=True)
        p_t = jnp.exp(s_t - m).astype(BF16)
        v_t = vt_scr[c, lanes, :]
        if c % nb:
            v_t = jnp.concatenate([vt_scr[c - 1, lanes, :], v_t], axis=1)
        v_t = jnp.concatenate([v_t, jnp.ones((2 * SUBLANES, v_t.shape[1]), BF16)], axis=0)
        return _dot(v_t, p_t), m

    def finish(c, p, o_t, m):
        den = o_t[LANES:LANES + 1, :]
        lse = m + jnp.log(den)
        o_pair = jnp.concatenate([o_t[0:ATT_HEAD_DIM, 0:blk] / den[:, 0:blk],
                                  o_t[ATT_HEAD_DIM:LANES, blk:2 * blk] / den[:, blk:2 * blk]], axis=0)
        l_pair = jnp.concatenate([jnp.broadcast_to(lse[:, 0:blk], (ATT_HEAD_DIM, blk)),
                                  jnp.broadcast_to(lse[:, blk:2 * blk], (ATT_HEAD_DIM, blk))], axis=0)
        o_dst[p, c * blk:(c + 1) * blk, :] = o_pair.T
        l_dst[p, c * blk:(c + 1) * blk, :] = l_pair.T

    chains = [(c, p) for c in range(n_blk) for p in range(ATT_OUT // LANES)]
    transpose_values(0)
    s_next = scores(*chains[0])
    pending = None
    for i, chain in enumerate(chains):
        s_cur = s_next
        if chain[1] == 0 and chain[0] + 1 < n_blk:
            transpose_values(chain[0] + 1)
        if i + 1 < len(chains):
            s_next = scores(*chains[i + 1])
        o_m = weights_values(*chain, s_cur)
        if pending is not None:
            finish(*pending)
        pending = chain + o_m
    finish(*pending)

    if dil > 1:
        for p in range(ATT_OUT // LANES):
            for r in range(dil):
                o_ref[p, pl.ds(r, length, stride=dil), :] = o_dst[p, r * length:(r + 1) * length, :]
                l_ref[p, pl.ds(r, length, stride=dil), :] = l_dst[p, r * length:(r + 1) * length, :]


def _attn_prompt(q, k, v, g, batch, seq):
    _, dil = ATT_GROUPS[g]
    nb = seq // dil // BAND_BLOCK
    _log2(nb)
    q, k, v = (t.reshape(batch, seq, ATT_OUT) for t in (q, k, v))
    in_spec = pl.BlockSpec((1, seq, ATT_OUT), lambda b: (b, 0, 0))
    n_pair = ATT_OUT // LANES
    out_spec = pl.BlockSpec((n_pair, seq, LANES), lambda b: (0, b, 0))
    scratch = [pltpu.VMEM((dil * nb, ATT_OUT, BAND_BLOCK), BF16), pltpu.VMEM((2 * BAND_BLOCK, 2 * LANES), F32)]
    if dil > 1:
        scratch += [pltpu.VMEM((n_pair, seq, LANES), F32)] * 2
    return pl.pallas_call(
        functools.partial(_attn_kernel, dil=dil, nb=nb),
        grid=(batch,),
        in_specs=[in_spec] * 3,
        out_specs=[out_spec] * 2,
        out_shape=[jax.ShapeDtypeStruct((n_pair, batch * seq, LANES), F32)] * 2,
        scratch_shapes=scratch,
        compiler_params=_params(),
        name=f"attn_prompt_g{g}",
    )(q, k, v)


def _attn_sample_kernel(q_ref, c_ref, n_ref, o_ref, l_ref, nc_ref, *, g, n_tok, wb, n_b):
    for bb in range(n_b):
        _attn_sample_one(q_ref, c_ref, n_ref, o_ref, l_ref, nc_ref, bb, g=g, n_tok=n_tok, wb=wb, n_b=n_b)


def _attn_sample_one(q_ref, c_ref, n_ref, o_ref, l_ref, nc_ref, bb, *, g, n_tok, wb, n_b):
    win, dil = ATT_GROUPS[g]
    nk = win // dil
    _log2(dil)
    n_col = ATT_HEADS * n_tok
    first_new = LANES - n_tok
    seq_lane = ((pl.program_id(0) * n_b + bb) * n_tok) % LANES
    new = [pltpu.roll(n_ref[kv], (first_new + LANES - seq_lane) % LANES, 1) for kv in range(2)]
    lane_t = lax.broadcasted_iota(jnp.int32, (ATT_OUT, LANES), 1)
    for kv in range(2):
        shifted = pltpu.roll(c_ref[bb, kv], wb - n_tok, 1)
        if wb > LANES:
            nc_ref[bb, kv, :, 0:wb - LANES] = shifted[:, 0:wb - LANES]
        nc_ref[bb, kv, :, wb - LANES:wb] = jnp.where(lane_t < first_new, shifted[:, wb - LANES:wb], new[kv])

    q = q_ref[bb][:, g * ATT_OUT:(g + 1) * ATT_OUT]
    row = lax.broadcasted_iota(jnp.int32, (n_col, ATT_OUT), 0)
    lane = lax.broadcasted_iota(jnp.int32, (n_col, ATT_OUT), 1)
    head_sel = (lane >> _log2(ATT_HEAD_DIM)) == (row >> _log2(n_tok))
    qbd = jnp.where(head_sel, q, jnp.zeros_like(q))

    s_c = _dot(qbd, c_ref[bb, 0].astype(BF16))
    s_n = _dot(qbd, new[0].astype(BF16))
    t_c = lax.broadcasted_iota(jnp.int32, (n_col, wb), 0) & (n_tok - 1)
    d_c = wb + t_c - lax.broadcasted_iota(jnp.int32, (n_col, wb), 1)
    ok_c = ((d_c & (dil - 1)) == 0) & (d_c <= nk * dil)
    t_n = lax.broadcasted_iota(jnp.int32, (n_col, LANES), 0) & (n_tok - 1)
    new_idx = lax.broadcasted_iota(jnp.int32, (n_col, LANES), 1) - first_new
    d_n = t_n - new_idx
    ok_n = (new_idx >= 0) & (d_n >= 0) & ((d_n & (dil - 1)) == 0) & (d_n <= nk * dil)
    s_c = jnp.where(ok_c, s_c, NEG)
    s_n = jnp.where(ok_n, s_n, NEG)
    m = jnp.maximum(jnp.max(s_c, axis=-1, keepdims=True), jnp.max(s_n, axis=-1, keepdims=True))
    p_c = jnp.exp(s_c - m)
    p_n = jnp.exp(s_n - m)
    den = jnp.sum(p_c, axis=-1, keepdims=True) + jnp.sum(p_n, axis=-1, keepdims=True)
    o = (_dot_nt(p_c.astype(BF16), c_ref[bb, 1].astype(BF16))
         + _dot_nt(p_n.astype(BF16), new[1].astype(BF16))) / den
    lse = jnp.broadcast_to(m + jnp.log(den), (n_col, ATT_OUT))
    o = jnp.where(head_sel, o, 0.0)
    lse = jnp.where(head_sel, lse, 0.0)
    o_t = o[0:n_tok]
    l_t = lse[0:n_tok]
    for h in range(1, ATT_HEADS):
        o_t = o_t + o[h * n_tok:(h + 1) * n_tok]
        l_t = l_t + lse[h * n_tok:(h + 1) * n_tok]
    o_ref[bb] = o_t
    l_ref[bb] = l_t


def _attn_sample(q_tiled, cache, kv_new, g, n_tok):
    db, _, _, wb = cache.shape
    n_b = max(1, min(SUBLANES, ATT_GROUPS[-1][0] // wb))
    assert wb % LANES == 0 and db % n_b == 0 and LANES % (n_b * n_tok) == 0 and kv_new.shape[2] % LANES == 0
    b3 = lambda i: (i, 0, 0)
    b4 = lambda i: (i, 0, 0, 0)
    return pl.pallas_call(
        functools.partial(_attn_sample_kernel, g=g, n_tok=n_tok, wb=wb, n_b=n_b),
        grid=(db // n_b,),
        in_specs=[pl.BlockSpec((n_b, ATT_HEADS * n_tok, ATT_W), b3),
                  pl.BlockSpec((n_b, 2, ATT_OUT, wb), b4),
                  pl.BlockSpec((2, ATT_OUT, LANES), lambda i: (0, 0, i * n_b * n_tok // LANES))],
        out_specs=[pl.BlockSpec((n_b, n_tok, ATT_OUT), b3), pl.BlockSpec((n_b, n_tok, ATT_OUT), b3),
                   pl.BlockSpec((n_b, 2, ATT_OUT, wb), b4)],
        out_shape=[jax.ShapeDtypeStruct((db, n_tok, ATT_OUT), F32), jax.ShapeDtypeStruct((db, n_tok, ATT_OUT), F32),
                   jax.ShapeDtypeStruct(cache.shape, F32)],
        compiler_params=_params(),
        name=f"attn_sample_g{g}",
    )(q_tiled, cache, kv_new)


def _merge_kernel(x_ref, on_ref, o0_ref, o1_ref, o2_ref, l0_ref, l1_ref, l2_ref, gpre_ref, wg_ref, wbr_ref,
                  wba_ref, wout_ref, gpost_ref, gffn_ref, x1_ref, h2_ref, *perm, tm, n_split, permute):
    rows_of = [slice(i * tm // n_split, (i + 1) * tm // n_split) for i in range(n_split)]
    n_pair = ATT_OUT // LANES

    def gates(rows):
        x = x_ref[rows, :]
        h = _rms(x, gpre_ref[...]).astype(BF16)
        return [_dot(h, wg_ref[:, c:c + D_MODEL]) for c in (0, RET_V, RET_V + D_MODEL)]

    def branches(rows, rg):
        r = (on_ref[rows, :] * (rg * jax.nn.sigmoid(rg))).astype(BF16)
        atts = []
        for p in range(n_pair):
            l0, l1, l2 = l0_ref[p, rows, :], l1_ref[p, rows, :], l2_ref[p, rows, :]
            mx = jnp.maximum(jnp.maximum(l0, l1), l2)
            e0, e1, e2 = jnp.exp(l0 - mx), jnp.exp(l1 - mx), jnp.exp(l2 - mx)
            att = (e0 * o0_ref[p, rows, :] + e1 * o1_ref[p, rows, :] + e2 * o2_ref[p, rows, :]) / (e0 + e1 + e2)
            atts.append(att.astype(BF16))
        return _dot(r, wbr_ref[...]), _dot(jnp.concatenate(atts, axis=1), wba_ref[...])

    def mixed(gr, ga, br, ba):
        mix = jax.nn.sigmoid(gr) * br + jax.nn.sigmoid(ga) * ba
        return _dot(mix.astype(BF16), wout_ref[...])

    def finish(rows, mo):
        x1 = x_ref[rows, :] + _rms(mo, gpost_ref[...])
        x1_ref[rows, :] = x1
        return _rms(x1, gffn_ref[...])

    g = [gates(rows) for rows in rows_of]
    b = [branches(rows, gi[0]) for rows, gi in zip(rows_of, g)]
    mo = [mixed(gi[1], gi[2], *bi) for gi, bi in zip(g, b)]
    h2 = [finish(rows, moi) for rows, moi in zip(rows_of, mo)]
    if not permute:
        for rows, h2i in zip(rows_of, h2):
            h2_ref[rows, :] = h2i.astype(BF16)
        return
    perm_ref, = perm
    nv = tm // SUBLANES
    assert nv % (tm // n_split) == 0 or (tm // n_split) % nv == 0
    for lc in range(D_MODEL // LANES):
        lanes = slice(lc * LANES, (lc + 1) * LANES)
        for s in range(SUBLANES):
            tok0 = s * nv
            i, off = divmod(tok0, tm // n_split)
            perm_ref[lc, pl.ds(s, nv, stride=SUBLANES), :] = h2[i][off:off + nv, lanes]
        h2_ref[:, lanes] = perm_ref[lc].astype(BF16)


def _merge(x2d, on, outs, lses, g_pre, w_gates, w_br, w_ba, w_out, g_post, g_ffn, tm, permute):
    m = x2d.shape[0]
    n_split = 2 if tm >= 2 * MXU_N else 1
    scratch = [pltpu.VMEM((D_MODEL // LANES, tm, LANES), F32)] if permute else []
    row = lambda i: (i, 0)
    vec = _resident((1, D_MODEL))
    att_spec = pl.BlockSpec((ATT_OUT // LANES, tm, LANES), lambda i: (0, i, 0))
    return pl.pallas_call(
        functools.partial(_merge_kernel, tm=tm, n_split=n_split, permute=permute),
        grid=(m // tm,),
        in_specs=[pl.BlockSpec((tm, D_MODEL), row), pl.BlockSpec((tm, RET_V), row)] + [att_spec] * 6 + [
            vec, _resident(w_gates.shape), _resident(w_br.shape), _resident(w_ba.shape), _resident(w_out.shape),
            vec, vec],
        out_specs=[pl.BlockSpec((tm, D_MODEL), row), pl.BlockSpec((tm, D_MODEL), row)],
        out_shape=[jax.ShapeDtypeStruct((m, D_MODEL), F32), jax.ShapeDtypeStruct((m, D_MODEL), BF16)],
        scratch_shapes=scratch,
        compiler_params=_params(),
        name="merge",
    )(x2d, on, *outs, *lses, g_pre, w_gates, w_br, w_ba, w_out, g_post, g_ffn)


def _ffn_kernel(*refs, tm, tiles_per_seq, seq_tok):
    if seq_tok is None:
        h2_ref, x1_ref, wup_ref, wdn_ref, cw_ref, cb_ref, g_ref, y_ref, cs_ref, carry_ref, f_ref, perm_ref = refs
    else:
        h2_ref, x1_ref, p1_ref, p2_ref, wup_ref, wdn_ref, cw_ref, cb_ref, g_ref, y_ref, u_ref, f_ref = refs
    h2 = h2_ref[...]
    if seq_tok is None:
        sub = lax.broadcasted_iota(jnp.int32, (SUBLANES, FF_CHUNK), 0)

        @pl.when(pl.program_id(0) % tiles_per_seq == 0)
        def _():
            carry_ref[...] = jnp.zeros_like(carry_ref)
    else:
        t = lax.broadcasted_iota(jnp.int32, (tm, FF_CHUNK), 0) & (seq_tok - 1)
        _log2(seq_tok)

    def up(j):
        return [_dot(h2, wup_ref[:, c:c + FF_CHUNK]) for c in (j * FF_CHUNK, D_FF + j * FF_CHUNK)]

    def conv(u, col):
        cols = slice(col, col + FF_CHUNK)
        if seq_tok is None:
            prev = carry_ref[:, cols]
            last2 = jnp.where(sub == 0, prev[SUBLANES - 1:SUBLANES], pltpu.roll(u[tm - 2 * SUBLANES:tm - SUBLANES], 1, 0))
            last1 = jnp.where(sub == 0, prev[2 * SUBLANES - 1:2 * SUBLANES], pltpu.roll(u[tm - SUBLANES:tm], 1, 0))
            u1 = jnp.concatenate([last1, u[0:tm - SUBLANES]], axis=0)
            u2 = jnp.concatenate([last2, last1, u[0:tm - 2 * SUBLANES]], axis=0)
            carry_ref[:, cols] = u[tm - 2 * SUBLANES:tm]
        else:
            u1 = jnp.where(t == 0, p1_ref[:, cols], pltpu.roll(u, 1, 0))
            u2 = jnp.where(t <= 1, p2_ref[:, cols], pltpu.roll(u, 2, 0))
            u_ref[:, cols] = u
        cw = cw_ref[:, cols]
        return cb_ref[:, cols] + cw[0:1] * u2 + cw[1:2] * u1 + cw[2:3] * u

    def gate(j, ug, uv):
        cg = conv(ug, j * FF_CHUNK)
        cv = conv(uv, D_FF + j * FF_CHUNK)
        f_ref[:, j * FF_CHUNK:(j + 1) * FF_CHUNK] = (jax.nn.gelu(cg, approximate=True) * cv).astype(BF16)

    n_chunks = D_FF // FF_CHUNK
    u_next = up(0)
    for j in range(n_chunks):
        u_cur = u_next
        if j + 1 < n_chunks:
            u_next = up(j + 1)
        gate(j, *u_cur)
    out = _rms(_dot(f_ref[...], wdn_ref[...]), g_ref[...])
    if seq_tok is None:
        nv = tm // SUBLANES
        for lc in range(D_MODEL // LANES):
            lanes = slice(lc * LANES, (lc + 1) * LANES)
            perm_ref[lc] = out[:, lanes]
            for s in range(SUBLANES):
                rows = slice(s * nv, (s + 1) * nv)
                y_ref[rows, lanes] = x1_ref[rows, lanes] + perm_ref[lc, pl.ds(s, nv, stride=SUBLANES), :]

        @pl.when(pl.program_id(0) % tiles_per_seq == tiles_per_seq - 1)
        def _():
            cs_ref[0, 0:1, :] = carry_ref[SUBLANES - 1:SUBLANES, :]
            cs_ref[0, 1:2, :] = carry_ref[2 * SUBLANES - 1:2 * SUBLANES, :]
    else:
        y_ref[...] = x1_ref[...] + out


def _ffn_prompt(h2, x1, w_up, w_dn, conv_w, conv_b, g_post, batch, seq, tm):
    m = h2.shape[0]
    tps = seq // tm
    row = lambda i: (i, 0)
    return pl.pallas_call(
        functools.partial(_ffn_kernel, tm=tm, tiles_per_seq=tps, seq_tok=None),
        grid=(m // tm,),
        in_specs=[pl.BlockSpec((tm, D_MODEL), row), pl.BlockSpec((tm, D_MODEL), row),
                  _resident(w_up.shape), _resident(w_dn.shape), _resident(conv_w.shape), _resident(conv_b.shape),
                  _resident((1, D_MODEL))],
        out_specs=[pl.BlockSpec((tm, D_MODEL), row),
                   pl.BlockSpec((1, CONV_W - 1, 2 * D_FF), lambda i: (i // tps, 0, 0))],
        out_shape=[jax.ShapeDtypeStruct((m, D_MODEL), F32),
                   jax.ShapeDtypeStruct((batch, CONV_W - 1, 2 * D_FF), F32)],
        scratch_shapes=[pltpu.VMEM((2 * SUBLANES, 2 * D_FF), F32), pltpu.VMEM((tm, D_FF), BF16),
                        pltpu.VMEM((D_MODEL // LANES, tm, LANES), F32)],
        compiler_params=_params(),
        name="ffn_prompt",
    )(h2, x1, w_up, w_dn, conv_w, conv_b, g_post)


def _ffn_sample(h2, x1, p1, p2, w_up, w_dn, conv_w, conv_b, g_post, n_tok):
    m = h2.shape[0]
    const = lambda i: (0, 0)
    full = lambda a: _resident(a.shape)
    return pl.pallas_call(
        functools.partial(_ffn_kernel, tm=m, tiles_per_seq=1, seq_tok=n_tok),
        grid=(1,),
        in_specs=[full(h2), full(x1), full(p1), full(p2), full(w_up), full(w_dn), full(conv_w), full(conv_b),
                  _resident((1, D_MODEL))],
        out_specs=[pl.BlockSpec((m, D_MODEL), const), pl.BlockSpec((m, 2 * D_FF), const)],
        out_shape=[jax.ShapeDtypeStruct((m, D_MODEL), F32), jax.ShapeDtypeStruct((m, 2 * D_FF), F32)],
        scratch_shapes=[pltpu.VMEM((m, D_FF), BF16)],
        compiler_params=_params(),
        name="ffn_sample",
    )(h2, x1, p1, p2, w_up, w_dn, conv_w, conv_b, g_post)


def _rot_tables(pos):
    pos = pos.astype(F32)
    rf = RET_THETA ** (-jnp.linspace(0.0, 1.0, RET_QK_DIM // 2, dtype=F32))
    ang = pos[:, None] * rf[None, :]
    cos, sin = jnp.cos(ang), jnp.sin(ang)
    cr = jnp.concatenate([cos, cos], axis=-1)
    sr = jnp.concatenate([-sin, sin], axis=-1)
    k_scale = RET_QK_DIM ** -0.5
    half = ROPE_DIM // 2
    af = ROPE_THETA ** (-jnp.arange(half, dtype=F32) / half)
    ang = pos[:, None] * af[None, :]
    cos, sin = jnp.cos(ang), jnp.sin(ang)
    n = pos.shape[0]
    rest = ATT_HEAD_DIM - ROPE_DIM
    zh = jnp.zeros((n, half), F32)
    ca = jnp.concatenate([cos, cos, jnp.ones((n, rest), F32)], axis=-1)
    sa1 = jnp.concatenate([zh, sin, jnp.zeros((n, rest), F32)], axis=-1)
    sa2 = jnp.concatenate([-sin, zh, jnp.zeros((n, rest), F32)], axis=-1)
    rep = MXU_N // ATT_HEAD_DIM
    ca, sa1, sa2 = (jnp.tile(t, (1, rep)) for t in (ca, sa1, sa2))
    return [cr, sr, cr * k_scale, sr * k_scale, ca, sa1, sa2]


def kernel(x_prompt, x_sample, cache_kv_g0, cache_kv_g1, cache_kv_g2, state_ret, state_conv, norm_mix_pre, w_in,
           w_branch_ret, w_branch_attn, w_out, norm_mix_post, norm_ffn_pre, w_ffn_up, conv_w, conv_b, w_ffn_down,
           norm_ffn_post):
    batch, seq, _ = x_prompt.shape
    db, n_tok, _ = x_sample.shape
    depth = w_in.shape[0]
    assert depth == 1 and seq % (ATT_GROUPS[-1][1] * BAND_BLOCK) == 0 and n_tok <= SUBLANES
    caches = (cache_kv_g0, cache_kv_g1, cache_kv_g2)

    wi = w_in[0]
    o_rg = 2 * RET_QK + RET_V
    o_aq = o_rg + RET_V
    o_gr = o_aq + 3 * ATT_W
    w_qkv = jnp.concatenate([wi[:, :o_rg], wi[:, o_aq:o_aq + ATT_W] * ATT_HEAD_DIM ** -0.5,
                             wi[:, o_aq + ATT_W:o_gr]], axis=1).astype(BF16)
    w_gates = jnp.concatenate([wi[:, o_rg:o_aq], wi[:, o_gr:]], axis=1).astype(BF16)
    w_br, w_ba, w_o = w_branch_ret[0].astype(BF16), w_branch_attn[0].astype(BF16), w_out[0].astype(BF16)
    w_up, w_dn = w_ffn_up[0].astype(BF16), w_ffn_down[0].astype(BF16)
    cb = conv_b[0][None, :]
    cw = conv_w[0]
    g_pre, g_post, g_ffn, g_post2 = (t[0][None, :] for t in (norm_mix_pre, norm_mix_post, norm_ffn_pre, norm_ffn_post))

    tm = 512
    xp = x_prompt.reshape(batch * seq, D_MODEL)
    proj_out = _proj(xp, g_pre, w_qkv, _rot_tables(jnp.arange(seq)), tm, seq=seq)
    rq, rk, rv = proj_out[:3]
    qkv_res = proj_out[3:3 + 3 * N_GROUPS]
    kv_last = proj_out[3 + 3 * N_GROUPS:]
    on, p_ret = _ret_prompt(rq, rk, rv, batch, seq)
    outs, lses = [], []
    for g in range(N_GROUPS):
        o_g, l_g = _attn_prompt(*qkv_res[3 * g:3 * g + 3], g, batch, seq)
        outs.append(o_g)
        lses.append(l_g)
    x1, h2 = _merge(xp, on, outs, lses, g_pre, w_gates, w_br, w_ba, w_o, g_post, g_ffn, tm, permute=True)
    y_p, p_conv = _ffn_prompt(h2, x1, w_up, w_dn, cw, cb, g_post2, batch, seq, tm)
    p_kv = [t.reshape(batch, 2, ATT_HEADS, ATT_HEAD_DIM, t.shape[3]).transpose(0, 4, 1, 2, 3)[None] for t in kv_last]

    ms = db * n_tok
    xs = x_sample.reshape(ms, D_MODEL)
    pos_s = jnp.tile(PAST_LEN + jnp.arange(n_tok), db)
    rq, rk, rv, aq, kv0, kv1, kv2 = _proj(xs, g_pre, w_qkv, _rot_tables(pos_s), ms)
    kvs = (kv0, kv1, kv2)
    on, s_ret = _ret_sample(rq, rk, rv, state_ret[0], n_tok)
    q_tiled = jnp.tile(aq.reshape(db, n_tok, ATT_W), (1, ATT_HEADS, 1))
    outs, lses, s_kv = [], [], []
    for g in range(N_GROUPS):
        cache = caches[g][0]
        wb = cache.shape[1]
        cache_t = cache.transpose(0, 2, 3, 4, 1).reshape(db, 2, ATT_OUT, wb)
        kv_new = kvs[g].reshape(ms, 2, ATT_OUT).transpose(1, 2, 0)
        o_g, l_g, nc = _attn_sample(q_tiled, cache_t, kv_new, g, n_tok)
        outs.append(o_g.reshape(ms, ATT_OUT // LANES, LANES).transpose(1, 0, 2))
        lses.append(l_g.reshape(ms, ATT_OUT // LANES, LANES).transpose(1, 0, 2))
        s_kv.append(nc.reshape(db, 2, ATT_HEADS, ATT_HEAD_DIM, wb).transpose(0, 4, 1, 2, 3)[None])
    x1, h2 = _merge(xs, on, outs, lses, g_pre, w_gates, w_br, w_ba, w_o, g_post, g_ffn, ms, permute=False)
    st = state_conv[0]
    zeros = jnp.zeros((db, n_tok - 1, 2 * D_FF), F32)
    p1 = jnp.concatenate([st[:, 1:2], zeros], axis=1).reshape(ms, 2 * D_FF)
    p2 = jnp.concatenate([st, zeros[:, :n_tok - 2]], axis=1).reshape(ms, 2 * D_FF)
    y_s, u_s = _ffn_sample(h2, x1, p1, p2, w_up, w_dn, cw, cb, g_post2, n_tok)
    s_conv = u_s.reshape(db, n_tok, 2 * D_FF)[:, n_tok - (CONV_W - 1):][None]

    return (y_p.reshape(x_prompt.shape), y_s.reshape(x_sample.shape), p_kv[0], p_kv[1], p_kv[2], p_ret[None],
            p_conv[None], s_kv[0], s_kv[1], s_kv[2], s_ret[None], s_conv)
```

```python
import functools

import numpy as np
import jax
import jax.numpy as jnp
from jax import lax
from jax.experimental import pallas as pl
from jax.experimental.pallas import tpu as pltpu

F32 = jnp.float32
BF16 = jnp.bfloat16

D_MODEL = 1024
PAST_LEN = 16384
RET_HEADS = 4
RET_QK_DIM = 128
RET_V_DIM = 256
RET_CHUNK = 128
RET_THETA = 10000.0
RET_QK = RET_HEADS * RET_QK_DIM
RET_V = RET_HEADS * RET_V_DIM
ATT_GROUPS = ((128, 1), (512, 4), (2048, 16))
N_GROUPS = 3
ATT_HEADS = 4
ATT_HEAD_DIM = 64
ROPE_DIM = ATT_HEAD_DIM // 4
ROPE_THETA = 500000.0
BAND_BLOCK = 128
ATT_OUT = ATT_HEADS * ATT_HEAD_DIM
ATT_W = N_GROUPS * ATT_OUT
D_FF = 2816
CONV_W = 3
EPS = 1e-6

LANES = 128
SUBLANES = 8
MXU_N = 256
VMEM_LIMIT = 56 * 1024 * 1024
NEG = -1e30
FF_CHUNK = MXU_N
RET_PAD = 2 * SUBLANES
QKV_COLS = 2 * RET_QK + RET_V + 3 * ATT_W


def _rms(x, g):
    return x * lax.rsqrt(jnp.mean(x * x, axis=-1, keepdims=True) + EPS) * g


def _dot(a, b):
    return jnp.dot(a, b, preferred_element_type=F32)


def _dot_nt(a, b):
    return lax.dot_general(a, b, (((1,), (1,)), ((), ())), preferred_element_type=F32)


def _resident(shape):
    return pl.BlockSpec(shape, lambda *_: (0,) * len(shape), pipeline_mode=pl.Buffered(1))


def _log2(n):
    assert n > 0 and n & (n - 1) == 0, n
    return n.bit_length() - 1


def _params(n_axes=1):
    return pltpu.CompilerParams(dimension_semantics=("arbitrary",) * n_axes, vmem_limit_bytes=VMEM_LIMIT)


def _proj_kernel(x_ref, g_ref, w_ref, crq_ref, srq_ref, crk_ref, srk_ref, ca_ref, sa1_ref, sa2_ref, *rest,
                 tm, by_residue, tiles_per_seq, n_cast):
    cast_in, rest = rest[:n_cast], rest[n_cast:]
    rq_ref, rk_ref, rv_ref = rest[:3]
    if by_residue:
        res_refs = rest[3:3 + 3 * N_GROUPS]
        kv_refs = rest[3 + 3 * N_GROUPS:3 + 4 * N_GROUPS]
        cast_out = rest[3 + 4 * N_GROUPS:3 + 4 * N_GROUPS + n_cast]
        scr = rest[3 + 4 * N_GROUPS + n_cast]
    else:
        aq_ref = rest[3]
        kv_refs = rest[4:4 + N_GROUPS]
        cast_out = rest[4 + N_GROUPS:4 + N_GROUPS + n_cast]
    for src, dst in zip(cast_in, cast_out):
        dst[...] = src[...].astype(BF16)
    h = _rms(x_ref[...], g_ref[...]).astype(BF16)

    def mm(c0, width=MXU_N):
        return _dot(h, w_ref[:, c0:c0 + width])

    def rot_ret(t, c, s):
        return t * c + pltpu.roll(t, RET_QK_DIM // 2, 1) * s

    ca, sa1, sa2 = ca_ref[...], sa1_ref[...], sa2_ref[...]

    def rot_att(t):
        return t * ca + pltpu.roll(t, ROPE_DIM // 2, 1) * sa1 + pltpu.roll(t, MXU_N - ROPE_DIM // 2, 1) * sa2

    def put_by_residue(ref, val, dil):
        if dil == 1:
            ref[0, 0] = val.astype(BF16)
            return
        for half in range(MXU_N // LANES):
            lanes = slice(half * LANES, (half + 1) * LANES)
            scr[half] = val[:, lanes]
            for r in range(dil):
                ref[0, r, :, lanes] = scr[half, pl.ds(r, tm // dil, stride=dil), :].astype(BF16)

    def put_ret(ref, c, c_ref, s_ref):
        def put(t):
            for half in range(MXU_N // LANES):
                lo = half * LANES
                col = c * MXU_N + lo
                ref[:, col:col + LANES] = rot_ret(t[:, lo:lo + LANES], c_ref[...], s_ref[...]).astype(BF16)
        return put

    def put_rv(c):
        def put(t):
            rv_ref[:, c * MXU_N:(c + 1) * MXU_N] = t.astype(BF16)
        return put

    def put_att(g, which):
        def put(t):
            if which < 2:
                t = rot_att(t)
            if not by_residue:
                if which == 0:
                    aq_ref[:, g * ATT_OUT:(g + 1) * ATT_OUT] = t.astype(BF16)
                else:
                    kv_refs[g][:, (which - 1) * ATT_OUT:which * ATT_OUT] = t
                return
            put_by_residue(res_refs[3 * g + which], t, ATT_GROUPS[g][1])
            if which == 0:
                return
            rows = kv_refs[g].shape[3]

            def put_window():
                kv_refs[g][0, which - 1] = t[tm - rows:tm].T

            if ATT_GROUPS[g][0] >= tiles_per_seq * tm:
                put_window()
            else:
                pl.when(pl.program_id(0) % tiles_per_seq == tiles_per_seq - 1)(put_window)
        return put

    base = 2 * RET_QK + RET_V
    groups = []
    for c in range(RET_QK // MXU_N):
        groups.append([(c * MXU_N, put_ret(rq_ref, c, crq_ref, srq_ref)),
                       (RET_QK + c * MXU_N, put_ret(rk_ref, c, crk_ref, srk_ref))])
    for c in range(RET_V // MXU_N):
        groups.append([(2 * RET_QK + c * MXU_N, put_rv(c))])
    for g in range(N_GROUPS):
        groups.append([(base + which * ATT_W + g * ATT_OUT, put_att(g, which)) for which in range(3)])
    for group in groups:
        results = [mm(col) for col, _ in group]
        for (_, epilogue), t in zip(group, results):
            epilogue(t)


def _slab_rows(rows, n_steps):
    tile = 2 * SUBLANES
    for k in range(-(-rows // (n_steps * tile)), rows // tile + 1):
        if rows % (k * tile) == 0:
            return k * tile
    raise ValueError((rows, n_steps))


def _proj(x2d, g_pre, w_qkv, tabs, tm, seq=None, casts=()):
    m = x2d.shape[0]
    n_steps = m // tm
    n_tab = tabs[0].shape[0] // tm
    row = lambda i: (i, 0)
    tab = lambda i: (i % n_tab, 0)
    in_specs = [pl.BlockSpec((tm, D_MODEL), row), _resident((1, D_MODEL)), _resident((D_MODEL, QKV_COLS))]
    in_specs += [pl.BlockSpec((tm, LANES), tab)] * 4 + [pl.BlockSpec((tm, MXU_N), tab)] * 3
    out_specs = [pl.BlockSpec((tm, w), row) for w in (RET_QK, RET_QK, RET_V)]
    out_shape = [jax.ShapeDtypeStruct((m, w), BF16) for w in (RET_QK, RET_QK, RET_V)]
    scratch = []
    if seq is None:
        widths = (ATT_W,) + (2 * ATT_OUT,) * N_GROUPS
        out_specs += [pl.BlockSpec((tm, w), row) for w in widths]
        out_shape += [jax.ShapeDtypeStruct((m, w), d) for w, d in zip(widths, (BF16,) + (F32,) * N_GROUPS)]
    else:
        batch, tps = m // seq, seq // tm
        for _, dil in ATT_GROUPS:
            assert tm % (dil * 2 * SUBLANES) == 0
            out_specs += [pl.BlockSpec((1, dil, tm // dil, ATT_OUT), lambda i: (i // tps, 0, i % tps, 0))] * 3
            out_shape += [jax.ShapeDtypeStruct((batch, dil, seq // dil, ATT_OUT), BF16)] * 3
        for win, _ in ATT_GROUPS:
            win = min(win, seq)
            if win == seq:
                out_specs.append(pl.BlockSpec((1, 2, ATT_OUT, tm), lambda i: (i // tps, 0, 0, i % tps)))
            else:
                assert win <= tm and win % LANES == 0
                out_specs.append(pl.BlockSpec((1, 2, ATT_OUT, win), lambda i: (i // tps, 0, 0, 0)))
            out_shape.append(jax.ShapeDtypeStruct((batch, 2, ATT_OUT, win), F32))
        scratch = [pltpu.VMEM((MXU_N // LANES, tm, LANES), F32)]
    for w in casts:
        rows, cols = w.shape
        slab = _slab_rows(rows, n_steps)
        spec = pl.BlockSpec((slab, cols), lambda i, n=rows // slab: (i * n // n_steps, 0))
        in_specs.append(spec)
        out_specs.append(spec)
        out_shape.append(jax.ShapeDtypeStruct(w.shape, BF16))
    return pl.pallas_call(
        functools.partial(_proj_kernel, tm=tm, by_residue=seq is not None,
                          tiles_per_seq=None if seq is None else seq // tm, n_cast=len(casts)),
        grid=(n_steps,),
        in_specs=in_specs,
        out_specs=out_specs,
        out_shape=out_shape,
        scratch_shapes=scratch,
        compiler_params=_params(),
        name="proj",
    )(x2d, g_pre, w_qkv, *tabs, *casts)


def _ret_kernel(q_ref, k_ref, v_ref, dec_ref, qd_ref, kd_ref, gc_ref, o_ref, st_ref, s_scr, *, n_chunks):
    s_scr[...] = jnp.zeros_like(s_scr)

    def body(c, carry):
        r0 = pl.multiple_of(c * RET_CHUNK, RET_CHUNK)
        rows = pl.ds(r0, RET_CHUNK)
        first = []
        for h in range(RET_HEADS):
            qk = slice(h * RET_QK_DIM, (h + 1) * RET_QK_DIM)
            vv = slice(h * RET_V_DIM, (h + 1) * RET_V_DIM)
            q = q_ref[rows, qk]
            k = k_ref[rows, qk]
            v = v_ref[rows, vv]
            s0 = s_scr[h]
            sc = _dot_nt(q, k)
            from_state = _dot(q, s0.astype(BF16))
            kd_t = (k.astype(F32) * kd_ref[h]).T.astype(BF16)
            first.append((sc, from_state, _dot(kd_t, v), s0, v))
        for h, (sc, from_state, inc, s0, v) in enumerate(first):
            vv = slice(h * RET_V_DIM, (h + 1) * RET_V_DIM)
            o = _dot((sc * dec_ref[h]).astype(BF16), v) + from_state * qd_ref[h]
            s_scr[h] = s0 * gc_ref[h] + inc
            o_ref[rows, vv] = o * lax.rsqrt(jnp.mean(o * o, axis=-1, keepdims=True) + EPS)
        return carry

    lax.fori_loop(0, n_chunks, body, 0, unroll=2)
    st_ref[0] = s_scr[...]


def _log_gamma():
    return np.log1p(-np.exp2(-5.0 - np.arange(RET_HEADS, dtype=np.float64)))


def _ret_tables(chunk):
    lg = _log_gamma()
    idx = np.arange(RET_CHUNK, dtype=np.float64)
    diff = idx[:, None] - idx[None, :]
    dec = np.where(diff[None] >= 0, np.exp(lg[:, None, None] * np.maximum(diff, 0.0)[None]), 0.0)
    qd = np.exp(lg[:, None] * (idx[None, :] + 1.0))
    kd = np.exp(lg[:, None] * (chunk - 1.0 - idx)[None, :])
    gc = np.exp(lg * chunk)
    qd = np.broadcast_to(qd[:, :, None], (RET_HEADS, RET_CHUNK, RET_V_DIM))
    kd = np.broadcast_to(kd[:, :, None], (RET_HEADS, RET_CHUNK, RET_QK_DIM))
    gc = np.broadcast_to(gc[:, None, None], (RET_HEADS, RET_QK_DIM, RET_V_DIM))
    return [jnp.asarray(t, F32) for t in (dec, qd, kd, gc)]


def _ret_prompt(rq, rk, rv, batch, seq):
    dec, qd, kd, gc = _ret_tables(RET_CHUNK)
    row = lambda b: (b, 0)
    return pl.pallas_call(
        functools.partial(_ret_kernel, n_chunks=seq // RET_CHUNK),
        grid=(batch,),
        in_specs=[pl.BlockSpec((seq, RET_QK), row), pl.BlockSpec((seq, RET_QK), row), pl.BlockSpec((seq, RET_V), row),
                  _resident(dec.shape), _resident(qd.shape), _resident(kd.shape), _resident(gc.shape)],
        out_specs=[pl.BlockSpec((seq, RET_V), row),
                   pl.BlockSpec((1, RET_HEADS, RET_QK_DIM, RET_V_DIM), lambda b: (b, 0, 0, 0))],
        out_shape=[jax.ShapeDtypeStruct((batch * seq, RET_V), F32),
                   jax.ShapeDtypeStruct((batch, RET_HEADS, RET_QK_DIM, RET_V_DIM), F32)],
        scratch_shapes=[pltpu.VMEM((RET_HEADS, RET_QK_DIM, RET_V_DIM), F32)],
        compiler_params=_params(),
        name="ret_prompt",
    )(rq, rk, rv, dec, qd, kd, gc)


def _ret_sample_kernel(q_ref, k_ref, kt_ref, v_ref, s_ref, dec_ref, qd_ref, kd_ref, gc_ref, o_ref, st_ref,
                       *, n_tok, n_b):
    pairs = [(b, h) for b in range(n_b) for h in range(RET_HEADS)]
    qk_of = lambda h: slice(h * RET_QK_DIM, (h + 1) * RET_QK_DIM)
    vv_of = lambda h: slice(h * RET_V_DIM, (h + 1) * RET_V_DIM)
    zeros = jnp.zeros((LANES - RET_PAD, RET_V_DIM), BF16)
    first = []
    for b, h in pairs:
        q = q_ref[b, :, qk_of(h)]
        k = k_ref[b, :, qk_of(h)]
        v = v_ref[b, :, vv_of(h)]
        s0 = s_ref[b, h]
        from_state = _dot(q.astype(BF16), s0.astype(BF16))
        k_dec_t = (kt_ref[b, h] * kd_ref[h]).astype(BF16)
        v_rows = jnp.concatenate([v.astype(BF16), zeros], axis=0)
        st_ref[b, h] = s0 * gc_ref[h] + _dot(k_dec_t, v_rows)
        scores = [jnp.sum(q * k[i:i + 1, :], axis=-1, keepdims=True) for i in range(n_tok)]
        first.append((from_state, scores))
    second = []
    for (b, h), (from_state, scores) in zip(pairs, first):
        v = v_ref[b, :, vv_of(h)]
        o = from_state * qd_ref[h]
        for i, sc_i in enumerate(scores):
            o = o + (sc_i * dec_ref[h][:, i:i + 1]) * v[i:i + 1, :]
        second.append((o, jnp.mean(o * o, axis=-1, keepdims=True)))
    for (b, h), (o, ms) in zip(pairs, second):
        o_ref[b, :, vv_of(h)] = o * lax.rsqrt(ms + EPS)


def _ret_sample(rq, rk, rv, state, n_tok):
    db = state.shape[0]
    pad = RET_PAD - n_tok
    n_b = 8

    def pad_rows(t):
        t = t.astype(F32).reshape(db, n_tok, t.shape[-1])
        return jnp.pad(t, ((0, 0), (0, pad), (0, 0)))

    q, k, v = pad_rows(rq), pad_rows(rk), pad_rows(rv)
    kt = jnp.pad(k, ((0, 0), (0, LANES - RET_PAD), (0, 0)))
    kt = kt.reshape(db, LANES, RET_HEADS, RET_QK_DIM).transpose(0, 2, 3, 1)
    lg = _log_gamma()
    idx = np.arange(RET_PAD, dtype=np.float64)
    diff = idx[:, None] - idx[None, :]
    dec = np.where(diff[None] >= 0, np.exp(lg[:, None, None] * np.maximum(diff, 0.0)[None]), 0.0)
    qd = np.broadcast_to(np.exp(lg[:, None] * (idx[None, :] + 1.0))[:, :, None], (RET_HEADS, RET_PAD, RET_V_DIM))
    kd = np.exp(lg[:, None] * (n_tok - 1.0 - np.arange(LANES, dtype=np.float64))[None, :])
    kd = np.broadcast_to(kd[:, None, :], (RET_HEADS, RET_QK_DIM, LANES))
    gc = np.broadcast_to(np.exp(lg * n_tok)[:, None, None], (RET_HEADS, RET_QK_DIM, RET_V_DIM))
    dec, qd, kd, gc = (jnp.asarray(t, F32) for t in (dec, qd, kd, gc))
    b3 = lambda i: (i, 0, 0)
    b4 = lambda i: (i, 0, 0, 0)
    o, st = pl.pallas_call(
        functools.partial(_ret_sample_kernel, n_tok=n_tok, n_b=n_b),
        grid=(db // n_b,),
        in_specs=[pl.BlockSpec((n_b, RET_PAD, RET_QK), b3), pl.BlockSpec((n_b, RET_PAD, RET_QK), b3),
                  pl.BlockSpec((n_b, RET_HEADS, RET_QK_DIM, LANES), b4),
                  pl.BlockSpec((n_b, RET_PAD, RET_V), b3),
                  pl.BlockSpec((n_b, RET_HEADS, RET_QK_DIM, RET_V_DIM), b4),
                  _resident(dec.shape), _resident(qd.shape), _resident(kd.shape), _resident(gc.shape)],
        out_specs=[pl.BlockSpec((n_b, RET_PAD, RET_V), b3),
                   pl.BlockSpec((n_b, RET_HEADS, RET_QK_DIM, RET_V_DIM), b4)],
        out_shape=[jax.ShapeDtypeStruct((db, RET_PAD, RET_V), F32),
                   jax.ShapeDtypeStruct(state.shape, F32)],
        compiler_params=_params(),
        name="ret_sample",
    )(q, k, kt, v, state, dec, qd, kd, gc)
    return o[:, :n_tok].reshape(db * n_tok, RET_V), st


def _attn_kernel(q_ref, k_ref, v_ref, o_ref, l_ref, vt_scr, bias_scr, *staging, dil, nb):
    n_blk = dil * nb
    length = nb * BAND_BLOCK
    blk = BAND_BLOCK
    o_dst, l_dst = staging if dil > 1 else (o_ref, l_ref)

    kk = lax.broadcasted_iota(jnp.int32, (2 * blk, 2 * LANES), 0)
    ql = lax.broadcasted_iota(jnp.int32, (2 * blk, 2 * LANES), 1) & (blk - 1)
    in_span = kk <= ql + blk
    bias_scr[...] = jnp.where(in_span & (kk >= ql), 0.0, NEG)

    def transpose_values(c):
        vt_scr[c] = v_ref[0, c * blk:(c + 1) * blk, :].astype(F32).T.astype(BF16)

    lane = lax.broadcasted_iota(jnp.int32, (blk, LANES), 1)

    def scores(c, p):
        lanes = slice(p * LANES, (p + 1) * LANES)
        k0 = 0 if c % nb == 0 else blk
        q128 = q_ref[0, c * blk:(c + 1) * blk, lanes]
        zero = jnp.zeros_like(q128)
        q_pair = jnp.concatenate([jnp.where(lane < ATT_HEAD_DIM, q128, zero),
                                  jnp.where(lane >= ATT_HEAD_DIM, q128, zero)], axis=0)
        keys = k_ref[0, c * blk - k0:(c + 1) * blk, lanes]
        return _dot_nt(keys, q_pair) + bias_scr[blk - k0:2 * blk, :]

    def weights_values(c, p, s_t):
        lanes = slice(p * LANES, (p + 1) * LANES)
        m = jnp.max(s_t, axis=0, keepdims=True)
        p_t = jnp.exp(s_t - m).astype(BF16)
        v_t = vt_scr[c, lanes, :]
        if c % nb:
            v_t = jnp.concatenate([vt_scr[c - 1, lanes, :], v_t], axis=1)
        v_t = jnp.concatenate([v_t, jnp.ones((2 * SUBLANES, v_t.shape[1]), BF16)], axis=0)
        return _dot(v_t, p_t), m

    def finish(c, p, o_t, m):
        den = o_t[LANES:LANES + 1, :]
        lse = m + jnp.log(den)
        o_pair = jnp.concatenate([o_t[0:ATT_HEAD_DIM, 0:blk] / den[:, 0:blk],
                                  o_t[ATT_HEAD_DIM:LANES, blk:2 * blk] / den[:, blk:2 * blk]], axis=0)
        l_pair = jnp.concatenate([jnp.broadcast_to(lse[:, 0:blk], (ATT_HEAD_DIM, blk)),
                                  jnp.broadcast_to(lse[:, blk:2 * blk], (ATT_HEAD_DIM, blk))], axis=0)
        o_dst[p, c * blk:(c + 1) * blk, :] = o_pair.T
        l_dst[p, c * blk:(c + 1) * blk, :] = l_pair.T

    chains = [(c, p) for c in range(n_blk) for p in range(ATT_OUT // LANES)]
    transpose_values(0)
    s_next = scores(*chains[0])
    pending = None
    for i, chain in enumerate(chains):
        s_cur = s_next
        if chain[1] == 0 and chain[0] + 1 < n_blk:
            transpose_values(chain[0] + 1)
        if i + 1 < len(chains):
            s_next = scores(*chains[i + 1])
        o_m = weights_values(*chain, s_cur)
        if pending is not None:
            finish(*pending)
        pending = chain + o_m
    finish(*pending)

    if dil > 1:
        for p in range(ATT_OUT // LANES):
            for r in range(dil):
                o_ref[p, pl.ds(r, length, stride=dil), :] = o_dst[p, r * length:(r + 1) * length, :]
                l_ref[p, pl.ds(r, length, stride=dil), :] = l_dst[p, r * length:(r + 1) * length, :]


def _attn_prompt(q, k, v, g, batch, seq):
    _, dil = ATT_GROUPS[g]
    nb = seq // dil // BAND_BLOCK
    _log2(nb)
    q, k, v = (t.reshape(batch, seq, ATT_OUT) for t in (q, k, v))
    in_spec = pl.BlockSpec((1, seq, ATT_OUT), lambda b: (b, 0, 0))
    n_pair = ATT_OUT // LANES
    out_spec = pl.BlockSpec((n_pair, seq, LANES), lambda b: (0, b, 0))
    scratch = [pltpu.VMEM((dil * nb, ATT_OUT, BAND_BLOCK), BF16), pltpu.VMEM((2 * BAND_BLOCK, 2 * LANES), F32)]
    if dil > 1:
        scratch += [pltpu.VMEM((n_pair, seq, LANES), F32)] * 2
    return pl.pallas_call(
        functools.partial(_attn_kernel, dil=dil, nb=nb),
        grid=(batch,),
        in_specs=[in_spec] * 3,
        out_specs=[out_spec] * 2,
        out_shape=[jax.ShapeDtypeStruct((n_pair, batch * seq, LANES), F32)] * 2,
        scratch_shapes=scratch,
        compiler_params=_params(),
        name=f"attn_prompt_g{g}",
    )(q, k, v)


def _attn_sample_kernel(q_ref, c_ref, n_ref, o_ref, l_ref, nc_ref, *, g, n_tok, wb, n_b):
    for bb in range(n_b):
        _attn_sample_one(q_ref, c_ref, n_ref, o_ref, l_ref, nc_ref, bb, g=g, n_tok=n_tok, wb=wb, n_b=n_b)


def _attn_sample_one(q_ref, c_ref, n_ref, o_ref, l_ref, nc_ref, bb, *, g, n_tok, wb, n_b):
    win, dil = ATT_GROUPS[g]
    nk = win // dil
    _log2(dil)
    n_col = ATT_HEADS * n_tok
    first_new = LANES - n_tok
    seq_lane = ((pl.program_id(0) * n_b + bb) * n_tok) % LANES
    new = [pltpu.roll(n_ref[kv], (first_new + LANES - seq_lane) % LANES, 1) for kv in range(2)]
    lane_t = lax.broadcasted_iota(jnp.int32, (ATT_OUT, LANES), 1)
    for kv in range(2):
        shifted = pltpu.roll(c_ref[bb, kv], wb - n_tok, 1)
        if wb > LANES:
            nc_ref[bb, kv, :, 0:wb - LANES] = shifted[:, 0:wb - LANES]
        nc_ref[bb, kv, :, wb - LANES:wb] = jnp.where(lane_t < first_new, shifted[:, wb - LANES:wb], new[kv])

    q = q_ref[bb][:, g * ATT_OUT:(g + 1) * ATT_OUT]
    row = lax.broadcasted_iota(jnp.int32, (n_col, ATT_OUT), 0)
    lane = lax.broadcasted_iota(jnp.int32, (n_col, ATT_OUT), 1)
    head_sel = (lane >> _log2(ATT_HEAD_DIM)) == (row >> _log2(n_tok))
    qbd = jnp.where(head_sel, q, jnp.zeros_like(q))

    s_c = _dot(qbd, c_ref[bb, 0].astype(BF16))
    s_n = _dot(qbd, new[0].astype(BF16))
    t_c = lax.broadcasted_iota(jnp.int32, (n_col, wb), 0) & (n_tok - 1)
    d_c = wb + t_c - lax.broadcasted_iota(jnp.int32, (n_col, wb), 1)
    ok_c = ((d_c & (dil - 1)) == 0) & (d_c <= nk * dil)
    t_n = lax.broadcasted_iota(jnp.int32, (n_col, LANES), 0) & (n_tok - 1)
    new_idx = lax.broadcasted_iota(jnp.int32, (n_col, LANES), 1) - first_new
    d_n = t_n - new_idx
    ok_n = (new_idx >= 0) & (d_n >= 0) & ((d_n & (dil - 1)) == 0) & (d_n <= nk * dil)
    s_c = jnp.where(ok_c, s_c, NEG)
    s_n = jnp.where(ok_n, s_n, NEG)
    m = jnp.maximum(jnp.max(s_c, axis=-1, keepdims=True), jnp.max(s_n, axis=-1, keepdims=True))
    p_c = jnp.exp(s_c - m)
    p_n = jnp.exp(s_n - m)
    den = jnp.sum(p_c, axis=-1, keepdims=True) + jnp.sum(p_n, axis=-1, keepdims=True)
    o = (_dot_nt(p_c.astype(BF16), c_ref[bb, 1].astype(BF16))
         + _dot_nt(p_n.astype(BF16), new[1].astype(BF16))) / den
    lse = jnp.broadcast_to(m + jnp.log(den), (n_col, ATT_OUT))
    o = jnp.where(head_sel, o, 0.0)
    lse = jnp.where(head_sel, lse, 0.0)
    o_t = o[0:n_tok]
    l_t = lse[0:n_tok]
    for h in range(1, ATT_HEADS):
        o_t = o_t + o[h * n_tok:(h + 1) * n_tok]
        l_t = l_t + lse[h * n_tok:(h + 1) * n_tok]
    o_ref[bb] = o_t
    l_ref[bb] = l_t


def _attn_sample(q_tiled, cache, kv_new, g, n_tok):
    db, _, _, wb = cache.shape
    n_b = max(1, min(SUBLANES, ATT_GROUPS[-1][0] // wb))
    assert wb % LANES == 0 and db % n_b == 0 and LANES % (n_b * n_tok) == 0 and kv_new.shape[2] % LANES == 0
    b3 = lambda i: (i, 0, 0)
    b4 = lambda i: (i, 0, 0, 0)
    return pl.pallas_call(
        functools.partial(_attn_sample_kernel, g=g, n_tok=n_tok, wb=wb, n_b=n_b),
        grid=(db // n_b,),
        in_specs=[pl.BlockSpec((n_b, ATT_HEADS * n_tok, ATT_W), b3),
                  pl.BlockSpec((n_b, 2, ATT_OUT, wb), b4),
                  pl.BlockSpec((2, ATT_OUT, LANES), lambda i: (0, 0, i * n_b * n_tok // LANES))],
        out_specs=[pl.BlockSpec((n_b, n_tok, ATT_OUT), b3), pl.BlockSpec((n_b, n_tok, ATT_OUT), b3),
                   pl.BlockSpec((n_b, 2, ATT_OUT, wb), b4)],
        out_shape=[jax.ShapeDtypeStruct((db, n_tok, ATT_OUT), F32), jax.ShapeDtypeStruct((db, n_tok, ATT_OUT), F32),
                   jax.ShapeDtypeStruct(cache.shape, F32)],
        compiler_params=_params(),
        name=f"attn_sample_g{g}",
    )(q_tiled, cache, kv_new)


def _merge_kernel(x_ref, on_ref, o0_ref, o1_ref, o2_ref, l0_ref, l1_ref, l2_ref, gpre_ref, wg_ref, wbr_ref,
                  wba_ref, wout_ref, gpost_ref, gffn_ref, x1_ref, h2_ref, *perm, tm, n_split, permute):
    rows_of = [slice(i * tm // n_split, (i + 1) * tm // n_split) for i in range(n_split)]
    n_pair = ATT_OUT // LANES

    def gates(rows):
        x = x_ref[rows, :]
        h = _rms(x, gpre_ref[...]).astype(BF16)
        return [_dot(h, wg_ref[:, c:c + D_MODEL]) for c in (0, RET_V, RET_V + D_MODEL)]

    def branches(rows, rg):
        r = (on_ref[rows, :] * (rg * jax.nn.sigmoid(rg))).astype(BF16)
        atts = []
        for p in range(n_pair):
            l0, l1, l2 = l0_ref[p, rows, :], l1_ref[p, rows, :], l2_ref[p, rows, :]
            mx = jnp.maximum(jnp.maximum(l0, l1), l2)
            e0, e1, e2 = jnp.exp(l0 - mx), jnp.exp(l1 - mx), jnp.exp(l2 - mx)
            att = (e0 * o0_ref[p, rows, :] + e1 * o1_ref[p, rows, :] + e2 * o2_ref[p, rows, :]) / (e0 + e1 + e2)
            atts.append(att.astype(BF16))
        return _dot(r, wbr_ref[...]), _dot(jnp.concatenate(atts, axis=1), wba_ref[...])

    def mixed(gr, ga, br, ba):
        mix = jax.nn.sigmoid(gr) * br + jax.nn.sigmoid(ga) * ba
        return _dot(mix.astype(BF16), wout_ref[...])

    def finish(rows, mo):
        x1 = x_ref[rows, :] + _rms(mo, gpost_ref[...])
        x1_ref[rows, :] = x1
        return _rms(x1, gffn_ref[...])

    g = [gates(rows) for rows in rows_of]
    b = [branches(rows, gi[0]) for rows, gi in zip(rows_of, g)]
    mo = [mixed(gi[1], gi[2], *bi) for gi, bi in zip(g, b)]
    h2 = [finish(rows, moi) for rows, moi in zip(rows_of, mo)]
    if not permute:
        for rows, h2i in zip(rows_of, h2):
            h2_ref[rows, :] = h2i.astype(BF16)
        return
    perm_ref, = perm
    nv = tm // SUBLANES
    assert nv % (tm // n_split) == 0 or (tm // n_split) % nv == 0
    for lc in range(D_MODEL // LANES):
        lanes = slice(lc * LANES, (lc + 1) * LANES)
        for s in range(SUBLANES):
            tok0 = s * nv
            i, off = divmod(tok0, tm // n_split)
            perm_ref[lc, pl.ds(s, nv, stride=SUBLANES), :] = h2[i][off:off + nv, lanes]
        h2_ref[:, lanes] = perm_ref[lc].astype(BF16)


def _merge(x2d, on, outs, lses, g_pre, w_gates, w_br, w_ba, w_out, g_post, g_ffn, tm, permute):
    m = x2d.shape[0]
    n_split = 2 if tm >= 2 * MXU_N else 1
    scratch = [pltpu.VMEM((D_MODEL // LANES, tm, LANES), F32)] if permute else []
    row = lambda i: (i, 0)
    vec = _resident((1, D_MODEL))
    att_spec = pl.BlockSpec((ATT_OUT // LANES, tm, LANES), lambda i: (0, i, 0))
    return pl.pallas_call(
        functools.partial(_merge_kernel, tm=tm, n_split=n_split, permute=permute),
        grid=(m // tm,),
        in_specs=[pl.BlockSpec((tm, D_MODEL), row), pl.BlockSpec((tm, RET_V), row)] + [att_spec] * 6 + [
            vec, _resident(w_gates.shape), _resident(w_br.shape), _resident(w_ba.shape), _resident(w_out.shape),
            vec, vec],
        out_specs=[pl.BlockSpec((tm, D_MODEL), row), pl.BlockSpec((tm, D_MODEL), row)],
        out_shape=[jax.ShapeDtypeStruct((m, D_MODEL), F32), jax.ShapeDtypeStruct((m, D_MODEL), BF16)],
        scratch_shapes=scratch,
        compiler_params=_params(),
        name="merge",
    )(x2d, on, *outs, *lses, g_pre, w_gates, w_br, w_ba, w_out, g_post, g_ffn)


def _ffn_kernel(*refs, tm, tiles_per_seq, seq_tok):
    if seq_tok is None:
        h2_ref, x1_ref, wup_ref, wdn_ref, cw_ref, cb_ref, g_ref, y_ref, cs_ref, carry_ref, f_ref, perm_ref = refs
    else:
        h2_ref, x1_ref, p1_ref, p2_ref, wup_ref, wdn_ref, cw_ref, cb_ref, g_ref, y_ref, u_ref, f_ref = refs
    h2 = h2_ref[...]
    if seq_tok is None:
        sub = lax.broadcasted_iota(jnp.int32, (SUBLANES, FF_CHUNK), 0)

        @pl.when(pl.program_id(0) % tiles_per_seq == 0)
        def _():
            carry_ref[...] = jnp.zeros_like(carry_ref)
    else:
        t = lax.broadcasted_iota(jnp.int32, (tm, FF_CHUNK), 0) & (seq_tok - 1)
        _log2(seq_tok)

    def up(j):
        return [_dot(h2, wup_ref[:, c:c + FF_CHUNK]) for c in (j * FF_CHUNK, D_FF + j * FF_CHUNK)]

    def conv(u, col):
        cols = slice(col, col + FF_CHUNK)
        if seq_tok is None:
            prev = carry_ref[:, cols]
            last2 = jnp.where(sub == 0, prev[SUBLANES - 1:SUBLANES], pltpu.roll(u[tm - 2 * SUBLANES:tm - SUBLANES], 1, 0))
            last1 = jnp.where(sub == 0, prev[2 * SUBLANES - 1:2 * SUBLANES], pltpu.roll(u[tm - SUBLANES:tm], 1, 0))
            u1 = jnp.concatenate([last1, u[0:tm - SUBLANES]], axis=0)
            u2 = jnp.concatenate([last2, last1, u[0:tm - 2 * SUBLANES]], axis=0)
            carry_ref[:, cols] = u[tm - 2 * SUBLANES:tm]
        else:
            u1 = jnp.where(t == 0, p1_ref[:, cols], pltpu.roll(u, 1, 0))
            u2 = jnp.where(t <= 1, p2_ref[:, cols], pltpu.roll(u, 2, 0))
            u_ref[:, cols] = u
        cw = cw_ref[:, cols]
        return cb_ref[:, cols] + cw[0:1] * u2 + cw[1:2] * u1 + cw[2:3] * u

    def gate(j, ug, uv):
        cg = conv(ug, j * FF_CHUNK)
        cv = conv(uv, D_FF + j * FF_CHUNK)
        f_ref[:, j * FF_CHUNK:(j + 1) * FF_CHUNK] = (jax.nn.gelu(cg, approximate=True) * cv).astype(BF16)

    n_chunks = D_FF // FF_CHUNK
    u_next = up(0)
    for j in range(n_chunks):
        u_cur = u_next
        if j + 1 < n_chunks:
            u_next = up(j + 1)
        gate(j, *u_cur)
    out = _rms(_dot(f_ref[...], wdn_ref[...]), g_ref[...])
    if seq_tok is None:
        nv = tm // SUBLANES
        for lc in range(D_MODEL // LANES):
            lanes = slice(lc * LANES, (lc + 1) * LANES)
            perm_ref[lc] = out[:, lanes]
            for s in range(SUBLANES):
                rows = slice(s * nv, (s + 1) * nv)
                y_ref[rows, lanes] = x1_ref[rows, lanes] + perm_ref[lc, pl.ds(s, nv, stride=SUBLANES), :]

        @pl.when(pl.program_id(0) % tiles_per_seq == tiles_per_seq - 1)
        def _():
            cs_ref[0, 0:1, :] = carry_ref[SUBLANES - 1:SUBLANES, :]
            cs_ref[0, 1:2, :] = carry_ref[2 * SUBLANES - 1:2 * SUBLANES, :]
    else:
        y_ref[...] = x1_ref[...] + out


def _ffn_prompt(h2, x1, w_up, w_dn, conv_w, conv_b, g_post, batch, seq, tm):
    m = h2.shape[0]
    tps = seq // tm
    row = lambda i: (i, 0)
    return pl.pallas_call(
        functools.partial(_ffn_kernel, tm=tm, tiles_per_seq=tps, seq_tok=None),
        grid=(m // tm,),
        in_specs=[pl.BlockSpec((tm, D_MODEL), row), pl.BlockSpec((tm, D_MODEL), row),
                  _resident(w_up.shape), _resident(w_dn.shape), _resident(conv_w.shape), _resident(conv_b.shape),
                  _resident((1, D_MODEL))],
        out_specs=[pl.BlockSpec((tm, D_MODEL), row),
                   pl.BlockSpec((1, CONV_W - 1, 2 * D_FF), lambda i: (i // tps, 0, 0))],
        out_shape=[jax.ShapeDtypeStruct((m, D_MODEL), F32),
                   jax.ShapeDtypeStruct((batch, CONV_W - 1, 2 * D_FF), F32)],
        scratch_shapes=[pltpu.VMEM((2 * SUBLANES, 2 * D_FF), F32), pltpu.VMEM((tm, D_FF), BF16),
                        pltpu.VMEM((D_MODEL // LANES, tm, LANES), F32)],
        compiler_params=_params(),
        name="ffn_prompt",
    )(h2, x1, w_up, w_dn, conv_w, conv_b, g_post)


def _ffn_sample(h2, x1, p1, p2, w_up, w_dn, conv_w, conv_b, g_post, n_tok):
    m = h2.shape[0]
    const = lambda i: (0, 0)
    full = lambda a: _resident(a.shape)
    return pl.pallas_call(
        functools.partial(_ffn_kernel, tm=m, tiles_per_seq=1, seq_tok=n_tok),
        grid=(1,),
        in_specs=[full(h2), full(x1), full(p1), full(p2), full(w_up), full(w_dn), full(conv_w), full(conv_b),
                  _resident((1, D_MODEL))],
        out_specs=[pl.BlockSpec((m, D_MODEL), const), pl.BlockSpec((m, 2 * D_FF), const)],
        out_shape=[jax.ShapeDtypeStruct((m, D_MODEL), F32), jax.ShapeDtypeStruct((m, 2 * D_FF), F32)],
        scratch_shapes=[pltpu.VMEM((m, D_FF), BF16)],
        compiler_params=_params(),
        name="ffn_sample",
    )(h2, x1, p1, p2, w_up, w_dn, conv_w, conv_b, g_post)


def _rot_tables(pos):
    pos = np.asarray(pos, np.float64)
    rf = RET_THETA ** (-np.linspace(0.0, 1.0, RET_QK_DIM // 2))
    ang = pos[:, None] * rf[None, :]
    cos, sin = np.cos(ang), np.sin(ang)
    cr = np.concatenate([cos, cos], axis=-1)
    sr = np.concatenate([-sin, sin], axis=-1)
    k_scale = RET_QK_DIM ** -0.5
    half = ROPE_DIM // 2
    af = ROPE_THETA ** (-np.arange(half) / half)
    ang = pos[:, None] * af[None, :]
    cos, sin = np.cos(ang), np.sin(ang)
    n = pos.shape[0]
    rest = ATT_HEAD_DIM - ROPE_DIM
    zh = np.zeros((n, half))
    ca = np.concatenate([cos, cos, np.ones((n, rest))], axis=-1)
    sa1 = np.concatenate([zh, sin, np.zeros((n, rest))], axis=-1)
    sa2 = np.concatenate([-sin, zh, np.zeros((n, rest))], axis=-1)
    rep = MXU_N // ATT_HEAD_DIM
    ca, sa1, sa2 = (np.tile(t, (1, rep)) for t in (ca, sa1, sa2))
    return [jnp.asarray(t, F32) for t in (cr, sr, cr * k_scale, sr * k_scale, ca, sa1, sa2)]


def kernel(x_prompt, x_sample, cache_kv_g0, cache_kv_g1, cache_kv_g2, state_ret, state_conv, norm_mix_pre, w_in,
           w_branch_ret, w_branch_attn, w_out, norm_mix_post, norm_ffn_pre, w_ffn_up, conv_w, conv_b, w_ffn_down,
           norm_ffn_post):
    batch, seq, _ = x_prompt.shape
    db, n_tok, _ = x_sample.shape
    depth = w_in.shape[0]
    assert depth == 1 and seq % (ATT_GROUPS[-1][1] * BAND_BLOCK) == 0 and n_tok <= SUBLANES
    caches = (cache_kv_g0, cache_kv_g1, cache_kv_g2)

    wi = w_in[0]
    o_rg = 2 * RET_QK + RET_V
    o_aq = o_rg + RET_V
    o_gr = o_aq + 3 * ATT_W
    w_qkv = jnp.concatenate([wi[:, :o_rg], wi[:, o_aq:o_aq + ATT_W] * ATT_HEAD_DIM ** -0.5,
                             wi[:, o_aq + ATT_W:o_gr]], axis=1).astype(BF16)
    w_gates = jnp.concatenate([wi[:, o_rg:o_aq], wi[:, o_gr:]], axis=1).astype(BF16)
    later_weights = (w_branch_ret[0], w_branch_attn[0], w_out[0], w_ffn_up[0], w_ffn_down[0])
    cb = conv_b[0][None, :]
    cw = conv_w[0]
    g_pre, g_post, g_ffn, g_post2 = (t[0][None, :] for t in (norm_mix_pre, norm_mix_post, norm_ffn_pre, norm_ffn_post))

    tm = 512
    xp = x_prompt.reshape(batch * seq, D_MODEL)
    proj_out = _proj(xp, g_pre, w_qkv, _rot_tables(np.arange(seq)), tm, seq=seq, casts=later_weights)
    rq, rk, rv = proj_out[:3]
    qkv_res = proj_out[3:3 + 3 * N_GROUPS]
    kv_last = proj_out[3 + 3 * N_GROUPS:3 + 4 * N_GROUPS]
    w_br, w_ba, w_o, w_up, w_dn = proj_out[3 + 4 * N_GROUPS:]
    on, p_ret = _ret_prompt(rq, rk, rv, batch, seq)
    outs, lses = [], []
    for g in range(N_GROUPS):
        o_g, l_g = _attn_prompt(*qkv_res[3 * g:3 * g + 3], g, batch, seq)
        outs.append(o_g)
        lses.append(l_g)
    x1, h2 = _merge(xp, on, outs, lses, g_pre, w_gates, w_br, w_ba, w_o, g_post, g_ffn, tm, permute=True)
    y_p, p_conv = _ffn_prompt(h2, x1, w_up, w_dn, cw, cb, g_post2, batch, seq, tm)
    p_kv = [t.reshape(batch, 2, ATT_HEADS, ATT_HEAD_DIM, t.shape[3]).transpose(0, 4, 1, 2, 3)[None] for t in kv_last]

    ms = db * n_tok
    xs = x_sample.reshape(ms, D_MODEL)
    pos_s = np.tile(PAST_LEN + np.arange(n_tok), db)
    rq, rk, rv, aq, kv0, kv1, kv2 = _proj(xs, g_pre, w_qkv, _rot_tables(pos_s), ms)
    kvs = (kv0, kv1, kv2)
    on, s_ret = _ret_sample(rq, rk, rv, state_ret[0], n_tok)
    q_tiled = jnp.tile(aq.reshape(db, n_tok, ATT_W), (1, ATT_HEADS, 1))
    outs, lses, s_kv = [], [], []
    for g in range(N_GROUPS):
        cache = caches[g][0]
        wb = cache.shape[1]
        cache_t = cache.transpose(0, 2, 3, 4, 1).reshape(db, 2, ATT_OUT, wb)
        kv_new = kvs[g].reshape(ms, 2, ATT_OUT).transpose(1, 2, 0)
        o_g, l_g, nc = _attn_sample(q_tiled, cache_t, kv_new, g, n_tok)
        outs.append(o_g.reshape(ms, ATT_OUT // LANES, LANES).transpose(1, 0, 2))
        lses.append(l_g.reshape(ms, ATT_OUT // LANES, LANES).transpose(1, 0, 2))
        s_kv.append(nc.reshape(db, 2, ATT_HEADS, ATT_HEAD_DIM, wb).transpose(0, 4, 1, 2, 3)[None])
    x1, h2 = _merge(xs, on, outs, lses, g_pre, w_gates, w_br, w_ba, w_o, g_post, g_ffn, ms, permute=False)
    st = state_conv[0]
    zeros = jnp.zeros((db, n_tok - 1, 2 * D_FF), F32)
    p1 = jnp.concatenate([st[:, 1:2], zeros], axis=1).reshape(ms, 2 * D_FF)
    p2 = jnp.concatenate([st, zeros[:, :n_tok - 2]], axis=1).reshape(ms, 2 * D_FF)
    y_s, u_s = _ffn_sample(h2, x1, p1, p2, w_up, w_dn, cw, cb, g_post2, n_tok)
    s_conv = u_s.reshape(db, n_tok, 2 * D_FF)[:, n_tok - (CONV_W - 1):][None]

    return (y_p.reshape(x_prompt.shape), y_s.reshape(x_sample.shape), p_kv[0], p_kv[1], p_kv[2], p_ret[None],
            p_conv[None], s_kv[0], s_kv[1], s_kv[2], s_ret[None], s_conv)
```

```python
import functools

import numpy as np
import jax
import jax.numpy as jnp
from jax import lax
from jax.experimental import pallas as pl
from jax.experimental.pallas import tpu as pltpu

F32 = jnp.float32
BF16 = jnp.bfloat16

D_MODEL = 1024
PAST_LEN = 16384
RET_HEADS = 4
RET_QK_DIM = 128
RET_V_DIM = 256
RET_CHUNK = 128
RET_THETA = 10000.0
RET_QK = RET_HEADS * RET_QK_DIM
RET_V = RET_HEADS * RET_V_DIM
ATT_GROUPS = ((128, 1), (512, 4), (2048, 16))
N_GROUPS = 3
ATT_HEADS = 4
ATT_HEAD_DIM = 64
ROPE_DIM = ATT_HEAD_DIM // 4
ROPE_THETA = 500000.0
BAND_BLOCK = 128
ATT_OUT = ATT_HEADS * ATT_HEAD_DIM
ATT_W = N_GROUPS * ATT_OUT
D_FF = 2816
CONV_W = 3
EPS = 1e-6

LANES = 128
SUBLANES = 8
MXU_N = 256
VMEM_LIMIT = 56 * 1024 * 1024
NEG = -1e30
FF_CHUNK = MXU_N
RET_PAD = 2 * SUBLANES
QKV_COLS = 2 * RET_QK + RET_V + 3 * ATT_W


def _rms(x, g):
    return x * lax.rsqrt(jnp.mean(x * x, axis=-1, keepdims=True) + EPS) * g


def _dot(a, b):
    return jnp.dot(a, b, preferred_element_type=F32)


def _dot_nt(a, b):
    return lax.dot_general(a, b, (((1,), (1,)), ((), ())), preferred_element_type=F32)


def _resident(shape):
    return pl.BlockSpec(shape, lambda *_: (0,) * len(shape), pipeline_mode=pl.Buffered(1))


def _log2(n):
    assert n > 0 and n & (n - 1) == 0, n
    return n.bit_length() - 1


def _params(n_axes=1):
    return pltpu.CompilerParams(dimension_semantics=("arbitrary",) * n_axes, vmem_limit_bytes=VMEM_LIMIT)


def _proj_kernel(x_ref, g_ref, w_ref, crq_ref, srq_ref, crk_ref, srk_ref, ca_ref, sa1_ref, sa2_ref, *rest,
                 tm, by_residue, tiles_per_seq, n_cast):
    cast_in, rest = rest[:n_cast], rest[n_cast:]
    rq_ref, rk_ref, rv_ref = rest[:3]
    if by_residue:
        res_refs = rest[3:3 + 3 * N_GROUPS]
        kv_refs = rest[3 + 3 * N_GROUPS:3 + 4 * N_GROUPS]
        cast_out = rest[3 + 4 * N_GROUPS:3 + 4 * N_GROUPS + n_cast]
        scr = rest[3 + 4 * N_GROUPS + n_cast]
    else:
        aq_ref = rest[3]
        kv_refs = rest[4:4 + N_GROUPS]
        cast_out = rest[4 + N_GROUPS:4 + N_GROUPS + n_cast]
    for src, dst in zip(cast_in, cast_out):
        dst[...] = src[...].astype(BF16)
    h = _rms(x_ref[...], g_ref[...]).astype(BF16)

    def mm(c0, width=MXU_N):
        return _dot(h, w_ref[:, c0:c0 + width])

    def rot_ret(t, c, s):
        return t * c + pltpu.roll(t, RET_QK_DIM // 2, 1) * s

    ca, sa1, sa2 = ca_ref[...], sa1_ref[...], sa2_ref[...]

    def rot_att(t):
        return t * ca + pltpu.roll(t, ROPE_DIM // 2, 1) * sa1 + pltpu.roll(t, MXU_N - ROPE_DIM // 2, 1) * sa2

    def put_by_residue(ref, val, dil):
        if dil == 1:
            ref[0, 0] = val.astype(BF16)
            return
        for half in range(MXU_N // LANES):
            lanes = slice(half * LANES, (half + 1) * LANES)
            scr[half] = val[:, lanes]
            for r in range(dil):
                ref[0, r, :, lanes] = scr[half, pl.ds(r, tm // dil, stride=dil), :].astype(BF16)

    def put_ret(ref, c, c_ref, s_ref):
        def put(t):
            for half in range(MXU_N // LANES):
                lo = half * LANES
                col = c * MXU_N + lo
                ref[:, col:col + LANES] = rot_ret(t[:, lo:lo + LANES], c_ref[...], s_ref[...]).astype(BF16)
        return put

    def put_rv(c):
        def put(t):
            rv_ref[:, c * MXU_N:(c + 1) * MXU_N] = t.astype(BF16)
        return put

    def put_att(g, which):
        def put(t):
            if which < 2:
                t = rot_att(t)
            if not by_residue:
                if which == 0:
                    aq_ref[:, g * ATT_OUT:(g + 1) * ATT_OUT] = t.astype(BF16)
                else:
                    kv_refs[g][:, (which - 1) * ATT_OUT:which * ATT_OUT] = t
                return
            put_by_residue(res_refs[3 * g + which], t, ATT_GROUPS[g][1])
            if which == 0:
                return
            rows = kv_refs[g].shape[3]

            def put_window():
                kv_refs[g][0, which - 1] = t[tm - rows:tm].T

            if ATT_GROUPS[g][0] >= tiles_per_seq * tm:
                put_window()
            else:
                pl.when(pl.program_id(0) % tiles_per_seq == tiles_per_seq - 1)(put_window)
        return put

    base = 2 * RET_QK + RET_V
    groups = []
    for c in range(RET_QK // MXU_N):
        groups.append([(c * MXU_N, put_ret(rq_ref, c, crq_ref, srq_ref)),
                       (RET_QK + c * MXU_N, put_ret(rk_ref, c, crk_ref, srk_ref))])
    for c in range(RET_V // MXU_N):
        groups.append([(2 * RET_QK + c * MXU_N, put_rv(c))])
    for g in range(N_GROUPS):
        groups.append([(base + which * ATT_W + g * ATT_OUT, put_att(g, which)) for which in range(3)])
    for group in groups:
        results = [mm(col) for col, _ in group]
        for (_, epilogue), t in zip(group, results):
            epilogue(t)


def _slab_rows(rows, n_steps):
    tile = 2 * SUBLANES
    for k in range(-(-rows // (n_steps * tile)), rows // tile + 1):
        if rows % (k * tile) == 0:
            return k * tile
    raise ValueError((rows, n_steps))


def _proj(x2d, g_pre, w_qkv, tabs, tm, seq=None, casts=()):
    m = x2d.shape[0]
    n_steps = m // tm
    n_tab = tabs[0].shape[0] // tm
    row = lambda i: (i, 0)
    tab = lambda i: (i % n_tab, 0)
    in_specs = [pl.BlockSpec((tm, D_MODEL), row), _resident((1, D_MODEL)), _resident((D_MODEL, QKV_COLS))]
    in_specs += [pl.BlockSpec((tm, LANES), tab)] * 4 + [pl.BlockSpec((tm, MXU_N), tab)] * 3
    out_specs = [pl.BlockSpec((tm, w), row) for w in (RET_QK, RET_QK, RET_V)]
    out_shape = [jax.ShapeDtypeStruct((m, w), BF16) for w in (RET_QK, RET_QK, RET_V)]
    scratch = []
    if seq is None:
        widths = (ATT_W,) + (2 * ATT_OUT,) * N_GROUPS
        out_specs += [pl.BlockSpec((tm, w), row) for w in widths]
        out_shape += [jax.ShapeDtypeStruct((m, w), d) for w, d in zip(widths, (BF16,) + (F32,) * N_GROUPS)]
    else:
        batch, tps = m // seq, seq // tm
        for _, dil in ATT_GROUPS:
            assert tm % (dil * 2 * SUBLANES) == 0
            out_specs += [pl.BlockSpec((1, dil, tm // dil, ATT_OUT), lambda i: (i // tps, 0, i % tps, 0))] * 3
            out_shape += [jax.ShapeDtypeStruct((batch, dil, seq // dil, ATT_OUT), BF16)] * 3
        for win, _ in ATT_GROUPS:
            win = min(win, seq)
            if win == seq:
                out_specs.append(pl.BlockSpec((1, 2, ATT_OUT, tm), lambda i: (i // tps, 0, 0, i % tps)))
            else:
                assert win <= tm and win % LANES == 0
                out_specs.append(pl.BlockSpec((1, 2, ATT_OUT, win), lambda i: (i // tps, 0, 0, 0)))
            out_shape.append(jax.ShapeDtypeStruct((batch, 2, ATT_OUT, win), F32))
        scratch = [pltpu.VMEM((MXU_N // LANES, tm, LANES), F32)]
    for w in casts:
        rows, cols = w.shape
        slab = _slab_rows(rows, n_steps)
        spec = pl.BlockSpec((slab, cols), lambda i, n=rows // slab: (i * n // n_steps, 0))
        in_specs.append(spec)
        out_specs.append(spec)
        out_shape.append(jax.ShapeDtypeStruct(w.shape, BF16))
    return pl.pallas_call(
        functools.partial(_proj_kernel, tm=tm, by_residue=seq is not None,
                          tiles_per_seq=None if seq is None else seq // tm, n_cast=len(casts)),
        grid=(n_steps,),
        in_specs=in_specs,
        out_specs=out_specs,
        out_shape=out_shape,
        scratch_shapes=scratch,
        compiler_params=_params(),
        name="proj",
    )(x2d, g_pre, w_qkv, *tabs, *casts)


def _ret_kernel(q_ref, k_ref, v_ref, dec_ref, qd_ref, kd_ref, gc_ref, o_ref, st_ref, s_scr, *, n_chunks):
    s_scr[...] = jnp.zeros_like(s_scr)

    def body(c, carry):
        r0 = pl.multiple_of(c * RET_CHUNK, RET_CHUNK)
        rows = pl.ds(r0, RET_CHUNK)
        first = []
        for h in range(RET_HEADS):
            qk = slice(h * RET_QK_DIM, (h + 1) * RET_QK_DIM)
            vv = slice(h * RET_V_DIM, (h + 1) * RET_V_DIM)
            q = q_ref[rows, qk]
            k = k_ref[rows, qk]
            v = v_ref[rows, vv]
            s0 = s_scr[h]
            sc = _dot_nt(q, k)
            from_state = _dot(q, s0.astype(BF16))
            kd_t = (k.astype(F32) * kd_ref[h]).T.astype(BF16)
            first.append((sc, from_state, _dot(kd_t, v), s0, v))
        for h, (sc, from_state, inc, s0, v) in enumerate(first):
            vv = slice(h * RET_V_DIM, (h + 1) * RET_V_DIM)
            o = _dot((sc * dec_ref[h]).astype(BF16), v) + from_state * qd_ref[h]
            s_scr[h] = s0 * gc_ref[h] + inc
            o_ref[rows, vv] = o * lax.rsqrt(jnp.mean(o * o, axis=-1, keepdims=True) + EPS)
        return carry

    lax.fori_loop(0, n_chunks, body, 0, unroll=2)
    st_ref[0] = s_scr[...]


def _log_gamma():
    return np.log1p(-np.exp2(-5.0 - np.arange(RET_HEADS, dtype=np.float64)))


def _ret_tables(chunk):
    lg = _log_gamma()
    idx = np.arange(RET_CHUNK, dtype=np.float64)
    diff = idx[:, None] - idx[None, :]
    dec = np.where(diff[None] >= 0, np.exp(lg[:, None, None] * np.maximum(diff, 0.0)[None]), 0.0)
    qd = np.exp(lg[:, None] * (idx[None, :] + 1.0))
    kd = np.exp(lg[:, None] * (chunk - 1.0 - idx)[None, :])
    gc = np.exp(lg * chunk)
    qd = np.broadcast_to(qd[:, :, None], (RET_HEADS, RET_CHUNK, RET_V_DIM))
    kd = np.broadcast_to(kd[:, :, None], (RET_HEADS, RET_CHUNK, RET_QK_DIM))
    gc = np.broadcast_to(gc[:, None, None], (RET_HEADS, RET_QK_DIM, RET_V_DIM))
    return [jnp.asarray(t, F32) for t in (dec, qd, kd, gc)]


def _ret_prompt(rq, rk, rv, batch, seq):
    dec, qd, kd, gc = _ret_tables(RET_CHUNK)
    row = lambda b: (b, 0)
    return pl.pallas_call(
        functools.partial(_ret_kernel, n_chunks=seq // RET_CHUNK),
        grid=(batch,),
        in_specs=[pl.BlockSpec((seq, RET_QK), row), pl.BlockSpec((seq, RET_QK), row), pl.BlockSpec((seq, RET_V), row),
                  _resident(dec.shape), _resident(qd.shape), _resident(kd.shape), _resident(gc.shape)],
        out_specs=[pl.BlockSpec((seq, RET_V), row),
                   pl.BlockSpec((1, RET_HEADS, RET_QK_DIM, RET_V_DIM), lambda b: (b, 0, 0, 0))],
        out_shape=[jax.ShapeDtypeStruct((batch * seq, RET_V), F32),
                   jax.ShapeDtypeStruct((batch, RET_HEADS, RET_QK_DIM, RET_V_DIM), F32)],
        scratch_shapes=[pltpu.VMEM((RET_HEADS, RET_QK_DIM, RET_V_DIM), F32)],
        compiler_params=_params(),
        name="ret_prompt",
    )(rq, rk, rv, dec, qd, kd, gc)


def _ret_sample_kernel(q_ref, k_ref, kt_ref, v_ref, s_ref, dec_ref, qd_ref, kd_ref, gc_ref, o_ref, st_ref,
                       *, n_tok, n_b):
    pairs = [(b, h) for b in range(n_b) for h in range(RET_HEADS)]
    qk_of = lambda h: slice(h * RET_QK_DIM, (h + 1) * RET_QK_DIM)
    vv_of = lambda h: slice(h * RET_V_DIM, (h + 1) * RET_V_DIM)
    zeros = jnp.zeros((LANES - RET_PAD, RET_V_DIM), BF16)
    first = []
    for b, h in pairs:
        q = q_ref[b, :, qk_of(h)]
        k = k_ref[b, :, qk_of(h)]
        v = v_ref[b, :, vv_of(h)]
        s0 = s_ref[b, h]
        from_state = _dot(q.astype(BF16), s0.astype(BF16))
        k_dec_t = (kt_ref[b, h] * kd_ref[h]).astype(BF16)
        v_rows = jnp.concatenate([v.astype(BF16), zeros], axis=0)
        st_ref[b, h] = s0 * gc_ref[h] + _dot(k_dec_t, v_rows)
        scores = [jnp.sum(q * k[i:i + 1, :], axis=-1, keepdims=True) for i in range(n_tok)]
        first.append((from_state, scores))
    second = []
    for (b, h), (from_state, scores) in zip(pairs, first):
        v = v_ref[b, :, vv_of(h)]
        o = from_state * qd_ref[h]
        for i, sc_i in enumerate(scores):
            o = o + (sc_i * dec_ref[h][:, i:i + 1]) * v[i:i + 1, :]
        second.append((o, jnp.mean(o * o, axis=-1, keepdims=True)))
    for (b, h), (o, ms) in zip(pairs, second):
        o_ref[b, :, vv_of(h)] = o * lax.rsqrt(ms + EPS)


def _ret_sample(rq, rk, rv, state, n_tok):
    db = state.shape[0]
    pad = RET_PAD - n_tok
    n_b = 8

    def pad_rows(t):
        t = t.astype(F32).reshape(db, n_tok, t.shape[-1])
        return jnp.pad(t, ((0, 0), (0, pad), (0, 0)))

    q, k, v = pad_rows(rq), pad_rows(rk), pad_rows(rv)
    kt = jnp.pad(k, ((0, 0), (0, LANES - RET_PAD), (0, 0)))
    kt = kt.reshape(db, LANES, RET_HEADS, RET_QK_DIM).transpose(0, 2, 3, 1)
    lg = _log_gamma()
    idx = np.arange(RET_PAD, dtype=np.float64)
    diff = idx[:, None] - idx[None, :]
    dec = np.where(diff[None] >= 0, np.exp(lg[:, None, None] * np.maximum(diff, 0.0)[None]), 0.0)
    qd = np.broadcast_to(np.exp(lg[:, None] * (idx[None, :] + 1.0))[:, :, None], (RET_HEADS, RET_PAD, RET_V_DIM))
    kd = np.exp(lg[:, None] * (n_tok - 1.0 - np.arange(LANES, dtype=np.float64))[None, :])
    kd = np.broadcast_to(kd[:, None, :], (RET_HEADS, RET_QK_DIM, LANES))
    gc = np.broadcast_to(np.exp(lg * n_tok)[:, None, None], (RET_HEADS, RET_QK_DIM, RET_V_DIM))
    dec, qd, kd, gc = (jnp.asarray(t, F32) for t in (dec, qd, kd, gc))
    b3 = lambda i: (i, 0, 0)
    b4 = lambda i: (i, 0, 0, 0)
    o, st = pl.pallas_call(
        functools.partial(_ret_sample_kernel, n_tok=n_tok, n_b=n_b),
        grid=(db // n_b,),
        in_specs=[pl.BlockSpec((n_b, RET_PAD, RET_QK), b3), pl.BlockSpec((n_b, RET_PAD, RET_QK), b3),
                  pl.BlockSpec((n_b, RET_HEADS, RET_QK_DIM, LANES), b4),
                  pl.BlockSpec((n_b, RET_PAD, RET_V), b3),
                  pl.BlockSpec((n_b, RET_HEADS, RET_QK_DIM, RET_V_DIM), b4),
                  _resident(dec.shape), _resident(qd.shape), _resident(kd.shape), _resident(gc.shape)],
        out_specs=[pl.BlockSpec((n_b, RET_PAD, RET_V), b3),
                   pl.BlockSpec((n_b, RET_HEADS, RET_QK_DIM, RET_V_DIM), b4)],
        out_shape=[jax.ShapeDtypeStruct((db, RET_PAD, RET_V), F32),
                   jax.ShapeDtypeStruct(state.shape, F32)],
        compiler_params=_params(),
        name="ret_sample",
    )(q, k, kt, v, state, dec, qd, kd, gc)
    return o[:, :n_tok].reshape(db * n_tok, RET_V), st


def _attn_kernel(q_ref, k_ref, v_ref, o_ref, l_ref, vt_scr, bias_scr, *staging, dil, nb):
    n_blk = dil * nb
    length = nb * BAND_BLOCK
    blk = BAND_BLOCK
    o_dst, l_dst = staging if dil > 1 else (o_ref, l_ref)

    kk = lax.broadcasted_iota(jnp.int32, (2 * blk, 2 * LANES), 0)
    ql = lax.broadcasted_iota(jnp.int32, (2 * blk, 2 * LANES), 1) & (blk - 1)
    in_span = kk <= ql + blk
    bias_scr[...] = jnp.where(in_span & (kk >= ql), 0.0, NEG)

    def transpose_values(c):
        vt_scr[c] = v_ref[0, c * blk:(c + 1) * blk, :].astype(F32).T.astype(BF16)

    lane = lax.broadcasted_iota(jnp.int32, (blk, LANES), 1)

    def scores(c, p):
        lanes = slice(p * LANES, (p + 1) * LANES)
        k0 = 0 if c % nb == 0 else blk
        q128 = q_ref[0, c * blk:(c + 1) * blk, lanes]
        zero = jnp.zeros_like(q128)
        q_pair = jnp.concatenate([jnp.where(lane < ATT_HEAD_DIM, q128, zero),
                                  jnp.where(lane >= ATT_HEAD_DIM, q128, zero)], axis=0)
        keys = k_ref[0, c * blk - k0:(c + 1) * blk, lanes]
        return _dot_nt(keys, q_pair) + bias_scr[blk - k0:2 * blk, :]

    def weights_values(c, p, s_t):
        lanes = slice(p * LANES, (p + 1) * LANES)
        m = jnp.max(s_t, axis=0, keepdims=True)
        p_t = jnp.exp(s_t - m).astype(BF16)
        v_t = vt_scr[c, lanes, :]
        if c % nb:
            v_t = jnp.concatenate([vt_scr[c - 1, lanes, :], v_t], axis=1)
        v_t = jnp.concatenate([v_t, jnp.ones((2 * SUBLANES, v_t.shape[1]), BF16)], axis=0)
        return _dot(v_t, p_t), m

    def finish(c, p, o_t, m):
        den = o_t[LANES:LANES + 1, :]
        lse = m + jnp.log(den)
        o_pair = jnp.concatenate([o_t[0:ATT_HEAD_DIM, 0:blk] / den[:, 0:blk],
                                  o_t[ATT_HEAD_DIM:LANES, blk:2 * blk] / den[:, blk:2 * blk]], axis=0)
        l_pair = jnp.concatenate([jnp.broadcast_to(lse[:, 0:blk], (ATT_HEAD_DIM, blk)),
                                  jnp.broadcast_to(lse[:, blk:2 * blk], (ATT_HEAD_DIM, blk))], axis=0)
        o_dst[p, c * blk:(c + 1) * blk, :] = o_pair.T
        l_dst[p, c * blk:(c + 1) * blk, :] = l_pair.T

    chains = [(c, p) for c in range(n_blk) for p in range(ATT_OUT // LANES)]
    transpose_values(0)
    s_next = scores(*chains[0])
    pending = None
    for i, chain in enumerate(chains):
        s_cur = s_next
        if chain[1] == 0 and chain[0] + 1 < n_blk:
            transpose_values(chain[0] + 1)
        if i + 1 < len(chains):
            s_next = scores(*chains[i + 1])
        o_m = weights_values(*chain, s_cur)
        if pending is not None:
            finish(*pending)
        pending = chain + o_m
    finish(*pending)

    if dil > 1:
        for p in range(ATT_OUT // LANES):
            for r in range(dil):
                o_ref[p, pl.ds(r, length, stride=dil), :] = o_dst[p, r * length:(r + 1) * length, :]
                l_ref[p, pl.ds(r, length, stride=dil), :] = l_dst[p, r * length:(r + 1) * length, :]


def _attn_prompt(q, k, v, g, batch, seq):
    _, dil = ATT_GROUPS[g]
    nb = seq // dil // BAND_BLOCK
    _log2(nb)
    q, k, v = (t.reshape(batch, seq, ATT_OUT) for t in (q, k, v))
    in_spec = pl.BlockSpec((1, seq, ATT_OUT), lambda b: (b, 0, 0))
    n_pair = ATT_OUT // LANES
    out_spec = pl.BlockSpec((n_pair, seq, LANES), lambda b: (0, b, 0))
    scratch = [pltpu.VMEM((dil * nb, ATT_OUT, BAND_BLOCK), BF16), pltpu.VMEM((2 * BAND_BLOCK, 2 * LANES), F32)]
    if dil > 1:
        scratch += [pltpu.VMEM((n_pair, seq, LANES), F32)] * 2
    return pl.pallas_call(
        functools.partial(_attn_kernel, dil=dil, nb=nb),
        grid=(batch,),
        in_specs=[in_spec] * 3,
        out_specs=[out_spec] * 2,
        out_shape=[jax.ShapeDtypeStruct((n_pair, batch * seq, LANES), F32)] * 2,
        scratch_shapes=scratch,
        compiler_params=_params(),
        name=f"attn_prompt_g{g}",
    )(q, k, v)


def _attn_sample_kernel(q_ref, c_ref, n_ref, o_ref, l_ref, nc_ref, *, g, n_tok, wb, n_b):
    for bb in range(n_b):
        _attn_sample_one(q_ref, c_ref, n_ref, o_ref, l_ref, nc_ref, bb, g=g, n_tok=n_tok, wb=wb, n_b=n_b)


def _attn_sample_one(q_ref, c_ref, n_ref, o_ref, l_ref, nc_ref, bb, **static):
    weights, finish = _attn_sample_phases(q_ref, c_ref, n_ref, o_ref, l_ref, nc_ref, bb, **static)
    finish(weights())


def _attn_sample_phases(q_ref, c_ref, n_ref, o_ref, l_ref, nc_ref, bb, *, g, n_tok, wb, n_b):
    win, dil = ATT_GROUPS[g]
    nk = win // dil
    _log2(dil)
    n_col = ATT_HEADS * n_tok
    first_new = LANES - n_tok
    seq_lane = ((pl.program_id(0) * n_b + bb) * n_tok) % LANES
    new = [pltpu.roll(n_ref[kv], (first_new + LANES - seq_lane) % LANES, 1) for kv in range(2)]
    lane_t = lax.broadcasted_iota(jnp.int32, (ATT_OUT, LANES), 1)
    for kv in range(2):
        shifted = pltpu.roll(c_ref[bb, kv], wb - n_tok, 1)
        if wb > LANES:
            nc_ref[bb, kv, :, 0:wb - LANES] = shifted[:, 0:wb - LANES]
        nc_ref[bb, kv, :, wb - LANES:wb] = jnp.where(lane_t < first_new, shifted[:, wb - LANES:wb], new[kv])

    q = q_ref[bb][:, g * ATT_OUT:(g + 1) * ATT_OUT]
    row = lax.broadcasted_iota(jnp.int32, (n_col, ATT_OUT), 0)
    lane = lax.broadcasted_iota(jnp.int32, (n_col, ATT_OUT), 1)
    head_sel = (lane >> _log2(ATT_HEAD_DIM)) == (row >> _log2(n_tok))
    qbd = jnp.where(head_sel, q, jnp.zeros_like(q))
    k_c, v_c = c_ref[bb, 0].astype(BF16), c_ref[bb, 1].astype(BF16)
    k_n, v_n = new[0].astype(BF16), new[1].astype(BF16)

    def weights():
        s_c = _dot(qbd, k_c)
        s_n = _dot(qbd, k_n)
        t_c = lax.broadcasted_iota(jnp.int32, (n_col, wb), 0) & (n_tok - 1)
        d_c = wb + t_c - lax.broadcasted_iota(jnp.int32, (n_col, wb), 1)
        ok_c = ((d_c & (dil - 1)) == 0) & (d_c <= nk * dil)
        t_n = lax.broadcasted_iota(jnp.int32, (n_col, LANES), 0) & (n_tok - 1)
        new_idx = lax.broadcasted_iota(jnp.int32, (n_col, LANES), 1) - first_new
        d_n = t_n - new_idx
        ok_n = (new_idx >= 0) & (d_n >= 0) & ((d_n & (dil - 1)) == 0) & (d_n <= nk * dil)
        s_c = jnp.where(ok_c, s_c, NEG)
        s_n = jnp.where(ok_n, s_n, NEG)
        m = jnp.maximum(jnp.max(s_c, axis=-1, keepdims=True), jnp.max(s_n, axis=-1, keepdims=True))
        p_c = jnp.exp(s_c - m)
        p_n = jnp.exp(s_n - m)
        den = jnp.sum(p_c, axis=-1, keepdims=True) + jnp.sum(p_n, axis=-1, keepdims=True)
        return p_c.astype(BF16), p_n.astype(BF16), m, den

    def finish(state):
        p_c, p_n, m, den = state
        o = (_dot_nt(p_c, v_c) + _dot_nt(p_n, v_n)) / den
        lse = jnp.broadcast_to(m + jnp.log(den), (n_col, ATT_OUT))
        o = jnp.where(head_sel, o, 0.0)
        lse = jnp.where(head_sel, lse, 0.0)
        o_t = o[0:n_tok]
        l_t = lse[0:n_tok]
        for h in range(1, ATT_HEADS):
            o_t = o_t + o[h * n_tok:(h + 1) * n_tok]
            l_t = l_t + lse[h * n_tok:(h + 1) * n_tok]
        o_ref[bb] = o_t
        l_ref[bb] = l_t

    return weights, finish


def _attn_sample(q_tiled, cache, kv_new, g, n_tok):
    db, _, _, wb = cache.shape
    n_b = max(1, min(SUBLANES, ATT_GROUPS[-1][0] // wb))
    assert wb % LANES == 0 and db % n_b == 0 and LANES % (n_b * n_tok) == 0 and kv_new.shape[2] % LANES == 0
    b3 = lambda i: (i, 0, 0)
    b4 = lambda i: (i, 0, 0, 0)
    return pl.pallas_call(
        functools.partial(_attn_sample_kernel, g=g, n_tok=n_tok, wb=wb, n_b=n_b),
        grid=(db // n_b,),
        in_specs=[pl.BlockSpec((n_b, ATT_HEADS * n_tok, ATT_W), b3),
                  pl.BlockSpec((n_b, 2, ATT_OUT, wb), b4),
                  pl.BlockSpec((2, ATT_OUT, LANES), lambda i: (0, 0, i * n_b * n_tok // LANES))],
        out_specs=[pl.BlockSpec((n_b, n_tok, ATT_OUT), b3), pl.BlockSpec((n_b, n_tok, ATT_OUT), b3),
                   pl.BlockSpec((n_b, 2, ATT_OUT, wb), b4)],
        out_shape=[jax.ShapeDtypeStruct((db, n_tok, ATT_OUT), F32), jax.ShapeDtypeStruct((db, n_tok, ATT_OUT), F32),
                   jax.ShapeDtypeStruct(cache.shape, F32)],
        compiler_params=_params(),
        name=f"attn_sample_g{g}",
    )(q_tiled, cache, kv_new)


def _merge_kernel(x_ref, on_ref, o0_ref, o1_ref, o2_ref, l0_ref, l1_ref, l2_ref, gpre_ref, wg_ref, wbr_ref,
                  wba_ref, wout_ref, gpost_ref, gffn_ref, x1_ref, h2_ref, *perm, tm, n_split, permute):
    rows_of = [slice(i * tm // n_split, (i + 1) * tm // n_split) for i in range(n_split)]
    n_pair = ATT_OUT // LANES

    def gates(rows):
        x = x_ref[rows, :]
        h = _rms(x, gpre_ref[...]).astype(BF16)
        return [_dot(h, wg_ref[:, c:c + D_MODEL]) for c in (0, RET_V, RET_V + D_MODEL)]

    def branches(rows, rg):
        r = (on_ref[rows, :] * (rg * jax.nn.sigmoid(rg))).astype(BF16)
        atts = []
        for p in range(n_pair):
            l0, l1, l2 = l0_ref[p, rows, :], l1_ref[p, rows, :], l2_ref[p, rows, :]
            mx = jnp.maximum(jnp.maximum(l0, l1), l2)
            e0, e1, e2 = jnp.exp(l0 - mx), jnp.exp(l1 - mx), jnp.exp(l2 - mx)
            att = (e0 * o0_ref[p, rows, :] + e1 * o1_ref[p, rows, :] + e2 * o2_ref[p, rows, :]) / (e0 + e1 + e2)
            atts.append(att.astype(BF16))
        return _dot(r, wbr_ref[...]), _dot(jnp.concatenate(atts, axis=1), wba_ref[...])

    def mixed(gr, ga, br, ba):
        mix = jax.nn.sigmoid(gr) * br + jax.nn.sigmoid(ga) * ba
        return _dot(mix.astype(BF16), wout_ref[...])

    def finish(rows, mo):
        x1 = x_ref[rows, :] + _rms(mo, gpost_ref[...])
        x1_ref[rows, :] = x1
        return _rms(x1, gffn_ref[...])

    g = [gates(rows) for rows in rows_of]
    b = [branches(rows, gi[0]) for rows, gi in zip(rows_of, g)]
    mo = [mixed(gi[1], gi[2], *bi) for gi, bi in zip(g, b)]
    h2 = [finish(rows, moi) for rows, moi in zip(rows_of, mo)]
    if not permute:
        for rows, h2i in zip(rows_of, h2):
            h2_ref[rows, :] = h2i.astype(BF16)
        return
    perm_ref, = perm
    nv = tm // SUBLANES
    assert nv % (tm // n_split) == 0 or (tm // n_split) % nv == 0
    for lc in range(D_MODEL // LANES):
        lanes = slice(lc * LANES, (lc + 1) * LANES)
        for s in range(SUBLANES):
            tok0 = s * nv
            i, off = divmod(tok0, tm // n_split)
            perm_ref[lc, pl.ds(s, nv, stride=SUBLANES), :] = h2[i][off:off + nv, lanes]
        h2_ref[:, lanes] = perm_ref[lc].astype(BF16)


def _merge(x2d, on, outs, lses, g_pre, w_gates, w_br, w_ba, w_out, g_post, g_ffn, tm, permute):
    m = x2d.shape[0]
    n_split = 2 if tm >= 2 * MXU_N else 1
    scratch = [pltpu.VMEM((D_MODEL // LANES, tm, LANES), F32)] if permute else []
    row = lambda i: (i, 0)
    vec = _resident((1, D_MODEL))
    att_spec = pl.BlockSpec((ATT_OUT // LANES, tm, LANES), lambda i: (0, i, 0))
    return pl.pallas_call(
        functools.partial(_merge_kernel, tm=tm, n_split=n_split, permute=permute),
        grid=(m // tm,),
        in_specs=[pl.BlockSpec((tm, D_MODEL), row), pl.BlockSpec((tm, RET_V), row)] + [att_spec] * 6 + [
            vec, _resident(w_gates.shape), _resident(w_br.shape), _resident(w_ba.shape), _resident(w_out.shape),
            vec, vec],
        out_specs=[pl.BlockSpec((tm, D_MODEL), row), pl.BlockSpec((tm, D_MODEL), row)],
        out_shape=[jax.ShapeDtypeStruct((m, D_MODEL), F32), jax.ShapeDtypeStruct((m, D_MODEL), BF16)],
        scratch_shapes=scratch,
        compiler_params=_params(),
        name="merge",
    )(x2d, on, *outs, *lses, g_pre, w_gates, w_br, w_ba, w_out, g_post, g_ffn)


def _ffn_kernel(*refs, tm, tiles_per_seq, seq_tok, side=None):
    if side is not None:
        (sq_ref, sc_ref, sn_ref), refs = refs[:3], refs[3:]
        so_ref, sl_ref, snc_ref = refs[9:12]
        refs = refs[:9] + refs[12:]
        side_refs = (sq_ref, sc_ref, sn_ref, so_ref, sl_ref, snc_ref)
    if seq_tok is None:
        h2_ref, x1_ref, wup_ref, wdn_ref, cw_ref, cb_ref, g_ref, y_ref, cs_ref, carry_ref, f_ref, perm_ref = refs
    else:
        h2_ref, x1_ref, p1_ref, p2_ref, wup_ref, wdn_ref, cw_ref, cb_ref, g_ref, y_ref, u_ref, f_ref = refs
    h2 = h2_ref[...]
    if seq_tok is None:
        sub = lax.broadcasted_iota(jnp.int32, (SUBLANES, FF_CHUNK), 0)

        @pl.when(pl.program_id(0) % tiles_per_seq == 0)
        def _():
            carry_ref[...] = jnp.zeros_like(carry_ref)
    else:
        t = lax.broadcasted_iota(jnp.int32, (tm, FF_CHUNK), 0) & (seq_tok - 1)
        _log2(seq_tok)

    def up(j):
        return [_dot(h2, wup_ref[:, c:c + FF_CHUNK]) for c in (j * FF_CHUNK, D_FF + j * FF_CHUNK)]

    def conv(u, col):
        cols = slice(col, col + FF_CHUNK)
        if seq_tok is None:
            prev = carry_ref[:, cols]
            last2 = jnp.where(sub == 0, prev[SUBLANES - 1:SUBLANES], pltpu.roll(u[tm - 2 * SUBLANES:tm - SUBLANES], 1, 0))
            last1 = jnp.where(sub == 0, prev[2 * SUBLANES - 1:2 * SUBLANES], pltpu.roll(u[tm - SUBLANES:tm], 1, 0))
            u1 = jnp.concatenate([last1, u[0:tm - SUBLANES]], axis=0)
            u2 = jnp.concatenate([last2, last1, u[0:tm - 2 * SUBLANES]], axis=0)
            carry_ref[:, cols] = u[tm - 2 * SUBLANES:tm]
        else:
            u1 = jnp.where(t == 0, p1_ref[:, cols], pltpu.roll(u, 1, 0))
            u2 = jnp.where(t <= 1, p2_ref[:, cols], pltpu.roll(u, 2, 0))
            u_ref[:, cols] = u
        cw = cw_ref[:, cols]
        return cb_ref[:, cols] + cw[0:1] * u2 + cw[1:2] * u1 + cw[2:3] * u

    def gate(j, ug, uv):
        cg = conv(ug, j * FF_CHUNK)
        cv = conv(uv, D_FF + j * FF_CHUNK)
        f_ref[:, j * FF_CHUNK:(j + 1) * FF_CHUNK] = (jax.nn.gelu(cg, approximate=True) * cv).astype(BF16)

    n_chunks = D_FF // FF_CHUNK
    u_next = up(0)
    for j in range(n_chunks):
        u_cur = u_next
        if j + 1 < n_chunks:
            u_next = up(j + 1)
        if j == 0 and side is not None:
            side_weights, side_finish = _attn_sample_phases(*side_refs, 0, g=side[0], n_tok=side[1], wb=side[2],
                                                            n_b=1)
        gate(j, *u_cur)
    if side is not None:
        side_state = side_weights()
    down = _dot(f_ref[...], wdn_ref[...])
    if side is not None:
        side_finish(side_state)
    out = _rms(down, g_ref[...])
    if seq_tok is None:
        nv = tm // SUBLANES
        for lc in range(D_MODEL // LANES):
            lanes = slice(lc * LANES, (lc + 1) * LANES)
            perm_ref[lc] = out[:, lanes]
            for s in range(SUBLANES):
                rows = slice(s * nv, (s + 1) * nv)
                y_ref[rows, lanes] = x1_ref[rows, lanes] + perm_ref[lc, pl.ds(s, nv, stride=SUBLANES), :]

        @pl.when(pl.program_id(0) % tiles_per_seq == tiles_per_seq - 1)
        def _():
            cs_ref[0, 0:1, :] = carry_ref[SUBLANES - 1:SUBLANES, :]
            cs_ref[0, 1:2, :] = carry_ref[2 * SUBLANES - 1:2 * SUBLANES, :]
    else:
        y_ref[...] = x1_ref[...] + out


def _ffn_prompt(h2, x1, w_up, w_dn, conv_w, conv_b, g_post, batch, seq, tm, side=None):
    m = h2.shape[0]
    tps = seq // tm
    n_steps = m // tm
    row = lambda i: (i, 0)
    in_specs = [pl.BlockSpec((tm, D_MODEL), row), pl.BlockSpec((tm, D_MODEL), row),
                _resident(w_up.shape), _resident(w_dn.shape), _resident(conv_w.shape), _resident(conv_b.shape),
                _resident((1, D_MODEL))]
    out_specs = [pl.BlockSpec((tm, D_MODEL), row),
                 pl.BlockSpec((1, CONV_W - 1, 2 * D_FF), lambda i: (i // tps, 0, 0))]
    out_shape = [jax.ShapeDtypeStruct((m, D_MODEL), F32),
                 jax.ShapeDtypeStruct((batch, CONV_W - 1, 2 * D_FF), F32)]
    operands = (h2, x1, w_up, w_dn, conv_w, conv_b, g_post)
    side_static = None
    if side is not None:
        g, n_tok, q_tiled, cache, kv_new = side
        db, _, _, wb = cache.shape
        assert db == n_steps and LANES % n_tok == 0, (db, n_steps)
        b3 = lambda i: (i, 0, 0)
        b4 = lambda i: (i, 0, 0, 0)
        in_specs = [pl.BlockSpec((1, ATT_HEADS * n_tok, ATT_W), b3), pl.BlockSpec((1, 2, ATT_OUT, wb), b4),
                    pl.BlockSpec((2, ATT_OUT, LANES), lambda i: (0, 0, i * n_tok // LANES))] + in_specs
        out_specs += [pl.BlockSpec((1, n_tok, ATT_OUT), b3), pl.BlockSpec((1, n_tok, ATT_OUT), b3),
                      pl.BlockSpec((1, 2, ATT_OUT, wb), b4)]
        out_shape += [jax.ShapeDtypeStruct((db, n_tok, ATT_OUT), F32)] * 2 + [jax.ShapeDtypeStruct(cache.shape, F32)]
        operands = (q_tiled, cache, kv_new) + operands
        side_static = (g, n_tok, wb)
    return pl.pallas_call(
        functools.partial(_ffn_kernel, tm=tm, tiles_per_seq=tps, seq_tok=None, side=side_static),
        grid=(n_steps,),
        in_specs=in_specs,
        out_specs=out_specs,
        out_shape=out_shape,
        scratch_shapes=[pltpu.VMEM((2 * SUBLANES, 2 * D_FF), F32), pltpu.VMEM((tm, D_FF), BF16),
                        pltpu.VMEM((D_MODEL // LANES, tm, LANES), F32)],
        compiler_params=_params(),
        name="ffn_prompt",
    )(*operands)


def _ffn_sample(h2, x1, p1, p2, w_up, w_dn, conv_w, conv_b, g_post, n_tok):
    m = h2.shape[0]
    const = lambda i: (0, 0)
    full = lambda a: _resident(a.shape)
    return pl.pallas_call(
        functools.partial(_ffn_kernel, tm=m, tiles_per_seq=1, seq_tok=n_tok),
        grid=(1,),
        in_specs=[full(h2), full(x1), full(p1), full(p2), full(w_up), full(w_dn), full(conv_w), full(conv_b),
                  _resident((1, D_MODEL))],
        out_specs=[pl.BlockSpec((m, D_MODEL), const), pl.BlockSpec((m, 2 * D_FF), const)],
        out_shape=[jax.ShapeDtypeStruct((m, D_MODEL), F32), jax.ShapeDtypeStruct((m, 2 * D_FF), F32)],
        scratch_shapes=[pltpu.VMEM((m, D_FF), BF16)],
        compiler_params=_params(),
        name="ffn_sample",
    )(h2, x1, p1, p2, w_up, w_dn, conv_w, conv_b, g_post)


def _rot_tables(pos):
    pos = np.asarray(pos, np.float64)
    rf = RET_THETA ** (-np.linspace(0.0, 1.0, RET_QK_DIM // 2))
    ang = pos[:, None] * rf[None, :]
    cos, sin = np.cos(ang), np.sin(ang)
    cr = np.concatenate([cos, cos], axis=-1)
    sr = np.concatenate([-sin, sin], axis=-1)
    k_scale = RET_QK_DIM ** -0.5
    half = ROPE_DIM // 2
    af = ROPE_THETA ** (-np.arange(half) / half)
    ang = pos[:, None] * af[None, :]
    cos, sin = np.cos(ang), np.sin(ang)
    n = pos.shape[0]
    rest = ATT_HEAD_DIM - ROPE_DIM
    zh = np.zeros((n, half))
    ca = np.concatenate([cos, cos, np.ones((n, rest))], axis=-1)
    sa1 = np.concatenate([zh, sin, np.zeros((n, rest))], axis=-1)
    sa2 = np.concatenate([-sin, zh, np.zeros((n, rest))], axis=-1)
    rep = MXU_N // ATT_HEAD_DIM
    ca, sa1, sa2 = (np.tile(t, (1, rep)) for t in (ca, sa1, sa2))
    return [jnp.asarray(t, F32) for t in (cr, sr, cr * k_scale, sr * k_scale, ca, sa1, sa2)]


def kernel(x_prompt, x_sample, cache_kv_g0, cache_kv_g1, cache_kv_g2, state_ret, state_conv, norm_mix_pre, w_in,
           w_branch_ret, w_branch_attn, w_out, norm_mix_post, norm_ffn_pre, w_ffn_up, conv_w, conv_b, w_ffn_down,
           norm_ffn_post):
    batch, seq, _ = x_prompt.shape
    db, n_tok, _ = x_sample.shape
    depth = w_in.shape[0]
    assert depth == 1 and seq % (ATT_GROUPS[-1][1] * BAND_BLOCK) == 0 and n_tok <= SUBLANES
    caches = (cache_kv_g0, cache_kv_g1, cache_kv_g2)

    wi = w_in[0]
    o_rg = 2 * RET_QK + RET_V
    o_aq = o_rg + RET_V
    o_gr = o_aq + 3 * ATT_W
    w_qkv = jnp.concatenate([wi[:, :o_rg], wi[:, o_aq:o_aq + ATT_W] * ATT_HEAD_DIM ** -0.5,
                             wi[:, o_aq + ATT_W:o_gr]], axis=1).astype(BF16)
    w_gates = jnp.concatenate([wi[:, o_rg:o_aq], wi[:, o_gr:]], axis=1).astype(BF16)
    later_weights = (w_branch_ret[0], w_branch_attn[0], w_out[0], w_ffn_up[0], w_ffn_down[0])
    cb = conv_b[0][None, :]
    cw = conv_w[0]
    g_pre, g_post, g_ffn, g_post2 = (t[0][None, :] for t in (norm_mix_pre, norm_mix_post, norm_ffn_pre, norm_ffn_post))

    tm = 512
    xp = x_prompt.reshape(batch * seq, D_MODEL)
    proj_out = _proj(xp, g_pre, w_qkv, _rot_tables(np.arange(seq)), tm, seq=seq, casts=later_weights)
    rq, rk, rv = proj_out[:3]
    qkv_res = proj_out[3:3 + 3 * N_GROUPS]
    kv_last = proj_out[3 + 3 * N_GROUPS:3 + 4 * N_GROUPS]
    w_br, w_ba, w_o, w_up, w_dn = proj_out[3 + 4 * N_GROUPS:]
    on, p_ret = _ret_prompt(rq, rk, rv, batch, seq)
    outs, lses = [], []
    for g in range(N_GROUPS):
        o_g, l_g = _attn_prompt(*qkv_res[3 * g:3 * g + 3], g, batch, seq)
        outs.append(o_g)
        lses.append(l_g)
    x1, h2 = _merge(xp, on, outs, lses, g_pre, w_gates, w_br, w_ba, w_o, g_post, g_ffn, tm, permute=True)
    p_kv = [t.reshape(batch, 2, ATT_HEADS, ATT_HEAD_DIM, t.shape[3]).transpose(0, 4, 1, 2, 3)[None] for t in kv_last]

    ms = db * n_tok
    xs = x_sample.reshape(ms, D_MODEL)
    pos_s = np.tile(PAST_LEN + np.arange(n_tok), db)
    rq, rk, rv, aq, kv0, kv1, kv2 = _proj(xs, g_pre, w_qkv, _rot_tables(pos_s), ms)
    kvs = (kv0, kv1, kv2)
    on_s, s_ret = _ret_sample(rq, rk, rv, state_ret[0], n_tok)
    q_tiled = jnp.tile(aq.reshape(db, n_tok, ATT_W), (1, ATT_HEADS, 1))
    ride = N_GROUPS - 1 if batch * seq // tm == db else None
    sample_attn = {}
    for g in range(N_GROUPS):
        cache = caches[g][0]
        wb = cache.shape[1]
        cache_t = cache.transpose(0, 2, 3, 4, 1).reshape(db, 2, ATT_OUT, wb)
        kv_new = kvs[g].reshape(ms, 2, ATT_OUT).transpose(1, 2, 0)
        if g == ride:
            y_p, p_conv, *sample_attn[g] = _ffn_prompt(h2, x1, w_up, w_dn, cw, cb, g_post2, batch, seq, tm,
                                                        side=(g, n_tok, q_tiled, cache_t, kv_new))
        else:
            sample_attn[g] = _attn_sample(q_tiled, cache_t, kv_new, g, n_tok)
    if ride is None:
        y_p, p_conv = _ffn_prompt(h2, x1, w_up, w_dn, cw, cb, g_post2, batch, seq, tm)
    outs, lses, s_kv = [], [], []
    for g in range(N_GROUPS):
        o_g, l_g, nc = sample_attn[g]
        outs.append(o_g.reshape(ms, ATT_OUT // LANES, LANES).transpose(1, 0, 2))
        lses.append(l_g.reshape(ms, ATT_OUT // LANES, LANES).transpose(1, 0, 2))
        s_kv.append(nc.reshape(db, 2, ATT_HEADS, ATT_HEAD_DIM, nc.shape[3]).transpose(0, 4, 1, 2, 3)[None])
    x1, h2 = _merge(xs, on_s, outs, lses, g_pre, w_gates, w_br, w_ba, w_o, g_post, g_ffn, ms, permute=False)
    st = state_conv[0]
    zeros = jnp.zeros((db, n_tok - 1, 2 * D_FF), F32)
    p1 = jnp.concatenate([st[:, 1:2], zeros], axis=1).reshape(ms, 2 * D_FF)
    p2 = jnp.concatenate([st, zeros[:, :n_tok - 2]], axis=1).reshape(ms, 2 * D_FF)
    y_s, u_s = _ffn_sample(h2, x1, p1, p2, w_up, w_dn, cw, cb, g_post2, n_tok)
    s_conv = u_s.reshape(db, n_tok, 2 * D_FF)[:, n_tok - (CONV_W - 1):][None]

    return (y_p.reshape(x_prompt.shape), y_s.reshape(x_sample.shape), p_kv[0], p_kv[1], p_kv[2], p_ret[None],
            p_conv[None], s_kv[0], s_kv[1], s_kv[2], s_ret[None], s_conv)
```

```python
import functools

import numpy as np
import jax
import jax.numpy as jnp
from jax import lax
from jax.experimental import pallas as pl
from jax.experimental.pallas import tpu as pltpu

F32 = jnp.float32
BF16 = jnp.bfloat16

D_MODEL = 1024
PAST_LEN = 16384
RET_HEADS = 4
RET_QK_DIM = 128
RET_V_DIM = 256
RET_CHUNK = 128
RET_THETA = 10000.0
RET_QK = RET_HEADS * RET_QK_DIM
RET_V = RET_HEADS * RET_V_DIM
ATT_GROUPS = ((128, 1), (512, 4), (2048, 16))
N_GROUPS = 3
ATT_HEADS = 4
ATT_HEAD_DIM = 64
ROPE_DIM = ATT_HEAD_DIM // 4
ROPE_THETA = 500000.0
BAND_BLOCK = 128
ATT_OUT = ATT_HEADS * ATT_HEAD_DIM
ATT_W = N_GROUPS * ATT_OUT
D_FF = 2816
CONV_W = 3
EPS = 1e-6

LANES = 128
SUBLANES = 8
MXU_N = 256
VMEM_LIMIT = 56 * 1024 * 1024
NEG = -1e30
FF_CHUNK = MXU_N
RET_PAD = 2 * SUBLANES
QKV_COLS = 2 * RET_QK + RET_V + 3 * ATT_W


def _rms(x, g):
    return x * lax.rsqrt(jnp.mean(x * x, axis=-1, keepdims=True) + EPS) * g


def _dot(a, b):
    return jnp.dot(a, b, preferred_element_type=F32)


def _dot_nt(a, b):
    return lax.dot_general(a, b, (((1,), (1,)), ((), ())), preferred_element_type=F32)


def _resident(shape):
    return pl.BlockSpec(shape, lambda *_: (0,) * len(shape), pipeline_mode=pl.Buffered(1))


def _log2(n):
    assert n > 0 and n & (n - 1) == 0, n
    return n.bit_length() - 1


def _params(n_axes=1):
    return pltpu.CompilerParams(dimension_semantics=("arbitrary",) * n_axes, vmem_limit_bytes=VMEM_LIMIT)


def _proj_kernel(x_ref, g_ref, w_ref, crq_ref, srq_ref, crk_ref, srk_ref, ca_ref, sa1_ref, sa2_ref, *rest,
                 tm, by_residue, tiles_per_seq, n_cast):
    cast_in, rest = rest[:n_cast], rest[n_cast:]
    rq_ref, rk_ref, rv_ref = rest[:3]
    if by_residue:
        res_refs = rest[3:3 + 3 * N_GROUPS]
        kv_refs = rest[3 + 3 * N_GROUPS:3 + 4 * N_GROUPS]
        cast_out = rest[3 + 4 * N_GROUPS:3 + 4 * N_GROUPS + n_cast]
        scr = rest[3 + 4 * N_GROUPS + n_cast]
    else:
        aq_ref = rest[3]
        kv_refs = rest[4:4 + N_GROUPS]
        cast_out = rest[4 + N_GROUPS:4 + N_GROUPS + n_cast]
    for src, dst in zip(cast_in, cast_out):
        dst[...] = src[...].astype(BF16)
    h = _rms(x_ref[...], g_ref[...]).astype(BF16)

    def mm(c0, width=MXU_N):
        return _dot(h, w_ref[:, c0:c0 + width])

    def rot_ret(t, c, s):
        return t * c + pltpu.roll(t, RET_QK_DIM // 2, 1) * s

    ca, sa1, sa2 = ca_ref[...], sa1_ref[...], sa2_ref[...]

    def rot_att(t):
        return t * ca + pltpu.roll(t, ROPE_DIM // 2, 1) * sa1 + pltpu.roll(t, MXU_N - ROPE_DIM // 2, 1) * sa2

    def put_by_residue(ref, val, dil):
        if dil == 1:
            ref[0, 0] = val.astype(BF16)
            return
        for half in range(MXU_N // LANES):
            lanes = slice(half * LANES, (half + 1) * LANES)
            scr[half] = val[:, lanes]
            for r in range(dil):
                ref[0, r, :, lanes] = scr[half, pl.ds(r, tm // dil, stride=dil), :].astype(BF16)

    def put_ret(ref, c, c_ref, s_ref):
        def put(t):
            for half in range(MXU_N // LANES):
                lo = half * LANES
                col = c * MXU_N + lo
                ref[:, col:col + LANES] = rot_ret(t[:, lo:lo + LANES], c_ref[...], s_ref[...]).astype(BF16)
        return put

    def put_rv(c):
        def put(t):
            rv_ref[:, c * MXU_N:(c + 1) * MXU_N] = t.astype(BF16)
        return put

    def put_att(g, which):
        def put(t):
            if which < 2:
                t = rot_att(t)
            if not by_residue:
                if which == 0:
                    aq_ref[:, g * ATT_OUT:(g + 1) * ATT_OUT] = t.astype(BF16)
                else:
                    kv_refs[g][:, (which - 1) * ATT_OUT:which * ATT_OUT] = t
                return
            put_by_residue(res_refs[3 * g + which], t, ATT_GROUPS[g][1])
            if which == 0:
                return
            rows = kv_refs[g].shape[3]

            def put_window():
                kv_refs[g][0, which - 1] = t[tm - rows:tm].T

            if ATT_GROUPS[g][0] >= tiles_per_seq * tm:
                put_window()
            else:
                pl.when(pl.program_id(0) % tiles_per_seq == tiles_per_seq - 1)(put_window)
        return put

    base = 2 * RET_QK + RET_V
    groups = []
    for c in range(RET_QK // MXU_N):
        groups.append([(c * MXU_N, put_ret(rq_ref, c, crq_ref, srq_ref)),
                       (RET_QK + c * MXU_N, put_ret(rk_ref, c, crk_ref, srk_ref))])
    for c in range(RET_V // MXU_N):
        groups.append([(2 * RET_QK + c * MXU_N, put_rv(c))])
    for g in range(N_GROUPS):
        groups.append([(base + which * ATT_W + g * ATT_OUT, put_att(g, which)) for which in range(3)])
    for group in groups:
        results = [mm(col) for col, _ in group]
        for (_, epilogue), t in zip(group, results):
            epilogue(t)


def _slab_rows(rows, n_steps):
    tile = 2 * SUBLANES
    for k in range(-(-rows // (n_steps * tile)), rows // tile + 1):
        if rows % (k * tile) == 0:
            return k * tile
    raise ValueError((rows, n_steps))


def _proj(x2d, g_pre, w_qkv, tabs, tm, seq=None, casts=()):
    m = x2d.shape[0]
    n_steps = m // tm
    n_tab = tabs[0].shape[0] // tm
    row = lambda i: (i, 0)
    tab = lambda i: (i % n_tab, 0)
    in_specs = [pl.BlockSpec((tm, D_MODEL), row), _resident((1, D_MODEL)), _resident((D_MODEL, QKV_COLS))]
    in_specs += [pl.BlockSpec((tm, LANES), tab)] * 4 + [pl.BlockSpec((tm, MXU_N), tab)] * 3
    out_specs = [pl.BlockSpec((tm, w), row) for w in (RET_QK, RET_QK, RET_V)]
    out_shape = [jax.ShapeDtypeStruct((m, w), BF16) for w in (RET_QK, RET_QK, RET_V)]
    scratch = []
    if seq is None:
        widths = (ATT_W,) + (2 * ATT_OUT,) * N_GROUPS
        out_specs += [pl.BlockSpec((tm, w), row) for w in widths]
        out_shape += [jax.ShapeDtypeStruct((m, w), d) for w, d in zip(widths, (BF16,) + (F32,) * N_GROUPS)]
    else:
        batch, tps = m // seq, seq // tm
        for _, dil in ATT_GROUPS:
            assert tm % (dil * 2 * SUBLANES) == 0
            out_specs += [pl.BlockSpec((1, dil, tm // dil, ATT_OUT), lambda i: (i // tps, 0, i % tps, 0))] * 3
            out_shape += [jax.ShapeDtypeStruct((batch, dil, seq // dil, ATT_OUT), BF16)] * 3
        for win, _ in ATT_GROUPS:
            win = min(win, seq)
            if win == seq:
                out_specs.append(pl.BlockSpec((1, 2, ATT_OUT, tm), lambda i: (i // tps, 0, 0, i % tps)))
            else:
                assert win <= tm and win % LANES == 0
                out_specs.append(pl.BlockSpec((1, 2, ATT_OUT, win), lambda i: (i // tps, 0, 0, 0)))
            out_shape.append(jax.ShapeDtypeStruct((batch, 2, ATT_OUT, win), F32))
        scratch = [pltpu.VMEM((MXU_N // LANES, tm, LANES), F32)]
    for w in casts:
        rows, cols = w.shape
        slab = _slab_rows(rows, n_steps)
        spec = pl.BlockSpec((slab, cols), lambda i, n=rows // slab: (i * n // n_steps, 0))
        in_specs.append(spec)
        out_specs.append(spec)
        out_shape.append(jax.ShapeDtypeStruct(w.shape, BF16))
    return pl.pallas_call(
        functools.partial(_proj_kernel, tm=tm, by_residue=seq is not None,
                          tiles_per_seq=None if seq is None else seq // tm, n_cast=len(casts)),
        grid=(n_steps,),
        in_specs=in_specs,
        out_specs=out_specs,
        out_shape=out_shape,
        scratch_shapes=scratch,
        compiler_params=_params(),
        name="proj",
    )(x2d, g_pre, w_qkv, *tabs, *casts)


def _ret_kernel(q_ref, k_ref, v_ref, dec_ref, qd_ref, kd_ref, gc_ref, o_ref, st_ref, s_scr, *, n_chunks):
    s_scr[...] = jnp.zeros_like(s_scr)

    def body(c, carry):
        r0 = pl.multiple_of(c * RET_CHUNK, RET_CHUNK)
        rows = pl.ds(r0, RET_CHUNK)
        first = []
        for h in range(RET_HEADS):
            qk = slice(h * RET_QK_DIM, (h + 1) * RET_QK_DIM)
            vv = slice(h * RET_V_DIM, (h + 1) * RET_V_DIM)
            q = q_ref[rows, qk]
            k = k_ref[rows, qk]
            v = v_ref[rows, vv]
            s0 = s_scr[h]
            sc = _dot_nt(q, k)
            from_state = _dot(q, s0.astype(BF16))
            kd_t = (k.astype(F32) * kd_ref[h]).T.astype(BF16)
            first.append((sc, from_state, _dot(kd_t, v), s0, v))
        for h, (sc, from_state, inc, s0, v) in enumerate(first):
            vv = slice(h * RET_V_DIM, (h + 1) * RET_V_DIM)
            o = _dot((sc * dec_ref[h]).astype(BF16), v) + from_state * qd_ref[h]
            s_scr[h] = s0 * gc_ref[h] + inc
            o_ref[rows, vv] = o * lax.rsqrt(jnp.mean(o * o, axis=-1, keepdims=True) + EPS)
        return carry

    lax.fori_loop(0, n_chunks, body, 0, unroll=2)
    st_ref[0] = s_scr[...]


def _log_gamma():
    return np.log1p(-np.exp2(-5.0 - np.arange(RET_HEADS, dtype=np.float64)))


def _ret_tables(chunk):
    lg = _log_gamma()
    idx = np.arange(RET_CHUNK, dtype=np.float64)
    diff = idx[:, None] - idx[None, :]
    dec = np.where(diff[None] >= 0, np.exp(lg[:, None, None] * np.maximum(diff, 0.0)[None]), 0.0)
    qd = np.exp(lg[:, None] * (idx[None, :] + 1.0))
    kd = np.exp(lg[:, None] * (chunk - 1.0 - idx)[None, :])
    gc = np.exp(lg * chunk)
    qd = np.broadcast_to(qd[:, :, None], (RET_HEADS, RET_CHUNK, RET_V_DIM))
    kd = np.broadcast_to(kd[:, :, None], (RET_HEADS, RET_CHUNK, RET_QK_DIM))
    gc = np.broadcast_to(gc[:, None, None], (RET_HEADS, RET_QK_DIM, RET_V_DIM))
    return [jnp.asarray(t, F32) for t in (dec, qd, kd, gc)]


def _ret_prompt(rq, rk, rv, batch, seq):
    dec, qd, kd, gc = _ret_tables(RET_CHUNK)
    row = lambda b: (b, 0)
    return pl.pallas_call(
        functools.partial(_ret_kernel, n_chunks=seq // RET_CHUNK),
        grid=(batch,),
        in_specs=[pl.BlockSpec((seq, RET_QK), row), pl.BlockSpec((seq, RET_QK), row), pl.BlockSpec((seq, RET_V), row),
                  _resident(dec.shape), _resident(qd.shape), _resident(kd.shape), _resident(gc.shape)],
        out_specs=[pl.BlockSpec((seq, RET_V), row),
                   pl.BlockSpec((1, RET_HEADS, RET_QK_DIM, RET_V_DIM), lambda b: (b, 0, 0, 0))],
        out_shape=[jax.ShapeDtypeStruct((batch * seq, RET_V), F32),
                   jax.ShapeDtypeStruct((batch, RET_HEADS, RET_QK_DIM, RET_V_DIM), F32)],
        scratch_shapes=[pltpu.VMEM((RET_HEADS, RET_QK_DIM, RET_V_DIM), F32)],
        compiler_params=_params(),
        name="ret_prompt",
    )(rq, rk, rv, dec, qd, kd, gc)


def _ret_sample_kernel(q_ref, k_ref, kt_ref, v_ref, s_ref, dec_ref, qd_ref, kd_ref, gc_ref, o_ref, st_ref,
                       *, n_tok, n_b):
    pairs = [(b, h) for b in range(n_b) for h in range(RET_HEADS)]
    qk_of = lambda h: slice(h * RET_QK_DIM, (h + 1) * RET_QK_DIM)
    vv_of = lambda h: slice(h * RET_V_DIM, (h + 1) * RET_V_DIM)
    zeros = jnp.zeros((LANES - RET_PAD, RET_V_DIM), BF16)
    first = []
    for b, h in pairs:
        q = q_ref[b, :, qk_of(h)]
        k = k_ref[b, :, qk_of(h)]
        v = v_ref[b, :, vv_of(h)]
        s0 = s_ref[b, h]
        from_state = _dot(q.astype(BF16), s0.astype(BF16))
        k_dec_t = (kt_ref[b, h] * kd_ref[h]).astype(BF16)
        v_rows = jnp.concatenate([v.astype(BF16), zeros], axis=0)
        st_ref[b, h] = s0 * gc_ref[h] + _dot(k_dec_t, v_rows)
        scores = [jnp.sum(q * k[i:i + 1, :], axis=-1, keepdims=True) for i in range(n_tok)]
        first.append((from_state, scores))
    second = []
    for (b, h), (from_state, scores) in zip(pairs, first):
        v = v_ref[b, :, vv_of(h)]
        o = from_state * qd_ref[h]
        for i, sc_i in enumerate(scores):
            o = o + (sc_i * dec_ref[h][:, i:i + 1]) * v[i:i + 1, :]
        second.append((o, jnp.mean(o * o, axis=-1, keepdims=True)))
    for (b, h), (o, ms) in zip(pairs, second):
        o_ref[b, :, vv_of(h)] = o * lax.rsqrt(ms + EPS)


def _ret_sample(rq, rk, rv, state, n_tok):
    db = state.shape[0]
    pad = RET_PAD - n_tok
    n_b = 8

    def pad_rows(t):
        t = t.astype(F32).reshape(db, n_tok, t.shape[-1])
        return jnp.pad(t, ((0, 0), (0, pad), (0, 0)))

    q, k, v = pad_rows(rq), pad_rows(rk), pad_rows(rv)
    kt = jnp.pad(k, ((0, 0), (0, LANES - RET_PAD), (0, 0)))
    kt = kt.reshape(db, LANES, RET_HEADS, RET_QK_DIM).transpose(0, 2, 3, 1)
    lg = _log_gamma()
    idx = np.arange(RET_PAD, dtype=np.float64)
    diff = idx[:, None] - idx[None, :]
    dec = np.where(diff[None] >= 0, np.exp(lg[:, None, None] * np.maximum(diff, 0.0)[None]), 0.0)
    qd = np.broadcast_to(np.exp(lg[:, None] * (idx[None, :] + 1.0))[:, :, None], (RET_HEADS, RET_PAD, RET_V_DIM))
    kd = np.exp(lg[:, None] * (n_tok - 1.0 - np.arange(LANES, dtype=np.float64))[None, :])
    kd = np.broadcast_to(kd[:, None, :], (RET_HEADS, RET_QK_DIM, LANES))
    gc = np.broadcast_to(np.exp(lg * n_tok)[:, None, None], (RET_HEADS, RET_QK_DIM, RET_V_DIM))
    dec, qd, kd, gc = (jnp.asarray(t, F32) for t in (dec, qd, kd, gc))
    b3 = lambda i: (i, 0, 0)
    b4 = lambda i: (i, 0, 0, 0)
    o, st = pl.pallas_call(
        functools.partial(_ret_sample_kernel, n_tok=n_tok, n_b=n_b),
        grid=(db // n_b,),
        in_specs=[pl.BlockSpec((n_b, RET_PAD, RET_QK), b3), pl.BlockSpec((n_b, RET_PAD, RET_QK), b3),
                  pl.BlockSpec((n_b, RET_HEADS, RET_QK_DIM, LANES), b4),
                  pl.BlockSpec((n_b, RET_PAD, RET_V), b3),
                  pl.BlockSpec((n_b, RET_HEADS, RET_QK_DIM, RET_V_DIM), b4),
                  _resident(dec.shape), _resident(qd.shape), _resident(kd.shape), _resident(gc.shape)],
        out_specs=[pl.BlockSpec((n_b, RET_PAD, RET_V), b3),
                   pl.BlockSpec((n_b, RET_HEADS, RET_QK_DIM, RET_V_DIM), b4)],
        out_shape=[jax.ShapeDtypeStruct((db, RET_PAD, RET_V), F32),
                   jax.ShapeDtypeStruct(state.shape, F32)],
        compiler_params=_params(),
        name="ret_sample",
    )(q, k, kt, v, state, dec, qd, kd, gc)
    return o[:, :n_tok].reshape(db * n_tok, RET_V), st


def _attn_kernel(q_ref, k_ref, v_ref, o_ref, l_ref, vt_scr, bias_scr, *staging, dil, nb):
    n_blk = dil * nb
    length = nb * BAND_BLOCK
    blk = BAND_BLOCK
    o_dst, l_dst = staging if dil > 1 else (o_ref, l_ref)

    kk = lax.broadcasted_iota(jnp.int32, (2 * blk, 2 * LANES), 0)
    ql = lax.broadcasted_iota(jnp.int32, (2 * blk, 2 * LANES), 1) & (blk - 1)
    in_span = kk <= ql + blk
    bias_scr[...] = jnp.where(in_span & (kk >= ql), 0.0, NEG)

    def transpose_values(c):
        vt_scr[c] = v_ref[0, c * blk:(c + 1) * blk, :].astype(F32).T.astype(BF16)

    lane = lax.broadcasted_iota(jnp.int32, (blk, LANES), 1)

    def scores(c, p):
        lanes = slice(p * LANES, (p + 1) * LANES)
        k0 = 0 if c % nb == 0 else blk
        q128 = q_ref[0, c * blk:(c + 1) * blk, lanes]
        zero = jnp.zeros_like(q128)
        q_pair = jnp.concatenate([jnp.where(lane < ATT_HEAD_DIM, q128, zero),
                                  jnp.where(lane >= ATT_HEAD_DIM, q128, zero)], axis=0)
        keys = k_ref[0, c * blk - k0:(c + 1) * blk, lanes]
        return _dot_nt(keys, q_pair) + bias_scr[blk - k0:2 * blk, :]

    def weights_values(c, p, s_t):
        lanes = slice(p * LANES, (p + 1) * LANES)
        m = jnp.max(s_t, axis=0, keepdims=True)
        p_t = jnp.exp(s_t - m).astype(BF16)
        v_t = vt_scr[c, lanes, :]
        if c % nb:
            v_t = jnp.concatenate([vt_scr[c - 1, lanes, :], v_t], axis=1)
        v_t = jnp.concatenate([v_t, jnp.ones((2 * SUBLANES, v_t.shape[1]), BF16)], axis=0)
        return _dot(v_t, p_t), m

    def finish(c, p, o_t, m):
        den = o_t[LANES:LANES + 1, :]
        lse = m + jnp.log(den)
        o_pair = jnp.concatenate([o_t[0:ATT_HEAD_DIM, 0:blk] / den[:, 0:blk],
                                  o_t[ATT_HEAD_DIM:LANES, blk:2 * blk] / den[:, blk:2 * blk]], axis=0)
        l_pair = jnp.concatenate([jnp.broadcast_to(lse[:, 0:blk], (ATT_HEAD_DIM, blk)),
                                  jnp.broadcast_to(lse[:, blk:2 * blk], (ATT_HEAD_DIM, blk))], axis=0)
        o_dst[p, c * blk:(c + 1) * blk, :] = o_pair.T
        l_dst[p, c * blk:(c + 1) * blk, :] = l_pair.T

    chains = [(c, p) for c in range(n_blk) for p in range(ATT_OUT // LANES)]
    transpose_values(0)
    s_next = scores(*chains[0])
    pending = None
    for i, chain in enumerate(chains):
        s_cur = s_next
        if chain[1] == 0 and chain[0] + 1 < n_blk:
            transpose_values(chain[0] + 1)
        if i + 1 < len(chains):
            s_next = scores(*chains[i + 1])
        o_m = weights_values(*chain, s_cur)
        if pending is not None:
            finish(*pending)
        pending = chain + o_m
    finish(*pending)

    if dil > 1:
        for p in range(ATT_OUT // LANES):
            for r in range(dil):
                o_ref[p, pl.ds(r, length, stride=dil), :] = o_dst[p, r * length:(r + 1) * length, :]
                l_ref[p, pl.ds(r, length, stride=dil), :] = l_dst[p, r * length:(r + 1) * length, :]


def _attn_prompt(q, k, v, g, batch, seq):
    _, dil = ATT_GROUPS[g]
    nb = seq // dil // BAND_BLOCK
    _log2(nb)
    q, k, v = (t.reshape(batch, seq, ATT_OUT) for t in (q, k, v))
    in_spec = pl.BlockSpec((1, seq, ATT_OUT), lambda b: (b, 0, 0))
    n_pair = ATT_OUT // LANES
    out_spec = pl.BlockSpec((n_pair, seq, LANES), lambda b: (0, b, 0))
    scratch = [pltpu.VMEM((dil * nb, ATT_OUT, BAND_BLOCK), BF16), pltpu.VMEM((2 * BAND_BLOCK, 2 * LANES), F32)]
    if dil > 1:
        scratch += [pltpu.VMEM((n_pair, seq, LANES), F32)] * 2
    return pl.pallas_call(
        functools.partial(_attn_kernel, dil=dil, nb=nb),
        grid=(batch,),
        in_specs=[in_spec] * 3,
        out_specs=[out_spec] * 2,
        out_shape=[jax.ShapeDtypeStruct((n_pair, batch * seq, LANES), F32)] * 2,
        scratch_shapes=scratch,
        compiler_params=_params(),
        name=f"attn_prompt_g{g}",
    )(q, k, v)


def _attn_sample_kernel(q_ref, c_ref, n_ref, o_ref, l_ref, nc_ref, *, g, n_tok, wb, n_b):
    for bb in range(n_b):
        _attn_sample_one(q_ref, c_ref, n_ref, o_ref, l_ref, nc_ref, bb, g=g, n_tok=n_tok, wb=wb, n_b=n_b)


def _attn_sample_one(q_ref, c_ref, n_ref, o_ref, l_ref, nc_ref, bb, **static):
    weights, finish = _attn_sample_phases(q_ref, c_ref, n_ref, o_ref, l_ref, nc_ref, bb, **static)
    finish(weights())


def _attn_sample_phases(q_ref, c_ref, n_ref, o_ref, l_ref, nc_ref, bb, *, g, n_tok, wb, n_b):
    win, dil = ATT_GROUPS[g]
    nk = win // dil
    _log2(dil)
    n_col = ATT_HEADS * n_tok
    first_new = LANES - n_tok
    seq_lane = ((pl.program_id(0) * n_b + bb) * n_tok) % LANES
    new = [pltpu.roll(n_ref[kv], (first_new + LANES - seq_lane) % LANES, 1) for kv in range(2)]
    lane_t = lax.broadcasted_iota(jnp.int32, (ATT_OUT, LANES), 1)
    for kv in range(2):
        shifted = pltpu.roll(c_ref[bb, kv], wb - n_tok, 1)
        if wb > LANES:
            nc_ref[bb, kv, :, 0:wb - LANES] = shifted[:, 0:wb - LANES]
        nc_ref[bb, kv, :, wb - LANES:wb] = jnp.where(lane_t < first_new, shifted[:, wb - LANES:wb], new[kv])

    q = q_ref[bb][:, g * ATT_OUT:(g + 1) * ATT_OUT]
    row = lax.broadcasted_iota(jnp.int32, (n_col, ATT_OUT), 0)
    lane = lax.broadcasted_iota(jnp.int32, (n_col, ATT_OUT), 1)
    head_sel = (lane >> _log2(ATT_HEAD_DIM)) == (row >> _log2(n_tok))
    qbd = jnp.where(head_sel, q, jnp.zeros_like(q))
    k_c, v_c = c_ref[bb, 0].astype(BF16), c_ref[bb, 1].astype(BF16)
    k_n, v_n = new[0].astype(BF16), new[1].astype(BF16)

    def weights():
        s_c = _dot(qbd, k_c)
        s_n = _dot(qbd, k_n)
        t_c = lax.broadcasted_iota(jnp.int32, (n_col, wb), 0) & (n_tok - 1)
        d_c = wb + t_c - lax.broadcasted_iota(jnp.int32, (n_col, wb), 1)
        ok_c = ((d_c & (dil - 1)) == 0) & (d_c <= nk * dil)
        t_n = lax.broadcasted_iota(jnp.int32, (n_col, LANES), 0) & (n_tok - 1)
        new_idx = lax.broadcasted_iota(jnp.int32, (n_col, LANES), 1) - first_new
        d_n = t_n - new_idx
        ok_n = (new_idx >= 0) & (d_n >= 0) & ((d_n & (dil - 1)) == 0) & (d_n <= nk * dil)
        s_c = jnp.where(ok_c, s_c, NEG)
        s_n = jnp.where(ok_n, s_n, NEG)
        m = jnp.maximum(jnp.max(s_c, axis=-1, keepdims=True), jnp.max(s_n, axis=-1, keepdims=True))
        p_c = jnp.exp(s_c - m)
        p_n = jnp.exp(s_n - m)
        den = jnp.sum(p_c, axis=-1, keepdims=True) + jnp.sum(p_n, axis=-1, keepdims=True)
        return p_c.astype(BF16), p_n.astype(BF16), m, den

    def finish(state):
        p_c, p_n, m, den = state
        o = (_dot_nt(p_c, v_c) + _dot_nt(p_n, v_n)) / den
        lse = jnp.broadcast_to(m + jnp.log(den), (n_col, ATT_OUT))
        o = jnp.where(head_sel, o, 0.0)
        lse = jnp.where(head_sel, lse, 0.0)
        o_t = o[0:n_tok]
        l_t = lse[0:n_tok]
        for h in range(1, ATT_HEADS):
            o_t = o_t + o[h * n_tok:(h + 1) * n_tok]
            l_t = l_t + lse[h * n_tok:(h + 1) * n_tok]
        o_ref[bb] = o_t
        l_ref[bb] = l_t

    return weights, finish


def _attn_sample(q_tiled, cache, kv_new, g, n_tok):
    db, _, _, wb = cache.shape
    n_b = max(1, min(SUBLANES, ATT_GROUPS[-1][0] // wb))
    assert wb % LANES == 0 and db % n_b == 0 and LANES % (n_b * n_tok) == 0 and kv_new.shape[2] % LANES == 0
    b3 = lambda i: (i, 0, 0)
    b4 = lambda i: (i, 0, 0, 0)
    return pl.pallas_call(
        functools.partial(_attn_sample_kernel, g=g, n_tok=n_tok, wb=wb, n_b=n_b),
        grid=(db // n_b,),
        in_specs=[pl.BlockSpec((n_b, ATT_HEADS * n_tok, ATT_W), b3),
                  pl.BlockSpec((n_b, 2, ATT_OUT, wb), b4),
                  pl.BlockSpec((2, ATT_OUT, LANES), lambda i: (0, 0, i * n_b * n_tok // LANES))],
        out_specs=[pl.BlockSpec((n_b, n_tok, ATT_OUT), b3), pl.BlockSpec((n_b, n_tok, ATT_OUT), b3),
                   pl.BlockSpec((n_b, 2, ATT_OUT, wb), b4)],
        out_shape=[jax.ShapeDtypeStruct((db, n_tok, ATT_OUT), F32), jax.ShapeDtypeStruct((db, n_tok, ATT_OUT), F32),
                   jax.ShapeDtypeStruct(cache.shape, F32)],
        compiler_params=_params(),
        name=f"attn_sample_g{g}",
    )(q_tiled, cache, kv_new)


def _merge_kernel(*refs, tm, n_split, permute, side=None):
    if side is not None:
        side_in, refs = refs[:3], refs[3:]
    (x_ref, on_ref, o0_ref, o1_ref, o2_ref, l0_ref, l1_ref, l2_ref, gpre_ref, wg_ref, wbr_ref, wba_ref, wout_ref,
     gpost_ref, gffn_ref, x1_ref, h2_ref), perm = refs[:17], refs[17:]
    if side is not None:
        side_refs, perm = side_in + perm[:3], perm[3:]
    rows_of = [slice(i * tm // n_split, (i + 1) * tm // n_split) for i in range(n_split)]
    n_pair = ATT_OUT // LANES

    def gates(rows):
        x = x_ref[rows, :]
        h = _rms(x, gpre_ref[...]).astype(BF16)
        return [_dot(h, wg_ref[:, c:c + D_MODEL]) for c in (0, RET_V, RET_V + D_MODEL)]

    def branches(rows, rg):
        r = (on_ref[rows, :] * (rg * jax.nn.sigmoid(rg))).astype(BF16)
        atts = []
        for p in range(n_pair):
            l0, l1, l2 = l0_ref[p, rows, :], l1_ref[p, rows, :], l2_ref[p, rows, :]
            mx = jnp.maximum(jnp.maximum(l0, l1), l2)
            e0, e1, e2 = jnp.exp(l0 - mx), jnp.exp(l1 - mx), jnp.exp(l2 - mx)
            att = (e0 * o0_ref[p, rows, :] + e1 * o1_ref[p, rows, :] + e2 * o2_ref[p, rows, :]) / (e0 + e1 + e2)
            atts.append(att.astype(BF16))
        return _dot(r, wbr_ref[...]), _dot(jnp.concatenate(atts, axis=1), wba_ref[...])

    def mixed(gr, ga, br, ba):
        mix = jax.nn.sigmoid(gr) * br + jax.nn.sigmoid(ga) * ba
        return _dot(mix.astype(BF16), wout_ref[...])

    def finish(rows, mo):
        x1 = x_ref[rows, :] + _rms(mo, gpost_ref[...])
        x1_ref[rows, :] = x1
        return _rms(x1, gffn_ref[...])

    g = []
    for i, rows in enumerate(rows_of):
        g.append(gates(rows))
        if i == 0 and side is not None:
            side_weights, side_finish = _attn_sample_phases(*side_refs, 0, g=side[0], n_tok=side[1], wb=side[2],
                                                            n_b=1)
    b = [branches(rows, gi[0]) for rows, gi in zip(rows_of, g)]
    if side is not None:
        side_state = side_weights()
    mo = [mixed(gi[1], gi[2], *bi) for gi, bi in zip(g, b)]
    if side is not None:
        side_finish(side_state)
    h2 = [finish(rows, moi) for rows, moi in zip(rows_of, mo)]
    if not permute:
        for rows, h2i in zip(rows_of, h2):
            h2_ref[rows, :] = h2i.astype(BF16)
        return
    perm_ref, = perm
    nv = tm // SUBLANES
    assert nv % (tm // n_split) == 0 or (tm // n_split) % nv == 0
    for lc in range(D_MODEL // LANES):
        lanes = slice(lc * LANES, (lc + 1) * LANES)
        for s in range(SUBLANES):
            tok0 = s * nv
            i, off = divmod(tok0, tm // n_split)
            perm_ref[lc, pl.ds(s, nv, stride=SUBLANES), :] = h2[i][off:off + nv, lanes]
        h2_ref[:, lanes] = perm_ref[lc].astype(BF16)


def _side_job(side, n_steps):
    g, n_tok, q_tiled, cache, kv_new = side
    db, _, _, wb = cache.shape
    assert db == n_steps and LANES % n_tok == 0, (db, n_steps)
    b3 = lambda i: (i, 0, 0)
    b4 = lambda i: (i, 0, 0, 0)
    in_specs = [pl.BlockSpec((1, ATT_HEADS * n_tok, ATT_W), b3), pl.BlockSpec((1, 2, ATT_OUT, wb), b4),
                pl.BlockSpec((2, ATT_OUT, LANES), lambda i: (0, 0, i * n_tok // LANES))]
    out_specs = [pl.BlockSpec((1, n_tok, ATT_OUT), b3), pl.BlockSpec((1, n_tok, ATT_OUT), b3),
                 pl.BlockSpec((1, 2, ATT_OUT, wb), b4)]
    out_shape = [jax.ShapeDtypeStruct((db, n_tok, ATT_OUT), F32)] * 2 + [jax.ShapeDtypeStruct(cache.shape, F32)]
    return in_specs, (q_tiled, cache, kv_new), out_specs, out_shape, (g, n_tok, wb)


def _merge(x2d, on, outs, lses, g_pre, w_gates, w_br, w_ba, w_out, g_post, g_ffn, tm, permute, side=None):
    m = x2d.shape[0]
    n_split = 2 if tm >= 2 * MXU_N else 1
    scratch = [pltpu.VMEM((D_MODEL // LANES, tm, LANES), F32)] if permute else []
    row = lambda i: (i, 0)
    vec = _resident((1, D_MODEL))
    att_spec = pl.BlockSpec((ATT_OUT // LANES, tm, LANES), lambda i: (0, i, 0))
    in_specs = [pl.BlockSpec((tm, D_MODEL), row), pl.BlockSpec((tm, RET_V), row)] + [att_spec] * 6 + [
        vec, _resident(w_gates.shape), _resident(w_br.shape), _resident(w_ba.shape), _resident(w_out.shape),
        vec, vec]
    out_specs = [pl.BlockSpec((tm, D_MODEL), row), pl.BlockSpec((tm, D_MODEL), row)]
    out_shape = [jax.ShapeDtypeStruct((m, D_MODEL), F32), jax.ShapeDtypeStruct((m, D_MODEL), BF16)]
    operands = (x2d, on, *outs, *lses, g_pre, w_gates, w_br, w_ba, w_out, g_post, g_ffn)
    side_static = None
    if side is not None:
        s_in, s_ops, s_out, s_shape, side_static = _side_job(side, m // tm)
        in_specs, operands = s_in + in_specs, s_ops + operands
        out_specs, out_shape = out_specs + s_out, out_shape + s_shape
    return pl.pallas_call(
        functools.partial(_merge_kernel, tm=tm, n_split=n_split, permute=permute, side=side_static),
        grid=(m // tm,),
        in_specs=in_specs,
        out_specs=out_specs,
        out_shape=out_shape,
        scratch_shapes=scratch,
        compiler_params=_params(),
        name="merge",
    )(*operands)


def _ffn_kernel(*refs, tm, tiles_per_seq, seq_tok, side=None):
    if side is not None:
        (sq_ref, sc_ref, sn_ref), refs = refs[:3], refs[3:]
        so_ref, sl_ref, snc_ref = refs[9:12]
        refs = refs[:9] + refs[12:]
        side_refs = (sq_ref, sc_ref, sn_ref, so_ref, sl_ref, snc_ref)
    if seq_tok is None:
        h2_ref, x1_ref, wup_ref, wdn_ref, cw_ref, cb_ref, g_ref, y_ref, cs_ref, carry_ref, f_ref, perm_ref = refs
    else:
        h2_ref, x1_ref, p1_ref, p2_ref, wup_ref, wdn_ref, cw_ref, cb_ref, g_ref, y_ref, u_ref, f_ref = refs
    h2 = h2_ref[...]
    if seq_tok is None:
        sub = lax.broadcasted_iota(jnp.int32, (SUBLANES, FF_CHUNK), 0)

        @pl.when(pl.program_id(0) % tiles_per_seq == 0)
        def _():
            carry_ref[...] = jnp.zeros_like(carry_ref)
    else:
        t = lax.broadcasted_iota(jnp.int32, (tm, FF_CHUNK), 0) & (seq_tok - 1)
        _log2(seq_tok)

    def up(j):
        return [_dot(h2, wup_ref[:, c:c + FF_CHUNK]) for c in (j * FF_CHUNK, D_FF + j * FF_CHUNK)]

    def conv(u, col):
        cols = slice(col, col + FF_CHUNK)
        if seq_tok is None:
            prev = carry_ref[:, cols]
            last2 = jnp.where(sub == 0, prev[SUBLANES - 1:SUBLANES], pltpu.roll(u[tm - 2 * SUBLANES:tm - SUBLANES], 1, 0))
            last1 = jnp.where(sub == 0, prev[2 * SUBLANES - 1:2 * SUBLANES], pltpu.roll(u[tm - SUBLANES:tm], 1, 0))
            u1 = jnp.concatenate([last1, u[0:tm - SUBLANES]], axis=0)
            u2 = jnp.concatenate([last2, last1, u[0:tm - 2 * SUBLANES]], axis=0)
            carry_ref[:, cols] = u[tm - 2 * SUBLANES:tm]
        else:
            u1 = jnp.where(t == 0, p1_ref[:, cols], pltpu.roll(u, 1, 0))
            u2 = jnp.where(t <= 1, p2_ref[:, cols], pltpu.roll(u, 2, 0))
            u_ref[:, cols] = u
        cw = cw_ref[:, cols]
        return cb_ref[:, cols] + cw[0:1] * u2 + cw[1:2] * u1 + cw[2:3] * u

    def gate(j, ug, uv):
        cg = conv(ug, j * FF_CHUNK)
        cv = conv(uv, D_FF + j * FF_CHUNK)
        f_ref[:, j * FF_CHUNK:(j + 1) * FF_CHUNK] = (jax.nn.gelu(cg, approximate=True) * cv).astype(BF16)

    n_chunks = D_FF // FF_CHUNK
    u_next = up(0)
    for j in range(n_chunks):
        u_cur = u_next
        if j + 1 < n_chunks:
            u_next = up(j + 1)
        if j == 0 and side is not None:
            side_weights, side_finish = _attn_sample_phases(*side_refs, 0, g=side[0], n_tok=side[1], wb=side[2],
                                                            n_b=1)
        gate(j, *u_cur)
    if side is not None:
        side_state = side_weights()
    down = _dot(f_ref[...], wdn_ref[...])
    if side is not None:
        side_finish(side_state)
    out = _rms(down, g_ref[...])
    if seq_tok is None:
        nv = tm // SUBLANES
        for lc in range(D_MODEL // LANES):
            lanes = slice(lc * LANES, (lc + 1) * LANES)
            perm_ref[lc] = out[:, lanes]
            for s in range(SUBLANES):
                rows = slice(s * nv, (s + 1) * nv)
                y_ref[rows, lanes] = x1_ref[rows, lanes] + perm_ref[lc, pl.ds(s, nv, stride=SUBLANES), :]

        @pl.when(pl.program_id(0) % tiles_per_seq == tiles_per_seq - 1)
        def _():
            cs_ref[0, 0:1, :] = carry_ref[SUBLANES - 1:SUBLANES, :]
            cs_ref[0, 1:2, :] = carry_ref[2 * SUBLANES - 1:2 * SUBLANES, :]
    else:
        y_ref[...] = x1_ref[...] + out


def _ffn_prompt(h2, x1, w_up, w_dn, conv_w, conv_b, g_post, batch, seq, tm, side=None):
    m = h2.shape[0]
    tps = seq // tm
    n_steps = m // tm
    row = lambda i: (i, 0)
    in_specs = [pl.BlockSpec((tm, D_MODEL), row), pl.BlockSpec((tm, D_MODEL), row),
                _resident(w_up.shape), _resident(w_dn.shape), _resident(conv_w.shape), _resident(conv_b.shape),
                _resident((1, D_MODEL))]
    out_specs = [pl.BlockSpec((tm, D_MODEL), row),
                 pl.BlockSpec((1, CONV_W - 1, 2 * D_FF), lambda i: (i // tps, 0, 0))]
    out_shape = [jax.ShapeDtypeStruct((m, D_MODEL), F32),
                 jax.ShapeDtypeStruct((batch, CONV_W - 1, 2 * D_FF), F32)]
    operands = (h2, x1, w_up, w_dn, conv_w, conv_b, g_post)
    side_static = None
    if side is not None:
        s_in, s_ops, s_out, s_shape, side_static = _side_job(side, n_steps)
        in_specs, operands = s_in + in_specs, s_ops + operands
        out_specs, out_shape = out_specs + s_out, out_shape + s_shape
    return pl.pallas_call(
        functools.partial(_ffn_kernel, tm=tm, tiles_per_seq=tps, seq_tok=None, side=side_static),
        grid=(n_steps,),
        in_specs=in_specs,
        out_specs=out_specs,
        out_shape=out_shape,
        scratch_shapes=[pltpu.VMEM((2 * SUBLANES, 2 * D_FF), F32), pltpu.VMEM((tm, D_FF), BF16),
                        pltpu.VMEM((D_MODEL // LANES, tm, LANES), F32)],
        compiler_params=_params(),
        name="ffn_prompt",
    )(*operands)


def _ffn_sample(h2, x1, p1, p2, w_up, w_dn, conv_w, conv_b, g_post, n_tok):
    m = h2.shape[0]
    const = lambda i: (0, 0)
    full = lambda a: _resident(a.shape)
    return pl.pallas_call(
        functools.partial(_ffn_kernel, tm=m, tiles_per_seq=1, seq_tok=n_tok),
        grid=(1,),
        in_specs=[full(h2), full(x1), full(p1), full(p2), full(w_up), full(w_dn), full(conv_w), full(conv_b),
                  _resident((1, D_MODEL))],
        out_specs=[pl.BlockSpec((m, D_MODEL), const), pl.BlockSpec((m, 2 * D_FF), const)],
        out_shape=[jax.ShapeDtypeStruct((m, D_MODEL), F32), jax.ShapeDtypeStruct((m, 2 * D_FF), F32)],
        scratch_shapes=[pltpu.VMEM((m, D_FF), BF16)],
        compiler_params=_params(),
        name="ffn_sample",
    )(h2, x1, p1, p2, w_up, w_dn, conv_w, conv_b, g_post)


def _rot_tables(pos):
    pos = np.asarray(pos, np.float64)
    rf = RET_THETA ** (-np.linspace(0.0, 1.0, RET_QK_DIM // 2))
    ang = pos[:, None] * rf[None, :]
    cos, sin = np.cos(ang), np.sin(ang)
    cr = np.concatenate([cos, cos], axis=-1)
    sr = np.concatenate([-sin, sin], axis=-1)
    k_scale = RET_QK_DIM ** -0.5
    half = ROPE_DIM // 2
    af = ROPE_THETA ** (-np.arange(half) / half)
    ang = pos[:, None] * af[None, :]
    cos, sin = np.cos(ang), np.sin(ang)
    n = pos.shape[0]
    rest = ATT_HEAD_DIM - ROPE_DIM
    zh = np.zeros((n, half))
    ca = np.concatenate([cos, cos, np.ones((n, rest))], axis=-1)
    sa1 = np.concatenate([zh, sin, np.zeros((n, rest))], axis=-1)
    sa2 = np.concatenate([-sin, zh, np.zeros((n, rest))], axis=-1)
    rep = MXU_N // ATT_HEAD_DIM
    ca, sa1, sa2 = (np.tile(t, (1, rep)) for t in (ca, sa1, sa2))
    return [jnp.asarray(t, F32) for t in (cr, sr, cr * k_scale, sr * k_scale, ca, sa1, sa2)]


def kernel(x_prompt, x_sample, cache_kv_g0, cache_kv_g1, cache_kv_g2, state_ret, state_conv, norm_mix_pre, w_in,
           w_branch_ret, w_branch_attn, w_out, norm_mix_post, norm_ffn_pre, w_ffn_up, conv_w, conv_b, w_ffn_down,
           norm_ffn_post):
    batch, seq, _ = x_prompt.shape
    db, n_tok, _ = x_sample.shape
    depth = w_in.shape[0]
    assert depth == 1 and seq % (ATT_GROUPS[-1][1] * BAND_BLOCK) == 0 and n_tok <= SUBLANES
    caches = (cache_kv_g0, cache_kv_g1, cache_kv_g2)

    wi = w_in[0]
    o_rg = 2 * RET_QK + RET_V
    o_aq = o_rg + RET_V
    o_gr = o_aq + 3 * ATT_W
    w_qkv = jnp.concatenate([wi[:, :o_rg], wi[:, o_aq:o_aq + ATT_W] * ATT_HEAD_DIM ** -0.5,
                             wi[:, o_aq + ATT_W:o_gr]], axis=1).astype(BF16)
    w_gates = jnp.concatenate([wi[:, o_rg:o_aq], wi[:, o_gr:]], axis=1).astype(BF16)
    later_weights = (w_branch_ret[0], w_branch_attn[0], w_out[0], w_ffn_up[0], w_ffn_down[0])
    cb = conv_b[0][None, :]
    cw = conv_w[0]
    g_pre, g_post, g_ffn, g_post2 = (t[0][None, :] for t in (norm_mix_pre, norm_mix_post, norm_ffn_pre, norm_ffn_post))

    tm = 512
    xp = x_prompt.reshape(batch * seq, D_MODEL)
    proj_out = _proj(xp, g_pre, w_qkv, _rot_tables(np.arange(seq)), tm, seq=seq, casts=later_weights)
    rq, rk, rv = proj_out[:3]
    qkv_res = proj_out[3:3 + 3 * N_GROUPS]
    kv_last = proj_out[3 + 3 * N_GROUPS:3 + 4 * N_GROUPS]
    w_br, w_ba, w_o, w_up, w_dn = proj_out[3 + 4 * N_GROUPS:]
    on, p_ret = _ret_prompt(rq, rk, rv, batch, seq)
    outs, lses = [], []
    for g in range(N_GROUPS):
        o_g, l_g = _attn_prompt(*qkv_res[3 * g:3 * g + 3], g, batch, seq)
        outs.append(o_g)
        lses.append(l_g)
    p_kv = [t.reshape(batch, 2, ATT_HEADS, ATT_HEAD_DIM, t.shape[3]).transpose(0, 4, 1, 2, 3)[None] for t in kv_last]

    ms = db * n_tok
    xs = x_sample.reshape(ms, D_MODEL)
    pos_s = np.tile(PAST_LEN + np.arange(n_tok), db)
    rq, rk, rv, aq, kv0, kv1, kv2 = _proj(xs, g_pre, w_qkv, _rot_tables(pos_s), ms)
    kvs = (kv0, kv1, kv2)
    on_s, s_ret = _ret_sample(rq, rk, rv, state_ret[0], n_tok)
    q_tiled = jnp.tile(aq.reshape(db, n_tok, ATT_W), (1, ATT_HEADS, 1))
    sides = []
    for g in range(N_GROUPS):
        cache = caches[g][0]
        wb = cache.shape[1]
        cache_t = cache.transpose(0, 2, 3, 4, 1).reshape(db, 2, ATT_OUT, wb)
        kv_new = kvs[g].reshape(ms, 2, ATT_OUT).transpose(1, 2, 0)
        sides.append((g, n_tok, q_tiled, cache_t, kv_new))

    riders = (N_GROUPS - 2, N_GROUPS - 1) if batch * seq // tm == db else (None, None)
    sample_attn = {g: _attn_sample(*sides[g][2:], g, n_tok) for g in range(N_GROUPS) if g not in riders}
    x1, h2, *rest = _merge(xp, on, outs, lses, g_pre, w_gates, w_br, w_ba, w_o, g_post, g_ffn, tm, permute=True,
                           side=None if riders[0] is None else sides[riders[0]])
    if rest:
        sample_attn[riders[0]] = rest
    y_p, p_conv, *rest = _ffn_prompt(h2, x1, w_up, w_dn, cw, cb, g_post2, batch, seq, tm,
                                     side=None if riders[1] is None else sides[riders[1]])
    if rest:
        sample_attn[riders[1]] = rest

    outs, lses, s_kv = [], [], []
    for g in range(N_GROUPS):
        o_g, l_g, nc = sample_attn[g]
        outs.append(o_g.reshape(ms, ATT_OUT // LANES, LANES).transpose(1, 0, 2))
        lses.append(l_g.reshape(ms, ATT_OUT // LANES, LANES).transpose(1, 0, 2))
        s_kv.append(nc.reshape(db, 2, ATT_HEADS, ATT_HEAD_DIM, nc.shape[3]).transpose(0, 4, 1, 2, 3)[None])
    x1, h2 = _merge(xs, on_s, outs, lses, g_pre, w_gates, w_br, w_ba, w_o, g_post, g_ffn, ms, permute=False)
    st = state_conv[0]
    zeros = jnp.zeros((db, n_tok - 1, 2 * D_FF), F32)
    p1 = jnp.concatenate([st[:, 1:2], zeros], axis=1).reshape(ms, 2 * D_FF)
    p2 = jnp.concatenate([st, zeros[:, :n_tok - 2]], axis=1).reshape(ms, 2 * D_FF)
    y_s, u_s = _ffn_sample(h2, x1, p1, p2, w_up, w_dn, cw, cb, g_post2, n_tok)
    s_conv = u_s.reshape(db, n_tok, 2 * D_FF)[:, n_tok - (CONV_W - 1):][None]

    return (y_p.reshape(x_prompt.shape), y_s.reshape(x_sample.shape), p_kv[0], p_kv[1], p_kv[2], p_ret[None],
            p_conv[None], s_kv[0], s_kv[1], s_kv[2], s_ret[None], s_conv)
```

```python
import functools

import numpy as np
import jax
import jax.numpy as jnp
from jax import lax
from jax.experimental import pallas as pl
from jax.experimental.pallas import tpu as pltpu

F32 = jnp.float32
BF16 = jnp.bfloat16

D_MODEL = 1024
PAST_LEN = 16384
RET_HEADS = 4
RET_QK_DIM = 128
RET_V_DIM = 256
RET_CHUNK = 128
RET_THETA = 10000.0
RET_QK = RET_HEADS * RET_QK_DIM
RET_V = RET_HEADS * RET_V_DIM
ATT_GROUPS = ((128, 1), (512, 4), (2048, 16))
N_GROUPS = 3
ATT_HEADS = 4
ATT_HEAD_DIM = 64
ROPE_DIM = ATT_HEAD_DIM // 4
ROPE_THETA = 500000.0
BAND_BLOCK = 128
ATT_OUT = ATT_HEADS * ATT_HEAD_DIM
ATT_W = N_GROUPS * ATT_OUT
D_FF = 2816
CONV_W = 3
EPS = 1e-6

LANES = 128
SUBLANES = 8
MXU_N = 256
VMEM_LIMIT = 56 * 1024 * 1024
NEG = -1e30
FF_CHUNK = MXU_N
ATTN_GROUP = 2
RET_PAD = 2 * SUBLANES
QKV_COLS = 2 * RET_QK + RET_V + 3 * ATT_W


def _rms(x, g):
    return x * lax.rsqrt(jnp.mean(x * x, axis=-1, keepdims=True) + EPS) * g


def _dot(a, b):
    return jnp.dot(a, b, preferred_element_type=F32)


def _dot_nt(a, b):
    return lax.dot_general(a, b, (((1,), (1,)), ((), ())), preferred_element_type=F32)


def _resident(shape):
    return pl.BlockSpec(shape, lambda *_: (0,) * len(shape), pipeline_mode=pl.Buffered(1))


def _log2(n):
    assert n > 0 and n & (n - 1) == 0, n
    return n.bit_length() - 1


def _params(n_axes=1):
    return pltpu.CompilerParams(dimension_semantics=("arbitrary",) * n_axes, vmem_limit_bytes=VMEM_LIMIT)


def _proj_kernel(x_ref, g_ref, w_ref, crq_ref, srq_ref, crk_ref, srk_ref, ca_ref, sa1_ref, sa2_ref, *rest,
                 tm, by_residue, tiles_per_seq, n_cast):
    cast_in, rest = rest[:n_cast], rest[n_cast:]
    rq_ref, rk_ref, rv_ref = rest[:3]
    if by_residue:
        res_refs = rest[3:3 + 3 * N_GROUPS]
        kv_refs = rest[3 + 3 * N_GROUPS:3 + 4 * N_GROUPS]
        cast_out = rest[3 + 4 * N_GROUPS:3 + 4 * N_GROUPS + n_cast]
        scr = rest[3 + 4 * N_GROUPS + n_cast]
    else:
        aq_ref = rest[3]
        kv_refs = rest[4:4 + N_GROUPS]
        cast_out = rest[4 + N_GROUPS:4 + N_GROUPS + n_cast]
    for src, dst in zip(cast_in, cast_out):
        dst[...] = src[...].astype(BF16)
    h = _rms(x_ref[...], g_ref[...]).astype(BF16)

    def mm(c0, width=MXU_N):
        return _dot(h, w_ref[:, c0:c0 + width])

    def rot_ret(t, c, s):
        return t * c + pltpu.roll(t, RET_QK_DIM // 2, 1) * s

    ca, sa1, sa2 = ca_ref[...], sa1_ref[...], sa2_ref[...]

    def rot_att(t):
        return t * ca + pltpu.roll(t, ROPE_DIM // 2, 1) * sa1 + pltpu.roll(t, MXU_N - ROPE_DIM // 2, 1) * sa2

    def put_by_residue(ref, val, dil):
        if dil == 1:
            ref[0, 0] = val.astype(BF16)
            return
        for half in range(MXU_N // LANES):
            lanes = slice(half * LANES, (half + 1) * LANES)
            scr[half] = val[:, lanes]
            for r in range(dil):
                ref[0, r, :, lanes] = scr[half, pl.ds(r, tm // dil, stride=dil), :].astype(BF16)

    def put_ret(ref, c, c_ref, s_ref):
        def put(t):
            for half in range(MXU_N // LANES):
                lo = half * LANES
                col = c * MXU_N + lo
                ref[:, col:col + LANES] = rot_ret(t[:, lo:lo + LANES], c_ref[...], s_ref[...]).astype(BF16)
        return put

    def put_rv(c):
        def put(t):
            rv_ref[:, c * MXU_N:(c + 1) * MXU_N] = t.astype(BF16)
        return put

    def put_att(g, which):
        def put(t):
            if which < 2:
                t = rot_att(t)
            if not by_residue:
                if which == 0:
                    aq_ref[:, g * ATT_OUT:(g + 1) * ATT_OUT] = t.astype(BF16)
                else:
                    kv_refs[g][:, (which - 1) * ATT_OUT:which * ATT_OUT] = t
                return
            put_by_residue(res_refs[3 * g + which], t, ATT_GROUPS[g][1])
            if which == 0:
                return
            rows = kv_refs[g].shape[3]
            if ATT_GROUPS[g][0] >= tiles_per_seq * tm:
                kv_refs[g][0, which - 1] = t[tm - rows:tm].T
            else:
                late_windows.append((g, which, t[tm - rows:tm]))
        return put

    late_windows = []

    base = 2 * RET_QK + RET_V
    groups = []
    for c in range(RET_QK // MXU_N):
        groups.append([(c * MXU_N, put_ret(rq_ref, c, crq_ref, srq_ref)),
                       (RET_QK + c * MXU_N, put_ret(rk_ref, c, crk_ref, srk_ref))])
    for c in range(RET_V // MXU_N):
        groups.append([(2 * RET_QK + c * MXU_N, put_rv(c))])
    for g in range(N_GROUPS):
        groups.append([(base + which * ATT_W + g * ATT_OUT, put_att(g, which)) for which in range(3)])
    for group in groups:
        results = [mm(col) for col, _ in group]
        for (_, epilogue), t in zip(group, results):
            epilogue(t)
    if late_windows:
        @pl.when(pl.program_id(0) % tiles_per_seq == tiles_per_seq - 1)
        def _():
            for g, which, t in late_windows:
                kv_refs[g][0, which - 1] = t.T


def _slab_rows(rows, n_steps):
    tile = 2 * SUBLANES
    for k in range(-(-rows // (n_steps * tile)), rows // tile + 1):
        if rows % (k * tile) == 0:
            return k * tile
    raise ValueError((rows, n_steps))


def _proj(x2d, g_pre, w_qkv, tabs, tm, seq=None, casts=()):
    m = x2d.shape[0]
    n_steps = m // tm
    n_tab = tabs[0].shape[0] // tm
    row = lambda i: (i, 0)
    tab = lambda i: (i % n_tab, 0)
    in_specs = [pl.BlockSpec((tm, D_MODEL), row), _resident((1, D_MODEL)), _resident((D_MODEL, QKV_COLS))]
    in_specs += [pl.BlockSpec((tm, LANES), tab)] * 4 + [pl.BlockSpec((tm, MXU_N), tab)] * 3
    out_specs = [pl.BlockSpec((tm, w), row) for w in (RET_QK, RET_QK, RET_V)]
    out_shape = [jax.ShapeDtypeStruct((m, w), BF16) for w in (RET_QK, RET_QK, RET_V)]
    scratch = []
    if seq is None:
        widths = (ATT_W,) + (2 * ATT_OUT,) * N_GROUPS
        out_specs += [pl.BlockSpec((tm, w), row) for w in widths]
        out_shape += [jax.ShapeDtypeStruct((m, w), d) for w, d in zip(widths, (BF16,) + (F32,) * N_GROUPS)]
    else:
        batch, tps = m // seq, seq // tm
        for _, dil in ATT_GROUPS:
            assert tm % (dil * 2 * SUBLANES) == 0
            out_specs += [pl.BlockSpec((1, dil, tm // dil, ATT_OUT), lambda i: (i // tps, 0, i % tps, 0))] * 3
            out_shape += [jax.ShapeDtypeStruct((batch, dil, seq // dil, ATT_OUT), BF16)] * 3
        for win, _ in ATT_GROUPS:
            win = min(win, seq)
            if win == seq:
                out_specs.append(pl.BlockSpec((1, 2, ATT_OUT, tm), lambda i: (i // tps, 0, 0, i % tps)))
            else:
                assert win <= tm and win % LANES == 0
                out_specs.append(pl.BlockSpec((1, 2, ATT_OUT, win), lambda i: (i // tps, 0, 0, 0)))
            out_shape.append(jax.ShapeDtypeStruct((batch, 2, ATT_OUT, win), F32))
        scratch = [pltpu.VMEM((MXU_N // LANES, tm, LANES), F32)]
    for w in casts:
        rows, cols = w.shape
        slab = _slab_rows(rows, n_steps)
        spec = pl.BlockSpec((slab, cols), lambda i, n=rows // slab: (i * n // n_steps, 0))
        in_specs.append(spec)
        out_specs.append(spec)
        out_shape.append(jax.ShapeDtypeStruct(w.shape, BF16))
    return pl.pallas_call(
        functools.partial(_proj_kernel, tm=tm, by_residue=seq is not None,
                          tiles_per_seq=None if seq is None else seq // tm, n_cast=len(casts)),
        grid=(n_steps,),
        in_specs=in_specs,
        out_specs=out_specs,
        out_shape=out_shape,
        scratch_shapes=scratch,
        compiler_params=_params(),
        name="proj",
    )(x2d, g_pre, w_qkv, *tabs, *casts)


def _ret_kernel(q_ref, k_ref, v_ref, dec_ref, qd_ref, kd_ref, gc_ref, o_ref, st_ref, s_scr, *, n_chunks):
    s_scr[...] = jnp.zeros_like(s_scr)

    def body(c, carry):
        r0 = pl.multiple_of(c * RET_CHUNK, RET_CHUNK)
        rows = pl.ds(r0, RET_CHUNK)
        first = []
        for h in range(RET_HEADS):
            qk = slice(h * RET_QK_DIM, (h + 1) * RET_QK_DIM)
            vv = slice(h * RET_V_DIM, (h + 1) * RET_V_DIM)
            q = q_ref[rows, qk]
            k = k_ref[rows, qk]
            v = v_ref[rows, vv]
            s0 = s_scr[h]
            sc = _dot_nt(q, k)
            from_state = _dot(q, s0.astype(BF16))
            kd_t = (k.astype(F32) * kd_ref[h]).T.astype(BF16)
            first.append((sc, from_state, _dot(kd_t, v), s0, v))
        for h, (sc, from_state, inc, s0, v) in enumerate(first):
            vv = slice(h * RET_V_DIM, (h + 1) * RET_V_DIM)
            o = _dot((sc * dec_ref[h]).astype(BF16), v) + from_state * qd_ref[h]
            s_scr[h] = s0 * gc_ref[h] + inc
            o_ref[rows, vv] = o * lax.rsqrt(jnp.mean(o * o, axis=-1, keepdims=True) + EPS)
        return carry

    lax.fori_loop(0, n_chunks, body, 0, unroll=2)
    st_ref[0] = s_scr[...]


def _log_gamma():
    return np.log1p(-np.exp2(-5.0 - np.arange(RET_HEADS, dtype=np.float64)))


def _ret_tables(chunk):
    lg = _log_gamma()
    idx = np.arange(RET_CHUNK, dtype=np.float64)
    diff = idx[:, None] - idx[None, :]
    dec = np.where(diff[None] >= 0, np.exp(lg[:, None, None] * np.maximum(diff, 0.0)[None]), 0.0)
    qd = np.exp(lg[:, None] * (idx[None, :] + 1.0))
    kd = np.exp(lg[:, None] * (chunk - 1.0 - idx)[None, :])
    gc = np.exp(lg * chunk)
    qd = np.broadcast_to(qd[:, :, None], (RET_HEADS, RET_CHUNK, RET_V_DIM))
    kd = np.broadcast_to(kd[:, :, None], (RET_HEADS, RET_CHUNK, RET_QK_DIM))
    gc = np.broadcast_to(gc[:, None, None], (RET_HEADS, RET_QK_DIM, RET_V_DIM))
    return [jnp.asarray(t, F32) for t in (dec, qd, kd, gc)]


def _ret_prompt(rq, rk, rv, batch, seq):
    dec, qd, kd, gc = _ret_tables(RET_CHUNK)
    row = lambda b: (b, 0)
    return pl.pallas_call(
        functools.partial(_ret_kernel, n_chunks=seq // RET_CHUNK),
        grid=(batch,),
        in_specs=[pl.BlockSpec((seq, RET_QK), row), pl.BlockSpec((seq, RET_QK), row), pl.BlockSpec((seq, RET_V), row),
                  _resident(dec.shape), _resident(qd.shape), _resident(kd.shape), _resident(gc.shape)],
        out_specs=[pl.BlockSpec((seq, RET_V), row),
                   pl.BlockSpec((1, RET_HEADS, RET_QK_DIM, RET_V_DIM), lambda b: (b, 0, 0, 0))],
        out_shape=[jax.ShapeDtypeStruct((batch * seq, RET_V), F32),
                   jax.ShapeDtypeStruct((batch, RET_HEADS, RET_QK_DIM, RET_V_DIM), F32)],
        scratch_shapes=[pltpu.VMEM((RET_HEADS, RET_QK_DIM, RET_V_DIM), F32)],
        compiler_params=_params(),
        name="ret_prompt",
    )(rq, rk, rv, dec, qd, kd, gc)


def _ret_sample_kernel(q_ref, k_ref, kt_ref, v_ref, s_ref, dec_ref, qd_ref, kd_ref, gc_ref, o_ref, st_ref,
                       *, n_tok, n_b):
    pairs = [(b, h) for b in range(n_b) for h in range(RET_HEADS)]
    qk_of = lambda h: slice(h * RET_QK_DIM, (h + 1) * RET_QK_DIM)
    vv_of = lambda h: slice(h * RET_V_DIM, (h + 1) * RET_V_DIM)
    zeros = jnp.zeros((LANES - RET_PAD, RET_V_DIM), BF16)
    first = []
    for b, h in pairs:
        q = q_ref[b, :, qk_of(h)]
        k = k_ref[b, :, qk_of(h)]
        v = v_ref[b, :, vv_of(h)]
        s0 = s_ref[b, h]
        from_state = _dot(q.astype(BF16), s0.astype(BF16))
        k_dec_t = (kt_ref[b, h] * kd_ref[h]).astype(BF16)
        v_rows = jnp.concatenate([v.astype(BF16), zeros], axis=0)
        st_ref[b, h] = s0 * gc_ref[h] + _dot(k_dec_t, v_rows)
        scores = [jnp.sum(q * k[i:i + 1, :], axis=-1, keepdims=True) for i in range(n_tok)]
        first.append((from_state, scores))
    second = []
    for (b, h), (from_state, scores) in zip(pairs, first):
        v = v_ref[b, :, vv_of(h)]
        o = from_state * qd_ref[h]
        for i, sc_i in enumerate(scores):
            o = o + (sc_i * dec_ref[h][:, i:i + 1]) * v[i:i + 1, :]
        second.append((o, jnp.mean(o * o, axis=-1, keepdims=True)))
    for (b, h), (o, ms) in zip(pairs, second):
        o_ref[b, :, vv_of(h)] = o * lax.rsqrt(ms + EPS)


def _ret_sample(rq, rk, rv, state, n_tok):
    db = state.shape[0]
    pad = RET_PAD - n_tok
    n_b = 8

    def pad_rows(t):
        t = t.astype(F32).reshape(db, n_tok, t.shape[-1])
        return jnp.pad(t, ((0, 0), (0, pad), (0, 0)))

    q, k, v = pad_rows(rq), pad_rows(rk), pad_rows(rv)
    kt = jnp.pad(k, ((0, 0), (0, LANES - RET_PAD), (0, 0)))
    kt = kt.reshape(db, LANES, RET_HEADS, RET_QK_DIM).transpose(0, 2, 3, 1)
    lg = _log_gamma()
    idx = np.arange(RET_PAD, dtype=np.float64)
    diff = idx[:, None] - idx[None, :]
    dec = np.where(diff[None] >= 0, np.exp(lg[:, None, None] * np.maximum(diff, 0.0)[None]), 0.0)
    qd = np.broadcast_to(np.exp(lg[:, None] * (idx[None, :] + 1.0))[:, :, None], (RET_HEADS, RET_PAD, RET_V_DIM))
    kd = np.exp(lg[:, None] * (n_tok - 1.0 - np.arange(LANES, dtype=np.float64))[None, :])
    kd = np.broadcast_to(kd[:, None, :], (RET_HEADS, RET_QK_DIM, LANES))
    gc = np.broadcast_to(np.exp(lg * n_tok)[:, None, None], (RET_HEADS, RET_QK_DIM, RET_V_DIM))
    dec, qd, kd, gc = (jnp.asarray(t, F32) for t in (dec, qd, kd, gc))
    b3 = lambda i: (i, 0, 0)
    b4 = lambda i: (i, 0, 0, 0)
    o, st = pl.pallas_call(
        functools.partial(_ret_sample_kernel, n_tok=n_tok, n_b=n_b),
        grid=(db // n_b,),
        in_specs=[pl.BlockSpec((n_b, RET_PAD, RET_QK), b3), pl.BlockSpec((n_b, RET_PAD, RET_QK), b3),
                  pl.BlockSpec((n_b, RET_HEADS, RET_QK_DIM, LANES), b4),
                  pl.BlockSpec((n_b, RET_PAD, RET_V), b3),
                  pl.BlockSpec((n_b, RET_HEADS, RET_QK_DIM, RET_V_DIM), b4),
                  _resident(dec.shape), _resident(qd.shape), _resident(kd.shape), _resident(gc.shape)],
        out_specs=[pl.BlockSpec((n_b, RET_PAD, RET_V), b3),
                   pl.BlockSpec((n_b, RET_HEADS, RET_QK_DIM, RET_V_DIM), b4)],
        out_shape=[jax.ShapeDtypeStruct((db, RET_PAD, RET_V), F32),
                   jax.ShapeDtypeStruct(state.shape, F32)],
        compiler_params=_params(),
        name="ret_sample",
    )(q, k, kt, v, state, dec, qd, kd, gc)
    return o[:, :n_tok].reshape(db * n_tok, RET_V), st


def _attn_kernel(q_ref, k_ref, v_ref, o_ref, l_ref, vt_scr, bias_scr, *staging, dil, nb):
    n_blk = dil * nb
    length = nb * BAND_BLOCK
    blk = BAND_BLOCK
    o_dst, l_dst = staging if dil > 1 else (o_ref, l_ref)

    kk = lax.broadcasted_iota(jnp.int32, (2 * blk, 2 * LANES), 0)
    ql = lax.broadcasted_iota(jnp.int32, (2 * blk, 2 * LANES), 1) & (blk - 1)
    in_span = kk <= ql + blk
    bias_scr[...] = jnp.where(in_span & (kk >= ql), 0.0, NEG)

    def transpose_values(c):
        vt_scr[c] = v_ref[0, c * blk:(c + 1) * blk, :].astype(F32).T.astype(BF16)

    lane = lax.broadcasted_iota(jnp.int32, (blk, LANES), 1)

    def scores(c, p):
        lanes = slice(p * LANES, (p + 1) * LANES)
        k0 = 0 if c % nb == 0 else blk
        q128 = q_ref[0, c * blk:(c + 1) * blk, lanes]
        zero = jnp.zeros_like(q128)
        q_pair = jnp.concatenate([jnp.where(lane < ATT_HEAD_DIM, q128, zero),
                                  jnp.where(lane >= ATT_HEAD_DIM, q128, zero)], axis=0)
        keys = k_ref[0, c * blk - k0:(c + 1) * blk, lanes]
        return _dot_nt(keys, q_pair) + bias_scr[blk - k0:2 * blk, :]

    def weights_values(c, p, s_t):
        lanes = slice(p * LANES, (p + 1) * LANES)
        m = jnp.max(s_t, axis=0, keepdims=True)
        p_t = jnp.exp(s_t - m).astype(BF16)
        v_t = vt_scr[c, lanes, :]
        if c % nb:
            v_t = jnp.concatenate([vt_scr[c - 1, lanes, :], v_t], axis=1)
        v_t = jnp.concatenate([v_t, jnp.ones((2 * SUBLANES, v_t.shape[1]), BF16)], axis=0)
        return _dot(v_t, p_t), m

    def finish(c, p, o_t, m):
        den = o_t[LANES:LANES + 1, :]
        lse = m + jnp.log(den)
        o_pair = jnp.concatenate([o_t[0:ATT_HEAD_DIM, 0:blk] / den[:, 0:blk],
                                  o_t[ATT_HEAD_DIM:LANES, blk:2 * blk] / den[:, blk:2 * blk]], axis=0)
        l_pair = jnp.concatenate([jnp.broadcast_to(lse[:, 0:blk], (ATT_HEAD_DIM, blk)),
                                  jnp.broadcast_to(lse[:, blk:2 * blk], (ATT_HEAD_DIM, blk))], axis=0)
        o_dst[p, c * blk:(c + 1) * blk, :] = o_pair.T
        l_dst[p, c * blk:(c + 1) * blk, :] = l_pair.T

    n_pair = ATT_OUT // LANES
    groups = [[(c, p) for c in range(c0, min(c0 + ATTN_GROUP, n_blk)) for p in range(n_pair)]
              for c0 in range(0, n_blk, ATTN_GROUP)]

    def stage_scores(group):
        for c in sorted({c for c, _ in group}):
            transpose_values(c)
        return [scores(*chain) for chain in group]

    s_next = stage_scores(groups[0])
    pending = []
    for i, group in enumerate(groups):
        s_cur = s_next
        if i + 1 < len(groups):
            s_next = stage_scores(groups[i + 1])
        o_m = [weights_values(*chain, s) for chain, s in zip(group, s_cur)]
        for args in pending:
            finish(*args)
        pending = [chain + om for chain, om in zip(group, o_m)]
    for args in pending:
        finish(*args)

    if dil > 1:
        for p in range(ATT_OUT // LANES):
            for r in range(dil):
                o_ref[p, pl.ds(r, length, stride=dil), :] = o_dst[p, r * length:(r + 1) * length, :]
                l_ref[p, pl.ds(r, length, stride=dil), :] = l_dst[p, r * length:(r + 1) * length, :]


def _attn_prompt(q, k, v, g, batch, seq):
    _, dil = ATT_GROUPS[g]
    nb = seq // dil // BAND_BLOCK
    _log2(nb)
    q, k, v = (t.reshape(batch, seq, ATT_OUT) for t in (q, k, v))
    in_spec = pl.BlockSpec((1, seq, ATT_OUT), lambda b: (b, 0, 0))
    n_pair = ATT_OUT // LANES
    out_spec = pl.BlockSpec((n_pair, seq, LANES), lambda b: (0, b, 0))
    scratch = [pltpu.VMEM((dil * nb, ATT_OUT, BAND_BLOCK), BF16), pltpu.VMEM((2 * BAND_BLOCK, 2 * LANES), F32)]
    if dil > 1:
        scratch += [pltpu.VMEM((n_pair, seq, LANES), F32)] * 2
    return pl.pallas_call(
        functools.partial(_attn_kernel, dil=dil, nb=nb),
        grid=(batch,),
        in_specs=[in_spec] * 3,
        out_specs=[out_spec] * 2,
        out_shape=[jax.ShapeDtypeStruct((n_pair, batch * seq, LANES), F32)] * 2,
        scratch_shapes=scratch,
        compiler_params=_params(),
        name=f"attn_prompt_g{g}",
    )(q, k, v)


def _attn_sample_kernel(q_ref, c_ref, n_ref, o_ref, l_ref, nc_ref, *, g, n_tok, wb, n_b):
    for bb in range(n_b):
        _attn_sample_one(q_ref, c_ref, n_ref, o_ref, l_ref, nc_ref, bb, g=g, n_tok=n_tok, wb=wb, n_b=n_b)


def _attn_sample_one(q_ref, c_ref, n_ref, o_ref, l_ref, nc_ref, bb, **static):
    weights, finish = _attn_sample_phases(q_ref, c_ref, n_ref, o_ref, l_ref, nc_ref, bb, **static)
    finish(weights())


def _attn_sample_phases(q_ref, c_ref, n_ref, o_ref, l_ref, nc_ref, bb, *, g, n_tok, wb, n_b):
    win, dil = ATT_GROUPS[g]
    nk = win // dil
    _log2(dil)
    n_col = ATT_HEADS * n_tok
    first_new = LANES - n_tok
    seq_lane = ((pl.program_id(0) * n_b + bb) * n_tok) % LANES
    new = [pltpu.roll(n_ref[kv], (first_new + LANES - seq_lane) % LANES, 1) for kv in range(2)]
    lane_t = lax.broadcasted_iota(jnp.int32, (ATT_OUT, LANES), 1)
    for kv in range(2):
        shifted = pltpu.roll(c_ref[bb, kv], wb - n_tok, 1)
        if wb > LANES:
            nc_ref[bb, kv, :, 0:wb - LANES] = shifted[:, 0:wb - LANES]
        nc_ref[bb, kv, :, wb - LANES:wb] = jnp.where(lane_t < first_new, shifted[:, wb - LANES:wb], new[kv])

    q = q_ref[bb][:, g * ATT_OUT:(g + 1) * ATT_OUT]
    row = lax.broadcasted_iota(jnp.int32, (n_col, ATT_OUT), 0)
    lane = lax.broadcasted_iota(jnp.int32, (n_col, ATT_OUT), 1)
    head_sel = (lane >> _log2(ATT_HEAD_DIM)) == (row >> _log2(n_tok))
    qbd = jnp.where(head_sel, q, jnp.zeros_like(q))
    k_c, v_c = c_ref[bb, 0].astype(BF16), c_ref[bb, 1].astype(BF16)
    k_n, v_n = new[0].astype(BF16), new[1].astype(BF16)

    def weights():
        s_c = _dot(qbd, k_c)
        s_n = _dot(qbd, k_n)
        t_c = lax.broadcasted_iota(jnp.int32, (n_col, wb), 0) & (n_tok - 1)
        d_c = wb + t_c - lax.broadcasted_iota(jnp.int32, (n_col, wb), 1)
        ok_c = ((d_c & (dil - 1)) == 0) & (d_c <= nk * dil)
        t_n = lax.broadcasted_iota(jnp.int32, (n_col, LANES), 0) & (n_tok - 1)
        new_idx = lax.broadcasted_iota(jnp.int32, (n_col, LANES), 1) - first_new
        d_n = t_n - new_idx
        ok_n = (new_idx >= 0) & (d_n >= 0) & ((d_n & (dil - 1)) == 0) & (d_n <= nk * dil)
        s_c = jnp.where(ok_c, s_c, NEG)
        s_n = jnp.where(ok_n, s_n, NEG)
        m = jnp.maximum(jnp.max(s_c, axis=-1, keepdims=True), jnp.max(s_n, axis=-1, keepdims=True))
        p_c = jnp.exp(s_c - m)
        p_n = jnp.exp(s_n - m)
        den = jnp.sum(p_c, axis=-1, keepdims=True) + jnp.sum(p_n, axis=-1, keepdims=True)
        return p_c.astype(BF16), p_n.astype(BF16), m, den

    def finish(state):
        p_c, p_n, m, den = state
        o = (_dot_nt(p_c, v_c) + _dot_nt(p_n, v_n)) / den
        lse = jnp.broadcast_to(m + jnp.log(den), (n_col, ATT_OUT))
        o = jnp.where(head_sel, o, 0.0)
        lse = jnp.where(head_sel, lse, 0.0)
        o_t = o[0:n_tok]
        l_t = lse[0:n_tok]
        for h in range(1, ATT_HEADS):
            o_t = o_t + o[h * n_tok:(h + 1) * n_tok]
            l_t = l_t + lse[h * n_tok:(h + 1) * n_tok]
        o_ref[bb] = o_t
        l_ref[bb] = l_t

    return weights, finish


def _attn_sample(q_tiled, cache, kv_new, g, n_tok):
    db, _, _, wb = cache.shape
    n_b = max(1, min(SUBLANES, ATT_GROUPS[-1][0] // wb))
    assert wb % LANES == 0 and db % n_b == 0 and LANES % (n_b * n_tok) == 0 and kv_new.shape[2] % LANES == 0
    b3 = lambda i: (i, 0, 0)
    b4 = lambda i: (i, 0, 0, 0)
    return pl.pallas_call(
        functools.partial(_attn_sample_kernel, g=g, n_tok=n_tok, wb=wb, n_b=n_b),
        grid=(db // n_b,),
        in_specs=[pl.BlockSpec((n_b, ATT_HEADS * n_tok, ATT_W), b3),
                  pl.BlockSpec((n_b, 2, ATT_OUT, wb), b4),
                  pl.BlockSpec((2, ATT_OUT, LANES), lambda i: (0, 0, i * n_b * n_tok // LANES))],
        out_specs=[pl.BlockSpec((n_b, n_tok, ATT_OUT), b3), pl.BlockSpec((n_b, n_tok, ATT_OUT), b3),
                   pl.BlockSpec((n_b, 2, ATT_OUT, wb), b4)],
        out_shape=[jax.ShapeDtypeStruct((db, n_tok, ATT_OUT), F32), jax.ShapeDtypeStruct((db, n_tok, ATT_OUT), F32),
                   jax.ShapeDtypeStruct(cache.shape, F32)],
        compiler_params=_params(),
        name=f"attn_sample_g{g}",
    )(q_tiled, cache, kv_new)


def _merge_kernel(*refs, tm, n_split, permute, side=None):
    if side is not None:
        side_in, refs = refs[:3], refs[3:]
    (x_ref, on_ref, o0_ref, o1_ref, o2_ref, l0_ref, l1_ref, l2_ref, gpre_ref, wg_ref, wbr_ref, wba_ref, wout_ref,
     gpost_ref, gffn_ref, x1_ref, h2_ref), perm = refs[:17], refs[17:]
    if side is not None:
        side_refs, perm = side_in + perm[:3], perm[3:]
    rows_of = [slice(i * tm // n_split, (i + 1) * tm // n_split) for i in range(n_split)]
    n_pair = ATT_OUT // LANES

    def gates(rows):
        x = x_ref[rows, :]
        h = _rms(x, gpre_ref[...]).astype(BF16)
        return [_dot(h, wg_ref[:, c:c + D_MODEL]) for c in (0, RET_V, RET_V + D_MODEL)]

    def branches(rows, rg):
        r = (on_ref[rows, :] * (rg * jax.nn.sigmoid(rg))).astype(BF16)
        atts = []
        for p in range(n_pair):
            l0, l1, l2 = l0_ref[p, rows, :], l1_ref[p, rows, :], l2_ref[p, rows, :]
            mx = jnp.maximum(jnp.maximum(l0, l1), l2)
            e0, e1, e2 = jnp.exp(l0 - mx), jnp.exp(l1 - mx), jnp.exp(l2 - mx)
            att = (e0 * o0_ref[p, rows, :] + e1 * o1_ref[p, rows, :] + e2 * o2_ref[p, rows, :]) / (e0 + e1 + e2)
            atts.append(att.astype(BF16))
        return _dot(r, wbr_ref[...]), _dot(jnp.concatenate(atts, axis=1), wba_ref[...])

    def mixed(gr, ga, br, ba):
        mix = jax.nn.sigmoid(gr) * br + jax.nn.sigmoid(ga) * ba
        return _dot(mix.astype(BF16), wout_ref[...])

    def finish(rows, mo):
        x1 = x_ref[rows, :] + _rms(mo, gpost_ref[...])
        x1_ref[rows, :] = x1
        return _rms(x1, gffn_ref[...])

    g = []
    for i, rows in enumerate(rows_of):
        g.append(gates(rows))
        if i == 0 and side is not None:
            side_weights, side_finish = _attn_sample_phases(*side_refs, 0, g=side[0], n_tok=side[1], wb=side[2],
                                                            n_b=1)
    b = [branches(rows, gi[0]) for rows, gi in zip(rows_of, g)]
    if side is not None:
        side_state = side_weights()
    mo = [mixed(gi[1], gi[2], *bi) for gi, bi in zip(g, b)]
    if side is not None:
        side_finish(side_state)
    h2 = [finish(rows, moi) for rows, moi in zip(rows_of, mo)]
    if not permute:
        for rows, h2i in zip(rows_of, h2):
            h2_ref[rows, :] = h2i.astype(BF16)
        return
    perm_ref, = perm
    nv = tm // SUBLANES
    assert nv % (tm // n_split) == 0 or (tm // n_split) % nv == 0
    for lc in range(D_MODEL // LANES):
        lanes = slice(lc * LANES, (lc + 1) * LANES)
        for s in range(SUBLANES):
            tok0 = s * nv
            i, off = divmod(tok0, tm // n_split)
            perm_ref[lc, pl.ds(s, nv, stride=SUBLANES), :] = h2[i][off:off + nv, lanes]
        h2_ref[:, lanes] = perm_ref[lc].astype(BF16)


def _side_job(side, n_steps):
    g, n_tok, q_tiled, cache, kv_new = side
    db, _, _, wb = cache.shape
    assert db == n_steps and LANES % n_tok == 0, (db, n_steps)
    b3 = lambda i: (i, 0, 0)
    b4 = lambda i: (i, 0, 0, 0)
    in_specs = [pl.BlockSpec((1, ATT_HEADS * n_tok, ATT_W), b3), pl.BlockSpec((1, 2, ATT_OUT, wb), b4),
                pl.BlockSpec((2, ATT_OUT, LANES), lambda i: (0, 0, i * n_tok // LANES))]
    out_specs = [pl.BlockSpec((1, n_tok, ATT_OUT), b3), pl.BlockSpec((1, n_tok, ATT_OUT), b3),
                 pl.BlockSpec((1, 2, ATT_OUT, wb), b4)]
    out_shape = [jax.ShapeDtypeStruct((db, n_tok, ATT_OUT), F32)] * 2 + [jax.ShapeDtypeStruct(cache.shape, F32)]
    return in_specs, (q_tiled, cache, kv_new), out_specs, out_shape, (g, n_tok, wb)


def _merge(x2d, on, outs, lses, g_pre, w_gates, w_br, w_ba, w_out, g_post, g_ffn, tm, permute, side=None):
    m = x2d.shape[0]
    n_split = 2 if tm >= 2 * MXU_N else 1
    scratch = [pltpu.VMEM((D_MODEL // LANES, tm, LANES), F32)] if permute else []
    row = lambda i: (i, 0)
    vec = _resident((1, D_MODEL))
    att_spec = pl.BlockSpec((ATT_OUT // LANES, tm, LANES), lambda i: (0, i, 0))
    in_specs = [pl.BlockSpec((tm, D_MODEL), row), pl.BlockSpec((tm, RET_V), row)] + [att_spec] * 6 + [
        vec, _resident(w_gates.shape), _resident(w_br.shape), _resident(w_ba.shape), _resident(w_out.shape),
        vec, vec]
    out_specs = [pl.BlockSpec((tm, D_MODEL), row), pl.BlockSpec((tm, D_MODEL), row)]
    out_shape = [jax.ShapeDtypeStruct((m, D_MODEL), F32), jax.ShapeDtypeStruct((m, D_MODEL), BF16)]
    operands = (x2d, on, *outs, *lses, g_pre, w_gates, w_br, w_ba, w_out, g_post, g_ffn)
    side_static = None
    if side is not None:
        s_in, s_ops, s_out, s_shape, side_static = _side_job(side, m // tm)
        in_specs, operands = s_in + in_specs, s_ops + operands
        out_specs, out_shape = out_specs + s_out, out_shape + s_shape
    return pl.pallas_call(
        functools.partial(_merge_kernel, tm=tm, n_split=n_split, permute=permute, side=side_static),
        grid=(m // tm,),
        in_specs=in_specs,
        out_specs=out_specs,
        out_shape=out_shape,
        scratch_shapes=scratch,
        compiler_params=_params(),
        name="merge",
    )(*operands)


def _ffn_kernel(*refs, tm, tiles_per_seq, seq_tok, side=None):
    if side is not None:
        (sq_ref, sc_ref, sn_ref), refs = refs[:3], refs[3:]
        so_ref, sl_ref, snc_ref = refs[9:12]
        refs = refs[:9] + refs[12:]
        side_refs = (sq_ref, sc_ref, sn_ref, so_ref, sl_ref, snc_ref)
    if seq_tok is None:
        h2_ref, x1_ref, wup_ref, wdn_ref, cw_ref, cb_ref, g_ref, y_ref, cs_ref, carry_ref, f_ref, perm_ref = refs
    else:
        h2_ref, x1_ref, p1_ref, p2_ref, wup_ref, wdn_ref, cw_ref, cb_ref, g_ref, y_ref, u_ref, f_ref = refs
    h2 = h2_ref[...]
    if seq_tok is None:
        sub = lax.broadcasted_iota(jnp.int32, (SUBLANES, FF_CHUNK), 0)

        @pl.when(pl.program_id(0) % tiles_per_seq == 0)
        def _():
            carry_ref[...] = jnp.zeros_like(carry_ref)
    else:
        t = lax.broadcasted_iota(jnp.int32, (tm, FF_CHUNK), 0) & (seq_tok - 1)
        _log2(seq_tok)

    def up(j):
        return [_dot(h2, wup_ref[:, c:c + FF_CHUNK]) for c in (j * FF_CHUNK, D_FF + j * FF_CHUNK)]

    def conv(u, col):
        cols = slice(col, col + FF_CHUNK)
        if seq_tok is None:
            prev = carry_ref[:, cols]
            last2 = jnp.where(sub == 0, prev[SUBLANES - 1:SUBLANES], pltpu.roll(u[tm - 2 * SUBLANES:tm - SUBLANES], 1, 0))
            last1 = jnp.where(sub == 0, prev[2 * SUBLANES - 1:2 * SUBLANES], pltpu.roll(u[tm - SUBLANES:tm], 1, 0))
            u1 = jnp.concatenate([last1, u[0:tm - SUBLANES]], axis=0)
            u2 = jnp.concatenate([last2, last1, u[0:tm - 2 * SUBLANES]], axis=0)
            carry_ref[:, cols] = u[tm - 2 * SUBLANES:tm]
        else:
            u1 = jnp.where(t == 0, p1_ref[:, cols], pltpu.roll(u, 1, 0))
            u2 = jnp.where(t <= 1, p2_ref[:, cols], pltpu.roll(u, 2, 0))
            u_ref[:, cols] = u
        cw = cw_ref[:, cols]
        return cb_ref[:, cols] + cw[0:1] * u2 + cw[1:2] * u1 + cw[2:3] * u

    def gate(j, ug, uv):
        cg = conv(ug, j * FF_CHUNK)
        cv = conv(uv, D_FF + j * FF_CHUNK)
        f_ref[:, j * FF_CHUNK:(j + 1) * FF_CHUNK] = (jax.nn.gelu(cg, approximate=True) * cv).astype(BF16)

    n_chunks = D_FF // FF_CHUNK
    u_next = up(0)
    for j in range(n_chunks):
        u_cur = u_next
        if j + 1 < n_chunks:
            u_next = up(j + 1)
        if j == 0 and side is not None:
            side_weights, side_finish = _attn_sample_phases(*side_refs, 0, g=side[0], n_tok=side[1], wb=side[2],
                                                            n_b=1)
        gate(j, *u_cur)
    if side is not None:
        side_state = side_weights()
    down = _dot(f_ref[...], wdn_ref[...])
    if side is not None:
        side_finish(side_state)
    out = _rms(down, g_ref[...])
    if seq_tok is None:
        nv = tm // SUBLANES
        for lc in range(D_MODEL // LANES):
            lanes = slice(lc * LANES, (lc + 1) * LANES)
            perm_ref[lc] = out[:, lanes]
            for s in range(SUBLANES):
                rows = slice(s * nv, (s + 1) * nv)
                y_ref[rows, lanes] = x1_ref[rows, lanes] + perm_ref[lc, pl.ds(s, nv, stride=SUBLANES), :]

        @pl.when(pl.program_id(0) % tiles_per_seq == tiles_per_seq - 1)
        def _():
            cs_ref[0, 0:1, :] = carry_ref[SUBLANES - 1:SUBLANES, :]
            cs_ref[0, 1:2, :] = carry_ref[2 * SUBLANES - 1:2 * SUBLANES, :]
    else:
        y_ref[...] = x1_ref[...] + out


def _ffn_prompt(h2, x1, w_up, w_dn, conv_w, conv_b, g_post, batch, seq, tm, side=None):
    m = h2.shape[0]
    tps = seq // tm
    n_steps = m // tm
    row = lambda i: (i, 0)
    in_specs = [pl.BlockSpec((tm, D_MODEL), row), pl.BlockSpec((tm, D_MODEL), row),
                _resident(w_up.shape), _resident(w_dn.shape), _resident(conv_w.shape), _resident(conv_b.shape),
                _resident((1, D_MODEL))]
    out_specs = [pl.BlockSpec((tm, D_MODEL), row),
                 pl.BlockSpec((1, CONV_W - 1, 2 * D_FF), lambda i: (i // tps, 0, 0))]
    out_shape = [jax.ShapeDtypeStruct((m, D_MODEL), F32),
                 jax.ShapeDtypeStruct((batch, CONV_W - 1, 2 * D_FF), F32)]
    operands = (h2, x1, w_up, w_dn, conv_w, conv_b, g_post)
    side_static = None
    if side is not None:
        s_in, s_ops, s_out, s_shape, side_static = _side_job(side, n_steps)
        in_specs, operands = s_in + in_specs, s_ops + operands
        out_specs, out_shape = out_specs + s_out, out_shape + s_shape
    return pl.pallas_call(
        functools.partial(_ffn_kernel, tm=tm, tiles_per_seq=tps, seq_tok=None, side=side_static),
        grid=(n_steps,),
        in_specs=in_specs,
        out_specs=out_specs,
        out_shape=out_shape,
        scratch_shapes=[pltpu.VMEM((2 * SUBLANES, 2 * D_FF), F32), pltpu.VMEM((tm, D_FF), BF16),
                        pltpu.VMEM((D_MODEL // LANES, tm, LANES), F32)],
        compiler_params=_params(),
        name="ffn_prompt",
    )(*operands)


def _ffn_sample(h2, x1, p1, p2, w_up, w_dn, conv_w, conv_b, g_post, n_tok):
    m = h2.shape[0]
    const = lambda i: (0, 0)
    full = lambda a: _resident(a.shape)
    return pl.pallas_call(
        functools.partial(_ffn_kernel, tm=m, tiles_per_seq=1, seq_tok=n_tok),
        grid=(1,),
        in_specs=[full(h2), full(x1), full(p1), full(p2), full(w_up), full(w_dn), full(conv_w), full(conv_b),
                  _resident((1, D_MODEL))],
        out_specs=[pl.BlockSpec((m, D_MODEL), const), pl.BlockSpec((m, 2 * D_FF), const)],
        out_shape=[jax.ShapeDtypeStruct((m, D_MODEL), F32), jax.ShapeDtypeStruct((m, 2 * D_FF), F32)],
        scratch_shapes=[pltpu.VMEM((m, D_FF), BF16)],
        compiler_params=_params(),
        name="ffn_sample",
    )(h2, x1, p1, p2, w_up, w_dn, conv_w, conv_b, g_post)


def _rot_tables(pos):
    pos = np.asarray(pos, np.float64)
    rf = RET_THETA ** (-np.linspace(0.0, 1.0, RET_QK_DIM // 2))
    ang = pos[:, None] * rf[None, :]
    cos, sin = np.cos(ang), np.sin(ang)
    cr = np.concatenate([cos, cos], axis=-1)
    sr = np.concatenate([-sin, sin], axis=-1)
    k_scale = RET_QK_DIM ** -0.5
    half = ROPE_DIM // 2
    af = ROPE_THETA ** (-np.arange(half) / half)
    ang = pos[:, None] * af[None, :]
    cos, sin = np.cos(ang), np.sin(ang)
    n = pos.shape[0]
    rest = ATT_HEAD_DIM - ROPE_DIM
    zh = np.zeros((n, half))
    ca = np.concatenate([cos, cos, np.ones((n, rest))], axis=-1)
    sa1 = np.concatenate([zh, sin, np.zeros((n, rest))], axis=-1)
    sa2 = np.concatenate([-sin, zh, np.zeros((n, rest))], axis=-1)
    rep = MXU_N // ATT_HEAD_DIM
    ca, sa1, sa2 = (np.tile(t, (1, rep)) for t in (ca, sa1, sa2))
    return [jnp.asarray(t, F32) for t in (cr, sr, cr * k_scale, sr * k_scale, ca, sa1, sa2)]


def kernel(x_prompt, x_sample, cache_kv_g0, cache_kv_g1, cache_kv_g2, state_ret, state_conv, norm_mix_pre, w_in,
           w_branch_ret, w_branch_attn, w_out, norm_mix_post, norm_ffn_pre, w_ffn_up, conv_w, conv_b, w_ffn_down,
           norm_ffn_post):
    batch, seq, _ = x_prompt.shape
    db, n_tok, _ = x_sample.shape
    depth = w_in.shape[0]
    assert depth == 1 and seq % (ATT_GROUPS[-1][1] * BAND_BLOCK) == 0 and n_tok <= SUBLANES
    caches = (cache_kv_g0, cache_kv_g1, cache_kv_g2)

    wi = w_in[0]
    o_rg = 2 * RET_QK + RET_V
    o_aq = o_rg + RET_V
    o_gr = o_aq + 3 * ATT_W
    w_qkv = jnp.concatenate([wi[:, :o_rg], wi[:, o_aq:o_aq + ATT_W] * ATT_HEAD_DIM ** -0.5,
                             wi[:, o_aq + ATT_W:o_gr]], axis=1).astype(BF16)
    w_gates = jnp.concatenate([wi[:, o_rg:o_aq], wi[:, o_gr:]], axis=1).astype(BF16)
    later_weights = (w_branch_ret[0], w_branch_attn[0], w_out[0], w_ffn_up[0], w_ffn_down[0])
    cb = conv_b[0][None, :]
    cw = conv_w[0]
    g_pre, g_post, g_ffn, g_post2 = (t[0][None, :] for t in (norm_mix_pre, norm_mix_post, norm_ffn_pre, norm_ffn_post))

    tm = 512
    xp = x_prompt.reshape(batch * seq, D_MODEL)
    proj_out = _proj(xp, g_pre, w_qkv, _rot_tables(np.arange(seq)), tm, seq=seq, casts=later_weights)
    rq, rk, rv = proj_out[:3]
    qkv_res = proj_out[3:3 + 3 * N_GROUPS]
    kv_last = proj_out[3 + 3 * N_GROUPS:3 + 4 * N_GROUPS]
    w_br, w_ba, w_o, w_up, w_dn = proj_out[3 + 4 * N_GROUPS:]
    on, p_ret = _ret_prompt(rq, rk, rv, batch, seq)
    outs, lses = [], []
    for g in range(N_GROUPS):
        o_g, l_g = _attn_prompt(*qkv_res[3 * g:3 * g + 3], g, batch, seq)
        outs.append(o_g)
        lses.append(l_g)
    p_kv = [t.reshape(batch, 2, ATT_HEADS, ATT_HEAD_DIM, t.shape[3]).transpose(0, 4, 1, 2, 3)[None] for t in kv_last]

    ms = db * n_tok
    xs = x_sample.reshape(ms, D_MODEL)
    pos_s = np.tile(PAST_LEN + np.arange(n_tok), db)
    rq, rk, rv, aq, kv0, kv1, kv2 = _proj(xs, g_pre, w_qkv, _rot_tables(pos_s), ms)
    kvs = (kv0, kv1, kv2)
    on_s, s_ret = _ret_sample(rq, rk, rv, state_ret[0], n_tok)
    q_tiled = jnp.tile(aq.reshape(db, n_tok, ATT_W), (1, ATT_HEADS, 1))
    sides = []
    for g in range(N_GROUPS):
        cache = caches[g][0]
        wb = cache.shape[1]
        cache_t = cache.transpose(0, 2, 3, 4, 1).reshape(db, 2, ATT_OUT, wb)
        kv_new = kvs[g].reshape(ms, 2, ATT_OUT).transpose(1, 2, 0)
        sides.append((g, n_tok, q_tiled, cache_t, kv_new))

    riders = (N_GROUPS - 2, N_GROUPS - 1) if batch * seq // tm == db else (None, None)
    sample_attn = {g: _attn_sample(*sides[g][2:], g, n_tok) for g in range(N_GROUPS) if g not in riders}
    x1, h2, *rest = _merge(xp, on, outs, lses, g_pre, w_gates, w_br, w_ba, w_o, g_post, g_ffn, tm, permute=True,
                           side=None if riders[0] is None else sides[riders[0]])
    if rest:
        sample_attn[riders[0]] = rest
    y_p, p_conv, *rest = _ffn_prompt(h2, x1, w_up, w_dn, cw, cb, g_post2, batch, seq, tm,
                                     side=None if riders[1] is None else sides[riders[1]])
    if rest:
        sample_attn[riders[1]] = rest

    outs, lses, s_kv = [], [], []
    for g in range(N_GROUPS):
        o_g, l_g, nc = sample_attn[g]
        outs.append(o_g.reshape(ms, ATT_OUT // LANES, LANES).transpose(1, 0, 2))
        lses.append(l_g.reshape(ms, ATT_OUT // LANES, LANES).transpose(1, 0, 2))
        s_kv.append(nc.reshape(db, 2, ATT_HEADS, ATT_HEAD_DIM, nc.shape[3]).transpose(0, 4, 1, 2, 3)[None])
    x1, h2 = _merge(xs, on_s, outs, lses, g_pre, w_gates, w_br, w_ba, w_o, g_post, g_ffn, ms, permute=False)
    st = state_conv[0]
    zeros = jnp.zeros((db, n_tok - 1, 2 * D_FF), F32)
    p1 = jnp.concatenate([st[:, 1:2], zeros], axis=1).reshape(ms, 2 * D_FF)
    p2 = jnp.concatenate([st, zeros[:, :n_tok - 2]], axis=1).reshape(ms, 2 * D_FF)
    y_s, u_s = _ffn_sample(h2, x1, p1, p2, w_up, w_dn, cw, cb, g_post2, n_tok)
    s_conv = u_s.reshape(db, n_tok, 2 * D_FF)[:, n_tok - (CONV_W - 1):][None]

    return (y_p.reshape(x_prompt.shape), y_s.reshape(x_sample.shape), p_kv[0], p_kv[1], p_kv[2], p_ret[None],
            p_conv[None], s_kv[0], s_kv[1], s_kv[2], s_ret[None], s_conv)
```

```python
import functools

import numpy as np
import jax
import jax.numpy as jnp
from jax import lax
from jax.experimental import pallas as pl
from jax.experimental.pallas import tpu as pltpu

F32 = jnp.float32
BF16 = jnp.bfloat16

D_MODEL = 1024
PAST_LEN = 16384
RET_HEADS = 4
RET_QK_DIM = 128
RET_V_DIM = 256
RET_CHUNK = 128
RET_THETA = 10000.0
RET_QK = RET_HEADS * RET_QK_DIM
RET_V = RET_HEADS * RET_V_DIM
ATT_GROUPS = ((128, 1), (512, 4), (2048, 16))
N_GROUPS = 3
ATT_HEADS = 4
ATT_HEAD_DIM = 64
ROPE_DIM = ATT_HEAD_DIM // 4
ROPE_THETA = 500000.0
BAND_BLOCK = 128
ATT_OUT = ATT_HEADS * ATT_HEAD_DIM
ATT_W = N_GROUPS * ATT_OUT
D_FF = 2816
CONV_W = 3
EPS = 1e-6

LANES = 128
SUBLANES = 8
MXU_N = 256
VMEM_LIMIT = 56 * 1024 * 1024
NEG = -1e30
FF_CHUNK = MXU_N
ATTN_GROUP = 2
RET_PAD = 2 * SUBLANES
QKV_COLS = 2 * RET_QK + RET_V + 3 * ATT_W


def _rms(x, g):
    return x * lax.rsqrt(jnp.mean(x * x, axis=-1, keepdims=True) + EPS) * g


def _dot(a, b):
    return jnp.dot(a, b, preferred_element_type=F32)


def _dot_nt(a, b):
    return lax.dot_general(a, b, (((1,), (1,)), ((), ())), preferred_element_type=F32)


def _resident(shape):
    return pl.BlockSpec(shape, lambda *_: (0,) * len(shape), pipeline_mode=pl.Buffered(1))


def _log2(n):
    assert n > 0 and n & (n - 1) == 0, n
    return n.bit_length() - 1


def _params(n_axes=1):
    return pltpu.CompilerParams(dimension_semantics=("arbitrary",) * n_axes, vmem_limit_bytes=VMEM_LIMIT)


def _proj_kernel(x_ref, g_ref, w_ref, crq_ref, srq_ref, crk_ref, srk_ref, ca_ref, sa1_ref, sa2_ref, *rest,
                 tm, by_residue, tiles_per_seq, n_cast, side=None):
    cast_in, rest = rest[:n_cast], rest[n_cast:]
    if side is not None:
        side_in, rest = rest[:3], rest[3:]
        n_out = 3 + 4 * N_GROUPS + n_cast
        side_weights, side_finish = _attn_sample_phases(*side_in, *rest[n_out:n_out + 3], 0, g=side[0],
                                                        n_tok=side[1], wb=side[2], n_b=1)
        rest = rest[:n_out] + rest[n_out + 3:]
    rq_ref, rk_ref, rv_ref = rest[:3]
    if by_residue:
        res_refs = rest[3:3 + 3 * N_GROUPS]
        kv_refs = rest[3 + 3 * N_GROUPS:3 + 4 * N_GROUPS]
        cast_out = rest[3 + 4 * N_GROUPS:3 + 4 * N_GROUPS + n_cast]
        scr = rest[3 + 4 * N_GROUPS + n_cast]
    else:
        aq_ref = rest[3]
        kv_refs = rest[4:4 + N_GROUPS]
        cast_out = rest[4 + N_GROUPS:4 + N_GROUPS + n_cast]
    for src, dst in zip(cast_in, cast_out):
        dst[...] = src[...].astype(BF16)
    h = _rms(x_ref[...], g_ref[...]).astype(BF16)

    def mm(c0, width=MXU_N):
        return _dot(h, w_ref[:, c0:c0 + width])

    def rot_ret(t, c, s):
        return t * c + pltpu.roll(t, RET_QK_DIM // 2, 1) * s

    ca, sa1, sa2 = ca_ref[...], sa1_ref[...], sa2_ref[...]

    def rot_att(t):
        return t * ca + pltpu.roll(t, ROPE_DIM // 2, 1) * sa1 + pltpu.roll(t, MXU_N - ROPE_DIM // 2, 1) * sa2

    def put_by_residue(ref, val, dil):
        if dil == 1:
            ref[0, 0] = val.astype(BF16)
            return
        for half in range(MXU_N // LANES):
            lanes = slice(half * LANES, (half + 1) * LANES)
            scr[half] = val[:, lanes]
            for r in range(dil):
                ref[0, r, :, lanes] = scr[half, pl.ds(r, tm // dil, stride=dil), :].astype(BF16)

    def put_ret(ref, c, c_ref, s_ref):
        def put(t):
            for half in range(MXU_N // LANES):
                lo = half * LANES
                col = c * MXU_N + lo
                ref[:, col:col + LANES] = rot_ret(t[:, lo:lo + LANES], c_ref[...], s_ref[...]).astype(BF16)
        return put

    def put_rv(c):
        def put(t):
            rv_ref[:, c * MXU_N:(c + 1) * MXU_N] = t.astype(BF16)
        return put

    def put_att(g, which):
        def put(t):
            if which < 2:
                t = rot_att(t)
            if not by_residue:
                if which == 0:
                    aq_ref[:, g * ATT_OUT:(g + 1) * ATT_OUT] = t.astype(BF16)
                else:
                    kv_refs[g][:, (which - 1) * ATT_OUT:which * ATT_OUT] = t
                return
            put_by_residue(res_refs[3 * g + which], t, ATT_GROUPS[g][1])
            if which == 0:
                return
            rows = kv_refs[g].shape[3]
            if ATT_GROUPS[g][0] >= tiles_per_seq * tm:
                kv_refs[g][0, which - 1] = t[tm - rows:tm].T
            else:
                late_windows.append((g, which, t[tm - rows:tm]))
        return put

    late_windows = []

    base = 2 * RET_QK + RET_V
    groups = []
    for c in range(RET_QK // MXU_N):
        groups.append([(c * MXU_N, put_ret(rq_ref, c, crq_ref, srq_ref)),
                       (RET_QK + c * MXU_N, put_ret(rk_ref, c, crk_ref, srk_ref))])
    for c in range(RET_V // MXU_N):
        groups.append([(2 * RET_QK + c * MXU_N, put_rv(c))])
    for g in range(N_GROUPS):
        groups.append([(base + which * ATT_W + g * ATT_OUT, put_att(g, which)) for which in range(3)])
    for group in groups:
        results = [mm(col) for col, _ in group]
        for (_, epilogue), t in zip(group, results):
            epilogue(t)
    if side is not None:
        side_finish(side_weights())
    if late_windows:
        @pl.when(pl.program_id(0) % tiles_per_seq == tiles_per_seq - 1)
        def _():
            for g, which, t in late_windows:
                kv_refs[g][0, which - 1] = t.T


def _slab_rows(rows, n_steps):
    tile = 2 * SUBLANES
    for k in range(-(-rows // (n_steps * tile)), rows // tile + 1):
        if rows % (k * tile) == 0:
            return k * tile
    raise ValueError((rows, n_steps))


def _proj(x2d, g_pre, w_qkv, tabs, tm, seq=None, casts=(), side=None):
    m = x2d.shape[0]
    n_steps = m // tm
    n_tab = tabs[0].shape[0] // tm
    row = lambda i: (i, 0)
    tab = lambda i: (i % n_tab, 0)
    in_specs = [pl.BlockSpec((tm, D_MODEL), row), _resident((1, D_MODEL)), _resident((D_MODEL, QKV_COLS))]
    in_specs += [pl.BlockSpec((tm, LANES), tab)] * 4 + [pl.BlockSpec((tm, MXU_N), tab)] * 3
    out_specs = [pl.BlockSpec((tm, w), row) for w in (RET_QK, RET_QK, RET_V)]
    out_shape = [jax.ShapeDtypeStruct((m, w), BF16) for w in (RET_QK, RET_QK, RET_V)]
    scratch = []
    if seq is None:
        widths = (ATT_W,) + (2 * ATT_OUT,) * N_GROUPS
        out_specs += [pl.BlockSpec((tm, w), row) for w in widths]
        out_shape += [jax.ShapeDtypeStruct((m, w), d) for w, d in zip(widths, (BF16,) + (F32,) * N_GROUPS)]
    else:
        batch, tps = m // seq, seq // tm
        for _, dil in ATT_GROUPS:
            assert tm % (dil * 2 * SUBLANES) == 0
            out_specs += [pl.BlockSpec((1, dil, tm // dil, ATT_OUT), lambda i: (i // tps, 0, i % tps, 0))] * 3
            out_shape += [jax.ShapeDtypeStruct((batch, dil, seq // dil, ATT_OUT), BF16)] * 3
        for win, _ in ATT_GROUPS:
            win = min(win, seq)
            if win == seq:
                out_specs.append(pl.BlockSpec((1, 2, ATT_OUT, tm), lambda i: (i // tps, 0, 0, i % tps)))
            else:
                assert win <= tm and win % LANES == 0
                out_specs.append(pl.BlockSpec((1, 2, ATT_OUT, win), lambda i: (i // tps, 0, 0, 0)))
            out_shape.append(jax.ShapeDtypeStruct((batch, 2, ATT_OUT, win), F32))
        scratch = [pltpu.VMEM((MXU_N // LANES, tm, LANES), F32)]
    for w in casts:
        rows, cols = w.shape
        slab = _slab_rows(rows, n_steps)
        spec = pl.BlockSpec((slab, cols), lambda i, n=rows // slab: (i * n // n_steps, 0))
        in_specs.append(spec)
        out_specs.append(spec)
        out_shape.append(jax.ShapeDtypeStruct(w.shape, BF16))
    operands = (x2d, g_pre, w_qkv, *tabs, *casts)
    side_static = None
    if side is not None:
        assert seq is not None
        s_in, s_ops, s_out, s_shape, side_static = _side_job(side, n_steps)
        in_specs, operands = in_specs + s_in, operands + s_ops
        out_specs, out_shape = out_specs + s_out, out_shape + s_shape
    return pl.pallas_call(
        functools.partial(_proj_kernel, tm=tm, by_residue=seq is not None,
                          tiles_per_seq=None if seq is None else seq // tm, n_cast=len(casts), side=side_static),
        grid=(n_steps,),
        in_specs=in_specs,
        out_specs=out_specs,
        out_shape=out_shape,
        scratch_shapes=scratch,
        compiler_params=_params(),
        name="proj",
    )(*operands)


def _ret_kernel(q_ref, k_ref, v_ref, dec_ref, qd_ref, kd_ref, gc_ref, o_ref, st_ref, s_scr, *, n_chunks):
    s_scr[...] = jnp.zeros_like(s_scr)

    def body(c, carry):
        r0 = pl.multiple_of(c * RET_CHUNK, RET_CHUNK)
        rows = pl.ds(r0, RET_CHUNK)
        first = []
        for h in range(RET_HEADS):
            qk = slice(h * RET_QK_DIM, (h + 1) * RET_QK_DIM)
            vv = slice(h * RET_V_DIM, (h + 1) * RET_V_DIM)
            q = q_ref[rows, qk]
            k = k_ref[rows, qk]
            v = v_ref[rows, vv]
            s0 = s_scr[h]
            sc = _dot_nt(q, k)
            from_state = _dot(q, s0.astype(BF16))
            kd_t = (k.astype(F32) * kd_ref[h]).T.astype(BF16)
            first.append((sc, from_state, _dot(kd_t, v), s0, v))
        for h, (sc, from_state, inc, s0, v) in enumerate(first):
            vv = slice(h * RET_V_DIM, (h + 1) * RET_V_DIM)
            o = _dot((sc * dec_ref[h]).astype(BF16), v) + from_state * qd_ref[h]
            s_scr[h] = s0 * gc_ref[h] + inc
            o_ref[rows, vv] = o * lax.rsqrt(jnp.mean(o * o, axis=-1, keepdims=True) + EPS)
        return carry

    lax.fori_loop(0, n_chunks, body, 0, unroll=8)
    st_ref[0] = s_scr[...]


def _log_gamma():
    return np.log1p(-np.exp2(-5.0 - np.arange(RET_HEADS, dtype=np.float64)))


def _ret_tables(chunk):
    lg = _log_gamma()
    idx = np.arange(RET_CHUNK, dtype=np.float64)
    diff = idx[:, None] - idx[None, :]
    dec = np.where(diff[None] >= 0, np.exp(lg[:, None, None] * np.maximum(diff, 0.0)[None]), 0.0)
    qd = np.exp(lg[:, None] * (idx[None, :] + 1.0))
    kd = np.exp(lg[:, None] * (chunk - 1.0 - idx)[None, :])
    gc = np.exp(lg * chunk)
    qd = np.broadcast_to(qd[:, :, None], (RET_HEADS, RET_CHUNK, RET_V_DIM))
    kd = np.broadcast_to(kd[:, :, None], (RET_HEADS, RET_CHUNK, RET_QK_DIM))
    gc = np.broadcast_to(gc[:, None, None], (RET_HEADS, RET_QK_DIM, RET_V_DIM))
    return [jnp.asarray(t, F32) for t in (dec, qd, kd, gc)]


def _ret_prompt(rq, rk, rv, batch, seq):
    dec, qd, kd, gc = _ret_tables(RET_CHUNK)
    row = lambda b: (b, 0)
    return pl.pallas_call(
        functools.partial(_ret_kernel, n_chunks=seq // RET_CHUNK),
        grid=(batch,),
        in_specs=[pl.BlockSpec((seq, RET_QK), row), pl.BlockSpec((seq, RET_QK), row), pl.BlockSpec((seq, RET_V), row),
                  _resident(dec.shape), _resident(qd.shape), _resident(kd.shape), _resident(gc.shape)],
        out_specs=[pl.BlockSpec((seq, RET_V), row),
                   pl.BlockSpec((1, RET_HEADS, RET_QK_DIM, RET_V_DIM), lambda b: (b, 0, 0, 0))],
        out_shape=[jax.ShapeDtypeStruct((batch * seq, RET_V), F32),
                   jax.ShapeDtypeStruct((batch, RET_HEADS, RET_QK_DIM, RET_V_DIM), F32)],
        scratch_shapes=[pltpu.VMEM((RET_HEADS, RET_QK_DIM, RET_V_DIM), F32)],
        compiler_params=_params(),
        name="ret_prompt",
    )(rq, rk, rv, dec, qd, kd, gc)


def _ret_sample_kernel(q_ref, k_ref, kt_ref, v_ref, s_ref, dec_ref, qd_ref, kd_ref, gc_ref, o_ref, st_ref,
                       *, n_tok, n_b):
    pairs = [(b, h) for b in range(n_b) for h in range(RET_HEADS)]
    qk_of = lambda h: slice(h * RET_QK_DIM, (h + 1) * RET_QK_DIM)
    vv_of = lambda h: slice(h * RET_V_DIM, (h + 1) * RET_V_DIM)
    zeros = jnp.zeros((LANES - RET_PAD, RET_V_DIM), BF16)
    first = []
    for b, h in pairs:
        q = q_ref[b, :, qk_of(h)]
        k = k_ref[b, :, qk_of(h)]
        v = v_ref[b, :, vv_of(h)]
        s0 = s_ref[b, h]
        from_state = _dot(q.astype(BF16), s0.astype(BF16))
        k_dec_t = (kt_ref[b, h] * kd_ref[h]).astype(BF16)
        v_rows = jnp.concatenate([v.astype(BF16), zeros], axis=0)
        st_ref[b, h] = s0 * gc_ref[h] + _dot(k_dec_t, v_rows)
        scores = [jnp.sum(q * k[i:i + 1, :], axis=-1, keepdims=True) for i in range(n_tok)]
        first.append((from_state, scores))
    second = []
    for (b, h), (from_state, scores) in zip(pairs, first):
        v = v_ref[b, :, vv_of(h)]
        o = from_state * qd_ref[h]
        for i, sc_i in enumerate(scores):
            o = o + (sc_i * dec_ref[h][:, i:i + 1]) * v[i:i + 1, :]
        second.append((o, jnp.mean(o * o, axis=-1, keepdims=True)))
    for (b, h), (o, ms) in zip(pairs, second):
        o_ref[b, :, vv_of(h)] = o * lax.rsqrt(ms + EPS)


def _ret_sample(rq, rk, rv, state, n_tok):
    db = state.shape[0]
    pad = RET_PAD - n_tok
    n_b = 8

    def pad_rows(t):
        t = t.astype(F32).reshape(db, n_tok, t.shape[-1])
        return jnp.pad(t, ((0, 0), (0, pad), (0, 0)))

    q, k, v = pad_rows(rq), pad_rows(rk), pad_rows(rv)
    kt = jnp.pad(k, ((0, 0), (0, LANES - RET_PAD), (0, 0)))
    kt = kt.reshape(db, LANES, RET_HEADS, RET_QK_DIM).transpose(0, 2, 3, 1)
    lg = _log_gamma()
    idx = np.arange(RET_PAD, dtype=np.float64)
    diff = idx[:, None] - idx[None, :]
    dec = np.where(diff[None] >= 0, np.exp(lg[:, None, None] * np.maximum(diff, 0.0)[None]), 0.0)
    qd = np.broadcast_to(np.exp(lg[:, None] * (idx[None, :] + 1.0))[:, :, None], (RET_HEADS, RET_PAD, RET_V_DIM))
    kd = np.exp(lg[:, None] * (n_tok - 1.0 - np.arange(LANES, dtype=np.float64))[None, :])
    kd = np.broadcast_to(kd[:, None, :], (RET_HEADS, RET_QK_DIM, LANES))
    gc = np.broadcast_to(np.exp(lg * n_tok)[:, None, None], (RET_HEADS, RET_QK_DIM, RET_V_DIM))
    dec, qd, kd, gc = (jnp.asarray(t, F32) for t in (dec, qd, kd, gc))
    b3 = lambda i: (i, 0, 0)
    b4 = lambda i: (i, 0, 0, 0)
    o, st = pl.pallas_call(
        functools.partial(_ret_sample_kernel, n_tok=n_tok, n_b=n_b),
        grid=(db // n_b,),
        in_specs=[pl.BlockSpec((n_b, RET_PAD, RET_QK), b3), pl.BlockSpec((n_b, RET_PAD, RET_QK), b3),
                  pl.BlockSpec((n_b, RET_HEADS, RET_QK_DIM, LANES), b4),
                  pl.BlockSpec((n_b, RET_PAD, RET_V), b3),
                  pl.BlockSpec((n_b, RET_HEADS, RET_QK_DIM, RET_V_DIM), b4),
                  _resident(dec.shape), _resident(qd.shape), _resident(kd.shape), _resident(gc.shape)],
        out_specs=[pl.BlockSpec((n_b, RET_PAD, RET_V), b3),
                   pl.BlockSpec((n_b, RET_HEADS, RET_QK_DIM, RET_V_DIM), b4)],
        out_shape=[jax.ShapeDtypeStruct((db, RET_PAD, RET_V), F32),
                   jax.ShapeDtypeStruct(state.shape, F32)],
        compiler_params=_params(),
        name="ret_sample",
    )(q, k, kt, v, state, dec, qd, kd, gc)
    return o[:, :n_tok].reshape(db * n_tok, RET_V), st


def _attn_kernel(q_ref, k_ref, v_ref, o_ref, l_ref, vt_scr, bias_scr, *staging, dil, nb):
    n_blk = dil * nb
    length = nb * BAND_BLOCK
    blk = BAND_BLOCK
    o_dst, l_dst = staging if dil > 1 else (o_ref, l_ref)

    kk = lax.broadcasted_iota(jnp.int32, (2 * blk, 2 * LANES), 0)
    ql = lax.broadcasted_iota(jnp.int32, (2 * blk, 2 * LANES), 1) & (blk - 1)
    in_span = kk <= ql + blk
    bias_scr[...] = jnp.where(in_span & (kk >= ql), 0.0, NEG)

    def transpose_values(c):
        vt_scr[c] = v_ref[0, c * blk:(c + 1) * blk, :].astype(F32).T.astype(BF16)

    lane = lax.broadcasted_iota(jnp.int32, (blk, LANES), 1)

    def scores(c, p):
        lanes = slice(p * LANES, (p + 1) * LANES)
        k0 = 0 if c % nb == 0 else blk
        q128 = q_ref[0, c * blk:(c + 1) * blk, lanes]
        zero = jnp.zeros_like(q128)
        q_pair = jnp.concatenate([jnp.where(lane < ATT_HEAD_DIM, q128, zero),
                                  jnp.where(lane >= ATT_HEAD_DIM, q128, zero)], axis=0)
        keys = k_ref[0, c * blk - k0:(c + 1) * blk, lanes]
        return _dot_nt(keys, q_pair) + bias_scr[blk - k0:2 * blk, :]

    def weights_values(c, p, s_t):
        lanes = slice(p * LANES, (p + 1) * LANES)
        m = jnp.max(s_t, axis=0, keepdims=True)
        p_t = jnp.exp(s_t - m).astype(BF16)
        v_t = vt_scr[c, lanes, :]
        if c % nb:
            v_t = jnp.concatenate([vt_scr[c - 1, lanes, :], v_t], axis=1)
        v_t = jnp.concatenate([v_t, jnp.ones((2 * SUBLANES, v_t.shape[1]), BF16)], axis=0)
        return _dot(v_t, p_t), m

    def finish(c, p, o_t, m):
        den = o_t[LANES:LANES + 1, :]
        lse = m + jnp.log(den)
        o_pair = jnp.concatenate([o_t[0:ATT_HEAD_DIM, 0:blk] / den[:, 0:blk],
                                  o_t[ATT_HEAD_DIM:LANES, blk:2 * blk] / den[:, blk:2 * blk]], axis=0)
        l_pair = jnp.concatenate([jnp.broadcast_to(lse[:, 0:blk], (ATT_HEAD_DIM, blk)),
                                  jnp.broadcast_to(lse[:, blk:2 * blk], (ATT_HEAD_DIM, blk))], axis=0)
        o_dst[p, c * blk:(c + 1) * blk, :] = o_pair.T
        l_dst[p, c * blk:(c + 1) * blk, :] = l_pair.T

    n_pair = ATT_OUT // LANES
    groups = [[(c, p) for c in range(c0, min(c0 + ATTN_GROUP, n_blk)) for p in range(n_pair)]
              for c0 in range(0, n_blk, ATTN_GROUP)]

    def stage_scores(group):
        for c in sorted({c for c, _ in group}):
            transpose_values(c)
        return [scores(*chain) for chain in group]

    s_next = stage_scores(groups[0])
    pending = []
    for i, group in enumerate(groups):
        s_cur = s_next
        if i + 1 < len(groups):
            s_next = stage_scores(groups[i + 1])
        o_m = [weights_values(*chain, s) for chain, s in zip(group, s_cur)]
        for args in pending:
            finish(*args)
        pending = [chain + om for chain, om in zip(group, o_m)]
    for args in pending:
        finish(*args)

    if dil > 1:
        for p in range(ATT_OUT // LANES):
            for r in range(dil):
                o_ref[p, pl.ds(r, length, stride=dil), :] = o_dst[p, r * length:(r + 1) * length, :]
                l_ref[p, pl.ds(r, length, stride=dil), :] = l_dst[p, r * length:(r + 1) * length, :]


def _attn_prompt(q, k, v, g, batch, seq):
    _, dil = ATT_GROUPS[g]
    nb = seq // dil // BAND_BLOCK
    _log2(nb)
    q, k, v = (t.reshape(batch, seq, ATT_OUT) for t in (q, k, v))
    in_spec = pl.BlockSpec((1, seq, ATT_OUT), lambda b: (b, 0, 0))
    n_pair = ATT_OUT // LANES
    out_spec = pl.BlockSpec((n_pair, seq, LANES), lambda b: (0, b, 0))
    scratch = [pltpu.VMEM((dil * nb, ATT_OUT, BAND_BLOCK), BF16), pltpu.VMEM((2 * BAND_BLOCK, 2 * LANES), F32)]
    if dil > 1:
        scratch += [pltpu.VMEM((n_pair, seq, LANES), F32)] * 2
    return pl.pallas_call(
        functools.partial(_attn_kernel, dil=dil, nb=nb),
        grid=(batch,),
        in_specs=[in_spec] * 3,
        out_specs=[out_spec] * 2,
        out_shape=[jax.ShapeDtypeStruct((n_pair, batch * seq, LANES), F32)] * 2,
        scratch_shapes=scratch,
        compiler_params=_params(),
        name=f"attn_prompt_g{g}",
    )(q, k, v)


def _attn_sample_kernel(q_ref, c_ref, n_ref, o_ref, l_ref, nc_ref, *, g, n_tok, wb, n_b):
    for bb in range(n_b):
        _attn_sample_one(q_ref, c_ref, n_ref, o_ref, l_ref, nc_ref, bb, g=g, n_tok=n_tok, wb=wb, n_b=n_b)


def _attn_sample_one(q_ref, c_ref, n_ref, o_ref, l_ref, nc_ref, bb, **static):
    weights, finish = _attn_sample_phases(q_ref, c_ref, n_ref, o_ref, l_ref, nc_ref, bb, **static)
    finish(weights())


def _attn_sample_phases(q_ref, c_ref, n_ref, o_ref, l_ref, nc_ref, bb, *, g, n_tok, wb, n_b):
    win, dil = ATT_GROUPS[g]
    nk = win // dil
    _log2(dil)
    n_col = ATT_HEADS * n_tok
    first_new = LANES - n_tok
    seq_lane = ((pl.program_id(0) * n_b + bb) * n_tok) % LANES
    new = [pltpu.roll(n_ref[kv], (first_new + LANES - seq_lane) % LANES, 1) for kv in range(2)]
    lane_t = lax.broadcasted_iota(jnp.int32, (ATT_OUT, LANES), 1)
    for kv in range(2):
        shifted = pltpu.roll(c_ref[bb, kv], wb - n_tok, 1)
        if wb > LANES:
            nc_ref[bb, kv, :, 0:wb - LANES] = shifted[:, 0:wb - LANES]
        nc_ref[bb, kv, :, wb - LANES:wb] = jnp.where(lane_t < first_new, shifted[:, wb - LANES:wb], new[kv])

    q = q_ref[bb][:, g * ATT_OUT:(g + 1) * ATT_OUT]
    row = lax.broadcasted_iota(jnp.int32, (n_col, ATT_OUT), 0)
    lane = lax.broadcasted_iota(jnp.int32, (n_col, ATT_OUT), 1)
    head_sel = (lane >> _log2(ATT_HEAD_DIM)) == (row >> _log2(n_tok))
    qbd = jnp.where(head_sel, q, jnp.zeros_like(q))
    k_c, v_c = c_ref[bb, 0].astype(BF16), c_ref[bb, 1].astype(BF16)
    k_n, v_n = new[0].astype(BF16), new[1].astype(BF16)

    def weights():
        s_c = _dot(qbd, k_c)
        s_n = _dot(qbd, k_n)
        t_c = lax.broadcasted_iota(jnp.int32, (n_col, wb), 0) & (n_tok - 1)
        d_c = wb + t_c - lax.broadcasted_iota(jnp.int32, (n_col, wb), 1)
        ok_c = ((d_c & (dil - 1)) == 0) & (d_c <= nk * dil)
        t_n = lax.broadcasted_iota(jnp.int32, (n_col, LANES), 0) & (n_tok - 1)
        new_idx = lax.broadcasted_iota(jnp.int32, (n_col, LANES), 1) - first_new
        d_n = t_n - new_idx
        ok_n = (new_idx >= 0) & (d_n >= 0) & ((d_n & (dil - 1)) == 0) & (d_n <= nk * dil)
        s_c = jnp.where(ok_c, s_c, NEG)
        s_n = jnp.where(ok_n, s_n, NEG)
        m = jnp.maximum(jnp.max(s_c, axis=-1, keepdims=True), jnp.max(s_n, axis=-1, keepdims=True))
        p_c = jnp.exp(s_c - m)
        p_n = jnp.exp(s_n - m)
        den = jnp.sum(p_c, axis=-1, keepdims=True) + jnp.sum(p_n, axis=-1, keepdims=True)
        return p_c.astype(BF16), p_n.astype(BF16), m, den

    def finish(state):
        p_c, p_n, m, den = state
        o = (_dot_nt(p_c, v_c) + _dot_nt(p_n, v_n)) / den
        lse = jnp.broadcast_to(m + jnp.log(den), (n_col, ATT_OUT))
        o = jnp.where(head_sel, o, 0.0)
        lse = jnp.where(head_sel, lse, 0.0)
        o_t = o[0:n_tok]
        l_t = lse[0:n_tok]
        for h in range(1, ATT_HEADS):
            o_t = o_t + o[h * n_tok:(h + 1) * n_tok]
            l_t = l_t + lse[h * n_tok:(h + 1) * n_tok]
        o_ref[bb] = o_t
        l_ref[bb] = l_t

    return weights, finish


def _attn_sample(q_tiled, cache, kv_new, g, n_tok):
    db, _, _, wb = cache.shape
    n_b = max(1, min(SUBLANES, ATT_GROUPS[-1][0] // wb))
    assert wb % LANES == 0 and db % n_b == 0 and LANES % (n_b * n_tok) == 0 and kv_new.shape[2] % LANES == 0
    b3 = lambda i: (i, 0, 0)
    b4 = lambda i: (i, 0, 0, 0)
    return pl.pallas_call(
        functools.partial(_attn_sample_kernel, g=g, n_tok=n_tok, wb=wb, n_b=n_b),
        grid=(db // n_b,),
        in_specs=[pl.BlockSpec((n_b, ATT_HEADS * n_tok, ATT_W), b3),
                  pl.BlockSpec((n_b, 2, ATT_OUT, wb), b4),
                  pl.BlockSpec((2, ATT_OUT, LANES), lambda i: (0, 0, i * n_b * n_tok // LANES))],
        out_specs=[pl.BlockSpec((n_b, n_tok, ATT_OUT), b3), pl.BlockSpec((n_b, n_tok, ATT_OUT), b3),
                   pl.BlockSpec((n_b, 2, ATT_OUT, wb), b4)],
        out_shape=[jax.ShapeDtypeStruct((db, n_tok, ATT_OUT), F32), jax.ShapeDtypeStruct((db, n_tok, ATT_OUT), F32),
                   jax.ShapeDtypeStruct(cache.shape, F32)],
        compiler_params=_params(),
        name=f"attn_sample_g{g}",
    )(q_tiled, cache, kv_new)


def _merge_kernel(*refs, tm, n_split, permute, side=None):
    if side is not None:
        side_in, refs = refs[:3], refs[3:]
    (x_ref, on_ref, o0_ref, o1_ref, o2_ref, l0_ref, l1_ref, l2_ref, gpre_ref, wg_ref, wbr_ref, wba_ref, wout_ref,
     gpost_ref, gffn_ref, x1_ref, h2_ref), perm = refs[:17], refs[17:]
    if side is not None:
        side_refs, perm = side_in + perm[:3], perm[3:]
    bounds = [i * tm // n_split for i in range(n_split + 1)]
    rows_of = [slice(a, b) for a, b in zip(bounds[:-1], bounds[1:])]
    n_pair = ATT_OUT // LANES

    def gates(rows):
        x = x_ref[rows, :]
        h = _rms(x, gpre_ref[...]).astype(BF16)
        return [_dot(h, wg_ref[:, c:c + D_MODEL]) for c in (0, RET_V, RET_V + D_MODEL)]

    def branches(rows, rg):
        r = (on_ref[rows, :] * (rg * jax.nn.sigmoid(rg))).astype(BF16)
        atts = []
        for p in range(n_pair):
            l0, l1, l2 = l0_ref[p, rows, :], l1_ref[p, rows, :], l2_ref[p, rows, :]
            mx = jnp.maximum(jnp.maximum(l0, l1), l2)
            e0, e1, e2 = jnp.exp(l0 - mx), jnp.exp(l1 - mx), jnp.exp(l2 - mx)
            att = (e0 * o0_ref[p, rows, :] + e1 * o1_ref[p, rows, :] + e2 * o2_ref[p, rows, :]) / (e0 + e1 + e2)
            atts.append(att.astype(BF16))
        return _dot(r, wbr_ref[...]), _dot(jnp.concatenate(atts, axis=1), wba_ref[...])

    def mixed(gr, ga, br, ba):
        mix = jax.nn.sigmoid(gr) * br + jax.nn.sigmoid(ga) * ba
        return _dot(mix.astype(BF16), wout_ref[...])

    def finish(rows, mo):
        x1 = x_ref[rows, :] + _rms(mo, gpost_ref[...])
        x1_ref[rows, :] = x1
        return _rms(x1, gffn_ref[...])

    g = []
    for i, rows in enumerate(rows_of):
        g.append(gates(rows))
        if i == 0 and side is not None:
            side_weights, side_finish = _attn_sample_phases(*side_refs, 0, g=side[0], n_tok=side[1], wb=side[2],
                                                            n_b=1)
    b = [branches(rows, gi[0]) for rows, gi in zip(rows_of, g)]
    if side is not None:
        side_state = side_weights()
    mo = [mixed(gi[1], gi[2], *bi) for gi, bi in zip(g, b)]
    if side is not None:
        side_finish(side_state)
    h2 = [finish(rows, moi) for rows, moi in zip(rows_of, mo)]
    if not permute:
        for rows, h2i in zip(rows_of, h2):
            h2_ref[rows, :] = h2i.astype(BF16)
        return
    perm_ref, = perm
    nv = tm // SUBLANES
    for lc in range(D_MODEL // LANES):
        lanes = slice(lc * LANES, (lc + 1) * LANES)
        for s in range(SUBLANES):
            tok0 = s * nv
            i = max(j for j, b in enumerate(bounds[:-1]) if b <= tok0)
            off = tok0 - bounds[i]
            assert tok0 + nv <= bounds[i + 1]
            perm_ref[lc, pl.ds(s, nv, stride=SUBLANES), :] = h2[i][off:off + nv, lanes]
        h2_ref[:, lanes] = perm_ref[lc].astype(BF16)


def _side_job(side, n_steps):
    g, n_tok, q_tiled, cache, kv_new = side
    db, _, _, wb = cache.shape
    assert db == n_steps and LANES % n_tok == 0, (db, n_steps)
    b3 = lambda i: (i, 0, 0)
    b4 = lambda i: (i, 0, 0, 0)
    in_specs = [pl.BlockSpec((1, ATT_HEADS * n_tok, ATT_W), b3), pl.BlockSpec((1, 2, ATT_OUT, wb), b4),
                pl.BlockSpec((2, ATT_OUT, LANES), lambda i: (0, 0, i * n_tok // LANES))]
    out_specs = [pl.BlockSpec((1, n_tok, ATT_OUT), b3), pl.BlockSpec((1, n_tok, ATT_OUT), b3),
                 pl.BlockSpec((1, 2, ATT_OUT, wb), b4)]
    out_shape = [jax.ShapeDtypeStruct((db, n_tok, ATT_OUT), F32)] * 2 + [jax.ShapeDtypeStruct(cache.shape, F32)]
    return in_specs, (q_tiled, cache, kv_new), out_specs, out_shape, (g, n_tok, wb)


def _merge(x2d, on, outs, lses, g_pre, w_gates, w_br, w_ba, w_out, g_post, g_ffn, tm, permute, side=None):
    m = x2d.shape[0]
    n_split = 2 if tm >= 2 * MXU_N else 1
    scratch = [pltpu.VMEM((D_MODEL // LANES, tm, LANES), F32)] if permute else []
    row = lambda i: (i, 0)
    vec = _resident((1, D_MODEL))
    att_spec = pl.BlockSpec((ATT_OUT // LANES, tm, LANES), lambda i: (0, i, 0))
    in_specs = [pl.BlockSpec((tm, D_MODEL), row), pl.BlockSpec((tm, RET_V), row)] + [att_spec] * 6 + [
        vec, _resident(w_gates.shape), _resident(w_br.shape), _resident(w_ba.shape), _resident(w_out.shape),
        vec, vec]
    out_specs = [pl.BlockSpec((tm, D_MODEL), row), pl.BlockSpec((tm, D_MODEL), row)]
    out_shape = [jax.ShapeDtypeStruct((m, D_MODEL), F32), jax.ShapeDtypeStruct((m, D_MODEL), BF16)]
    operands = (x2d, on, *outs, *lses, g_pre, w_gates, w_br, w_ba, w_out, g_post, g_ffn)
    side_static = None
    if side is not None:
        s_in, s_ops, s_out, s_shape, side_static = _side_job(side, m // tm)
        in_specs, operands = s_in + in_specs, s_ops + operands
        out_specs, out_shape = out_specs + s_out, out_shape + s_shape
    return pl.pallas_call(
        functools.partial(_merge_kernel, tm=tm, n_split=n_split, permute=permute, side=side_static),
        grid=(m // tm,),
        in_specs=in_specs,
        out_specs=out_specs,
        out_shape=out_shape,
        scratch_shapes=scratch,
        compiler_params=_params(),
        name="merge",
    )(*operands)


def _ffn_kernel(*refs, tm, tiles_per_seq, seq_tok, side=None):
    if side is not None:
        (sq_ref, sc_ref, sn_ref), refs = refs[:3], refs[3:]
        so_ref, sl_ref, snc_ref = refs[9:12]
        refs = refs[:9] + refs[12:]
        side_refs = (sq_ref, sc_ref, sn_ref, so_ref, sl_ref, snc_ref)
    if seq_tok is None:
        h2_ref, x1_ref, wup_ref, wdn_ref, cw_ref, cb_ref, g_ref, y_ref, cs_ref, carry_ref, f_ref, perm_ref = refs
    else:
        h2_ref, x1_ref, p1_ref, p2_ref, wup_ref, wdn_ref, cw_ref, cb_ref, g_ref, y_ref, u_ref, f_ref = refs
    h2 = h2_ref[...]
    if seq_tok is None:
        sub = lax.broadcasted_iota(jnp.int32, (SUBLANES, FF_CHUNK), 0)

        @pl.when(pl.program_id(0) % tiles_per_seq == 0)
        def _():
            carry_ref[...] = jnp.zeros_like(carry_ref)
    else:
        t = lax.broadcasted_iota(jnp.int32, (tm, FF_CHUNK), 0) & (seq_tok - 1)
        _log2(seq_tok)

    def up(j):
        return [_dot(h2, wup_ref[:, c:c + FF_CHUNK]) for c in (j * FF_CHUNK, D_FF + j * FF_CHUNK)]

    def conv(u, col):
        cols = slice(col, col + FF_CHUNK)
        if seq_tok is None:
            prev = carry_ref[:, cols]
            last2 = jnp.where(sub == 0, prev[SUBLANES - 1:SUBLANES], pltpu.roll(u[tm - 2 * SUBLANES:tm - SUBLANES], 1, 0))
            last1 = jnp.where(sub == 0, prev[2 * SUBLANES - 1:2 * SUBLANES], pltpu.roll(u[tm - SUBLANES:tm], 1, 0))
            u1 = jnp.concatenate([last1, u[0:tm - SUBLANES]], axis=0)
            u2 = jnp.concatenate([last2, last1, u[0:tm - 2 * SUBLANES]], axis=0)
            carry_ref[:, cols] = u[tm - 2 * SUBLANES:tm]
        else:
            u1 = jnp.where(t == 0, p1_ref[:, cols], pltpu.roll(u, 1, 0))
            u2 = jnp.where(t <= 1, p2_ref[:, cols], pltpu.roll(u, 2, 0))
            u_ref[:, cols] = u
        cw = cw_ref[:, cols]
        return cb_ref[:, cols] + cw[0:1] * u2 + cw[1:2] * u1 + cw[2:3] * u

    def gate(j, ug, uv):
        cg = conv(ug, j * FF_CHUNK)
        cv = conv(uv, D_FF + j * FF_CHUNK)
        f_ref[:, j * FF_CHUNK:(j + 1) * FF_CHUNK] = (jax.nn.gelu(cg, approximate=True) * cv).astype(BF16)

    n_chunks = D_FF // FF_CHUNK
    u_next = up(0)
    for j in range(n_chunks):
        u_cur = u_next
        if j + 1 < n_chunks:
            u_next = up(j + 1)
        if j == 0 and side is not None:
            side_weights, side_finish = _attn_sample_phases(*side_refs, 0, g=side[0], n_tok=side[1], wb=side[2],
                                                            n_b=1)
        gate(j, *u_cur)
    if side is not None:
        side_state = side_weights()
    down = _dot(f_ref[...], wdn_ref[...])
    if side is not None:
        side_finish(side_state)
    out = _rms(down, g_ref[...])
    if seq_tok is None:
        nv = tm // SUBLANES
        for lc in range(D_MODEL // LANES):
            lanes = slice(lc * LANES, (lc + 1) * LANES)
            perm_ref[lc] = out[:, lanes]
            for s in range(SUBLANES):
                rows = slice(s * nv, (s + 1) * nv)
                y_ref[rows, lanes] = x1_ref[rows, lanes] + perm_ref[lc, pl.ds(s, nv, stride=SUBLANES), :]

        @pl.when(pl.program_id(0) % tiles_per_seq == tiles_per_seq - 1)
        def _():
            cs_ref[0, 0:1, :] = carry_ref[SUBLANES - 1:SUBLANES, :]
            cs_ref[0, 1:2, :] = carry_ref[2 * SUBLANES - 1:2 * SUBLANES, :]
    else:
        y_ref[...] = x1_ref[...] + out


def _ffn_prompt(h2, x1, w_up, w_dn, conv_w, conv_b, g_post, batch, seq, tm, side=None):
    m = h2.shape[0]
    tps = seq // tm
    n_steps = m // tm
    row = lambda i: (i, 0)
    in_specs = [pl.BlockSpec((tm, D_MODEL), row), pl.BlockSpec((tm, D_MODEL), row),
                _resident(w_up.shape), _resident(w_dn.shape), _resident(conv_w.shape), _resident(conv_b.shape),
                _resident((1, D_MODEL))]
    out_specs = [pl.BlockSpec((tm, D_MODEL), row),
                 pl.BlockSpec((1, CONV_W - 1, 2 * D_FF), lambda i: (i // tps, 0, 0))]
    out_shape = [jax.ShapeDtypeStruct((m, D_MODEL), F32),
                 jax.ShapeDtypeStruct((batch, CONV_W - 1, 2 * D_FF), F32)]
    operands = (h2, x1, w_up, w_dn, conv_w, conv_b, g_post)
    side_static = None
    if side is not None:
        s_in, s_ops, s_out, s_shape, side_static = _side_job(side, n_steps)
        in_specs, operands = s_in + in_specs, s_ops + operands
        out_specs, out_shape = out_specs + s_out, out_shape + s_shape
    return pl.pallas_call(
        functools.partial(_ffn_kernel, tm=tm, tiles_per_seq=tps, seq_tok=None, side=side_static),
        grid=(n_steps,),
        in_specs=in_specs,
        out_specs=out_specs,
        out_shape=out_shape,
        scratch_shapes=[pltpu.VMEM((2 * SUBLANES, 2 * D_FF), F32), pltpu.VMEM((tm, D_FF), BF16),
                        pltpu.VMEM((D_MODEL // LANES, tm, LANES), F32)],
        compiler_params=_params(),
        name="ffn_prompt",
    )(*operands)


def _ffn_sample(h2, x1, p1, p2, w_up, w_dn, conv_w, conv_b, g_post, n_tok):
    m = h2.shape[0]
    const = lambda i: (0, 0)
    full = lambda a: _resident(a.shape)
    return pl.pallas_call(
        functools.partial(_ffn_kernel, tm=m, tiles_per_seq=1, seq_tok=n_tok),
        grid=(1,),
        in_specs=[full(h2), full(x1), full(p1), full(p2), full(w_up), full(w_dn), full(conv_w), full(conv_b),
                  _resident((1, D_MODEL))],
        out_specs=[pl.BlockSpec((m, D_MODEL), const), pl.BlockSpec((m, 2 * D_FF), const)],
        out_shape=[jax.ShapeDtypeStruct((m, D_MODEL), F32), jax.ShapeDtypeStruct((m, 2 * D_FF), F32)],
        scratch_shapes=[pltpu.VMEM((m, D_FF), BF16)],
        compiler_params=_params(),
        name="ffn_sample",
    )(h2, x1, p1, p2, w_up, w_dn, conv_w, conv_b, g_post)


def _rot_tables(pos):
    pos = np.asarray(pos, np.float64)
    rf = RET_THETA ** (-np.linspace(0.0, 1.0, RET_QK_DIM // 2))
    ang = pos[:, None] * rf[None, :]
    cos, sin = np.cos(ang), np.sin(ang)
    cr = np.concatenate([cos, cos], axis=-1)
    sr = np.concatenate([-sin, sin], axis=-1)
    k_scale = RET_QK_DIM ** -0.5
    half = ROPE_DIM // 2
    af = ROPE_THETA ** (-np.arange(half) / half)
    ang = pos[:, None] * af[None, :]
    cos, sin = np.cos(ang), np.sin(ang)
    n = pos.shape[0]
    rest = ATT_HEAD_DIM - ROPE_DIM
    zh = np.zeros((n, half))
    ca = np.concatenate([cos, cos, np.ones((n, rest))], axis=-1)
    sa1 = np.concatenate([zh, sin, np.zeros((n, rest))], axis=-1)
    sa2 = np.concatenate([-sin, zh, np.zeros((n, rest))], axis=-1)
    rep = MXU_N // ATT_HEAD_DIM
    ca, sa1, sa2 = (np.tile(t, (1, rep)) for t in (ca, sa1, sa2))
    return [jnp.asarray(t, F32) for t in (cr, sr, cr * k_scale, sr * k_scale, ca, sa1, sa2)]


def kernel(x_prompt, x_sample, cache_kv_g0, cache_kv_g1, cache_kv_g2, state_ret, state_conv, norm_mix_pre, w_in,
           w_branch_ret, w_branch_attn, w_out, norm_mix_post, norm_ffn_pre, w_ffn_up, conv_w, conv_b, w_ffn_down,
           norm_ffn_post):
    batch, seq, _ = x_prompt.shape
    db, n_tok, _ = x_sample.shape
    depth = w_in.shape[0]
    assert depth == 1 and seq % (ATT_GROUPS[-1][1] * BAND_BLOCK) == 0 and n_tok <= SUBLANES
    caches = (cache_kv_g0, cache_kv_g1, cache_kv_g2)

    wi = w_in[0]
    o_rg = 2 * RET_QK + RET_V
    o_aq = o_rg + RET_V
    o_gr = o_aq + 3 * ATT_W
    w_qkv = jnp.concatenate([wi[:, :o_rg], wi[:, o_aq:o_aq + ATT_W] * ATT_HEAD_DIM ** -0.5,
                             wi[:, o_aq + ATT_W:o_gr]], axis=1).astype(BF16)
    w_gates = jnp.concatenate([wi[:, o_rg:o_aq], wi[:, o_gr:]], axis=1).astype(BF16)
    later_weights = (w_branch_ret[0], w_branch_attn[0], w_out[0], w_ffn_up[0], w_ffn_down[0])
    cb = conv_b[0][None, :]
    cw = conv_w[0]
    g_pre, g_post, g_ffn, g_post2 = (t[0][None, :] for t in (norm_mix_pre, norm_mix_post, norm_ffn_pre, norm_ffn_post))

    tm = 512
    ms = db * n_tok
    xs = x_sample.reshape(ms, D_MODEL)
    pos_s = np.tile(PAST_LEN + np.arange(n_tok), db)
    rq_s, rk_s, rv_s, aq, kv0, kv1, kv2 = _proj(xs, g_pre, w_qkv, _rot_tables(pos_s), ms)
    kvs = (kv0, kv1, kv2)
    q_tiled = jnp.tile(aq.reshape(db, n_tok, ATT_W), (1, ATT_HEADS, 1))
    sides = []
    for g in range(N_GROUPS):
        cache = caches[g][0]
        wb = cache.shape[1]
        cache_t = cache.transpose(0, 2, 3, 4, 1).reshape(db, 2, ATT_OUT, wb)
        kv_new = kvs[g].reshape(ms, 2, ATT_OUT).transpose(1, 2, 0)
        sides.append((g, n_tok, q_tiled, cache_t, kv_new))
    ride = batch * seq // tm == db and N_GROUPS == 3
    sample_attn = {} if ride else {g: _attn_sample(*sides[g][2:], g, n_tok) for g in range(N_GROUPS)}

    xp = x_prompt.reshape(batch * seq, D_MODEL)
    proj_out = _proj(xp, g_pre, w_qkv, _rot_tables(np.arange(seq)), tm, seq=seq, casts=later_weights,
                     side=sides[0] if ride else None)
    rq, rk, rv = proj_out[:3]
    qkv_res = proj_out[3:3 + 3 * N_GROUPS]
    kv_last = proj_out[3 + 3 * N_GROUPS:3 + 4 * N_GROUPS]
    n_main = 3 + 4 * N_GROUPS + len(later_weights)
    w_br, w_ba, w_o, w_up, w_dn = proj_out[3 + 4 * N_GROUPS:n_main]
    if ride:
        sample_attn[0] = proj_out[n_main:]
    on, p_ret = _ret_prompt(rq, rk, rv, batch, seq)
    outs, lses = [], []
    for g in range(N_GROUPS):
        o_g, l_g = _attn_prompt(*qkv_res[3 * g:3 * g + 3], g, batch, seq)
        outs.append(o_g)
        lses.append(l_g)
    p_kv = [t.reshape(batch, 2, ATT_HEADS, ATT_HEAD_DIM, t.shape[3]).transpose(0, 4, 1, 2, 3)[None] for t in kv_last]
    x1, h2, *rest = _merge(xp, on, outs, lses, g_pre, w_gates, w_br, w_ba, w_o, g_post, g_ffn, tm, permute=True,
                           side=sides[1] if ride else None)
    if ride:
        sample_attn[1] = rest
    y_p, p_conv, *rest = _ffn_prompt(h2, x1, w_up, w_dn, cw, cb, g_post2, batch, seq, tm,
                                     side=sides[2] if ride else None)
    if ride:
        sample_attn[2] = rest

    on_s, s_ret = _ret_sample(rq_s, rk_s, rv_s, state_ret[0], n_tok)
    outs, lses, s_kv = [], [], []
    for g in range(N_GROUPS):
        o_g, l_g, nc = sample_attn[g]
        outs.append(o_g.reshape(ms, ATT_OUT // LANES, LANES).transpose(1, 0, 2))
        lses.append(l_g.reshape(ms, ATT_OUT // LANES, LANES).transpose(1, 0, 2))
        s_kv.append(nc.reshape(db, 2, ATT_HEADS, ATT_HEAD_DIM, nc.shape[3]).transpose(0, 4, 1, 2, 3)[None])
    x1, h2 = _merge(xs, on_s, outs, lses, g_pre, w_gates, w_br, w_ba, w_o, g_post, g_ffn, ms, permute=False)
    st = state_conv[0]
    zeros = jnp.zeros((db, n_tok - 1, 2 * D_FF), F32)
    p1 = jnp.concatenate([st[:, 1:2], zeros], axis=1).reshape(ms, 2 * D_FF)
    p2 = jnp.concatenate([st, zeros[:, :n_tok - 2]], axis=1).reshape(ms, 2 * D_FF)
    y_s, u_s = _ffn_sample(h2, x1, p1, p2, w_up, w_dn, cw, cb, g_post2, n_tok)
    s_conv = u_s.reshape(db, n_tok, 2 * D_FF)[:, n_tok - (CONV_W - 1):][None]

    return (y_p.reshape(x_prompt.shape), y_s.reshape(x_sample.shape), p_kv[0], p_kv[1], p_kv[2], p_ret[None],
            p_conv[None], s_kv[0], s_kv[1], s_kv[2], s_ret[None], s_conv)
```

```python
import functools

import numpy as np
import jax
import jax.numpy as jnp
from jax import lax
from jax.experimental import pallas as pl
from jax.experimental.pallas import tpu as pltpu

F32 = jnp.float32
BF16 = jnp.bfloat16

D_MODEL = 1024
PAST_LEN = 16384
RET_HEADS = 4
RET_QK_DIM = 128
RET_V_DIM = 256
RET_CHUNK = 128
RET_THETA = 10000.0
RET_QK = RET_HEADS * RET_QK_DIM
RET_V = RET_HEADS * RET_V_DIM
ATT_GROUPS = ((128, 1), (512, 4), (2048, 16))
N_GROUPS = 3
ATT_HEADS = 4
ATT_HEAD_DIM = 64
ROPE_DIM = ATT_HEAD_DIM // 4
ROPE_THETA = 500000.0
BAND_BLOCK = 128
ATT_OUT = ATT_HEADS * ATT_HEAD_DIM
ATT_W = N_GROUPS * ATT_OUT
D_FF = 2816
CONV_W = 3
EPS = 1e-6

LANES = 128
SUBLANES = 8
MXU_N = 256
VMEM_LIMIT = 56 * 1024 * 1024
PROMPT_TILE = 512
NEG = -1e30
FF_CHUNK = MXU_N
ATTN_GROUP = 2
RET_PAD = 2 * SUBLANES
QKV_COLS = 2 * RET_QK + RET_V + 3 * ATT_W


def _rms(x, g):
    return x * lax.rsqrt(jnp.mean(x * x, axis=-1, keepdims=True) + EPS) * g


def _dot(a, b):
    return jnp.dot(a, b, preferred_element_type=F32)


def _dot_nt(a, b):
    return lax.dot_general(a, b, (((1,), (1,)), ((), ())), preferred_element_type=F32)


def _resident(shape):
    return pl.BlockSpec(shape, lambda *_: (0,) * len(shape), pipeline_mode=pl.Buffered(1))


def _log2(n):
    assert n > 0 and n & (n - 1) == 0, n
    return n.bit_length() - 1


def _params(n_axes=1):
    return pltpu.CompilerParams(dimension_semantics=("arbitrary",) * n_axes, vmem_limit_bytes=VMEM_LIMIT)


def _proj_kernel(x_ref, g_ref, w_ref, crq_ref, srq_ref, crk_ref, srk_ref, ca_ref, sa1_ref, sa2_ref, *rest,
                 tm, by_residue, tiles_per_seq, n_cast, side=None):
    cast_in, rest = rest[:n_cast], rest[n_cast:]
    if side is not None:
        side_in, rest = rest[:3], rest[3:]
        n_out = 3 + 4 * N_GROUPS + n_cast
        side_weights, side_finish = _attn_sample_phases(*side_in, *rest[n_out:n_out + 3], 0, g=side[0],
                                                        n_tok=side[1], wb=side[2], n_b=1)
        rest = rest[:n_out] + rest[n_out + 3:]
    rq_ref, rk_ref, rv_ref = rest[:3]
    if by_residue:
        res_refs = rest[3:3 + 3 * N_GROUPS]
        kv_refs = rest[3 + 3 * N_GROUPS:3 + 4 * N_GROUPS]
        cast_out = rest[3 + 4 * N_GROUPS:3 + 4 * N_GROUPS + n_cast]
        scr = rest[3 + 4 * N_GROUPS + n_cast]
    else:
        aq_ref = rest[3]
        kv_refs = rest[4:4 + N_GROUPS]
        cast_out = rest[4 + N_GROUPS:4 + N_GROUPS + n_cast]
    for src, dst in zip(cast_in, cast_out):
        dst[...] = src[...].astype(BF16)
    h = _rms(x_ref[...], g_ref[...]).astype(BF16)

    def mm(c0, width=MXU_N):
        return _dot(h, w_ref[:, c0:c0 + width])

    def rot_ret(t, c, s):
        return t * c + pltpu.roll(t, RET_QK_DIM // 2, 1) * s

    ca, sa1, sa2 = ca_ref[...], sa1_ref[...], sa2_ref[...]

    def rot_att(t):
        return t * ca + pltpu.roll(t, ROPE_DIM // 2, 1) * sa1 + pltpu.roll(t, MXU_N - ROPE_DIM // 2, 1) * sa2

    def put_by_residue(ref, val, dil):
        if dil == 1:
            ref[0, 0] = val.astype(BF16)
            return
        for half in range(MXU_N // LANES):
            lanes = slice(half * LANES, (half + 1) * LANES)
            scr[half] = val[:, lanes]
            for r in range(dil):
                ref[0, r, :, lanes] = scr[half, pl.ds(r, tm // dil, stride=dil), :].astype(BF16)

    def put_ret(ref, c, c_ref, s_ref):
        def put(t):
            for half in range(MXU_N // LANES):
                lo = half * LANES
                col = c * MXU_N + lo
                ref[:, col:col + LANES] = rot_ret(t[:, lo:lo + LANES], c_ref[...], s_ref[...]).astype(BF16)
        return put

    def put_rv(c):
        def put(t):
            rv_ref[:, c * MXU_N:(c + 1) * MXU_N] = t.astype(BF16)
        return put

    def put_att(g, which):
        def put(t):
            if which < 2:
                t = rot_att(t)
            if not by_residue:
                if which == 0:
                    aq_ref[:, g * ATT_OUT:(g + 1) * ATT_OUT] = t.astype(BF16)
                else:
                    kv_refs[g][:, (which - 1) * ATT_OUT:which * ATT_OUT] = t
                return
            put_by_residue(res_refs[3 * g + which], t, ATT_GROUPS[g][1])
            if which == 0:
                return
            rows = kv_refs[g].shape[3]
            if ATT_GROUPS[g][0] >= tiles_per_seq * tm:
                kv_refs[g][0, which - 1] = t[tm - rows:tm].T
            else:
                late_windows.append((g, which, t[tm - rows:tm]))
        return put

    late_windows = []

    base = 2 * RET_QK + RET_V
    groups = []
    for c in range(RET_QK // MXU_N):
        groups.append([(c * MXU_N, put_ret(rq_ref, c, crq_ref, srq_ref)),
                       (RET_QK + c * MXU_N, put_ret(rk_ref, c, crk_ref, srk_ref))])
    for c in range(RET_V // MXU_N):
        groups.append([(2 * RET_QK + c * MXU_N, put_rv(c))])
    for g in range(N_GROUPS):
        groups.append([(base + which * ATT_W + g * ATT_OUT, put_att(g, which)) for which in range(3)])
    for group in groups:
        results = [mm(col) for col, _ in group]
        for (_, epilogue), t in zip(group, results):
            epilogue(t)
    if side is not None:
        side_finish(side_weights())
    if late_windows:
        @pl.when(pl.program_id(0) % tiles_per_seq == tiles_per_seq - 1)
        def _():
            for g, which, t in late_windows:
                kv_refs[g][0, which - 1] = t.T


def _slab_rows(rows, n_steps):
    tile = 2 * SUBLANES
    for k in range(-(-rows // (n_steps * tile)), rows // tile + 1):
        if rows % (k * tile) == 0:
            return k * tile
    raise ValueError((rows, n_steps))


def _proj(x2d, g_pre, w_qkv, tabs, tm, seq=None, casts=(), side=None):
    m = x2d.shape[0]
    n_steps = m // tm
    n_tab = tabs[0].shape[0] // tm
    row = lambda i: (i, 0)
    tab = lambda i: (i % n_tab, 0)
    in_specs = [pl.BlockSpec((tm, D_MODEL), row), _resident((1, D_MODEL)), _resident((D_MODEL, QKV_COLS))]
    in_specs += [pl.BlockSpec((tm, LANES), tab)] * 4 + [pl.BlockSpec((tm, MXU_N), tab)] * 3
    out_specs = [pl.BlockSpec((tm, w), row) for w in (RET_QK, RET_QK, RET_V)]
    out_shape = [jax.ShapeDtypeStruct((m, w), BF16) for w in (RET_QK, RET_QK, RET_V)]
    scratch = []
    if seq is None:
        widths = (ATT_W,) + (2 * ATT_OUT,) * N_GROUPS
        out_specs += [pl.BlockSpec((tm, w), row) for w in widths]
        out_shape += [jax.ShapeDtypeStruct((m, w), d) for w, d in zip(widths, (BF16,) + (F32,) * N_GROUPS)]
    else:
        batch, tps = m // seq, seq // tm
        for _, dil in ATT_GROUPS:
            assert tm % (dil * 2 * SUBLANES) == 0
            out_specs += [pl.BlockSpec((1, dil, tm // dil, ATT_OUT), lambda i: (i // tps, 0, i % tps, 0))] * 3
            out_shape += [jax.ShapeDtypeStruct((batch, dil, seq // dil, ATT_OUT), BF16)] * 3
        for win, _ in ATT_GROUPS:
            win = min(win, seq)
            if win == seq:
                out_specs.append(pl.BlockSpec((1, 2, ATT_OUT, tm), lambda i: (i // tps, 0, 0, i % tps)))
            else:
                assert win <= tm and win % LANES == 0
                out_specs.append(pl.BlockSpec((1, 2, ATT_OUT, win), lambda i: (i // tps, 0, 0, 0)))
            out_shape.append(jax.ShapeDtypeStruct((batch, 2, ATT_OUT, win), F32))
        scratch = [pltpu.VMEM((MXU_N // LANES, tm, LANES), F32)]
    for w in casts:
        rows, cols = w.shape
        slab = _slab_rows(rows, n_steps)
        spec = pl.BlockSpec((slab, cols), lambda i, n=rows // slab: (i * n // n_steps, 0))
        in_specs.append(spec)
        out_specs.append(spec)
        out_shape.append(jax.ShapeDtypeStruct(w.shape, BF16))
    operands = (x2d, g_pre, w_qkv, *tabs, *casts)
    side_static = None
    if side is not None:
        assert seq is not None
        s_in, s_ops, s_out, s_shape, side_static = _side_job(side, n_steps)
        in_specs, operands = in_specs + s_in, operands + s_ops
        out_specs, out_shape = out_specs + s_out, out_shape + s_shape
    return pl.pallas_call(
        functools.partial(_proj_kernel, tm=tm, by_residue=seq is not None,
                          tiles_per_seq=None if seq is None else seq // tm, n_cast=len(casts), side=side_static),
        grid=(n_steps,),
        in_specs=in_specs,
        out_specs=out_specs,
        out_shape=out_shape,
        scratch_shapes=scratch,
        compiler_params=_params(),
        name="proj",
    )(*operands)


def _ret_kernel(q_ref, k_ref, v_ref, dec_ref, qd_ref, kd_ref, gc_ref, o_ref, st_ref, s_scr, *, n_chunks):
    s_scr[...] = jnp.zeros_like(s_scr)

    def body(c, carry):
        r0 = pl.multiple_of(c * RET_CHUNK, RET_CHUNK)
        rows = pl.ds(r0, RET_CHUNK)
        first = []
        for h in range(RET_HEADS):
            qk = slice(h * RET_QK_DIM, (h + 1) * RET_QK_DIM)
            vv = slice(h * RET_V_DIM, (h + 1) * RET_V_DIM)
            q = q_ref[rows, qk]
            k = k_ref[rows, qk]
            v = v_ref[rows, vv]
            s0 = s_scr[h]
            sc = _dot_nt(q, k)
            from_state = _dot(q, s0.astype(BF16))
            kd_t = (k.astype(F32) * kd_ref[h]).T.astype(BF16)
            first.append((sc, from_state, _dot(kd_t, v), s0, v))
        for h, (sc, from_state, inc, s0, v) in enumerate(first):
            vv = slice(h * RET_V_DIM, (h + 1) * RET_V_DIM)
            o = _dot((sc * dec_ref[h]).astype(BF16), v) + from_state * qd_ref[h]
            s_scr[h] = s0 * gc_ref[h] + inc
            o_ref[rows, vv] = o * lax.rsqrt(jnp.mean(o * o, axis=-1, keepdims=True) + EPS)
        return carry

    lax.fori_loop(0, n_chunks, body, 0, unroll=8)
    st_ref[0] = s_scr[...]


def _log_gamma():
    return np.log1p(-np.exp2(-5.0 - np.arange(RET_HEADS, dtype=np.float64)))


def _ret_tables(chunk):
    lg = _log_gamma()
    idx = np.arange(RET_CHUNK, dtype=np.float64)
    diff = idx[:, None] - idx[None, :]
    dec = np.where(diff[None] >= 0, np.exp(lg[:, None, None] * np.maximum(diff, 0.0)[None]), 0.0)
    qd = np.exp(lg[:, None] * (idx[None, :] + 1.0))
    kd = np.exp(lg[:, None] * (chunk - 1.0 - idx)[None, :])
    gc = np.exp(lg * chunk)
    qd = np.broadcast_to(qd[:, :, None], (RET_HEADS, RET_CHUNK, RET_V_DIM))
    kd = np.broadcast_to(kd[:, :, None], (RET_HEADS, RET_CHUNK, RET_QK_DIM))
    gc = np.broadcast_to(gc[:, None, None], (RET_HEADS, RET_QK_DIM, RET_V_DIM))
    return [jnp.asarray(t, F32) for t in (dec, qd, kd, gc)]


def _ret_prompt(rq, rk, rv, batch, seq):
    dec, qd, kd, gc = _ret_tables(RET_CHUNK)
    row = lambda b: (b, 0)
    return pl.pallas_call(
        functools.partial(_ret_kernel, n_chunks=seq // RET_CHUNK),
        grid=(batch,),
        in_specs=[pl.BlockSpec((seq, RET_QK), row), pl.BlockSpec((seq, RET_QK), row), pl.BlockSpec((seq, RET_V), row),
                  _resident(dec.shape), _resident(qd.shape), _resident(kd.shape), _resident(gc.shape)],
        out_specs=[pl.BlockSpec((seq, RET_V), row),
                   pl.BlockSpec((1, RET_HEADS, RET_QK_DIM, RET_V_DIM), lambda b: (b, 0, 0, 0))],
        out_shape=[jax.ShapeDtypeStruct((batch * seq, RET_V), F32),
                   jax.ShapeDtypeStruct((batch, RET_HEADS, RET_QK_DIM, RET_V_DIM), F32)],
        scratch_shapes=[pltpu.VMEM((RET_HEADS, RET_QK_DIM, RET_V_DIM), F32)],
        compiler_params=_params(),
        name="ret_prompt",
    )(rq, rk, rv, dec, qd, kd, gc)


def _ret_sample_kernel(q_ref, k_ref, kt_ref, v_ref, s_ref, dec_ref, qd_ref, kd_ref, gc_ref, o_ref, st_ref,
                       *, n_tok, n_b):
    pairs = [(b, h) for b in range(n_b) for h in range(RET_HEADS)]
    qk_of = lambda h: slice(h * RET_QK_DIM, (h + 1) * RET_QK_DIM)
    vv_of = lambda h: slice(h * RET_V_DIM, (h + 1) * RET_V_DIM)
    first = []
    for b, h in pairs:
        q = q_ref[b, :, qk_of(h)]
        k = k_ref[b, :, qk_of(h)]
        v = v_ref[b, :, vv_of(h)]
        s0 = s_ref[b, h]
        from_state = _dot(q.astype(BF16), s0.astype(BF16))
        k_dec_t = (kt_ref[b, h] * kd_ref[h]).astype(BF16)
        st_ref[b, h] = s0 * gc_ref[h] + _dot(k_dec_t, v.astype(BF16))
        scores = [jnp.sum(q * k[i:i + 1, :], axis=-1, keepdims=True) for i in range(n_tok)]
        first.append((from_state, scores))
    second = []
    for (b, h), (from_state, scores) in zip(pairs, first):
        v = v_ref[b, :, vv_of(h)]
        o = from_state * qd_ref[h]
        for i, sc_i in enumerate(scores):
            o = o + (sc_i * dec_ref[h][:, i:i + 1]) * v[i:i + 1, :]
        second.append((o, jnp.mean(o * o, axis=-1, keepdims=True)))
    for (b, h), (o, ms) in zip(pairs, second):
        o_ref[b, :, vv_of(h)] = o * lax.rsqrt(ms + EPS)


def _ret_sample(rq, rk, rv, state, n_tok):
    db = state.shape[0]
    pad = RET_PAD - n_tok
    n_b = 8

    def pad_rows(t):
        t = t.astype(F32).reshape(db, n_tok, t.shape[-1])
        return jnp.pad(t, ((0, 0), (0, pad), (0, 0)))

    q, k, v = pad_rows(rq), pad_rows(rk), pad_rows(rv)
    kt = k.reshape(db, RET_PAD, RET_HEADS, RET_QK_DIM).transpose(0, 2, 3, 1)
    lg = _log_gamma()
    idx = np.arange(RET_PAD, dtype=np.float64)
    diff = idx[:, None] - idx[None, :]
    dec = np.where(diff[None] >= 0, np.exp(lg[:, None, None] * np.maximum(diff, 0.0)[None]), 0.0)
    qd = np.broadcast_to(np.exp(lg[:, None] * (idx[None, :] + 1.0))[:, :, None], (RET_HEADS, RET_PAD, RET_V_DIM))
    kd = np.broadcast_to(np.exp(lg[:, None] * (n_tok - 1.0 - idx)[None, :])[:, None, :],
                         (RET_HEADS, RET_QK_DIM, RET_PAD))
    gc = np.broadcast_to(np.exp(lg * n_tok)[:, None, None], (RET_HEADS, RET_QK_DIM, RET_V_DIM))
    dec, qd, kd, gc = (jnp.asarray(t, F32) for t in (dec, qd, kd, gc))
    b3 = lambda i: (i, 0, 0)
    b4 = lambda i: (i, 0, 0, 0)
    o, st = pl.pallas_call(
        functools.partial(_ret_sample_kernel, n_tok=n_tok, n_b=n_b),
        grid=(db // n_b,),
        in_specs=[pl.BlockSpec((n_b, RET_PAD, RET_QK), b3), pl.BlockSpec((n_b, RET_PAD, RET_QK), b3),
                  pl.BlockSpec((n_b, RET_HEADS, RET_QK_DIM, RET_PAD), b4),
                  pl.BlockSpec((n_b, RET_PAD, RET_V), b3),
                  pl.BlockSpec((n_b, RET_HEADS, RET_QK_DIM, RET_V_DIM), b4),
                  _resident(dec.shape), _resident(qd.shape), _resident(kd.shape), _resident(gc.shape)],
        out_specs=[pl.BlockSpec((n_b, RET_PAD, RET_V), b3),
                   pl.BlockSpec((n_b, RET_HEADS, RET_QK_DIM, RET_V_DIM), b4)],
        out_shape=[jax.ShapeDtypeStruct((db, RET_PAD, RET_V), F32),
                   jax.ShapeDtypeStruct(state.shape, F32)],
        compiler_params=_params(),
        name="ret_sample",
    )(q, k, kt, v, state, dec, qd, kd, gc)
    return o[:, :n_tok].reshape(db * n_tok, RET_V), st


def _attn_kernel(q_ref, k_ref, v_ref, o_ref, l_ref, vt_scr, bias_scr, *staging, dil, nb):
    n_blk = dil * nb
    length = nb * BAND_BLOCK
    blk = BAND_BLOCK
    o_dst, l_dst = staging if dil > 1 else (o_ref, l_ref)

    kk = lax.broadcasted_iota(jnp.int32, (2 * blk, 2 * LANES), 0)
    ql = lax.broadcasted_iota(jnp.int32, (2 * blk, 2 * LANES), 1) & (blk - 1)
    in_span = kk <= ql + blk
    bias_scr[...] = jnp.where(in_span & (kk >= ql), 0.0, NEG)

    def transpose_values(c):
        vt_scr[c] = v_ref[0, c * blk:(c + 1) * blk, :].astype(F32).T.astype(BF16)

    lane = lax.broadcasted_iota(jnp.int32, (blk, LANES), 1)

    def scores(c, p):
        lanes = slice(p * LANES, (p + 1) * LANES)
        k0 = 0 if c % nb == 0 else blk
        q128 = q_ref[0, c * blk:(c + 1) * blk, lanes]
        zero = jnp.zeros_like(q128)
        q_pair = jnp.concatenate([jnp.where(lane < ATT_HEAD_DIM, q128, zero),
                                  jnp.where(lane >= ATT_HEAD_DIM, q128, zero)], axis=0)
        keys = k_ref[0, c * blk - k0:(c + 1) * blk, lanes]
        return _dot_nt(keys, q_pair) + bias_scr[blk - k0:2 * blk, :]

    def weights_values(c, p, s_t):
        lanes = slice(p * LANES, (p + 1) * LANES)
        m = jnp.max(s_t, axis=0, keepdims=True)
        p_t = jnp.exp(s_t - m).astype(BF16)
        v_t = vt_scr[c, lanes, :]
        if c % nb:
            v_t = jnp.concatenate([vt_scr[c - 1, lanes, :], v_t], axis=1)
        v_t = jnp.concatenate([v_t, jnp.ones((2 * SUBLANES, v_t.shape[1]), BF16)], axis=0)
        return _dot(v_t, p_t), m

    def finish(c, p, o_t, m):
        den = o_t[LANES:LANES + 1, :]
        lse = m + jnp.log(den)
        o_pair = jnp.concatenate([o_t[0:ATT_HEAD_DIM, 0:blk] / den[:, 0:blk],
                                  o_t[ATT_HEAD_DIM:LANES, blk:2 * blk] / den[:, blk:2 * blk]], axis=0)
        l_pair = jnp.concatenate([jnp.broadcast_to(lse[:, 0:blk], (ATT_HEAD_DIM, blk)),
                                  jnp.broadcast_to(lse[:, blk:2 * blk], (ATT_HEAD_DIM, blk))], axis=0)
        o_dst[p, c * blk:(c + 1) * blk, :] = o_pair.T
        l_dst[p, c * blk:(c + 1) * blk, :] = l_pair.T

    n_pair = ATT_OUT // LANES
    groups = [[(c, p) for c in range(c0, min(c0 + ATTN_GROUP, n_blk)) for p in range(n_pair)]
              for c0 in range(0, n_blk, ATTN_GROUP)]

    def stage_scores(group):
        for c in sorted({c for c, _ in group}):
            transpose_values(c)
        return [scores(*chain) for chain in group]

    s_next = stage_scores(groups[0])
    pending = []
    for i, group in enumerate(groups):
        s_cur = s_next
        if i + 1 < len(groups):
            s_next = stage_scores(groups[i + 1])
        o_m = [weights_values(*chain, s) for chain, s in zip(group, s_cur)]
        for args in pending:
            finish(*args)
        pending = [chain + om for chain, om in zip(group, o_m)]
    for args in pending:
        finish(*args)

    if dil > 1:
        for p in range(ATT_OUT // LANES):
            for r in range(dil):
                o_ref[p, pl.ds(r, length, stride=dil), :] = o_dst[p, r * length:(r + 1) * length, :]
                l_ref[p, pl.ds(r, length, stride=dil), :] = l_dst[p, r * length:(r + 1) * length, :]


def _attn_prompt(q, k, v, g, batch, seq):
    _, dil = ATT_GROUPS[g]
    nb = seq // dil // BAND_BLOCK
    _log2(nb)
    q, k, v = (t.reshape(batch, seq, ATT_OUT) for t in (q, k, v))
    in_spec = pl.BlockSpec((1, seq, ATT_OUT), lambda b: (b, 0, 0))
    n_pair = ATT_OUT // LANES
    out_spec = pl.BlockSpec((n_pair, seq, LANES), lambda b: (0, b, 0))
    scratch = [pltpu.VMEM((dil * nb, ATT_OUT, BAND_BLOCK), BF16), pltpu.VMEM((2 * BAND_BLOCK, 2 * LANES), F32)]
    if dil > 1:
        scratch += [pltpu.VMEM((n_pair, seq, LANES), F32)] * 2
    return pl.pallas_call(
        functools.partial(_attn_kernel, dil=dil, nb=nb),
        grid=(batch,),
        in_specs=[in_spec] * 3,
        out_specs=[out_spec] * 2,
        out_shape=[jax.ShapeDtypeStruct((n_pair, batch * seq, LANES), F32)] * 2,
        scratch_shapes=scratch,
        compiler_params=_params(),
        name=f"attn_prompt_g{g}",
    )(q, k, v)


def _attn_sample_kernel(q_ref, c_ref, n_ref, o_ref, l_ref, nc_ref, *, g, n_tok, wb, n_b):
    for bb in range(n_b):
        _attn_sample_one(q_ref, c_ref, n_ref, o_ref, l_ref, nc_ref, bb, g=g, n_tok=n_tok, wb=wb, n_b=n_b)


def _attn_sample_one(q_ref, c_ref, n_ref, o_ref, l_ref, nc_ref, bb, **static):
    weights, finish = _attn_sample_phases(q_ref, c_ref, n_ref, o_ref, l_ref, nc_ref, bb, **static)
    finish(weights())


def _attn_sample_phases(q_ref, c_ref, n_ref, o_ref, l_ref, nc_ref, bb, *, g, n_tok, wb, n_b):
    win, dil = ATT_GROUPS[g]
    nk = win // dil
    _log2(dil)
    n_col = ATT_HEADS * n_tok
    first_new = LANES - n_tok
    seq_lane = ((pl.program_id(0) * n_b + bb) * n_tok) % LANES
    new = [pltpu.roll(n_ref[kv], (first_new + LANES - seq_lane) % LANES, 1) for kv in range(2)]
    lane_t = lax.broadcasted_iota(jnp.int32, (ATT_OUT, LANES), 1)
    for kv in range(2):
        shifted = pltpu.roll(c_ref[bb, kv], wb - n_tok, 1)
        if wb > LANES:
            nc_ref[bb, kv, :, 0:wb - LANES] = shifted[:, 0:wb - LANES]
        nc_ref[bb, kv, :, wb - LANES:wb] = jnp.where(lane_t < first_new, shifted[:, wb - LANES:wb], new[kv])

    q = q_ref[bb][:, g * ATT_OUT:(g + 1) * ATT_OUT]
    row = lax.broadcasted_iota(jnp.int32, (n_col, ATT_OUT), 0)
    lane = lax.broadcasted_iota(jnp.int32, (n_col, ATT_OUT), 1)
    head_sel = (lane >> _log2(ATT_HEAD_DIM)) == (row >> _log2(n_tok))
    qbd = jnp.where(head_sel, q, jnp.zeros_like(q))
    k_c, v_c = c_ref[bb, 0].astype(BF16), c_ref[bb, 1].astype(BF16)
    k_n, v_n = new[0].astype(BF16), new[1].astype(BF16)

    def weights():
        s_c = _dot(qbd, k_c)
        s_n = _dot(qbd, k_n)
        t_c = lax.broadcasted_iota(jnp.int32, (n_col, wb), 0) & (n_tok - 1)
        d_c = wb + t_c - lax.broadcasted_iota(jnp.int32, (n_col, wb), 1)
        ok_c = ((d_c & (dil - 1)) == 0) & (d_c <= nk * dil)
        t_n = lax.broadcasted_iota(jnp.int32, (n_col, LANES), 0) & (n_tok - 1)
        new_idx = lax.broadcasted_iota(jnp.int32, (n_col, LANES), 1) - first_new
        d_n = t_n - new_idx
        ok_n = (new_idx >= 0) & (d_n >= 0) & ((d_n & (dil - 1)) == 0) & (d_n <= nk * dil)
        s_c = jnp.where(ok_c, s_c, NEG)
        s_n = jnp.where(ok_n, s_n, NEG)
        m = jnp.maximum(jnp.max(s_c, axis=-1, keepdims=True), jnp.max(s_n, axis=-1, keepdims=True))
        p_c = jnp.exp(s_c - m)
        p_n = jnp.exp(s_n - m)
        den = jnp.sum(p_c, axis=-1, keepdims=True) + jnp.sum(p_n, axis=-1, keepdims=True)
        return p_c.astype(BF16), p_n.astype(BF16), m, den

    def finish(state):
        p_c, p_n, m, den = state
        o = (_dot_nt(p_c, v_c) + _dot_nt(p_n, v_n)) / den
        lse = jnp.broadcast_to(m + jnp.log(den), (n_col, ATT_OUT))
        o = jnp.where(head_sel, o, 0.0)
        lse = jnp.where(head_sel, lse, 0.0)
        o_t = o[0:n_tok]
        l_t = lse[0:n_tok]
        for h in range(1, ATT_HEADS):
            o_t = o_t + o[h * n_tok:(h + 1) * n_tok]
            l_t = l_t + lse[h * n_tok:(h + 1) * n_tok]
        o_ref[bb] = o_t
        l_ref[bb] = l_t

    return weights, finish


def _attn_sample(q_tiled, cache, kv_new, g, n_tok):
    db, _, _, wb = cache.shape
    n_b = max(1, min(SUBLANES, ATT_GROUPS[-1][0] // wb))
    assert wb % LANES == 0 and db % n_b == 0 and LANES % (n_b * n_tok) == 0 and kv_new.shape[2] % LANES == 0
    b3 = lambda i: (i, 0, 0)
    b4 = lambda i: (i, 0, 0, 0)
    return pl.pallas_call(
        functools.partial(_attn_sample_kernel, g=g, n_tok=n_tok, wb=wb, n_b=n_b),
        grid=(db // n_b,),
        in_specs=[pl.BlockSpec((n_b, ATT_HEADS * n_tok, ATT_W), b3),
                  pl.BlockSpec((n_b, 2, ATT_OUT, wb), b4),
                  pl.BlockSpec((2, ATT_OUT, LANES), lambda i: (0, 0, i * n_b * n_tok // LANES))],
        out_specs=[pl.BlockSpec((n_b, n_tok, ATT_OUT), b3), pl.BlockSpec((n_b, n_tok, ATT_OUT), b3),
                   pl.BlockSpec((n_b, 2, ATT_OUT, wb), b4)],
        out_shape=[jax.ShapeDtypeStruct((db, n_tok, ATT_OUT), F32), jax.ShapeDtypeStruct((db, n_tok, ATT_OUT), F32),
                   jax.ShapeDtypeStruct(cache.shape, F32)],
        compiler_params=_params(),
        name=f"attn_sample_g{g}",
    )(q_tiled, cache, kv_new)


def _merge_kernel(*refs, tm, n_split, permute, side=None):
    if side is not None:
        side_in, refs = refs[:3], refs[3:]
    (x_ref, on_ref, o0_ref, o1_ref, o2_ref, l0_ref, l1_ref, l2_ref, gpre_ref, wg_ref, wbr_ref, wba_ref, wout_ref,
     gpost_ref, gffn_ref, x1_ref, h2_ref), perm = refs[:17], refs[17:]
    if side is not None:
        side_refs, perm = side_in + perm[:3], perm[3:]
    bounds = [i * tm // n_split for i in range(n_split + 1)]
    rows_of = [slice(a, b) for a, b in zip(bounds[:-1], bounds[1:])]
    n_pair = ATT_OUT // LANES

    def gates(rows):
        x = x_ref[rows, :]
        h = _rms(x, gpre_ref[...]).astype(BF16)
        return [_dot(h, wg_ref[:, c:c + D_MODEL]) for c in (0, RET_V, RET_V + D_MODEL)]

    def branches(rows, rg):
        r = (on_ref[rows, :] * (rg * jax.nn.sigmoid(rg))).astype(BF16)
        atts = []
        for p in range(n_pair):
            l0, l1, l2 = l0_ref[p, rows, :], l1_ref[p, rows, :], l2_ref[p, rows, :]
            mx = jnp.maximum(jnp.maximum(l0, l1), l2)
            e0, e1, e2 = jnp.exp(l0 - mx), jnp.exp(l1 - mx), jnp.exp(l2 - mx)
            att = (e0 * o0_ref[p, rows, :] + e1 * o1_ref[p, rows, :] + e2 * o2_ref[p, rows, :]) / (e0 + e1 + e2)
            atts.append(att.astype(BF16))
        return _dot(r, wbr_ref[...]), _dot(jnp.concatenate(atts, axis=1), wba_ref[...])

    def mixed(gr, ga, br, ba):
        mix = jax.nn.sigmoid(gr) * br + jax.nn.sigmoid(ga) * ba
        return _dot(mix.astype(BF16), wout_ref[...])

    def finish(rows, mo):
        x1 = x_ref[rows, :] + _rms(mo, gpost_ref[...])
        x1_ref[rows, :] = x1
        return _rms(x1, gffn_ref[...])

    g = []
    for i, rows in enumerate(rows_of):
        g.append(gates(rows))
        if i == 0 and side is not None:
            side_weights, side_finish = _attn_sample_phases(*side_refs, 0, g=side[0], n_tok=side[1], wb=side[2],
                                                            n_b=1)
    b = [branches(rows, gi[0]) for rows, gi in zip(rows_of, g)]
    if side is not None:
        side_state = side_weights()
    mo = [mixed(gi[1], gi[2], *bi) for gi, bi in zip(g, b)]
    if side is not None:
        side_finish(side_state)
    h2 = [finish(rows, moi) for rows, moi in zip(rows_of, mo)]
    if not permute:
        for rows, h2i in zip(rows_of, h2):
            h2_ref[rows, :] = h2i.astype(BF16)
        return
    perm_ref, = perm
    nv = tm // SUBLANES
    for lc in range(D_MODEL // LANES):
        lanes = slice(lc * LANES, (lc + 1) * LANES)
        for s in range(SUBLANES):
            tok0 = s * nv
            i = max(j for j, b in enumerate(bounds[:-1]) if b <= tok0)
            off = tok0 - bounds[i]
            assert tok0 + nv <= bounds[i + 1]
            perm_ref[lc, pl.ds(s, nv, stride=SUBLANES), :] = h2[i][off:off + nv, lanes]
        h2_ref[:, lanes] = perm_ref[lc].astype(BF16)


def _side_job(side, n_steps):
    g, n_tok, q_tiled, cache, kv_new = side
    db, _, _, wb = cache.shape
    assert db == n_steps and LANES % n_tok == 0, (db, n_steps)
    b3 = lambda i: (i, 0, 0)
    b4 = lambda i: (i, 0, 0, 0)
    in_specs = [pl.BlockSpec((1, ATT_HEADS * n_tok, ATT_W), b3), pl.BlockSpec((1, 2, ATT_OUT, wb), b4),
                pl.BlockSpec((2, ATT_OUT, LANES), lambda i: (0, 0, i * n_tok // LANES))]
    out_specs = [pl.BlockSpec((1, n_tok, ATT_OUT), b3), pl.BlockSpec((1, n_tok, ATT_OUT), b3),
                 pl.BlockSpec((1, 2, ATT_OUT, wb), b4)]
    out_shape = [jax.ShapeDtypeStruct((db, n_tok, ATT_OUT), F32)] * 2 + [jax.ShapeDtypeStruct(cache.shape, F32)]
    return in_specs, (q_tiled, cache, kv_new), out_specs, out_shape, (g, n_tok, wb)


def _merge(x2d, on, outs, lses, g_pre, w_gates, w_br, w_ba, w_out, g_post, g_ffn, tm, permute, side=None):
    m = x2d.shape[0]
    n_split = 2 if tm >= 2 * MXU_N else 1
    scratch = [pltpu.VMEM((D_MODEL // LANES, tm, LANES), F32)] if permute else []
    row = lambda i: (i, 0)
    vec = _resident((1, D_MODEL))
    att_spec = pl.BlockSpec((ATT_OUT // LANES, tm, LANES), lambda i: (0, i, 0))
    in_specs = [pl.BlockSpec((tm, D_MODEL), row), pl.BlockSpec((tm, RET_V), row)] + [att_spec] * 6 + [
        vec, _resident(w_gates.shape), _resident(w_br.shape), _resident(w_ba.shape), _resident(w_out.shape),
        vec, vec]
    out_specs = [pl.BlockSpec((tm, D_MODEL), row), pl.BlockSpec((tm, D_MODEL), row)]
    out_shape = [jax.ShapeDtypeStruct((m, D_MODEL), F32), jax.ShapeDtypeStruct((m, D_MODEL), BF16)]
    operands = (x2d, on, *outs, *lses, g_pre, w_gates, w_br, w_ba, w_out, g_post, g_ffn)
    side_static = None
    if side is not None:
        s_in, s_ops, s_out, s_shape, side_static = _side_job(side, m // tm)
        in_specs, operands = s_in + in_specs, s_ops + operands
        out_specs, out_shape = out_specs + s_out, out_shape + s_shape
    return pl.pallas_call(
        functools.partial(_merge_kernel, tm=tm, n_split=n_split, permute=permute, side=side_static),
        grid=(m // tm,),
        in_specs=in_specs,
        out_specs=out_specs,
        out_shape=out_shape,
        scratch_shapes=scratch,
        compiler_params=_params(),
        name="merge",
    )(*operands)


def _ffn_kernel(*refs, tm, tiles_per_seq, seq_tok, side=None):
    if side is not None:
        (sq_ref, sc_ref, sn_ref), refs = refs[:3], refs[3:]
        so_ref, sl_ref, snc_ref = refs[9:12]
        refs = refs[:9] + refs[12:]
        side_refs = (sq_ref, sc_ref, sn_ref, so_ref, sl_ref, snc_ref)
    if seq_tok is None:
        h2_ref, x1_ref, wup_ref, wdn_ref, cw_ref, cb_ref, g_ref, y_ref, cs_ref, carry_ref, f_ref, perm_ref = refs
    else:
        h2_ref, x1_ref, p1_ref, p2_ref, wup_ref, wdn_ref, cw_ref, cb_ref, g_ref, y_ref, u_ref, f_ref = refs
    h2 = h2_ref[...]
    if seq_tok is None:
        sub = lax.broadcasted_iota(jnp.int32, (SUBLANES, FF_CHUNK), 0)

        @pl.when(pl.program_id(0) % tiles_per_seq == 0)
        def _():
            carry_ref[...] = jnp.zeros_like(carry_ref)
    else:
        t = lax.broadcasted_iota(jnp.int32, (tm, FF_CHUNK), 0) & (seq_tok - 1)
        _log2(seq_tok)

    def up(j):
        return [_dot(h2, wup_ref[:, c:c + FF_CHUNK]) for c in (j * FF_CHUNK, D_FF + j * FF_CHUNK)]

    def conv(u, col):
        cols = slice(col, col + FF_CHUNK)
        if seq_tok is None:
            prev = carry_ref[:, cols]
            last2 = jnp.where(sub == 0, prev[SUBLANES - 1:SUBLANES], pltpu.roll(u[tm - 2 * SUBLANES:tm - SUBLANES], 1, 0))
            last1 = jnp.where(sub == 0, prev[2 * SUBLANES - 1:2 * SUBLANES], pltpu.roll(u[tm - SUBLANES:tm], 1, 0))
            u1 = jnp.concatenate([last1, u[0:tm - SUBLANES]], axis=0)
            u2 = jnp.concatenate([last2, last1, u[0:tm - 2 * SUBLANES]], axis=0)
            carry_ref[:, cols] = u[tm - 2 * SUBLANES:tm]
        else:
            u1 = jnp.where(t == 0, p1_ref[:, cols], pltpu.roll(u, 1, 0))
            u2 = jnp.where(t <= 1, p2_ref[:, cols], pltpu.roll(u, 2, 0))
            u_ref[:, cols] = u
        cw = cw_ref[:, cols]
        return cb_ref[:, cols] + cw[0:1] * u2 + cw[1:2] * u1 + cw[2:3] * u

    def gate(j, ug, uv):
        cg = conv(ug, j * FF_CHUNK)
        cv = conv(uv, D_FF + j * FF_CHUNK)
        f_ref[:, j * FF_CHUNK:(j + 1) * FF_CHUNK] = (jax.nn.gelu(cg, approximate=True) * cv).astype(BF16)

    n_chunks = D_FF // FF_CHUNK
    u_next = up(0)
    for j in range(n_chunks):
        u_cur = u_next
        if j + 1 < n_chunks:
            u_next = up(j + 1)
        if j == 0 and side is not None:
            side_weights, side_finish = _attn_sample_phases(*side_refs, 0, g=side[0], n_tok=side[1], wb=side[2],
                                                            n_b=1)
        gate(j, *u_cur)
    if side is not None:
        side_state = side_weights()
    down = _dot(f_ref[...], wdn_ref[...])
    if side is not None:
        side_finish(side_state)
    out = _rms(down, g_ref[...])
    if seq_tok is None:
        nv = tm // SUBLANES
        for lc in range(D_MODEL // LANES):
            lanes = slice(lc * LANES, (lc + 1) * LANES)
            perm_ref[lc] = out[:, lanes]
            for s in range(SUBLANES):
                rows = slice(s * nv, (s + 1) * nv)
                y_ref[rows, lanes] = x1_ref[rows, lanes] + perm_ref[lc, pl.ds(s, nv, stride=SUBLANES), :]

        @pl.when(pl.program_id(0) % tiles_per_seq == tiles_per_seq - 1)
        def _():
            cs_ref[0, 0:1, :] = carry_ref[SUBLANES - 1:SUBLANES, :]
            cs_ref[0, 1:2, :] = carry_ref[2 * SUBLANES - 1:2 * SUBLANES, :]
    else:
        y_ref[...] = x1_ref[...] + out


def _ffn_prompt(h2, x1, w_up, w_dn, conv_w, conv_b, g_post, batch, seq, tm, side=None):
    m = h2.shape[0]
    tps = seq // tm
    n_steps = m // tm
    row = lambda i: (i, 0)
    in_specs = [pl.BlockSpec((tm, D_MODEL), row), pl.BlockSpec((tm, D_MODEL), row),
                _resident(w_up.shape), _resident(w_dn.shape), _resident(conv_w.shape), _resident(conv_b.shape),
                _resident((1, D_MODEL))]
    out_specs = [pl.BlockSpec((tm, D_MODEL), row),
                 pl.BlockSpec((1, CONV_W - 1, 2 * D_FF), lambda i: (i // tps, 0, 0))]
    out_shape = [jax.ShapeDtypeStruct((m, D_MODEL), F32),
                 jax.ShapeDtypeStruct((batch, CONV_W - 1, 2 * D_FF), F32)]
    operands = (h2, x1, w_up, w_dn, conv_w, conv_b, g_post)
    side_static = None
    if side is not None:
        s_in, s_ops, s_out, s_shape, side_static = _side_job(side, n_steps)
        in_specs, operands = s_in + in_specs, s_ops + operands
        out_specs, out_shape = out_specs + s_out, out_shape + s_shape
    return pl.pallas_call(
        functools.partial(_ffn_kernel, tm=tm, tiles_per_seq=tps, seq_tok=None, side=side_static),
        grid=(n_steps,),
        in_specs=in_specs,
        out_specs=out_specs,
        out_shape=out_shape,
        scratch_shapes=[pltpu.VMEM((2 * SUBLANES, 2 * D_FF), F32), pltpu.VMEM((tm, D_FF), BF16),
                        pltpu.VMEM((D_MODEL // LANES, tm, LANES), F32)],
        compiler_params=_params(),
        name="ffn_prompt",
    )(*operands)


def _ffn_sample(h2, x1, p1, p2, w_up, w_dn, conv_w, conv_b, g_post, n_tok):
    m = h2.shape[0]
    const = lambda i: (0, 0)
    full = lambda a: _resident(a.shape)
    return pl.pallas_call(
        functools.partial(_ffn_kernel, tm=m, tiles_per_seq=1, seq_tok=n_tok),
        grid=(1,),
        in_specs=[full(h2), full(x1), full(p1), full(p2), full(w_up), full(w_dn), full(conv_w), full(conv_b),
                  _resident((1, D_MODEL))],
        out_specs=[pl.BlockSpec((m, D_MODEL), const), pl.BlockSpec((m, 2 * D_FF), const)],
        out_shape=[jax.ShapeDtypeStruct((m, D_MODEL), F32), jax.ShapeDtypeStruct((m, 2 * D_FF), F32)],
        scratch_shapes=[pltpu.VMEM((m, D_FF), BF16)],
        compiler_params=_params(),
        name="ffn_sample",
    )(h2, x1, p1, p2, w_up, w_dn, conv_w, conv_b, g_post)


def _rot_tables(pos):
    pos = np.asarray(pos, np.float64)
    rf = RET_THETA ** (-np.linspace(0.0, 1.0, RET_QK_DIM // 2))
    ang = pos[:, None] * rf[None, :]
    cos, sin = np.cos(ang), np.sin(ang)
    cr = np.concatenate([cos, cos], axis=-1)
    sr = np.concatenate([-sin, sin], axis=-1)
    k_scale = RET_QK_DIM ** -0.5
    half = ROPE_DIM // 2
    af = ROPE_THETA ** (-np.arange(half) / half)
    ang = pos[:, None] * af[None, :]
    cos, sin = np.cos(ang), np.sin(ang)
    n = pos.shape[0]
    rest = ATT_HEAD_DIM - ROPE_DIM
    zh = np.zeros((n, half))
    ca = np.concatenate([cos, cos, np.ones((n, rest))], axis=-1)
    sa1 = np.concatenate([zh, sin, np.zeros((n, rest))], axis=-1)
    sa2 = np.concatenate([-sin, zh, np.zeros((n, rest))], axis=-1)
    rep = MXU_N // ATT_HEAD_DIM
    ca, sa1, sa2 = (np.tile(t, (1, rep)) for t in (ca, sa1, sa2))
    return [jnp.asarray(t, F32) for t in (cr, sr, cr * k_scale, sr * k_scale, ca, sa1, sa2)]


def kernel(x_prompt, x_sample, cache_kv_g0, cache_kv_g1, cache_kv_g2, state_ret, state_conv, norm_mix_pre, w_in,
           w_branch_ret, w_branch_attn, w_out, norm_mix_post, norm_ffn_pre, w_ffn_up, conv_w, conv_b, w_ffn_down,
           norm_ffn_post):
    batch, seq, _ = x_prompt.shape
    db, n_tok, _ = x_sample.shape
    depth = w_in.shape[0]
    assert depth == 1 and seq % (ATT_GROUPS[-1][1] * BAND_BLOCK) == 0 and n_tok <= SUBLANES
    caches = (cache_kv_g0, cache_kv_g1, cache_kv_g2)

    wi = w_in[0]
    o_rg = 2 * RET_QK + RET_V
    o_aq = o_rg + RET_V
    o_gr = o_aq + 3 * ATT_W
    w_qkv = jnp.concatenate([wi[:, :o_rg], wi[:, o_aq:o_aq + ATT_W] * ATT_HEAD_DIM ** -0.5,
                             wi[:, o_aq + ATT_W:o_gr]], axis=1).astype(BF16)
    w_gates = jnp.concatenate([wi[:, o_rg:o_aq], wi[:, o_gr:]], axis=1).astype(BF16)
    later_weights = (w_branch_ret[0], w_branch_attn[0], w_out[0], w_ffn_up[0], w_ffn_down[0])
    cb = conv_b[0][None, :]
    cw = conv_w[0]
    g_pre, g_post, g_ffn, g_post2 = (t[0][None, :] for t in (norm_mix_pre, norm_mix_post, norm_ffn_pre, norm_ffn_post))

    tm = PROMPT_TILE
    ms = db * n_tok
    xs = x_sample.reshape(ms, D_MODEL)
    pos_s = np.tile(PAST_LEN + np.arange(n_tok), db)
    rq_s, rk_s, rv_s, aq, kv0, kv1, kv2 = _proj(xs, g_pre, w_qkv, _rot_tables(pos_s), ms)
    kvs = (kv0, kv1, kv2)
    q_tiled = jnp.tile(aq.reshape(db, n_tok, ATT_W), (1, ATT_HEADS, 1))
    sides = []
    for g in range(N_GROUPS):
        cache = caches[g][0]
        wb = cache.shape[1]
        cache_t = cache.transpose(0, 2, 3, 4, 1).reshape(db, 2, ATT_OUT, wb)
        kv_new = kvs[g].reshape(ms, 2, ATT_OUT).transpose(1, 2, 0)
        sides.append((g, n_tok, q_tiled, cache_t, kv_new))
    ride = batch * seq // tm == db and N_GROUPS == 3
    sample_attn = {} if ride else {g: _attn_sample(*sides[g][2:], g, n_tok) for g in range(N_GROUPS)}

    xp = x_prompt.reshape(batch * seq, D_MODEL)
    proj_out = _proj(xp, g_pre, w_qkv, _rot_tables(np.arange(seq)), tm, seq=seq, casts=later_weights,
                     side=sides[0] if ride else None)
    rq, rk, rv = proj_out[:3]
    qkv_res = proj_out[3:3 + 3 * N_GROUPS]
    kv_last = proj_out[3 + 3 * N_GROUPS:3 + 4 * N_GROUPS]
    n_main = 3 + 4 * N_GROUPS + len(later_weights)
    w_br, w_ba, w_o, w_up, w_dn = proj_out[3 + 4 * N_GROUPS:n_main]
    if ride:
        sample_attn[0] = proj_out[n_main:]
    on, p_ret = _ret_prompt(rq, rk, rv, batch, seq)
    outs, lses = [], []
    for g in range(N_GROUPS):
        o_g, l_g = _attn_prompt(*qkv_res[3 * g:3 * g + 3], g, batch, seq)
        outs.append(o_g)
        lses.append(l_g)
    p_kv = [t.reshape(batch, 2, ATT_HEADS, ATT_HEAD_DIM, t.shape[3]).transpose(0, 4, 1, 2, 3)[None] for t in kv_last]
    x1, h2, *rest = _merge(xp, on, outs, lses, g_pre, w_gates, w_br, w_ba, w_o, g_post, g_ffn, tm, permute=True,
                           side=sides[1] if ride else None)
    if ride:
        sample_attn[1] = rest
    y_p, p_conv, *rest = _ffn_prompt(h2, x1, w_up, w_dn, cw, cb, g_post2, batch, seq, tm,
                                     side=sides[2] if ride else None)
    if ride:
        sample_attn[2] = rest

    on_s, s_ret = _ret_sample(rq_s, rk_s, rv_s, state_ret[0], n_tok)
    outs, lses, s_kv = [], [], []
    for g in range(N_GROUPS):
        o_g, l_g, nc = sample_attn[g]
        outs.append(o_g.reshape(ms, ATT_OUT // LANES, LANES).transpose(1, 0, 2))
        lses.append(l_g.reshape(ms, ATT_OUT // LANES, LANES).transpose(1, 0, 2))
        s_kv.append(nc.reshape(db, 2, ATT_HEADS, ATT_HEAD_DIM, nc.shape[3]).transpose(0, 4, 1, 2, 3)[None])
    x1, h2 = _merge(xs, on_s, outs, lses, g_pre, w_gates, w_br, w_ba, w_o, g_post, g_ffn, ms, permute=False)
    st = state_conv[0]
    zeros = jnp.zeros((db, n_tok - 1, 2 * D_FF), F32)
    p1 = jnp.concatenate([st[:, 1:2], zeros], axis=1).reshape(ms, 2 * D_FF)
    p2 = jnp.concatenate([st, zeros[:, :n_tok - 2]], axis=1).reshape(ms, 2 * D_FF)
    y_s, u_s = _ffn_sample(h2, x1, p1, p2, w_up, w_dn, cw, cb, g_post2, n_tok)
    s_conv = u_s.reshape(db, n_tok, 2 * D_FF)[:, n_tok - (CONV_W - 1):][None]

    return (y_p.reshape(x_prompt.shape), y_s.reshape(x_sample.shape), p_kv[0], p_kv[1], p_kv[2], p_ret[None],
            p_conv[None], s_kv[0], s_kv[1], s_kv[2], s_ret[None], s_conv)
```

```python
import functools

import numpy as np
import jax
import jax.numpy as jnp
from jax import lax
from jax.experimental import pallas as pl
from jax.experimental.pallas import tpu as pltpu

F32 = jnp.float32
BF16 = jnp.bfloat16

D_MODEL = 1024
PAST_LEN = 16384
RET_HEADS = 4
RET_QK_DIM = 128
RET_V_DIM = 256
RET_CHUNK = 128
RET_THETA = 10000.0
RET_QK = RET_HEADS * RET_QK_DIM
RET_V = RET_HEADS * RET_V_DIM
ATT_GROUPS = ((128, 1), (512, 4), (2048, 16))
N_GROUPS = 3
ATT_HEADS = 4
ATT_HEAD_DIM = 64
ROPE_DIM = ATT_HEAD_DIM // 4
ROPE_THETA = 500000.0
BAND_BLOCK = 128
ATT_OUT = ATT_HEADS * ATT_HEAD_DIM
ATT_W = N_GROUPS * ATT_OUT
D_FF = 2816
CONV_W = 3
EPS = 1e-6

LANES = 128
SUBLANES = 8
MXU_N = 256
VMEM_LIMIT = 56 * 1024 * 1024
PROMPT_TILE = 512
NEG = -1e30
FF_CHUNK = MXU_N
ATTN_GROUP = 2
RET_PAD = 2 * SUBLANES
QKV_COLS = 2 * RET_QK + RET_V + 3 * ATT_W


def _rms(x, g):
    return x * lax.rsqrt(jnp.mean(x * x, axis=-1, keepdims=True) + EPS) * g


def _dot(a, b):
    return jnp.dot(a, b, preferred_element_type=F32)


def _dot_nt(a, b):
    return lax.dot_general(a, b, (((1,), (1,)), ((), ())), preferred_element_type=F32)


def _resident(shape):
    return pl.BlockSpec(shape, lambda *_: (0,) * len(shape), pipeline_mode=pl.Buffered(1))


def _log2(n):
    assert n > 0 and n & (n - 1) == 0, n
    return n.bit_length() - 1


def _params(n_axes=1):
    return pltpu.CompilerParams(dimension_semantics=("arbitrary",) * n_axes, vmem_limit_bytes=VMEM_LIMIT)


def _proj_kernel(x_ref, g_ref, w_ref, crq_ref, srq_ref, crk_ref, srk_ref, ca_ref, sa1_ref, sa2_ref, *rest,
                 tm, by_residue, tiles_per_seq, n_cast, side=None):
    cast_in, rest = rest[:n_cast], rest[n_cast:]
    if side is not None:
        side_in, rest = rest[:3], rest[3:]
        n_out = 3 + 4 * N_GROUPS + n_cast
        side_weights, side_finish = _attn_sample_phases(*side_in, *rest[n_out:n_out + 3], 0, g=side[0],
                                                        n_tok=side[1], wb=side[2], n_b=1)
        rest = rest[:n_out] + rest[n_out + 3:]
    rq_ref, rk_ref, rv_ref = rest[:3]
    if by_residue:
        res_refs = rest[3:3 + 3 * N_GROUPS]
        kv_refs = rest[3 + 3 * N_GROUPS:3 + 4 * N_GROUPS]
        cast_out = rest[3 + 4 * N_GROUPS:3 + 4 * N_GROUPS + n_cast]
        scr = rest[3 + 4 * N_GROUPS + n_cast]
    else:
        aq_ref = rest[3]
        kv_refs = rest[4:4 + N_GROUPS]
        cast_out = rest[4 + N_GROUPS:4 + N_GROUPS + n_cast]
    for src, dst in zip(cast_in, cast_out):
        dst[...] = src[...].astype(BF16)
    h = _rms(x_ref[...], g_ref[...]).astype(BF16)

    def mm(c0, width=MXU_N):
        return _dot(h, w_ref[:, c0:c0 + width])

    def rot_ret(t, c, s):
        return t * c + pltpu.roll(t, RET_QK_DIM // 2, 1) * s

    ca, sa1, sa2 = ca_ref[...], sa1_ref[...], sa2_ref[...]

    def rot_att(t):
        return t * ca + pltpu.roll(t, ROPE_DIM // 2, 1) * sa1 + pltpu.roll(t, MXU_N - ROPE_DIM // 2, 1) * sa2

    def put_by_residue(ref, val, dil):
        if dil == 1:
            ref[0, 0] = val.astype(BF16)
            return
        for half in range(MXU_N // LANES):
            lanes = slice(half * LANES, (half + 1) * LANES)
            scr[half] = val[:, lanes]
            for r in range(dil):
                ref[0, r, :, lanes] = scr[half, pl.ds(r, tm // dil, stride=dil), :].astype(BF16)

    def put_ret(ref, c, c_ref, s_ref):
        def put(t):
            for half in range(MXU_N // LANES):
                lo = half * LANES
                col = c * MXU_N + lo
                ref[:, col:col + LANES] = rot_ret(t[:, lo:lo + LANES], c_ref[...], s_ref[...]).astype(BF16)
        return put

    def put_rv(c):
        def put(t):
            rv_ref[:, c * MXU_N:(c + 1) * MXU_N] = t.astype(BF16)
        return put

    def put_att(g, which):
        def put(t):
            if which < 2:
                t = rot_att(t)
            if not by_residue:
                if which == 0:
                    aq_ref[:, g * ATT_OUT:(g + 1) * ATT_OUT] = t.astype(BF16)
                else:
                    kv_refs[g][:, (which - 1) * ATT_OUT:which * ATT_OUT] = t
                return
            put_by_residue(res_refs[3 * g + which], t, ATT_GROUPS[g][1])
            if which == 0:
                return
            rows = kv_refs[g].shape[3]
            if ATT_GROUPS[g][0] >= tiles_per_seq * tm:
                kv_refs[g][0, which - 1] = t[tm - rows:tm].T
            else:
                late_windows.append((g, which, t[tm - rows:tm]))
        return put

    late_windows = []

    base = 2 * RET_QK + RET_V
    groups = []
    for g in reversed(range(N_GROUPS)):
        groups.append([(base + which * ATT_W + g * ATT_OUT, put_att(g, which)) for which in range(3)])
    for c in range(RET_QK // MXU_N):
        groups.append([(c * MXU_N, put_ret(rq_ref, c, crq_ref, srq_ref)),
                       (RET_QK + c * MXU_N, put_ret(rk_ref, c, crk_ref, srk_ref))])
    for c in range(RET_V // MXU_N):
        groups.append([(2 * RET_QK + c * MXU_N, put_rv(c))])
    for group in groups:
        results = [mm(col) for col, _ in group]
        for (_, epilogue), t in zip(group, results):
            epilogue(t)
    if side is not None:
        side_finish(side_weights())
    if late_windows:
        @pl.when(pl.program_id(0) % tiles_per_seq == tiles_per_seq - 1)
        def _():
            for g, which, t in late_windows:
                kv_refs[g][0, which - 1] = t.T


def _slab_rows(rows, n_steps):
    tile = 2 * SUBLANES
    for k in range(-(-rows // (n_steps * tile)), rows // tile + 1):
        if rows % (k * tile) == 0:
            return k * tile
    raise ValueError((rows, n_steps))


def _proj(x2d, g_pre, w_qkv, tabs, tm, seq=None, casts=(), side=None):
    m = x2d.shape[0]
    n_steps = m // tm
    n_tab = tabs[0].shape[0] // tm
    row = lambda i: (i, 0)
    tab = lambda i: (i % n_tab, 0)
    in_specs = [pl.BlockSpec((tm, D_MODEL), row), _resident((1, D_MODEL)), _resident((D_MODEL, QKV_COLS))]
    in_specs += [pl.BlockSpec((tm, LANES), tab)] * 4 + [pl.BlockSpec((tm, MXU_N), tab)] * 3
    out_specs = [pl.BlockSpec((tm, w), row) for w in (RET_QK, RET_QK, RET_V)]
    out_shape = [jax.ShapeDtypeStruct((m, w), BF16) for w in (RET_QK, RET_QK, RET_V)]
    scratch = []
    if seq is None:
        widths = (ATT_W,) + (2 * ATT_OUT,) * N_GROUPS
        out_specs += [pl.BlockSpec((tm, w), row) for w in widths]
        out_shape += [jax.ShapeDtypeStruct((m, w), d) for w, d in zip(widths, (BF16,) + (F32,) * N_GROUPS)]
    else:
        batch, tps = m // seq, seq // tm
        for _, dil in ATT_GROUPS:
            assert tm % (dil * 2 * SUBLANES) == 0
            out_specs += [pl.BlockSpec((1, dil, tm // dil, ATT_OUT), lambda i: (i // tps, 0, i % tps, 0))] * 3
            out_shape += [jax.ShapeDtypeStruct((batch, dil, seq // dil, ATT_OUT), BF16)] * 3
        for win, _ in ATT_GROUPS:
            win = min(win, seq)
            if win == seq:
                out_specs.append(pl.BlockSpec((1, 2, ATT_OUT, tm), lambda i: (i // tps, 0, 0, i % tps)))
            else:
                assert win <= tm and win % LANES == 0
                out_specs.append(pl.BlockSpec((1, 2, ATT_OUT, win), lambda i: (i // tps, 0, 0, 0)))
            out_shape.append(jax.ShapeDtypeStruct((batch, 2, ATT_OUT, win), F32))
        scratch = [pltpu.VMEM((MXU_N // LANES, tm, LANES), F32)]
    for w in casts:
        rows, cols = w.shape
        slab = _slab_rows(rows, n_steps)
        spec = pl.BlockSpec((slab, cols), lambda i, n=rows // slab: (i * n // n_steps, 0))
        in_specs.append(spec)
        out_specs.append(spec)
        out_shape.append(jax.ShapeDtypeStruct(w.shape, BF16))
    operands = (x2d, g_pre, w_qkv, *tabs, *casts)
    side_static = None
    if side is not None:
        assert seq is not None
        s_in, s_ops, s_out, s_shape, side_static = _side_job(side, n_steps)
        in_specs, operands = in_specs + s_in, operands + s_ops
        out_specs, out_shape = out_specs + s_out, out_shape + s_shape
    return pl.pallas_call(
        functools.partial(_proj_kernel, tm=tm, by_residue=seq is not None,
                          tiles_per_seq=None if seq is None else seq // tm, n_cast=len(casts), side=side_static),
        grid=(n_steps,),
        in_specs=in_specs,
        out_specs=out_specs,
        out_shape=out_shape,
        scratch_shapes=scratch,
        compiler_params=_params(),
        name="proj",
    )(*operands)


def _ret_kernel(q_ref, k_ref, v_ref, dec_ref, qd_ref, kd_ref, gc_ref, o_ref, st_ref, s_scr, *, n_chunks):
    s_scr[...] = jnp.zeros_like(s_scr)

    def body(c, carry):
        r0 = pl.multiple_of(c * RET_CHUNK, RET_CHUNK)
        rows = pl.ds(r0, RET_CHUNK)
        first = []
        for h in range(RET_HEADS):
            qk = slice(h * RET_QK_DIM, (h + 1) * RET_QK_DIM)
            vv = slice(h * RET_V_DIM, (h + 1) * RET_V_DIM)
            q = q_ref[rows, qk]
            k = k_ref[rows, qk]
            v = v_ref[rows, vv]
            s0 = s_scr[h]
            sc = _dot_nt(q, k)
            from_state = _dot(q, s0.astype(BF16))
            kd_t = (k.astype(F32) * kd_ref[h]).T.astype(BF16)
            first.append((sc, from_state, _dot(kd_t, v), s0, v))
        for h, (sc, from_state, inc, s0, v) in enumerate(first):
            vv = slice(h * RET_V_DIM, (h + 1) * RET_V_DIM)
            o = _dot((sc * dec_ref[h]).astype(BF16), v) + from_state * qd_ref[h]
            s_scr[h] = s0 * gc_ref[h] + inc
            o_ref[rows, vv] = o * lax.rsqrt(jnp.mean(o * o, axis=-1, keepdims=True) + EPS)
        return carry

    lax.fori_loop(0, n_chunks, body, 0, unroll=8)
    st_ref[0] = s_scr[...]


def _log_gamma():
    return np.log1p(-np.exp2(-5.0 - np.arange(RET_HEADS, dtype=np.float64)))


def _ret_tables(chunk):
    lg = _log_gamma()
    idx = np.arange(RET_CHUNK, dtype=np.float64)
    diff = idx[:, None] - idx[None, :]
    dec = np.where(diff[None] >= 0, np.exp(lg[:, None, None] * np.maximum(diff, 0.0)[None]), 0.0)
    qd = np.exp(lg[:, None] * (idx[None, :] + 1.0))
    kd = np.exp(lg[:, None] * (chunk - 1.0 - idx)[None, :])
    gc = np.exp(lg * chunk)
    qd = np.broadcast_to(qd[:, :, None], (RET_HEADS, RET_CHUNK, RET_V_DIM))
    kd = np.broadcast_to(kd[:, :, None], (RET_HEADS, RET_CHUNK, RET_QK_DIM))
    gc = np.broadcast_to(gc[:, None, None], (RET_HEADS, RET_QK_DIM, RET_V_DIM))
    return [jnp.asarray(t, F32) for t in (dec, qd, kd, gc)]


def _ret_prompt(rq, rk, rv, batch, seq):
    dec, qd, kd, gc = _ret_tables(RET_CHUNK)
    row = lambda b: (b, 0)
    return pl.pallas_call(
        functools.partial(_ret_kernel, n_chunks=seq // RET_CHUNK),
        grid=(batch,),
        in_specs=[pl.BlockSpec((seq, RET_QK), row), pl.BlockSpec((seq, RET_QK), row), pl.BlockSpec((seq, RET_V), row),
                  _resident(dec.shape), _resident(qd.shape), _resident(kd.shape), _resident(gc.shape)],
        out_specs=[pl.BlockSpec((seq, RET_V), row),
                   pl.BlockSpec((1, RET_HEADS, RET_QK_DIM, RET_V_DIM), lambda b: (b, 0, 0, 0))],
        out_shape=[jax.ShapeDtypeStruct((batch * seq, RET_V), F32),
                   jax.ShapeDtypeStruct((batch, RET_HEADS, RET_QK_DIM, RET_V_DIM), F32)],
        scratch_shapes=[pltpu.VMEM((RET_HEADS, RET_QK_DIM, RET_V_DIM), F32)],
        compiler_params=_params(),
        name="ret_prompt",
    )(rq, rk, rv, dec, qd, kd, gc)


def _ret_sample_kernel(q_ref, k_ref, kt_ref, v_ref, s_ref, dec_ref, qd_ref, kd_ref, gc_ref, o_ref, st_ref,
                       *, n_tok, n_b):
    pairs = [(b, h) for b in range(n_b) for h in range(RET_HEADS)]
    qk_of = lambda h: slice(h * RET_QK_DIM, (h + 1) * RET_QK_DIM)
    vv_of = lambda h: slice(h * RET_V_DIM, (h + 1) * RET_V_DIM)
    first = []
    for b, h in pairs:
        q = q_ref[b, :, qk_of(h)]
        k = k_ref[b, :, qk_of(h)]
        v = v_ref[b, :, vv_of(h)]
        s0 = s_ref[b, h]
        from_state = _dot(q.astype(BF16), s0.astype(BF16))
        k_dec_t = (kt_ref[b, h] * kd_ref[h]).astype(BF16)
        st_ref[b, h] = s0 * gc_ref[h] + _dot(k_dec_t, v.astype(BF16))
        scores = [jnp.sum(q * k[i:i + 1, :], axis=-1, keepdims=True) for i in range(n_tok)]
        first.append((from_state, scores))
    second = []
    for (b, h), (from_state, scores) in zip(pairs, first):
        v = v_ref[b, :, vv_of(h)]
        o = from_state * qd_ref[h]
        for i, sc_i in enumerate(scores):
            o = o + (sc_i * dec_ref[h][:, i:i + 1]) * v[i:i + 1, :]
        second.append((o, jnp.mean(o * o, axis=-1, keepdims=True)))
    for (b, h), (o, ms) in zip(pairs, second):
        o_ref[b, :, vv_of(h)] = o * lax.rsqrt(ms + EPS)


def _ret_sample(rq, rk, rv, state, n_tok):
    db = state.shape[0]
    pad = RET_PAD - n_tok
    n_b = 8

    def pad_rows(t):
        t = t.astype(F32).reshape(db, n_tok, t.shape[-1])
        return jnp.pad(t, ((0, 0), (0, pad), (0, 0)))

    q, k, v = pad_rows(rq), pad_rows(rk), pad_rows(rv)
    kt = k.reshape(db, RET_PAD, RET_HEADS, RET_QK_DIM).transpose(0, 2, 3, 1)
    lg = _log_gamma()
    idx = np.arange(RET_PAD, dtype=np.float64)
    diff = idx[:, None] - idx[None, :]
    dec = np.where(diff[None] >= 0, np.exp(lg[:, None, None] * np.maximum(diff, 0.0)[None]), 0.0)
    qd = np.broadcast_to(np.exp(lg[:, None] * (idx[None, :] + 1.0))[:, :, None], (RET_HEADS, RET_PAD, RET_V_DIM))
    kd = np.broadcast_to(np.exp(lg[:, None] * (n_tok - 1.0 - idx)[None, :])[:, None, :],
                         (RET_HEADS, RET_QK_DIM, RET_PAD))
    gc = np.broadcast_to(np.exp(lg * n_tok)[:, None, None], (RET_HEADS, RET_QK_DIM, RET_V_DIM))
    dec, qd, kd, gc = (jnp.asarray(t, F32) for t in (dec, qd, kd, gc))
    b3 = lambda i: (i, 0, 0)
    b4 = lambda i: (i, 0, 0, 0)
    o, st = pl.pallas_call(
        functools.partial(_ret_sample_kernel, n_tok=n_tok, n_b=n_b),
        grid=(db // n_b,),
        in_specs=[pl.BlockSpec((n_b, RET_PAD, RET_QK), b3), pl.BlockSpec((n_b, RET_PAD, RET_QK), b3),
                  pl.BlockSpec((n_b, RET_HEADS, RET_QK_DIM, RET_PAD), b4),
                  pl.BlockSpec((n_b, RET_PAD, RET_V), b3),
                  pl.BlockSpec((n_b, RET_HEADS, RET_QK_DIM, RET_V_DIM), b4),
                  _resident(dec.shape), _resident(qd.shape), _resident(kd.shape), _resident(gc.shape)],
        out_specs=[pl.BlockSpec((n_b, RET_PAD, RET_V), b3),
                   pl.BlockSpec((n_b, RET_HEADS, RET_QK_DIM, RET_V_DIM), b4)],
        out_shape=[jax.ShapeDtypeStruct((db, RET_PAD, RET_V), F32),
                   jax.ShapeDtypeStruct(state.shape, F32)],
        compiler_params=_params(),
        name="ret_sample",
    )(q, k, kt, v, state, dec, qd, kd, gc)
    return o[:, :n_tok].reshape(db * n_tok, RET_V), st


def _attn_kernel(q_ref, k_ref, v_ref, o_ref, l_ref, vt_scr, bias_scr, *staging, dil, nb):
    n_blk = dil * nb
    length = nb * BAND_BLOCK
    blk = BAND_BLOCK
    o_dst, l_dst = staging if dil > 1 else (o_ref, l_ref)

    kk = lax.broadcasted_iota(jnp.int32, (2 * blk, 2 * LANES), 0)
    ql = lax.broadcasted_iota(jnp.int32, (2 * blk, 2 * LANES), 1) & (blk - 1)
    in_span = kk <= ql + blk
    bias_scr[...] = jnp.where(in_span & (kk >= ql), 0.0, NEG)

    def transpose_values(c):
        vt_scr[c] = v_ref[0, c * blk:(c + 1) * blk, :].astype(F32).T.astype(BF16)

    lane = lax.broadcasted_iota(jnp.int32, (blk, LANES), 1)

    def scores(c, p):
        lanes = slice(p * LANES, (p + 1) * LANES)
        k0 = 0 if c % nb == 0 else blk
        q128 = q_ref[0, c * blk:(c + 1) * blk, lanes]
        zero = jnp.zeros_like(q128)
        q_pair = jnp.concatenate([jnp.where(lane < ATT_HEAD_DIM, q128, zero),
                                  jnp.where(lane >= ATT_HEAD_DIM, q128, zero)], axis=0)
        keys = k_ref[0, c * blk - k0:(c + 1) * blk, lanes]
        return _dot_nt(keys, q_pair) + bias_scr[blk - k0:2 * blk, :]

    def weights_values(c, p, s_t):
        lanes = slice(p * LANES, (p + 1) * LANES)
        m = jnp.max(s_t, axis=0, keepdims=True)
        p_t = jnp.exp(s_t - m).astype(BF16)
        v_t = vt_scr[c, lanes, :]
        if c % nb:
            v_t = jnp.concatenate([vt_scr[c - 1, lanes, :], v_t], axis=1)
        v_t = jnp.concatenate([v_t, jnp.ones((2 * SUBLANES, v_t.shape[1]), BF16)], axis=0)
        return _dot(v_t, p_t), m

    def finish(c, p, o_t, m):
        den = o_t[LANES:LANES + 1, :]
        lse = m + jnp.log(den)
        o_pair = jnp.concatenate([o_t[0:ATT_HEAD_DIM, 0:blk] / den[:, 0:blk],
                                  o_t[ATT_HEAD_DIM:LANES, blk:2 * blk] / den[:, blk:2 * blk]], axis=0)
        l_pair = jnp.concatenate([jnp.broadcast_to(lse[:, 0:blk], (ATT_HEAD_DIM, blk)),
                                  jnp.broadcast_to(lse[:, blk:2 * blk], (ATT_HEAD_DIM, blk))], axis=0)
        o_dst[p, c * blk:(c + 1) * blk, :] = o_pair.T
        l_dst[p, c * blk:(c + 1) * blk, :] = l_pair.T

    n_pair = ATT_OUT // LANES
    groups = [[(c, p) for c in range(c0, min(c0 + ATTN_GROUP, n_blk)) for p in range(n_pair)]
              for c0 in range(0, n_blk, ATTN_GROUP)]

    def stage_scores(group):
        for c in sorted({c for c, _ in group}):
            transpose_values(c)
        return [scores(*chain) for chain in group]

    s_next = stage_scores(groups[0])
    pending = []
    for i, group in enumerate(groups):
        s_cur = s_next
        if i + 1 < len(groups):
            s_next = stage_scores(groups[i + 1])
        o_m = [weights_values(*chain, s) for chain, s in zip(group, s_cur)]
        for args in pending:
            finish(*args)
        pending = [chain + om for chain, om in zip(group, o_m)]
    for args in pending:
        finish(*args)

    if dil > 1:
        for p in range(ATT_OUT // LANES):
            for r in range(dil):
                o_ref[p, pl.ds(r, length, stride=dil), :] = o_dst[p, r * length:(r + 1) * length, :]
                l_ref[p, pl.ds(r, length, stride=dil), :] = l_dst[p, r * length:(r + 1) * length, :]


def _attn_prompt(q, k, v, g, batch, seq):
    _, dil = ATT_GROUPS[g]
    nb = seq // dil // BAND_BLOCK
    _log2(nb)
    q, k, v = (t.reshape(batch, seq, ATT_OUT) for t in (q, k, v))
    in_spec = pl.BlockSpec((1, seq, ATT_OUT), lambda b: (b, 0, 0))
    n_pair = ATT_OUT // LANES
    out_spec = pl.BlockSpec((n_pair, seq, LANES), lambda b: (0, b, 0))
    scratch = [pltpu.VMEM((dil * nb, ATT_OUT, BAND_BLOCK), BF16), pltpu.VMEM((2 * BAND_BLOCK, 2 * LANES), F32)]
    if dil > 1:
        scratch += [pltpu.VMEM((n_pair, seq, LANES), F32)] * 2
    return pl.pallas_call(
        functools.partial(_attn_kernel, dil=dil, nb=nb),
        grid=(batch,),
        in_specs=[in_spec] * 3,
        out_specs=[out_spec] * 2,
        out_shape=[jax.ShapeDtypeStruct((n_pair, batch * seq, LANES), F32)] * 2,
        scratch_shapes=scratch,
        compiler_params=_params(),
        name=f"attn_prompt_g{g}",
    )(q, k, v)


def _attn_sample_kernel(q_ref, c_ref, n_ref, o_ref, l_ref, nc_ref, *, g, n_tok, wb, n_b):
    for bb in range(n_b):
        _attn_sample_one(q_ref, c_ref, n_ref, o_ref, l_ref, nc_ref, bb, g=g, n_tok=n_tok, wb=wb, n_b=n_b)


def _attn_sample_one(q_ref, c_ref, n_ref, o_ref, l_ref, nc_ref, bb, **static):
    weights, finish = _attn_sample_phases(q_ref, c_ref, n_ref, o_ref, l_ref, nc_ref, bb, **static)
    finish(weights())


def _attn_sample_phases(q_ref, c_ref, n_ref, o_ref, l_ref, nc_ref, bb, *, g, n_tok, wb, n_b):
    win, dil = ATT_GROUPS[g]
    nk = win // dil
    _log2(dil)
    n_col = ATT_HEADS * n_tok
    first_new = LANES - n_tok
    seq_lane = ((pl.program_id(0) * n_b + bb) * n_tok) % LANES
    new = [pltpu.roll(n_ref[kv], (first_new + LANES - seq_lane) % LANES, 1) for kv in range(2)]
    lane_t = lax.broadcasted_iota(jnp.int32, (ATT_OUT, LANES), 1)
    for kv in range(2):
        shifted = pltpu.roll(c_ref[bb, kv], wb - n_tok, 1)
        if wb > LANES:
            nc_ref[bb, kv, :, 0:wb - LANES] = shifted[:, 0:wb - LANES]
        nc_ref[bb, kv, :, wb - LANES:wb] = jnp.where(lane_t < first_new, shifted[:, wb - LANES:wb], new[kv])

    q = q_ref[bb][:, g * ATT_OUT:(g + 1) * ATT_OUT]
    row = lax.broadcasted_iota(jnp.int32, (n_col, ATT_OUT), 0)
    lane = lax.broadcasted_iota(jnp.int32, (n_col, ATT_OUT), 1)
    head_sel = (lane >> _log2(ATT_HEAD_DIM)) == (row >> _log2(n_tok))
    qbd = jnp.where(head_sel, q, jnp.zeros_like(q))
    k_c, v_c = c_ref[bb, 0].astype(BF16), c_ref[bb, 1].astype(BF16)
    k_n, v_n = new[0].astype(BF16), new[1].astype(BF16)

    def weights():
        s_c = _dot(qbd, k_c)
        s_n = _dot(qbd, k_n)
        t_c = lax.broadcasted_iota(jnp.int32, (n_col, wb), 0) & (n_tok - 1)
        d_c = wb + t_c - lax.broadcasted_iota(jnp.int32, (n_col, wb), 1)
        ok_c = ((d_c & (dil - 1)) == 0) & (d_c <= nk * dil)
        t_n = lax.broadcasted_iota(jnp.int32, (n_col, LANES), 0) & (n_tok - 1)
        new_idx = lax.broadcasted_iota(jnp.int32, (n_col, LANES), 1) - first_new
        d_n = t_n - new_idx
        ok_n = (new_idx >= 0) & (d_n >= 0) & ((d_n & (dil - 1)) == 0) & (d_n <= nk * dil)
        s_c = jnp.where(ok_c, s_c, NEG)
        s_n = jnp.where(ok_n, s_n, NEG)
        m = jnp.maximum(jnp.max(s_c, axis=-1, keepdims=True), jnp.max(s_n, axis=-1, keepdims=True))
        p_c = jnp.exp(s_c - m)
        p_n = jnp.exp(s_n - m)
        den = jnp.sum(p_c, axis=-1, keepdims=True) + jnp.sum(p_n, axis=-1, keepdims=True)
        return p_c.astype(BF16), p_n.astype(BF16), m, den

    def finish(state):
        p_c, p_n, m, den = state
        o = (_dot_nt(p_c, v_c) + _dot_nt(p_n, v_n)) / den
        lse = jnp.broadcast_to(m + jnp.log(den), (n_col, ATT_OUT))
        o = jnp.where(head_sel, o, 0.0)
        lse = jnp.where(head_sel, lse, 0.0)
        o_t = o[0:n_tok]
        l_t = lse[0:n_tok]
        for h in range(1, ATT_HEADS):
            o_t = o_t + o[h * n_tok:(h + 1) * n_tok]
            l_t = l_t + lse[h * n_tok:(h + 1) * n_tok]
        o_ref[bb] = o_t
        l_ref[bb] = l_t

    return weights, finish


def _attn_sample(q_tiled, cache, kv_new, g, n_tok):
    db, _, _, wb = cache.shape
    n_b = max(1, min(SUBLANES, ATT_GROUPS[-1][0] // wb))
    assert wb % LANES == 0 and db % n_b == 0 and LANES % (n_b * n_tok) == 0 and kv_new.shape[2] % LANES == 0
    b3 = lambda i: (i, 0, 0)
    b4 = lambda i: (i, 0, 0, 0)
    return pl.pallas_call(
        functools.partial(_attn_sample_kernel, g=g, n_tok=n_tok, wb=wb, n_b=n_b),
        grid=(db // n_b,),
        in_specs=[pl.BlockSpec((n_b, ATT_HEADS * n_tok, ATT_W), b3),
                  pl.BlockSpec((n_b, 2, ATT_OUT, wb), b4),
                  pl.BlockSpec((2, ATT_OUT, LANES), lambda i: (0, 0, i * n_b * n_tok // LANES))],
        out_specs=[pl.BlockSpec((n_b, n_tok, ATT_OUT), b3), pl.BlockSpec((n_b, n_tok, ATT_OUT), b3),
                   pl.BlockSpec((n_b, 2, ATT_OUT, wb), b4)],
        out_shape=[jax.ShapeDtypeStruct((db, n_tok, ATT_OUT), F32), jax.ShapeDtypeStruct((db, n_tok, ATT_OUT), F32),
                   jax.ShapeDtypeStruct(cache.shape, F32)],
        compiler_params=_params(),
        name=f"attn_sample_g{g}",
    )(q_tiled, cache, kv_new)


def _merge_kernel(*refs, tm, n_split, permute, side=None):
    if side is not None:
        side_in, refs = refs[:3], refs[3:]
    (x_ref, on_ref, o0_ref, o1_ref, o2_ref, l0_ref, l1_ref, l2_ref, gpre_ref, wg_ref, wbr_ref, wba_ref, wout_ref,
     gpost_ref, gffn_ref, x1_ref, h2_ref), perm = refs[:17], refs[17:]
    if side is not None:
        side_refs, perm = side_in + perm[:3], perm[3:]
    bounds = [i * tm // n_split for i in range(n_split + 1)]
    rows_of = [slice(a, b) for a, b in zip(bounds[:-1], bounds[1:])]
    n_pair = ATT_OUT // LANES

    def gates(rows):
        x = x_ref[rows, :]
        h = _rms(x, gpre_ref[...]).astype(BF16)
        return [_dot(h, wg_ref[:, c:c + D_MODEL]) for c in (0, RET_V, RET_V + D_MODEL)]

    def branches(rows, rg):
        r = (on_ref[rows, :] * (rg * jax.nn.sigmoid(rg))).astype(BF16)
        atts = []
        for p in range(n_pair):
            l0, l1, l2 = l0_ref[p, rows, :], l1_ref[p, rows, :], l2_ref[p, rows, :]
            mx = jnp.maximum(jnp.maximum(l0, l1), l2)
            e0, e1, e2 = jnp.exp(l0 - mx), jnp.exp(l1 - mx), jnp.exp(l2 - mx)
            att = (e0 * o0_ref[p, rows, :] + e1 * o1_ref[p, rows, :] + e2 * o2_ref[p, rows, :]) / (e0 + e1 + e2)
            atts.append(att.astype(BF16))
        return _dot(r, wbr_ref[...]), _dot(jnp.concatenate(atts, axis=1), wba_ref[...])

    def mixed(gr, ga, br, ba):
        mix = jax.nn.sigmoid(gr) * br + jax.nn.sigmoid(ga) * ba
        return _dot(mix.astype(BF16), wout_ref[...])

    def finish(rows, mo):
        x1 = x_ref[rows, :] + _rms(mo, gpost_ref[...])
        x1_ref[rows, :] = x1
        return _rms(x1, gffn_ref[...])

    g = []
    for i, rows in enumerate(rows_of):
        g.append(gates(rows))
        if i == 0 and side is not None:
            side_weights, side_finish = _attn_sample_phases(*side_refs, 0, g=side[0], n_tok=side[1], wb=side[2],
                                                            n_b=1)
    b = [branches(rows, gi[0]) for rows, gi in zip(rows_of, g)]
    if side is not None:
        side_state = side_weights()
    mo = [mixed(gi[1], gi[2], *bi) for gi, bi in zip(g, b)]
    if side is not None:
        side_finish(side_state)
    h2 = [finish(rows, moi) for rows, moi in zip(rows_of, mo)]
    if not permute:
        for rows, h2i in zip(rows_of, h2):
            h2_ref[rows, :] = h2i.astype(BF16)
        return
    perm_ref, = perm
    nv = tm // SUBLANES
    for lc in range(D_MODEL // LANES):
        lanes = slice(lc * LANES, (lc + 1) * LANES)
        for s in range(SUBLANES):
            tok0 = s * nv
            i = max(j for j, b in enumerate(bounds[:-1]) if b <= tok0)
            off = tok0 - bounds[i]
            assert tok0 + nv <= bounds[i + 1]
            perm_ref[lc, pl.ds(s, nv, stride=SUBLANES), :] = h2[i][off:off + nv, lanes]
        h2_ref[:, lanes] = perm_ref[lc].astype(BF16)


def _side_job(side, n_steps):
    g, n_tok, q_tiled, cache, kv_new = side
    db, _, _, wb = cache.shape
    assert db == n_steps and LANES % n_tok == 0, (db, n_steps)
    b3 = lambda i: (i, 0, 0)
    b4 = lambda i: (i, 0, 0, 0)
    in_specs = [pl.BlockSpec((1, ATT_HEADS * n_tok, ATT_W), b3), pl.BlockSpec((1, 2, ATT_OUT, wb), b4),
                pl.BlockSpec((2, ATT_OUT, LANES), lambda i: (0, 0, i * n_tok // LANES))]
    out_specs = [pl.BlockSpec((1, n_tok, ATT_OUT), b3), pl.BlockSpec((1, n_tok, ATT_OUT), b3),
                 pl.BlockSpec((1, 2, ATT_OUT, wb), b4)]
    out_shape = [jax.ShapeDtypeStruct((db, n_tok, ATT_OUT), F32)] * 2 + [jax.ShapeDtypeStruct(cache.shape, F32)]
    return in_specs, (q_tiled, cache, kv_new), out_specs, out_shape, (g, n_tok, wb)


def _merge(x2d, on, outs, lses, g_pre, w_gates, w_br, w_ba, w_out, g_post, g_ffn, tm, permute, side=None):
    m = x2d.shape[0]
    n_split = 2 if tm >= 2 * MXU_N else 1
    scratch = [pltpu.VMEM((D_MODEL // LANES, tm, LANES), F32)] if permute else []
    row = lambda i: (i, 0)
    vec = _resident((1, D_MODEL))
    att_spec = pl.BlockSpec((ATT_OUT // LANES, tm, LANES), lambda i: (0, i, 0))
    in_specs = [pl.BlockSpec((tm, D_MODEL), row), pl.BlockSpec((tm, RET_V), row)] + [att_spec] * 6 + [
        vec, _resident(w_gates.shape), _resident(w_br.shape), _resident(w_ba.shape), _resident(w_out.shape),
        vec, vec]
    out_specs = [pl.BlockSpec((tm, D_MODEL), row), pl.BlockSpec((tm, D_MODEL), row)]
    out_shape = [jax.ShapeDtypeStruct((m, D_MODEL), F32), jax.ShapeDtypeStruct((m, D_MODEL), BF16)]
    operands = (x2d, on, *outs, *lses, g_pre, w_gates, w_br, w_ba, w_out, g_post, g_ffn)
    side_static = None
    if side is not None:
        s_in, s_ops, s_out, s_shape, side_static = _side_job(side, m // tm)
        in_specs, operands = s_in + in_specs, s_ops + operands
        out_specs, out_shape = out_specs + s_out, out_shape + s_shape
    return pl.pallas_call(
        functools.partial(_merge_kernel, tm=tm, n_split=n_split, permute=permute, side=side_static),
        grid=(m // tm,),
        in_specs=in_specs,
        out_specs=out_specs,
        out_shape=out_shape,
        scratch_shapes=scratch,
        compiler_params=_params(),
        name="merge",
    )(*operands)


def _ffn_kernel(*refs, tm, tiles_per_seq, seq_tok, side=None):
    if side is not None:
        (sq_ref, sc_ref, sn_ref), refs = refs[:3], refs[3:]
        so_ref, sl_ref, snc_ref = refs[9:12]
        refs = refs[:9] + refs[12:]
        side_refs = (sq_ref, sc_ref, sn_ref, so_ref, sl_ref, snc_ref)
    if seq_tok is None:
        h2_ref, x1_ref, wup_ref, wdn_ref, cw_ref, cb_ref, g_ref, y_ref, cs_ref, carry_ref, f_ref, perm_ref = refs
    else:
        h2_ref, x1_ref, p1_ref, p2_ref, wup_ref, wdn_ref, cw_ref, cb_ref, g_ref, y_ref, u_ref, f_ref = refs
    h2 = h2_ref[...]
    if seq_tok is None:
        sub = lax.broadcasted_iota(jnp.int32, (SUBLANES, FF_CHUNK), 0)

        @pl.when(pl.program_id(0) % tiles_per_seq == 0)
        def _():
            carry_ref[...] = jnp.zeros_like(carry_ref)
    else:
        t = lax.broadcasted_iota(jnp.int32, (tm, FF_CHUNK), 0) & (seq_tok - 1)
        _log2(seq_tok)

    def up(j):
        return [_dot(h2, wup_ref[:, c:c + FF_CHUNK]) for c in (j * FF_CHUNK, D_FF + j * FF_CHUNK)]

    def conv(u, col):
        cols = slice(col, col + FF_CHUNK)
        if seq_tok is None:
            prev = carry_ref[:, cols]
            last2 = jnp.where(sub == 0, prev[SUBLANES - 1:SUBLANES], pltpu.roll(u[tm - 2 * SUBLANES:tm - SUBLANES], 1, 0))
            last1 = jnp.where(sub == 0, prev[2 * SUBLANES - 1:2 * SUBLANES], pltpu.roll(u[tm - SUBLANES:tm], 1, 0))
            u1 = jnp.concatenate([last1, u[0:tm - SUBLANES]], axis=0)
            u2 = jnp.concatenate([last2, last1, u[0:tm - 2 * SUBLANES]], axis=0)
            carry_ref[:, cols] = u[tm - 2 * SUBLANES:tm]
        else:
            u1 = jnp.where(t == 0, p1_ref[:, cols], pltpu.roll(u, 1, 0))
            u2 = jnp.where(t <= 1, p2_ref[:, cols], pltpu.roll(u, 2, 0))
            u_ref[:, cols] = u
        cw = cw_ref[:, cols]
        return cb_ref[:, cols] + cw[0:1] * u2 + cw[1:2] * u1 + cw[2:3] * u

    def gate(j, ug, uv):
        cg = conv(ug, j * FF_CHUNK)
        cv = conv(uv, D_FF + j * FF_CHUNK)
        f_ref[:, j * FF_CHUNK:(j + 1) * FF_CHUNK] = (jax.nn.gelu(cg, approximate=True) * cv).astype(BF16)

    n_chunks = D_FF // FF_CHUNK
    u_next = up(0)
    for j in range(n_chunks):
        u_cur = u_next
        if j + 1 < n_chunks:
            u_next = up(j + 1)
        if j == 0 and side is not None:
            side_weights, side_finish = _attn_sample_phases(*side_refs, 0, g=side[0], n_tok=side[1], wb=side[2],
                                                            n_b=1)
        gate(j, *u_cur)
    if side is not None:
        side_state = side_weights()
    down = _dot(f_ref[...], wdn_ref[...])
    if side is not None:
        side_finish(side_state)
    out = _rms(down, g_ref[...])
    if seq_tok is None:
        nv = tm // SUBLANES
        for lc in range(D_MODEL // LANES):
            lanes = slice(lc * LANES, (lc + 1) * LANES)
            perm_ref[lc] = out[:, lanes]
            for s in range(SUBLANES):
                rows = slice(s * nv, (s + 1) * nv)
                y_ref[rows, lanes] = x1_ref[rows, lanes] + perm_ref[lc, pl.ds(s, nv, stride=SUBLANES), :]

        @pl.when(pl.program_id(0) % tiles_per_seq == tiles_per_seq - 1)
        def _():
            cs_ref[0, 0:1, :] = carry_ref[SUBLANES - 1:SUBLANES, :]
            cs_ref[0, 1:2, :] = carry_ref[2 * SUBLANES - 1:2 * SUBLANES, :]
    else:
        y_ref[...] = x1_ref[...] + out


def _ffn_prompt(h2, x1, w_up, w_dn, conv_w, conv_b, g_post, batch, seq, tm, side=None):
    m = h2.shape[0]
    tps = seq // tm
    n_steps = m // tm
    row = lambda i: (i, 0)
    in_specs = [pl.BlockSpec((tm, D_MODEL), row), pl.BlockSpec((tm, D_MODEL), row),
                _resident(w_up.shape), _resident(w_dn.shape), _resident(conv_w.shape), _resident(conv_b.shape),
                _resident((1, D_MODEL))]
    out_specs = [pl.BlockSpec((tm, D_MODEL), row),
                 pl.BlockSpec((1, CONV_W - 1, 2 * D_FF), lambda i: (i // tps, 0, 0))]
    out_shape = [jax.ShapeDtypeStruct((m, D_MODEL), F32),
                 jax.ShapeDtypeStruct((batch, CONV_W - 1, 2 * D_FF), F32)]
    operands = (h2, x1, w_up, w_dn, conv_w, conv_b, g_post)
    side_static = None
    if side is not None:
        s_in, s_ops, s_out, s_shape, side_static = _side_job(side, n_steps)
        in_specs, operands = s_in + in_specs, s_ops + operands
        out_specs, out_shape = out_specs + s_out, out_shape + s_shape
    return pl.pallas_call(
        functools.partial(_ffn_kernel, tm=tm, tiles_per_seq=tps, seq_tok=None, side=side_static),
        grid=(n_steps,),
        in_specs=in_specs,
        out_specs=out_specs,
        out_shape=out_shape,
        scratch_shapes=[pltpu.VMEM((2 * SUBLANES, 2 * D_FF), F32), pltpu.VMEM((tm, D_FF), BF16),
                        pltpu.VMEM((D_MODEL // LANES, tm, LANES), F32)],
        compiler_params=_params(),
        name="ffn_prompt",
    )(*operands)


def _ffn_sample(h2, x1, p1, p2, w_up, w_dn, conv_w, conv_b, g_post, n_tok):
    m = h2.shape[0]
    const = lambda i: (0, 0)
    full = lambda a: _resident(a.shape)
    return pl.pallas_call(
        functools.partial(_ffn_kernel, tm=m, tiles_per_seq=1, seq_tok=n_tok),
        grid=(1,),
        in_specs=[full(h2), full(x1), full(p1), full(p2), full(w_up), full(w_dn), full(conv_w), full(conv_b),
                  _resident((1, D_MODEL))],
        out_specs=[pl.BlockSpec((m, D_MODEL), const), pl.BlockSpec((m, 2 * D_FF), const)],
        out_shape=[jax.ShapeDtypeStruct((m, D_MODEL), F32), jax.ShapeDtypeStruct((m, 2 * D_FF), F32)],
        scratch_shapes=[pltpu.VMEM((m, D_FF), BF16)],
        compiler_params=_params(),
        name="ffn_sample",
    )(h2, x1, p1, p2, w_up, w_dn, conv_w, conv_b, g_post)


def _rot_tables(pos):
    pos = np.asarray(pos, np.float64)
    rf = RET_THETA ** (-np.linspace(0.0, 1.0, RET_QK_DIM // 2))
    ang = pos[:, None] * rf[None, :]
    cos, sin = np.cos(ang), np.sin(ang)
    cr = np.concatenate([cos, cos], axis=-1)
    sr = np.concatenate([-sin, sin], axis=-1)
    k_scale = RET_QK_DIM ** -0.5
    half = ROPE_DIM // 2
    af = ROPE_THETA ** (-np.arange(half) / half)
    ang = pos[:, None] * af[None, :]
    cos, sin = np.cos(ang), np.sin(ang)
    n = pos.shape[0]
    rest = ATT_HEAD_DIM - ROPE_DIM
    zh = np.zeros((n, half))
    ca = np.concatenate([cos, cos, np.ones((n, rest))], axis=-1)
    sa1 = np.concatenate([zh, sin, np.zeros((n, rest))], axis=-1)
    sa2 = np.concatenate([-sin, zh, np.zeros((n, rest))], axis=-1)
    rep = MXU_N // ATT_HEAD_DIM
    ca, sa1, sa2 = (np.tile(t, (1, rep)) for t in (ca, sa1, sa2))
    return [jnp.asarray(t, F32) for t in (cr, sr, cr * k_scale, sr * k_scale, ca, sa1, sa2)]


def kernel(x_prompt, x_sample, cache_kv_g0, cache_kv_g1, cache_kv_g2, state_ret, state_conv, norm_mix_pre, w_in,
           w_branch_ret, w_branch_attn, w_out, norm_mix_post, norm_ffn_pre, w_ffn_up, conv_w, conv_b, w_ffn_down,
           norm_ffn_post):
    batch, seq, _ = x_prompt.shape
    db, n_tok, _ = x_sample.shape
    depth = w_in.shape[0]
    assert depth == 1 and seq % (ATT_GROUPS[-1][1] * BAND_BLOCK) == 0 and n_tok <= SUBLANES
    caches = (cache_kv_g0, cache_kv_g1, cache_kv_g2)

    wi = w_in[0]
    o_rg = 2 * RET_QK + RET_V
    o_aq = o_rg + RET_V
    o_gr = o_aq + 3 * ATT_W
    w_qkv = jnp.concatenate([wi[:, :o_rg], wi[:, o_aq:o_aq + ATT_W] * ATT_HEAD_DIM ** -0.5,
                             wi[:, o_aq + ATT_W:o_gr]], axis=1).astype(BF16)
    w_gates = jnp.concatenate([wi[:, o_rg:o_aq], wi[:, o_gr:]], axis=1).astype(BF16)
    later_weights = (w_branch_ret[0], w_branch_attn[0], w_out[0], w_ffn_up[0], w_ffn_down[0])
    cb = conv_b[0][None, :]
    cw = conv_w[0]
    g_pre, g_post, g_ffn, g_post2 = (t[0][None, :] for t in (norm_mix_pre, norm_mix_post, norm_ffn_pre, norm_ffn_post))

    tm = PROMPT_TILE
    ms = db * n_tok
    xs = x_sample.reshape(ms, D_MODEL)
    pos_s = np.tile(PAST_LEN + np.arange(n_tok), db)
    rq_s, rk_s, rv_s, aq, kv0, kv1, kv2 = _proj(xs, g_pre, w_qkv, _rot_tables(pos_s), ms)
    kvs = (kv0, kv1, kv2)
    q_tiled = jnp.tile(aq.reshape(db, n_tok, ATT_W), (1, ATT_HEADS, 1))
    sides = []
    for g in range(N_GROUPS):
        cache = caches[g][0]
        wb = cache.shape[1]
        cache_t = cache.transpose(0, 2, 3, 4, 1).reshape(db, 2, ATT_OUT, wb)
        kv_new = kvs[g].reshape(ms, 2, ATT_OUT).transpose(1, 2, 0)
        sides.append((g, n_tok, q_tiled, cache_t, kv_new))
    ride = batch * seq // tm == db and N_GROUPS == 3
    sample_attn = {} if ride else {g: _attn_sample(*sides[g][2:], g, n_tok) for g in range(N_GROUPS)}

    xp = x_prompt.reshape(batch * seq, D_MODEL)
    proj_out = _proj(xp, g_pre, w_qkv, _rot_tables(np.arange(seq)), tm, seq=seq, casts=later_weights,
                     side=sides[0] if ride else None)
    rq, rk, rv = proj_out[:3]
    qkv_res = proj_out[3:3 + 3 * N_GROUPS]
    kv_last = proj_out[3 + 3 * N_GROUPS:3 + 4 * N_GROUPS]
    n_main = 3 + 4 * N_GROUPS + len(later_weights)
    w_br, w_ba, w_o, w_up, w_dn = proj_out[3 + 4 * N_GROUPS:n_main]
    if ride:
        sample_attn[0] = proj_out[n_main:]
    on, p_ret = _ret_prompt(rq, rk, rv, batch, seq)
    outs, lses = [], []
    for g in range(N_GROUPS):
        o_g, l_g = _attn_prompt(*qkv_res[3 * g:3 * g + 3], g, batch, seq)
        outs.append(o_g)
        lses.append(l_g)
    p_kv = [t.reshape(batch, 2, ATT_HEADS, ATT_HEAD_DIM, t.shape[3]).transpose(0, 4, 1, 2, 3)[None] for t in kv_last]
    x1, h2, *rest = _merge(xp, on, outs, lses, g_pre, w_gates, w_br, w_ba, w_o, g_post, g_ffn, tm, permute=True,
                           side=sides[1] if ride else None)
    if ride:
        sample_attn[1] = rest
    y_p, p_conv, *rest = _ffn_prompt(h2, x1, w_up, w_dn, cw, cb, g_post2, batch, seq, tm,
                                     side=sides[2] if ride else None)
    if ride:
        sample_attn[2] = rest

    on_s, s_ret = _ret_sample(rq_s, rk_s, rv_s, state_ret[0], n_tok)
    outs, lses, s_kv = [], [], []
    for g in range(N_GROUPS):
        o_g, l_g, nc = sample_attn[g]
        outs.append(o_g.reshape(ms, ATT_OUT // LANES, LANES).transpose(1, 0, 2))
        lses.append(l_g.reshape(ms, ATT_OUT // LANES, LANES).transpose(1, 0, 2))
        s_kv.append(nc.reshape(db, 2, ATT_HEADS, ATT_HEAD_DIM, nc.shape[3]).transpose(0, 4, 1, 2, 3)[None])
    x1, h2 = _merge(xs, on_s, outs, lses, g_pre, w_gates, w_br, w_ba, w_o, g_post, g_ffn, ms, permute=False)
    st = state_conv[0]
    zeros = jnp.zeros((db, n_tok - 1, 2 * D_FF), F32)
    p1 = jnp.concatenate([st[:, 1:2], zeros], axis=1).reshape(ms, 2 * D_FF)
    p2 = jnp.concatenate([st, zeros[:, :n_tok - 2]], axis=1).reshape(ms, 2 * D_FF)
    y_s, u_s = _ffn_sample(h2, x1, p1, p2, w_up, w_dn, cw, cb, g_post2, n_tok)
    s_conv = u_s.reshape(db, n_tok, 2 * D_FF)[:, n_tok - (CONV_W - 1):][None]

    return (y_p.reshape(x_prompt.shape), y_s.reshape(x_sample.shape), p_kv[0], p_kv[1], p_kv[2], p_ret[None],
            p_conv[None], s_kv[0], s_kv[1], s_kv[2], s_ret[None], s_conv)
```

```python
import functools

import numpy as np
import jax
import jax.numpy as jnp
from jax import lax
from jax.experimental import pallas as pl
from jax.experimental.pallas import tpu as pltpu

F32 = jnp.float32
BF16 = jnp.bfloat16

D_MODEL = 1024
PAST_LEN = 16384
RET_HEADS = 4
RET_QK_DIM = 128
RET_V_DIM = 256
RET_CHUNK = 128
RET_THETA = 10000.0
RET_QK = RET_HEADS * RET_QK_DIM
RET_V = RET_HEADS * RET_V_DIM
ATT_GROUPS = ((128, 1), (512, 4), (2048, 16))
N_GROUPS = 3
ATT_HEADS = 4
ATT_HEAD_DIM = 64
ROPE_DIM = ATT_HEAD_DIM // 4
ROPE_THETA = 500000.0
BAND_BLOCK = 128
ATT_OUT = ATT_HEADS * ATT_HEAD_DIM
ATT_W = N_GROUPS * ATT_OUT
D_FF = 2816
CONV_W = 3
EPS = 1e-6

LANES = 128
SUBLANES = 8
MXU_N = 256
VMEM_LIMIT = 56 * 1024 * 1024
PROMPT_TILE = 512
NEG = -1e30
FF_CHUNK = MXU_N
ATTN_GROUP = 2
RET_PAD = 2 * SUBLANES
QKV_COLS = 2 * RET_QK + RET_V + 3 * ATT_W


def _rms(x, g):
    return x * lax.rsqrt(jnp.mean(x * x, axis=-1, keepdims=True) + EPS) * g


def _dot(a, b):
    return jnp.dot(a, b, preferred_element_type=F32)


def _dot_nt(a, b):
    return lax.dot_general(a, b, (((1,), (1,)), ((), ())), preferred_element_type=F32)


def _resident(shape):
    return pl.BlockSpec(shape, lambda *_: (0,) * len(shape), pipeline_mode=pl.Buffered(1))


def _log2(n):
    assert n > 0 and n & (n - 1) == 0, n
    return n.bit_length() - 1


def _params(n_axes=1):
    return pltpu.CompilerParams(dimension_semantics=("arbitrary",) * n_axes, vmem_limit_bytes=VMEM_LIMIT)


def _proj_kernel(x_ref, g_ref, w_ref, crq_ref, srq_ref, crk_ref, srk_ref, ca_ref, sa1_ref, sa2_ref, *rest,
                 tm, by_residue, tiles_per_seq, n_cast, side=None):
    cast_in, rest = rest[:n_cast], rest[n_cast:]
    if side is not None:
        side_in, rest = rest[:3], rest[3:]
        n_out = 3 + 4 * N_GROUPS + n_cast
        side_weights, side_finish = _attn_sample_phases(*side_in, *rest[n_out:n_out + 3], 0, g=side[0],
                                                        n_tok=side[1], wb=side[2], n_b=1)
        rest = rest[:n_out] + rest[n_out + 3:]
    rq_ref, rk_ref, rv_ref = rest[:3]
    if by_residue:
        res_refs = rest[3:3 + 3 * N_GROUPS]
        kv_refs = rest[3 + 3 * N_GROUPS:3 + 4 * N_GROUPS]
        cast_out = rest[3 + 4 * N_GROUPS:3 + 4 * N_GROUPS + n_cast]
        scr, scr2 = rest[3 + 4 * N_GROUPS + n_cast:3 + 4 * N_GROUPS + n_cast + 2]
    else:
        aq_ref = rest[3]
        kv_refs = rest[4:4 + N_GROUPS]
        cast_out = rest[4 + N_GROUPS:4 + N_GROUPS + n_cast]
    for src, dst in zip(cast_in, cast_out):
        dst[...] = src[...].astype(BF16)
    h = _rms(x_ref[...], g_ref[...]).astype(BF16)

    def mm(c0, width=MXU_N):
        return _dot(h, w_ref[:, c0:c0 + width])

    def rot_ret(t, c, s):
        return t * c + pltpu.roll(t, RET_QK_DIM // 2, 1) * s

    ca, sa1, sa2 = ca_ref[...], sa1_ref[...], sa2_ref[...]

    def rot_att(t):
        return t * ca + pltpu.roll(t, ROPE_DIM // 2, 1) * sa1 + pltpu.roll(t, MXU_N - ROPE_DIM // 2, 1) * sa2

    def put_by_residue(ref, val, dil):
        if dil == 1:
            ref[0, 0] = val.astype(BF16)
            return
        inner = 4
        outer = dil // inner
        two_pass = dil % SUBLANES == 0 and outer % SUBLANES != 0
        for half in range(MXU_N // LANES):
            lanes = slice(half * LANES, (half + 1) * LANES)
            scr[half] = val[:, lanes]
            if not two_pass:
                for r in range(dil):
                    ref[0, r, :, lanes] = scr[half, pl.ds(r, tm // dil, stride=dil), :].astype(BF16)
                continue
            sub = tm // inner
            for b in range(inner):
                scr2[half, b * sub:(b + 1) * sub] = scr[half, pl.ds(b, sub, stride=inner), :]
            for b in range(inner):
                for a in range(outer):
                    ref[0, inner * a + b, :, lanes] = scr2[half, pl.ds(b * sub + a, tm // dil, stride=outer),
                                                           :].astype(BF16)

    def put_ret(ref, c, c_ref, s_ref):
        def put(t):
            for half in range(MXU_N // LANES):
                lo = half * LANES
                col = c * MXU_N + lo
                ref[:, col:col + LANES] = rot_ret(t[:, lo:lo + LANES], c_ref[...], s_ref[...]).astype(BF16)
        return put

    def put_rv(c):
        def put(t):
            rv_ref[:, c * MXU_N:(c + 1) * MXU_N] = t.astype(BF16)
        return put

    def put_att(g, which):
        def put(t):
            if which < 2:
                t = rot_att(t)
            if not by_residue:
                if which == 0:
                    aq_ref[:, g * ATT_OUT:(g + 1) * ATT_OUT] = t.astype(BF16)
                else:
                    kv_refs[g][:, (which - 1) * ATT_OUT:which * ATT_OUT] = t
                return
            put_by_residue(res_refs[3 * g + which], t, ATT_GROUPS[g][1])
            if which == 0:
                return
            rows = kv_refs[g].shape[3]
            if ATT_GROUPS[g][0] >= tiles_per_seq * tm:
                kv_refs[g][0, which - 1] = t[tm - rows:tm].T
            else:
                late_windows.append((g, which, t[tm - rows:tm]))
        return put

    late_windows = []

    base = 2 * RET_QK + RET_V
    groups = []
    for g in reversed(range(N_GROUPS)):
        groups.append([(base + which * ATT_W + g * ATT_OUT, put_att(g, which)) for which in range(3)])
    for c in range(RET_QK // MXU_N):
        groups.append([(c * MXU_N, put_ret(rq_ref, c, crq_ref, srq_ref)),
                       (RET_QK + c * MXU_N, put_ret(rk_ref, c, crk_ref, srk_ref))])
    for c in range(RET_V // MXU_N):
        groups.append([(2 * RET_QK + c * MXU_N, put_rv(c))])
    for group in groups:
        results = [mm(col) for col, _ in group]
        for (_, epilogue), t in zip(group, results):
            epilogue(t)
    if side is not None:
        side_finish(side_weights())
    if late_windows:
        @pl.when(pl.program_id(0) % tiles_per_seq == tiles_per_seq - 1)
        def _():
            for g, which, t in late_windows:
                kv_refs[g][0, which - 1] = t.T


def _slab_rows(rows, n_steps):
    tile = 2 * SUBLANES
    for k in range(-(-rows // (n_steps * tile)), rows // tile + 1):
        if rows % (k * tile) == 0:
            return k * tile
    raise ValueError((rows, n_steps))


def _proj(x2d, g_pre, w_qkv, tabs, tm, seq=None, casts=(), side=None):
    m = x2d.shape[0]
    n_steps = m // tm
    n_tab = tabs[0].shape[0] // tm
    row = lambda i: (i, 0)
    tab = lambda i: (i % n_tab, 0)
    in_specs = [pl.BlockSpec((tm, D_MODEL), row), _resident((1, D_MODEL)), _resident((D_MODEL, QKV_COLS))]
    in_specs += [pl.BlockSpec((tm, LANES), tab)] * 4 + [pl.BlockSpec((tm, MXU_N), tab)] * 3
    out_specs = [pl.BlockSpec((tm, w), row) for w in (RET_QK, RET_QK, RET_V)]
    out_shape = [jax.ShapeDtypeStruct((m, w), BF16) for w in (RET_QK, RET_QK, RET_V)]
    scratch = []
    if seq is None:
        widths = (ATT_W,) + (2 * ATT_OUT,) * N_GROUPS
        out_specs += [pl.BlockSpec((tm, w), row) for w in widths]
        out_shape += [jax.ShapeDtypeStruct((m, w), d) for w, d in zip(widths, (BF16,) + (F32,) * N_GROUPS)]
    else:
        batch, tps = m // seq, seq // tm
        for _, dil in ATT_GROUPS:
            assert tm % (dil * 2 * SUBLANES) == 0
            out_specs += [pl.BlockSpec((1, dil, tm // dil, ATT_OUT), lambda i: (i // tps, 0, i % tps, 0))] * 3
            out_shape += [jax.ShapeDtypeStruct((batch, dil, seq // dil, ATT_OUT), BF16)] * 3
        for win, _ in ATT_GROUPS:
            win = min(win, seq)
            if win == seq:
                out_specs.append(pl.BlockSpec((1, 2, ATT_OUT, tm), lambda i: (i // tps, 0, 0, i % tps)))
            else:
                assert win <= tm and win % LANES == 0
                out_specs.append(pl.BlockSpec((1, 2, ATT_OUT, win), lambda i: (i // tps, 0, 0, 0)))
            out_shape.append(jax.ShapeDtypeStruct((batch, 2, ATT_OUT, win), F32))
        scratch = [pltpu.VMEM((MXU_N // LANES, tm, LANES), F32)] * 2
    for w in casts:
        rows, cols = w.shape
        slab = _slab_rows(rows, n_steps)
        spec = pl.BlockSpec((slab, cols), lambda i, n=rows // slab: (i * n // n_steps, 0))
        in_specs.append(spec)
        out_specs.append(spec)
        out_shape.append(jax.ShapeDtypeStruct(w.shape, BF16))
    operands = (x2d, g_pre, w_qkv, *tabs, *casts)
    side_static = None
    if side is not None:
        assert seq is not None
        s_in, s_ops, s_out, s_shape, side_static = _side_job(side, n_steps)
        in_specs, operands = in_specs + s_in, operands + s_ops
        out_specs, out_shape = out_specs + s_out, out_shape + s_shape
    return pl.pallas_call(
        functools.partial(_proj_kernel, tm=tm, by_residue=seq is not None,
                          tiles_per_seq=None if seq is None else seq // tm, n_cast=len(casts), side=side_static),
        grid=(n_steps,),
        in_specs=in_specs,
        out_specs=out_specs,
        out_shape=out_shape,
        scratch_shapes=scratch,
        compiler_params=_params(),
        name="proj",
    )(*operands)


def _ret_kernel(q_ref, k_ref, v_ref, dec_ref, qd_ref, kd_ref, gc_ref, o_ref, st_ref, s_scr, *, n_chunks):
    s_scr[...] = jnp.zeros_like(s_scr)

    def body(c, carry):
        r0 = pl.multiple_of(c * RET_CHUNK, RET_CHUNK)
        rows = pl.ds(r0, RET_CHUNK)
        first = []
        for h in range(RET_HEADS):
            qk = slice(h * RET_QK_DIM, (h + 1) * RET_QK_DIM)
            vv = slice(h * RET_V_DIM, (h + 1) * RET_V_DIM)
            q = q_ref[rows, qk]
            k = k_ref[rows, qk]
            v = v_ref[rows, vv]
            s0 = s_scr[h]
            sc = _dot_nt(q, k)
            from_state = _dot(q, s0.astype(BF16))
            kd_t = (k.astype(F32) * kd_ref[h]).T.astype(BF16)
            first.append((sc, from_state, _dot(kd_t, v), s0, v))
        for h, (sc, from_state, inc, s0, v) in enumerate(first):
            vv = slice(h * RET_V_DIM, (h + 1) * RET_V_DIM)
            o = _dot((sc * dec_ref[h]).astype(BF16), v) + from_state * qd_ref[h]
            s_scr[h] = s0 * gc_ref[h] + inc
            o_ref[rows, vv] = o * lax.rsqrt(jnp.mean(o * o, axis=-1, keepdims=True) + EPS)
        return carry

    lax.fori_loop(0, n_chunks, body, 0, unroll=8)
    st_ref[0] = s_scr[...]


def _log_gamma():
    return np.log1p(-np.exp2(-5.0 - np.arange(RET_HEADS, dtype=np.float64)))


def _ret_tables(chunk):
    lg = _log_gamma()
    idx = np.arange(RET_CHUNK, dtype=np.float64)
    diff = idx[:, None] - idx[None, :]
    dec = np.where(diff[None] >= 0, np.exp(lg[:, None, None] * np.maximum(diff, 0.0)[None]), 0.0)
    qd = np.exp(lg[:, None] * (idx[None, :] + 1.0))
    kd = np.exp(lg[:, None] * (chunk - 1.0 - idx)[None, :])
    gc = np.exp(lg * chunk)
    qd = np.broadcast_to(qd[:, :, None], (RET_HEADS, RET_CHUNK, RET_V_DIM))
    kd = np.broadcast_to(kd[:, :, None], (RET_HEADS, RET_CHUNK, RET_QK_DIM))
    gc = np.broadcast_to(gc[:, None, None], (RET_HEADS, RET_QK_DIM, RET_V_DIM))
    return [jnp.asarray(t, F32) for t in (dec, qd, kd, gc)]


def _ret_prompt(rq, rk, rv, batch, seq):
    dec, qd, kd, gc = _ret_tables(RET_CHUNK)
    row = lambda b: (b, 0)
    return pl.pallas_call(
        functools.partial(_ret_kernel, n_chunks=seq // RET_CHUNK),
        grid=(batch,),
        in_specs=[pl.BlockSpec((seq, RET_QK), row), pl.BlockSpec((seq, RET_QK), row), pl.BlockSpec((seq, RET_V), row),
                  _resident(dec.shape), _resident(qd.shape), _resident(kd.shape), _resident(gc.shape)],
        out_specs=[pl.BlockSpec((seq, RET_V), row),
                   pl.BlockSpec((1, RET_HEADS, RET_QK_DIM, RET_V_DIM), lambda b: (b, 0, 0, 0))],
        out_shape=[jax.ShapeDtypeStruct((batch * seq, RET_V), F32),
                   jax.ShapeDtypeStruct((batch, RET_HEADS, RET_QK_DIM, RET_V_DIM), F32)],
        scratch_shapes=[pltpu.VMEM((RET_HEADS, RET_QK_DIM, RET_V_DIM), F32)],
        compiler_params=_params(),
        name="ret_prompt",
    )(rq, rk, rv, dec, qd, kd, gc)


def _ret_sample_kernel(q_ref, k_ref, kt_ref, v_ref, s_ref, dec_ref, qd_ref, kd_ref, gc_ref, o_ref, st_ref,
                       *, n_tok, n_b):
    pairs = [(b, h) for b in range(n_b) for h in range(RET_HEADS)]
    qk_of = lambda h: slice(h * RET_QK_DIM, (h + 1) * RET_QK_DIM)
    vv_of = lambda h: slice(h * RET_V_DIM, (h + 1) * RET_V_DIM)
    first = []
    for b, h in pairs:
        q = q_ref[b, :, qk_of(h)]
        k = k_ref[b, :, qk_of(h)]
        v = v_ref[b, :, vv_of(h)]
        s0 = s_ref[b, h]
        from_state = _dot(q.astype(BF16), s0.astype(BF16))
        k_dec_t = (kt_ref[b, h] * kd_ref[h]).astype(BF16)
        st_ref[b, h] = s0 * gc_ref[h] + _dot(k_dec_t, v.astype(BF16))
        scores = [jnp.sum(q * k[i:i + 1, :], axis=-1, keepdims=True) for i in range(n_tok)]
        first.append((from_state, scores))
    second = []
    for (b, h), (from_state, scores) in zip(pairs, first):
        v = v_ref[b, :, vv_of(h)]
        o = from_state * qd_ref[h]
        for i, sc_i in enumerate(scores):
            o = o + (sc_i * dec_ref[h][:, i:i + 1]) * v[i:i + 1, :]
        second.append((o, jnp.mean(o * o, axis=-1, keepdims=True)))
    for (b, h), (o, ms) in zip(pairs, second):
        o_ref[b, :, vv_of(h)] = o * lax.rsqrt(ms + EPS)


def _ret_sample(rq, rk, rv, state, n_tok):
    db = state.shape[0]
    pad = RET_PAD - n_tok
    n_b = 8

    def pad_rows(t):
        t = t.astype(F32).reshape(db, n_tok, t.shape[-1])
        return jnp.pad(t, ((0, 0), (0, pad), (0, 0)))

    q, k, v = pad_rows(rq), pad_rows(rk), pad_rows(rv)
    kt = k.reshape(db, RET_PAD, RET_HEADS, RET_QK_DIM).transpose(0, 2, 3, 1)
    lg = _log_gamma()
    idx = np.arange(RET_PAD, dtype=np.float64)
    diff = idx[:, None] - idx[None, :]
    dec = np.where(diff[None] >= 0, np.exp(lg[:, None, None] * np.maximum(diff, 0.0)[None]), 0.0)
    qd = np.broadcast_to(np.exp(lg[:, None] * (idx[None, :] + 1.0))[:, :, None], (RET_HEADS, RET_PAD, RET_V_DIM))
    kd = np.broadcast_to(np.exp(lg[:, None] * (n_tok - 1.0 - idx)[None, :])[:, None, :],
                         (RET_HEADS, RET_QK_DIM, RET_PAD))
    gc = np.broadcast_to(np.exp(lg * n_tok)[:, None, None], (RET_HEADS, RET_QK_DIM, RET_V_DIM))
    dec, qd, kd, gc = (jnp.asarray(t, F32) for t in (dec, qd, kd, gc))
    b3 = lambda i: (i, 0, 0)
    b4 = lambda i: (i, 0, 0, 0)
    o, st = pl.pallas_call(
        functools.partial(_ret_sample_kernel, n_tok=n_tok, n_b=n_b),
        grid=(db // n_b,),
        in_specs=[pl.BlockSpec((n_b, RET_PAD, RET_QK), b3), pl.BlockSpec((n_b, RET_PAD, RET_QK), b3),
                  pl.BlockSpec((n_b, RET_HEADS, RET_QK_DIM, RET_PAD), b4),
                  pl.BlockSpec((n_b, RET_PAD, RET_V), b3),
                  pl.BlockSpec((n_b, RET_HEADS, RET_QK_DIM, RET_V_DIM), b4),
                  _resident(dec.shape), _resident(qd.shape), _resident(kd.shape), _resident(gc.shape)],
        out_specs=[pl.BlockSpec((n_b, RET_PAD, RET_V), b3),
                   pl.BlockSpec((n_b, RET_HEADS, RET_QK_DIM, RET_V_DIM), b4)],
        out_shape=[jax.ShapeDtypeStruct((db, RET_PAD, RET_V), F32),
                   jax.ShapeDtypeStruct(state.shape, F32)],
        compiler_params=_params(),
        name="ret_sample",
    )(q, k, kt, v, state, dec, qd, kd, gc)
    return o[:, :n_tok].reshape(db * n_tok, RET_V), st


def _attn_kernel(q_ref, k_ref, v_ref, o_ref, l_ref, vt_scr, bias_scr, *staging, dil, nb):
    n_blk = dil * nb
    length = nb * BAND_BLOCK
    blk = BAND_BLOCK
    o_dst, l_dst, tmp = staging if dil > 1 else (o_ref, l_ref, None)

    kk = lax.broadcasted_iota(jnp.int32, (2 * blk, 2 * LANES), 0)
    ql = lax.broadcasted_iota(jnp.int32, (2 * blk, 2 * LANES), 1) & (blk - 1)
    in_span = kk <= ql + blk
    bias_scr[...] = jnp.where(in_span & (kk >= ql), 0.0, NEG)

    def transpose_values(c):
        vt_scr[c] = v_ref[0, c * blk:(c + 1) * blk, :].astype(F32).T.astype(BF16)

    lane = lax.broadcasted_iota(jnp.int32, (blk, LANES), 1)

    def scores(c, p):
        lanes = slice(p * LANES, (p + 1) * LANES)
        k0 = 0 if c % nb == 0 else blk
        q128 = q_ref[0, c * blk:(c + 1) * blk, lanes]
        zero = jnp.zeros_like(q128)
        q_pair = jnp.concatenate([jnp.where(lane < ATT_HEAD_DIM, q128, zero),
                                  jnp.where(lane >= ATT_HEAD_DIM, q128, zero)], axis=0)
        keys = k_ref[0, c * blk - k0:(c + 1) * blk, lanes]
        return _dot_nt(keys, q_pair) + bias_scr[blk - k0:2 * blk, :]

    def weights_values(c, p, s_t):
        lanes = slice(p * LANES, (p + 1) * LANES)
        m = jnp.max(s_t, axis=0, keepdims=True)
        p_t = jnp.exp(s_t - m).astype(BF16)
        v_t = vt_scr[c, lanes, :]
        if c % nb:
            v_t = jnp.concatenate([vt_scr[c - 1, lanes, :], v_t], axis=1)
        v_t = jnp.concatenate([v_t, jnp.ones((2 * SUBLANES, v_t.shape[1]), BF16)], axis=0)
        return _dot(v_t, p_t), m

    def finish(c, p, o_t, m):
        den = o_t[LANES:LANES + 1, :]
        lse = m + jnp.log(den)
        o_pair = jnp.concatenate([o_t[0:ATT_HEAD_DIM, 0:blk] / den[:, 0:blk],
                                  o_t[ATT_HEAD_DIM:LANES, blk:2 * blk] / den[:, blk:2 * blk]], axis=0)
        l_pair = jnp.concatenate([jnp.broadcast_to(lse[:, 0:blk], (ATT_HEAD_DIM, blk)),
                                  jnp.broadcast_to(lse[:, blk:2 * blk], (ATT_HEAD_DIM, blk))], axis=0)
        o_dst[p, c * blk:(c + 1) * blk, :] = o_pair.T
        l_dst[p, c * blk:(c + 1) * blk, :] = l_pair.T

    n_pair = ATT_OUT // LANES
    groups = [[(c, p) for c in range(c0, min(c0 + ATTN_GROUP, n_blk)) for p in range(n_pair)]
              for c0 in range(0, n_blk, ATTN_GROUP)]

    def stage_scores(group):
        for c in sorted({c for c, _ in group}):
            transpose_values(c)
        return [scores(*chain) for chain in group]

    s_next = stage_scores(groups[0])
    pending = []
    for i, group in enumerate(groups):
        s_cur = s_next
        if i + 1 < len(groups):
            s_next = stage_scores(groups[i + 1])
        o_m = [weights_values(*chain, s) for chain, s in zip(group, s_cur)]
        for args in pending:
            finish(*args)
        pending = [chain + om for chain, om in zip(group, o_m)]
    for args in pending:
        finish(*args)

    if dil > 1:
        inner = 4
        outer = dil // inner
        two_pass = dil % SUBLANES == 0 and outer % SUBLANES != 0
        for dst_ref, src in ((o_ref, o_dst), (l_ref, l_dst)):
            for p in range(ATT_OUT // LANES):
                if not two_pass:
                    for r in range(dil):
                        dst_ref[p, pl.ds(r, length, stride=dil), :] = src[p, r * length:(r + 1) * length, :]
                    continue
                sub = outer * length
                for b in range(inner):
                    for a in range(outer):
                        r = inner * a + b
                        tmp[p, pl.ds(b * sub + a, length, stride=outer), :] = src[p, r * length:(r + 1) * length, :]
                for b in range(inner):
                    dst_ref[p, pl.ds(b, sub, stride=inner), :] = tmp[p, b * sub:(b + 1) * sub, :]


def _attn_groups_kernel(*refs, geometry, batch):
    n_g = len(geometry)
    ins, outs, scratch = refs[:3 * n_g], refs[3 * n_g:5 * n_g], refs[5 * n_g:]
    group = pl.program_id(0) // batch
    for g, (dil, nb) in enumerate(geometry):
        @pl.when(group == g)
        def _(g=g, dil=dil, nb=nb):
            staging = scratch[2:] if dil > 1 else ()
            _attn_kernel(*ins[3 * g:3 * g + 3], *outs[2 * g:2 * g + 2], scratch[0], scratch[1], *staging,
                         dil=dil, nb=nb)


def _attn_prompt(qkv, batch, seq):
    n_g = len(ATT_GROUPS)
    geometry = tuple((dil, seq // dil // BAND_BLOCK) for _, dil in ATT_GROUPS)
    for _, nb in geometry:
        _log2(nb)
    n_pair = ATT_OUT // LANES
    in_specs, out_specs = [], []
    for g in range(n_g):
        seq_of = lambda i, g=g: jnp.clip(i - g * batch, 0, batch - 1)
        in_specs += [pl.BlockSpec((1, seq, ATT_OUT), lambda i, s=seq_of: (s(i), 0, 0))] * 3
        out_specs += [pl.BlockSpec((n_pair, seq, LANES), lambda i, s=seq_of: (0, s(i), 0))] * 2
    scratch = [pltpu.VMEM((seq // BAND_BLOCK, ATT_OUT, BAND_BLOCK), BF16),
               pltpu.VMEM((2 * BAND_BLOCK, 2 * LANES), F32)] + [pltpu.VMEM((n_pair, seq, LANES), F32)] * 3
    res = pl.pallas_call(
        functools.partial(_attn_groups_kernel, geometry=geometry, batch=batch),
        grid=(n_g * batch,),
        in_specs=in_specs,
        out_specs=out_specs,
        out_shape=[jax.ShapeDtypeStruct((n_pair, batch * seq, LANES), F32)] * (2 * n_g),
        scratch_shapes=scratch,
        compiler_params=_params(),
        name="attn_prompt",
    )(*(t.reshape(batch, seq, ATT_OUT) for t in qkv))
    return res[0::2], res[1::2]


def _attn_sample_kernel(q_ref, c_ref, n_ref, o_ref, l_ref, nc_ref, *, g, n_tok, wb, n_b):
    for bb in range(n_b):
        _attn_sample_one(q_ref, c_ref, n_ref, o_ref, l_ref, nc_ref, bb, g=g, n_tok=n_tok, wb=wb, n_b=n_b)


def _attn_sample_one(q_ref, c_ref, n_ref, o_ref, l_ref, nc_ref, bb, **static):
    weights, finish = _attn_sample_phases(q_ref, c_ref, n_ref, o_ref, l_ref, nc_ref, bb, **static)
    finish(weights())


def _attn_sample_phases(q_ref, c_ref, n_ref, o_ref, l_ref, nc_ref, bb, *, g, n_tok, wb, n_b):
    win, dil = ATT_GROUPS[g]
    nk = win // dil
    _log2(dil)
    n_col = ATT_HEADS * n_tok
    first_new = LANES - n_tok
    seq_lane = ((pl.program_id(0) * n_b + bb) * n_tok) % LANES
    new = [pltpu.roll(n_ref[kv], (first_new + LANES - seq_lane) % LANES, 1) for kv in range(2)]
    lane_t = lax.broadcasted_iota(jnp.int32, (ATT_OUT, LANES), 1)
    for kv in range(2):
        shifted = pltpu.roll(c_ref[bb, kv], wb - n_tok, 1)
        if wb > LANES:
            nc_ref[bb, kv, :, 0:wb - LANES] = shifted[:, 0:wb - LANES]
        nc_ref[bb, kv, :, wb - LANES:wb] = jnp.where(lane_t < first_new, shifted[:, wb - LANES:wb], new[kv])

    q = q_ref[bb][:, g * ATT_OUT:(g + 1) * ATT_OUT]
    row = lax.broadcasted_iota(jnp.int32, (n_col, ATT_OUT), 0)
    lane = lax.broadcasted_iota(jnp.int32, (n_col, ATT_OUT), 1)
    head_sel = (lane >> _log2(ATT_HEAD_DIM)) == (row >> _log2(n_tok))
    qbd = jnp.where(head_sel, q, jnp.zeros_like(q))
    k_c, v_c = c_ref[bb, 0].astype(BF16), c_ref[bb, 1].astype(BF16)
    k_n, v_n = new[0].astype(BF16), new[1].astype(BF16)

    def weights():
        s_c = _dot(qbd, k_c)
        s_n = _dot(qbd, k_n)
        t_c = lax.broadcasted_iota(jnp.int32, (n_col, wb), 0) & (n_tok - 1)
        d_c = wb + t_c - lax.broadcasted_iota(jnp.int32, (n_col, wb), 1)
        ok_c = ((d_c & (dil - 1)) == 0) & (d_c <= nk * dil)
        t_n = lax.broadcasted_iota(jnp.int32, (n_col, LANES), 0) & (n_tok - 1)
        new_idx = lax.broadcasted_iota(jnp.int32, (n_col, LANES), 1) - first_new
        d_n = t_n - new_idx
        ok_n = (new_idx >= 0) & (d_n >= 0) & ((d_n & (dil - 1)) == 0) & (d_n <= nk * dil)
        s_c = jnp.where(ok_c, s_c, NEG)
        s_n = jnp.where(ok_n, s_n, NEG)
        m = jnp.maximum(jnp.max(s_c, axis=-1, keepdims=True), jnp.max(s_n, axis=-1, keepdims=True))
        p_c = jnp.exp(s_c - m)
        p_n = jnp.exp(s_n - m)
        den = jnp.sum(p_c, axis=-1, keepdims=True) + jnp.sum(p_n, axis=-1, keepdims=True)
        return p_c.astype(BF16), p_n.astype(BF16), m, den

    def finish(state):
        p_c, p_n, m, den = state
        o = (_dot_nt(p_c, v_c) + _dot_nt(p_n, v_n)) / den
        lse = jnp.broadcast_to(m + jnp.log(den), (n_col, ATT_OUT))
        o = jnp.where(head_sel, o, 0.0)
        lse = jnp.where(head_sel, lse, 0.0)
        o_t = o[0:n_tok]
        l_t = lse[0:n_tok]
        for h in range(1, ATT_HEADS):
            o_t = o_t + o[h * n_tok:(h + 1) * n_tok]
            l_t = l_t + lse[h * n_tok:(h + 1) * n_tok]
        o_ref[bb] = o_t
        l_ref[bb] = l_t

    return weights, finish


def _attn_sample(q_tiled, cache, kv_new, g, n_tok):
    db, _, _, wb = cache.shape
    n_b = max(1, min(SUBLANES, ATT_GROUPS[-1][0] // wb))
    assert wb % LANES == 0 and db % n_b == 0 and LANES % (n_b * n_tok) == 0 and kv_new.shape[2] % LANES == 0
    b3 = lambda i: (i, 0, 0)
    b4 = lambda i: (i, 0, 0, 0)
    return pl.pallas_call(
        functools.partial(_attn_sample_kernel, g=g, n_tok=n_tok, wb=wb, n_b=n_b),
        grid=(db // n_b,),
        in_specs=[pl.BlockSpec((n_b, ATT_HEADS * n_tok, ATT_W), b3),
                  pl.BlockSpec((n_b, 2, ATT_OUT, wb), b4),
                  pl.BlockSpec((2, ATT_OUT, LANES), lambda i: (0, 0, i * n_b * n_tok // LANES))],
        out_specs=[pl.BlockSpec((n_b, n_tok, ATT_OUT), b3), pl.BlockSpec((n_b, n_tok, ATT_OUT), b3),
                   pl.BlockSpec((n_b, 2, ATT_OUT, wb), b4)],
        out_shape=[jax.ShapeDtypeStruct((db, n_tok, ATT_OUT), F32), jax.ShapeDtypeStruct((db, n_tok, ATT_OUT), F32),
                   jax.ShapeDtypeStruct(cache.shape, F32)],
        compiler_params=_params(),
        name=f"attn_sample_g{g}",
    )(q_tiled, cache, kv_new)


def _merge_kernel(*refs, tm, n_split, permute, side=None):
    if side is not None:
        side_in, refs = refs[:3], refs[3:]
    (x_ref, on_ref, o0_ref, o1_ref, o2_ref, l0_ref, l1_ref, l2_ref, gpre_ref, wg_ref, wbr_ref, wba_ref, wout_ref,
     gpost_ref, gffn_ref, x1_ref, h2_ref), perm = refs[:17], refs[17:]
    if side is not None:
        side_refs, perm = side_in + perm[:3], perm[3:]
    bounds = [i * tm // n_split for i in range(n_split + 1)]
    rows_of = [slice(a, b) for a, b in zip(bounds[:-1], bounds[1:])]
    n_pair = ATT_OUT // LANES

    def gates(rows):
        x = x_ref[rows, :]
        h = _rms(x, gpre_ref[...]).astype(BF16)
        return [_dot(h, wg_ref[:, c:c + D_MODEL]) for c in (0, RET_V, RET_V + D_MODEL)]

    def branches(rows, rg):
        r = (on_ref[rows, :] * (rg * jax.nn.sigmoid(rg))).astype(BF16)
        atts = []
        for p in range(n_pair):
            l0, l1, l2 = l0_ref[p, rows, :], l1_ref[p, rows, :], l2_ref[p, rows, :]
            mx = jnp.maximum(jnp.maximum(l0, l1), l2)
            e0, e1, e2 = jnp.exp(l0 - mx), jnp.exp(l1 - mx), jnp.exp(l2 - mx)
            att = (e0 * o0_ref[p, rows, :] + e1 * o1_ref[p, rows, :] + e2 * o2_ref[p, rows, :]) / (e0 + e1 + e2)
            atts.append(att.astype(BF16))
        return _dot(r, wbr_ref[...]), _dot(jnp.concatenate(atts, axis=1), wba_ref[...])

    def mixed(gr, ga, br, ba):
        mix = jax.nn.sigmoid(gr) * br + jax.nn.sigmoid(ga) * ba
        return _dot(mix.astype(BF16), wout_ref[...])

    def finish(rows, mo):
        x1 = x_ref[rows, :] + _rms(mo, gpost_ref[...])
        x1_ref[rows, :] = x1
        return _rms(x1, gffn_ref[...])

    g = []
    for i, rows in enumerate(rows_of):
        g.append(gates(rows))
        if i == 0 and side is not None:
            side_weights, side_finish = _attn_sample_phases(*side_refs, 0, g=side[0], n_tok=side[1], wb=side[2],
                                                            n_b=1)
    b = [branches(rows, gi[0]) for rows, gi in zip(rows_of, g)]
    if side is not None:
        side_state = side_weights()
    mo = [mixed(gi[1], gi[2], *bi) for gi, bi in zip(g, b)]
    if side is not None:
        side_finish(side_state)
    h2 = [finish(rows, moi) for rows, moi in zip(rows_of, mo)]
    if not permute:
        for rows, h2i in zip(rows_of, h2):
            h2_ref[rows, :] = h2i.astype(BF16)
        return
    perm_ref, = perm
    nv = tm // SUBLANES
    for lc in range(D_MODEL // LANES):
        lanes = slice(lc * LANES, (lc + 1) * LANES)
        for s in range(SUBLANES):
            tok0 = s * nv
            i = max(j for j, b in enumerate(bounds[:-1]) if b <= tok0)
            off = tok0 - bounds[i]
            assert tok0 + nv <= bounds[i + 1]
            perm_ref[lc, pl.ds(s, nv, stride=SUBLANES), :] = h2[i][off:off + nv, lanes]
        h2_ref[:, lanes] = perm_ref[lc].astype(BF16)


def _side_job(side, n_steps):
    g, n_tok, q_tiled, cache, kv_new = side
    db, _, _, wb = cache.shape
    assert db == n_steps and LANES % n_tok == 0, (db, n_steps)
    b3 = lambda i: (i, 0, 0)
    b4 = lambda i: (i, 0, 0, 0)
    in_specs = [pl.BlockSpec((1, ATT_HEADS * n_tok, ATT_W), b3), pl.BlockSpec((1, 2, ATT_OUT, wb), b4),
                pl.BlockSpec((2, ATT_OUT, LANES), lambda i: (0, 0, i * n_tok // LANES))]
    out_specs = [pl.BlockSpec((1, n_tok, ATT_OUT), b3), pl.BlockSpec((1, n_tok, ATT_OUT), b3),
                 pl.BlockSpec((1, 2, ATT_OUT, wb), b4)]
    out_shape = [jax.ShapeDtypeStruct((db, n_tok, ATT_OUT), F32)] * 2 + [jax.ShapeDtypeStruct(cache.shape, F32)]
    return in_specs, (q_tiled, cache, kv_new), out_specs, out_shape, (g, n_tok, wb)


def _merge(x2d, on, outs, lses, g_pre, w_gates, w_br, w_ba, w_out, g_post, g_ffn, tm, permute, side=None):
    m = x2d.shape[0]
    n_split = 2 if tm >= 2 * MXU_N else 1
    scratch = [pltpu.VMEM((D_MODEL // LANES, tm, LANES), F32)] if permute else []
    row = lambda i: (i, 0)
    vec = _resident((1, D_MODEL))
    att_spec = pl.BlockSpec((ATT_OUT // LANES, tm, LANES), lambda i: (0, i, 0))
    in_specs = [pl.BlockSpec((tm, D_MODEL), row), pl.BlockSpec((tm, RET_V), row)] + [att_spec] * 6 + [
        vec, _resident(w_gates.shape), _resident(w_br.shape), _resident(w_ba.shape), _resident(w_out.shape),
        vec, vec]
    out_specs = [pl.BlockSpec((tm, D_MODEL), row), pl.BlockSpec((tm, D_MODEL), row)]
    out_shape = [jax.ShapeDtypeStruct((m, D_MODEL), F32), jax.ShapeDtypeStruct((m, D_MODEL), BF16)]
    operands = (x2d, on, *outs, *lses, g_pre, w_gates, w_br, w_ba, w_out, g_post, g_ffn)
    side_static = None
    if side is not None:
        s_in, s_ops, s_out, s_shape, side_static = _side_job(side, m // tm)
        in_specs, operands = s_in + in_specs, s_ops + operands
        out_specs, out_shape = out_specs + s_out, out_shape + s_shape
    return pl.pallas_call(
        functools.partial(_merge_kernel, tm=tm, n_split=n_split, permute=permute, side=side_static),
        grid=(m // tm,),
        in_specs=in_specs,
        out_specs=out_specs,
        out_shape=out_shape,
        scratch_shapes=scratch,
        compiler_params=_params(),
        name="merge",
    )(*operands)


def _ffn_kernel(*refs, tm, tiles_per_seq, seq_tok, side=None):
    if side is not None:
        (sq_ref, sc_ref, sn_ref), refs = refs[:3], refs[3:]
        so_ref, sl_ref, snc_ref = refs[9:12]
        refs = refs[:9] + refs[12:]
        side_refs = (sq_ref, sc_ref, sn_ref, so_ref, sl_ref, snc_ref)
    if seq_tok is None:
        h2_ref, x1_ref, wup_ref, wdn_ref, cw_ref, cb_ref, g_ref, y_ref, cs_ref, carry_ref, f_ref, perm_ref = refs
    else:
        h2_ref, x1_ref, p1_ref, p2_ref, wup_ref, wdn_ref, cw_ref, cb_ref, g_ref, y_ref, u_ref, f_ref = refs
    h2 = h2_ref[...]
    if seq_tok is None:
        sub = lax.broadcasted_iota(jnp.int32, (SUBLANES, FF_CHUNK), 0)

        @pl.when(pl.program_id(0) % tiles_per_seq == 0)
        def _():
            carry_ref[...] = jnp.zeros_like(carry_ref)
    else:
        t = lax.broadcasted_iota(jnp.int32, (tm, FF_CHUNK), 0) & (seq_tok - 1)
        _log2(seq_tok)

    def up(j):
        return [_dot(h2, wup_ref[:, c:c + FF_CHUNK]) for c in (j * FF_CHUNK, D_FF + j * FF_CHUNK)]

    def conv(u, col):
        cols = slice(col, col + FF_CHUNK)
        if seq_tok is None:
            prev = carry_ref[:, cols]
            last2 = jnp.where(sub == 0, prev[SUBLANES - 1:SUBLANES], pltpu.roll(u[tm - 2 * SUBLANES:tm - SUBLANES], 1, 0))
            last1 = jnp.where(sub == 0, prev[2 * SUBLANES - 1:2 * SUBLANES], pltpu.roll(u[tm - SUBLANES:tm], 1, 0))
            u1 = jnp.concatenate([last1, u[0:tm - SUBLANES]], axis=0)
            u2 = jnp.concatenate([last2, last1, u[0:tm - 2 * SUBLANES]], axis=0)
            carry_ref[:, cols] = u[tm - 2 * SUBLANES:tm]
        else:
            u1 = jnp.where(t == 0, p1_ref[:, cols], pltpu.roll(u, 1, 0))
            u2 = jnp.where(t <= 1, p2_ref[:, cols], pltpu.roll(u, 2, 0))
            u_ref[:, cols] = u
        cw = cw_ref[:, cols]
        return cb_ref[:, cols] + cw[0:1] * u2 + cw[1:2] * u1 + cw[2:3] * u

    def gate(j, ug, uv):
        cg = conv(ug, j * FF_CHUNK)
        cv = conv(uv, D_FF + j * FF_CHUNK)
        f_ref[:, j * FF_CHUNK:(j + 1) * FF_CHUNK] = (jax.nn.gelu(cg, approximate=True) * cv).astype(BF16)

    n_chunks = D_FF // FF_CHUNK
    u_next = up(0)
    for j in range(n_chunks):
        u_cur = u_next
        if j + 1 < n_chunks:
            u_next = up(j + 1)
        if j == 0 and side is not None:
            side_weights, side_finish = _attn_sample_phases(*side_refs, 0, g=side[0], n_tok=side[1], wb=side[2],
                                                            n_b=1)
        gate(j, *u_cur)
    if side is not None:
        side_state = side_weights()
    down = _dot(f_ref[...], wdn_ref[...])
    if side is not None:
        side_finish(side_state)
    out = _rms(down, g_ref[...])
    if seq_tok is None:
        nv = tm // SUBLANES
        for lc in range(D_MODEL // LANES):
            lanes = slice(lc * LANES, (lc + 1) * LANES)
            perm_ref[lc] = out[:, lanes]
            for s in range(SUBLANES):
                rows = slice(s * nv, (s + 1) * nv)
                y_ref[rows, lanes] = x1_ref[rows, lanes] + perm_ref[lc, pl.ds(s, nv, stride=SUBLANES), :]

        @pl.when(pl.program_id(0) % tiles_per_seq == tiles_per_seq - 1)
        def _():
            cs_ref[0, 0:1, :] = carry_ref[SUBLANES - 1:SUBLANES, :]
            cs_ref[0, 1:2, :] = carry_ref[2 * SUBLANES - 1:2 * SUBLANES, :]
    else:
        y_ref[...] = x1_ref[...] + out


def _ffn_prompt(h2, x1, w_up, w_dn, conv_w, conv_b, g_post, batch, seq, tm, side=None):
    m = h2.shape[0]
    tps = seq // tm
    n_steps = m // tm
    row = lambda i: (i, 0)
    in_specs = [pl.BlockSpec((tm, D_MODEL), row), pl.BlockSpec((tm, D_MODEL), row),
                _resident(w_up.shape), _resident(w_dn.shape), _resident(conv_w.shape), _resident(conv_b.shape),
                _resident((1, D_MODEL))]
    out_specs = [pl.BlockSpec((tm, D_MODEL), row),
                 pl.BlockSpec((1, CONV_W - 1, 2 * D_FF), lambda i: (i // tps, 0, 0))]
    out_shape = [jax.ShapeDtypeStruct((m, D_MODEL), F32),
                 jax.ShapeDtypeStruct((batch, CONV_W - 1, 2 * D_FF), F32)]
    operands = (h2, x1, w_up, w_dn, conv_w, conv_b, g_post)
    side_static = None
    if side is not None:
        s_in, s_ops, s_out, s_shape, side_static = _side_job(side, n_steps)
        in_specs, operands = s_in + in_specs, s_ops + operands
        out_specs, out_shape = out_specs + s_out, out_shape + s_shape
    return pl.pallas_call(
        functools.partial(_ffn_kernel, tm=tm, tiles_per_seq=tps, seq_tok=None, side=side_static),
        grid=(n_steps,),
        in_specs=in_specs,
        out_specs=out_specs,
        out_shape=out_shape,
        scratch_shapes=[pltpu.VMEM((2 * SUBLANES, 2 * D_FF), F32), pltpu.VMEM((tm, D_FF), BF16),
                        pltpu.VMEM((D_MODEL // LANES, tm, LANES), F32)],
        compiler_params=_params(),
        name="ffn_prompt",
    )(*operands)


def _ffn_sample(h2, x1, p1, p2, w_up, w_dn, conv_w, conv_b, g_post, n_tok):
    m = h2.shape[0]
    const = lambda i: (0, 0)
    full = lambda a: _resident(a.shape)
    return pl.pallas_call(
        functools.partial(_ffn_kernel, tm=m, tiles_per_seq=1, seq_tok=n_tok),
        grid=(1,),
        in_specs=[full(h2), full(x1), full(p1), full(p2), full(w_up), full(w_dn), full(conv_w), full(conv_b),
                  _resident((1, D_MODEL))],
        out_specs=[pl.BlockSpec((m, D_MODEL), const), pl.BlockSpec((m, 2 * D_FF), const)],
        out_shape=[jax.ShapeDtypeStruct((m, D_MODEL), F32), jax.ShapeDtypeStruct((m, 2 * D_FF), F32)],
        scratch_shapes=[pltpu.VMEM((m, D_FF), BF16)],
        compiler_params=_params(),
        name="ffn_sample",
    )(h2, x1, p1, p2, w_up, w_dn, conv_w, conv_b, g_post)


def _rot_tables(pos):
    pos = np.asarray(pos, np.float64)
    rf = RET_THETA ** (-np.linspace(0.0, 1.0, RET_QK_DIM // 2))
    ang = pos[:, None] * rf[None, :]
    cos, sin = np.cos(ang), np.sin(ang)
    cr = np.concatenate([cos, cos], axis=-1)
    sr = np.concatenate([-sin, sin], axis=-1)
    k_scale = RET_QK_DIM ** -0.5
    half = ROPE_DIM // 2
    af = ROPE_THETA ** (-np.arange(half) / half)
    ang = pos[:, None] * af[None, :]
    cos, sin = np.cos(ang), np.sin(ang)
    n = pos.shape[0]
    rest = ATT_HEAD_DIM - ROPE_DIM
    zh = np.zeros((n, half))
    ca = np.concatenate([cos, cos, np.ones((n, rest))], axis=-1)
    sa1 = np.concatenate([zh, sin, np.zeros((n, rest))], axis=-1)
    sa2 = np.concatenate([-sin, zh, np.zeros((n, rest))], axis=-1)
    rep = MXU_N // ATT_HEAD_DIM
    ca, sa1, sa2 = (np.tile(t, (1, rep)) for t in (ca, sa1, sa2))
    return [jnp.asarray(t, F32) for t in (cr, sr, cr * k_scale, sr * k_scale, ca, sa1, sa2)]


def kernel(x_prompt, x_sample, cache_kv_g0, cache_kv_g1, cache_kv_g2, state_ret, state_conv, norm_mix_pre, w_in,
           w_branch_ret, w_branch_attn, w_out, norm_mix_post, norm_ffn_pre, w_ffn_up, conv_w, conv_b, w_ffn_down,
           norm_ffn_post):
    batch, seq, _ = x_prompt.shape
    db, n_tok, _ = x_sample.shape
    depth = w_in.shape[0]
    assert depth == 1 and seq % (ATT_GROUPS[-1][1] * BAND_BLOCK) == 0 and n_tok <= SUBLANES
    caches = (cache_kv_g0, cache_kv_g1, cache_kv_g2)

    wi = w_in[0]
    o_rg = 2 * RET_QK + RET_V
    o_aq = o_rg + RET_V
    o_gr = o_aq + 3 * ATT_W
    w_qkv = jnp.concatenate([wi[:, :o_rg], wi[:, o_aq:o_aq + ATT_W] * ATT_HEAD_DIM ** -0.5,
                             wi[:, o_aq + ATT_W:o_gr]], axis=1).astype(BF16)
    w_gates = jnp.concatenate([wi[:, o_rg:o_aq], wi[:, o_gr:]], axis=1).astype(BF16)
    later_weights = (w_branch_ret[0], w_branch_attn[0], w_out[0], w_ffn_up[0], w_ffn_down[0])
    cb = conv_b[0][None, :]
    cw = conv_w[0]
    g_pre, g_post, g_ffn, g_post2 = (t[0][None, :] for t in (norm_mix_pre, norm_mix_post, norm_ffn_pre, norm_ffn_post))

    tm = PROMPT_TILE
    ms = db * n_tok
    xs = x_sample.reshape(ms, D_MODEL)
    pos_s = np.tile(PAST_LEN + np.arange(n_tok), db)
    rq_s, rk_s, rv_s, aq, kv0, kv1, kv2 = _proj(xs, g_pre, w_qkv, _rot_tables(pos_s), ms)
    kvs = (kv0, kv1, kv2)
    q_tiled = jnp.tile(aq.reshape(db, n_tok, ATT_W), (1, ATT_HEADS, 1))
    sides = []
    for g in range(N_GROUPS):
        cache = caches[g][0]
        wb = cache.shape[1]
        cache_t = cache.transpose(0, 2, 3, 4, 1).reshape(db, 2, ATT_OUT, wb)
        kv_new = kvs[g].reshape(ms, 2, ATT_OUT).transpose(1, 2, 0)
        sides.append((g, n_tok, q_tiled, cache_t, kv_new))
    ride = batch * seq // tm == db and N_GROUPS == 3
    sample_attn = {} if ride else {g: _attn_sample(*sides[g][2:], g, n_tok) for g in range(N_GROUPS)}

    xp = x_prompt.reshape(batch * seq, D_MODEL)
    proj_out = _proj(xp, g_pre, w_qkv, _rot_tables(np.arange(seq)), tm, seq=seq, casts=later_weights,
                     side=sides[0] if ride else None)
    rq, rk, rv = proj_out[:3]
    qkv_res = proj_out[3:3 + 3 * N_GROUPS]
    kv_last = proj_out[3 + 3 * N_GROUPS:3 + 4 * N_GROUPS]
    n_main = 3 + 4 * N_GROUPS + len(later_weights)
    w_br, w_ba, w_o, w_up, w_dn = proj_out[3 + 4 * N_GROUPS:n_main]
    if ride:
        sample_attn[0] = proj_out[n_main:]
    on, p_ret = _ret_prompt(rq, rk, rv, batch, seq)
    outs, lses = _attn_prompt(qkv_res, batch, seq)
    p_kv =[t.reshape(batch, 2, ATT_HEADS, ATT_HEAD_DIM, t.shape[3]).transpose(0, 4, 1, 2, 3)[None] for t in kv_last]
    x1, h2, *rest = _merge(xp, on, outs, lses, g_pre, w_gates, w_br, w_ba, w_o, g_post, g_ffn, tm, permute=True,
                           side=sides[1] if ride else None)
    if ride:
        sample_attn[1] = rest
    y_p, p_conv, *rest = _ffn_prompt(h2, x1, w_up, w_dn, cw, cb, g_post2, batch, seq, tm,
                                     side=sides[2] if ride else None)
    if ride:
        sample_attn[2] = rest

    on_s, s_ret = _ret_sample(rq_s, rk_s, rv_s, state_ret[0], n_tok)
    outs, lses, s_kv = [], [], []
    for g in range(N_GROUPS):
        o_g, l_g, nc = sample_attn[g]
        outs.append(o_g.reshape(ms, ATT_OUT // LANES, LANES).transpose(1, 0, 2))
        lses.append(l_g.reshape(ms, ATT_OUT // LANES, LANES).transpose(1, 0, 2))
        s_kv.append(nc.reshape(db, 2, ATT_HEADS, ATT_HEAD_DIM, nc.shape[3]).transpose(0, 4, 1, 2, 3)[None])
    x1, h2 = _merge(xs, on_s, outs, lses, g_pre, w_gates, w_br, w_ba, w_o, g_post, g_ffn, ms, permute=False)
    st = state_conv[0]
    zeros = jnp.zeros((db, n_tok - 1, 2 * D_FF), F32)
    p1 = jnp.concatenate([st[:, 1:2], zeros], axis=1).reshape(ms, 2 * D_FF)
    p2 = jnp.concatenate([st, zeros[:, :n_tok - 2]], axis=1).reshape(ms, 2 * D_FF)
    y_s, u_s = _ffn_sample(h2, x1, p1, p2, w_up, w_dn, cw, cb, g_post2, n_tok)
    s_conv = u_s.reshape(db, n_tok, 2 * D_FF)[:, n_tok - (CONV_W - 1):][None]

    return (y_p.reshape(x_prompt.shape), y_s.reshape(x_sample.shape), p_kv[0], p_kv[1], p_kv[2], p_ret[None],
            p_conv[None], s_kv[0], s_kv[1], s_kv[2], s_ret[None], s_conv)
```

```python
import functools

import numpy as np
import jax
import jax.numpy as jnp
from jax import lax
from jax.experimental import pallas as pl
from jax.experimental.pallas import tpu as pltpu

F32 = jnp.float32
BF16 = jnp.bfloat16

D_MODEL = 1024
PAST_LEN = 16384
RET_HEADS = 4
RET_QK_DIM = 128
RET_V_DIM = 256
RET_CHUNK = 128
RET_THETA = 10000.0
RET_QK = RET_HEADS * RET_QK_DIM
RET_V = RET_HEADS * RET_V_DIM
ATT_GROUPS = ((128, 1), (512, 4), (2048, 16))
N_GROUPS = 3
ATT_HEADS = 4
ATT_HEAD_DIM = 64
ROPE_DIM = ATT_HEAD_DIM // 4
ROPE_THETA = 500000.0
BAND_BLOCK = 128
ATT_OUT = ATT_HEADS * ATT_HEAD_DIM
ATT_W = N_GROUPS * ATT_OUT
D_FF = 2816
CONV_W = 3
EPS = 1e-6

LANES = 128
SUBLANES = 8
MXU_N = 256
VMEM_LIMIT = 56 * 1024 * 1024
PROMPT_TILE = 512
NEG = -1e30
FF_CHUNK = MXU_N
ATTN_GROUP = 2
RET_PAD = 2 * SUBLANES
QKV_COLS = 2 * RET_QK + RET_V + 3 * ATT_W


def _rms(x, g):
    return x * lax.rsqrt(jnp.mean(x * x, axis=-1, keepdims=True) + EPS) * g


def _dot(a, b):
    return jnp.dot(a, b, preferred_element_type=F32)


def _dot_nt(a, b):
    return lax.dot_general(a, b, (((1,), (1,)), ((), ())), preferred_element_type=F32)


def _resident(shape):
    return pl.BlockSpec(shape, lambda *_: (0,) * len(shape), pipeline_mode=pl.Buffered(1))


def _log2(n):
    assert n > 0 and n & (n - 1) == 0, n
    return n.bit_length() - 1


def _params(n_axes=1):
    return pltpu.CompilerParams(dimension_semantics=("arbitrary",) * n_axes, vmem_limit_bytes=VMEM_LIMIT)


def _proj_kernel(x_ref, g_ref, w_ref, crq_ref, srq_ref, crk_ref, srk_ref, ca_ref, sa1_ref, sa2_ref, *rest,
                 tm, by_residue, tiles_per_seq, n_cast, side=None):
    cast_in, rest = rest[:n_cast], rest[n_cast:]
    if side is not None:
        side_in, rest = rest[:3], rest[3:]
        n_out = 3 + 4 * N_GROUPS + n_cast
        side_weights, side_finish = _attn_sample_phases(*side_in, *rest[n_out:n_out + 3], 0, g=side[0],
                                                        n_tok=side[1], wb=side[2], n_b=1)
        rest = rest[:n_out] + rest[n_out + 3:]
    rq_ref, rk_ref, rv_ref = rest[:3]
    if by_residue:
        res_refs = rest[3:3 + 3 * N_GROUPS]
        kv_refs = rest[3 + 3 * N_GROUPS:3 + 4 * N_GROUPS]
        cast_out = rest[3 + 4 * N_GROUPS:3 + 4 * N_GROUPS + n_cast]
        scr, scr2 = rest[3 + 4 * N_GROUPS + n_cast:3 + 4 * N_GROUPS + n_cast + 2]
    else:
        aq_ref = rest[3]
        kv_refs = rest[4:4 + N_GROUPS]
        cast_out = rest[4 + N_GROUPS:4 + N_GROUPS + n_cast]
    for src, dst in zip(cast_in, cast_out):
        dst[...] = src[...].astype(BF16)
    h = _rms(x_ref[...], g_ref[...]).astype(BF16)

    def mm(c0, width=MXU_N):
        return _dot(h, w_ref[:, c0:c0 + width])

    def rot_ret(t, c, s):
        return t * c + pltpu.roll(t, RET_QK_DIM // 2, 1) * s

    ca, sa1, sa2 = ca_ref[...], sa1_ref[...], sa2_ref[...]

    def rot_att(t):
        return t * ca + pltpu.roll(t, ROPE_DIM // 2, 1) * sa1 + pltpu.roll(t, MXU_N - ROPE_DIM // 2, 1) * sa2

    def put_by_residue(ref, val, dil):
        if dil == 1:
            ref[0, 0] = val.astype(BF16)
            return
        inner = 4
        outer = dil // inner
        two_pass = dil % SUBLANES == 0 and outer % SUBLANES != 0
        for half in range(MXU_N // LANES):
            lanes = slice(half * LANES, (half + 1) * LANES)
            scr[half] = val[:, lanes]
            if not two_pass:
                for r in range(dil):
                    ref[0, r, :, lanes] = scr[half, pl.ds(r, tm // dil, stride=dil), :].astype(BF16)
                continue
            sub = tm // inner
            for b in range(inner):
                scr2[half, b * sub:(b + 1) * sub] = scr[half, pl.ds(b, sub, stride=inner), :]
            for b in range(inner):
                for a in range(outer):
                    ref[0, inner * a + b, :, lanes] = scr2[half, pl.ds(b * sub + a, tm // dil, stride=outer),
                                                           :].astype(BF16)

    def put_ret(ref, c, c_ref, s_ref):
        def put(t):
            for half in range(MXU_N // LANES):
                lo = half * LANES
                col = c * MXU_N + lo
                ref[:, col:col + LANES] = rot_ret(t[:, lo:lo + LANES], c_ref[...], s_ref[...]).astype(BF16)
        return put

    def put_rv(c):
        def put(t):
            rv_ref[:, c * MXU_N:(c + 1) * MXU_N] = t.astype(BF16)
        return put

    def put_att(g, which):
        def put(t):
            if which < 2:
                t = rot_att(t)
            if not by_residue:
                if which == 0:
                    aq_ref[:, g * ATT_OUT:(g + 1) * ATT_OUT] = t.astype(BF16)
                else:
                    kv_refs[g][:, (which - 1) * ATT_OUT:which * ATT_OUT] = t
                return
            put_by_residue(res_refs[3 * g + which], t, ATT_GROUPS[g][1])
            if which == 0:
                return
            rows = kv_refs[g].shape[3]
            if ATT_GROUPS[g][0] >= tiles_per_seq * tm:
                kv_refs[g][0, which - 1] = t[tm - rows:tm].T
            else:
                late_windows.append((g, which, t[tm - rows:tm]))
        return put

    late_windows = []

    base = 2 * RET_QK + RET_V
    groups = []
    for g in reversed(range(N_GROUPS)):
        groups.append([(base + which * ATT_W + g * ATT_OUT, put_att(g, which)) for which in range(3)])
    for c in range(RET_QK // MXU_N):
        groups.append([(c * MXU_N, put_ret(rq_ref, c, crq_ref, srq_ref)),
                       (RET_QK + c * MXU_N, put_ret(rk_ref, c, crk_ref, srk_ref))])
    for c in range(RET_V // MXU_N):
        groups.append([(2 * RET_QK + c * MXU_N, put_rv(c))])
    for group in groups:
        results = [mm(col) for col, _ in group]
        for (_, epilogue), t in zip(group, results):
            epilogue(t)
    if side is not None:
        side_finish(side_weights())
    if late_windows:
        @pl.when(pl.program_id(0) % tiles_per_seq == tiles_per_seq - 1)
        def _():
            for g, which, t in late_windows:
                kv_refs[g][0, which - 1] = t.T


def _slab_rows(rows, n_steps):
    tile = 2 * SUBLANES
    for k in range(-(-rows // (n_steps * tile)), rows // tile + 1):
        if rows % (k * tile) == 0:
            return k * tile
    raise ValueError((rows, n_steps))


def _proj(x2d, g_pre, w_qkv, tabs, tm, seq=None, casts=(), side=None):
    m = x2d.shape[0]
    n_steps = m // tm
    n_tab = tabs[0].shape[0] // tm
    row = lambda i: (i, 0)
    tab = lambda i: (i % n_tab, 0)
    in_specs = [pl.BlockSpec((tm, D_MODEL), row), _resident((1, D_MODEL)), _resident((D_MODEL, QKV_COLS))]
    in_specs += [pl.BlockSpec((tm, LANES), tab)] * 4 + [pl.BlockSpec((tm, MXU_N), tab)] * 3
    out_specs = [pl.BlockSpec((tm, w), row) for w in (RET_QK, RET_QK, RET_V)]
    out_shape = [jax.ShapeDtypeStruct((m, w), BF16) for w in (RET_QK, RET_QK, RET_V)]
    scratch = []
    if seq is None:
        widths = (ATT_W,) + (2 * ATT_OUT,) * N_GROUPS
        out_specs += [pl.BlockSpec((tm, w), row) for w in widths]
        out_shape += [jax.ShapeDtypeStruct((m, w), d) for w, d in zip(widths, (BF16,) + (F32,) * N_GROUPS)]
    else:
        batch, tps = m // seq, seq // tm
        for _, dil in ATT_GROUPS:
            assert tm % (dil * 2 * SUBLANES) == 0
            out_specs += [pl.BlockSpec((1, dil, tm // dil, ATT_OUT), lambda i: (i // tps, 0, i % tps, 0))] * 3
            out_shape += [jax.ShapeDtypeStruct((batch, dil, seq // dil, ATT_OUT), BF16)] * 3
        for win, _ in ATT_GROUPS:
            win = min(win, seq)
            if win == seq:
                out_specs.append(pl.BlockSpec((1, 2, ATT_OUT, tm), lambda i: (i // tps, 0, 0, i % tps)))
            else:
                assert win <= tm and win % LANES == 0
                out_specs.append(pl.BlockSpec((1, 2, ATT_OUT, win), lambda i: (i // tps, 0, 0, 0)))
            out_shape.append(jax.ShapeDtypeStruct((batch, 2, ATT_OUT, win), F32))
        scratch = [pltpu.VMEM((MXU_N // LANES, tm, LANES), F32)] * 2
    for w in casts:
        rows, cols = w.shape
        slab = _slab_rows(rows, n_steps)
        spec = pl.BlockSpec((slab, cols), lambda i, n=rows // slab: (i * n // n_steps, 0))
        in_specs.append(spec)
        out_specs.append(spec)
        out_shape.append(jax.ShapeDtypeStruct(w.shape, BF16))
    operands = (x2d, g_pre, w_qkv, *tabs, *casts)
    side_static = None
    if side is not None:
        assert seq is not None
        s_in, s_ops, s_out, s_shape, side_static = _side_job(side, n_steps)
        in_specs, operands = in_specs + s_in, operands + s_ops
        out_specs, out_shape = out_specs + s_out, out_shape + s_shape
    return pl.pallas_call(
        functools.partial(_proj_kernel, tm=tm, by_residue=seq is not None,
                          tiles_per_seq=None if seq is None else seq // tm, n_cast=len(casts), side=side_static),
        grid=(n_steps,),
        in_specs=in_specs,
        out_specs=out_specs,
        out_shape=out_shape,
        scratch_shapes=scratch,
        compiler_params=_params(),
        name="proj",
    )(*operands)


def _ret_kernel(q_ref, k_ref, v_ref, dec_ref, qd_ref, kd_ref, gc_ref, o_ref, st_ref, s_scr, *, n_chunks):
    s_scr[...] = jnp.zeros_like(s_scr)

    def body(c, carry):
        r0 = pl.multiple_of(c * RET_CHUNK, RET_CHUNK)
        rows = pl.ds(r0, RET_CHUNK)
        first = []
        for h in range(RET_HEADS):
            qk = slice(h * RET_QK_DIM, (h + 1) * RET_QK_DIM)
            vv = slice(h * RET_V_DIM, (h + 1) * RET_V_DIM)
            q = q_ref[rows, qk]
            k = k_ref[rows, qk]
            v = v_ref[rows, vv]
            s0 = s_scr[h]
            sc = _dot_nt(q, k)
            from_state = _dot(q, s0.astype(BF16))
            k_dec = (k.astype(F32) * kd_ref[h]).astype(BF16)
            inc = lax.dot_general(k_dec, v, (((0,), (0,)), ((), ())), preferred_element_type=F32)
            first.append((sc, from_state, inc, s0, v))
        for h, (sc, from_state, inc, s0, v) in enumerate(first):
            vv = slice(h * RET_V_DIM, (h + 1) * RET_V_DIM)
            o = _dot((sc * dec_ref[h]).astype(BF16), v) + from_state * qd_ref[h]
            s_scr[h] = s0 * gc_ref[h] + inc
            o_ref[rows, vv] = o * lax.rsqrt(jnp.mean(o * o, axis=-1, keepdims=True) + EPS)
        return carry

    lax.fori_loop(0, n_chunks, body, 0, unroll=8)
    st_ref[0] = s_scr[...]


def _log_gamma():
    return np.log1p(-np.exp2(-5.0 - np.arange(RET_HEADS, dtype=np.float64)))


def _ret_tables(chunk):
    lg = _log_gamma()
    idx = np.arange(RET_CHUNK, dtype=np.float64)
    diff = idx[:, None] - idx[None, :]
    dec = np.where(diff[None] >= 0, np.exp(lg[:, None, None] * np.maximum(diff, 0.0)[None]), 0.0)
    qd = np.exp(lg[:, None] * (idx[None, :] + 1.0))
    kd = np.exp(lg[:, None] * (chunk - 1.0 - idx)[None, :])
    gc = np.exp(lg * chunk)
    qd = np.broadcast_to(qd[:, :, None], (RET_HEADS, RET_CHUNK, RET_V_DIM))
    kd = np.broadcast_to(kd[:, :, None], (RET_HEADS, RET_CHUNK, RET_QK_DIM))
    gc = np.broadcast_to(gc[:, None, None], (RET_HEADS, RET_QK_DIM, RET_V_DIM))
    return [jnp.asarray(t, F32) for t in (dec, qd, kd, gc)]


def _ret_prompt(rq, rk, rv, batch, seq):
    dec, qd, kd, gc = _ret_tables(RET_CHUNK)
    row = lambda b: (b, 0)
    return pl.pallas_call(
        functools.partial(_ret_kernel, n_chunks=seq // RET_CHUNK),
        grid=(batch,),
        in_specs=[pl.BlockSpec((seq, RET_QK), row), pl.BlockSpec((seq, RET_QK), row), pl.BlockSpec((seq, RET_V), row),
                  _resident(dec.shape), _resident(qd.shape), _resident(kd.shape), _resident(gc.shape)],
        out_specs=[pl.BlockSpec((seq, RET_V), row),
                   pl.BlockSpec((1, RET_HEADS, RET_QK_DIM, RET_V_DIM), lambda b: (b, 0, 0, 0))],
        out_shape=[jax.ShapeDtypeStruct((batch * seq, RET_V), F32),
                   jax.ShapeDtypeStruct((batch, RET_HEADS, RET_QK_DIM, RET_V_DIM), F32)],
        scratch_shapes=[pltpu.VMEM((RET_HEADS, RET_QK_DIM, RET_V_DIM), F32)],
        compiler_params=_params(),
        name="ret_prompt",
    )(rq, rk, rv, dec, qd, kd, gc)


def _ret_sample_kernel(q_ref, k_ref, kt_ref, v_ref, s_ref, dec_ref, qd_ref, kd_ref, gc_ref, o_ref, st_ref,
                       *, n_tok, n_b):
    pairs = [(b, h) for b in range(n_b) for h in range(RET_HEADS)]
    qk_of = lambda h: slice(h * RET_QK_DIM, (h + 1) * RET_QK_DIM)
    vv_of = lambda h: slice(h * RET_V_DIM, (h + 1) * RET_V_DIM)
    first = []
    for b, h in pairs:
        q = q_ref[b, :, qk_of(h)]
        k = k_ref[b, :, qk_of(h)]
        v = v_ref[b, :, vv_of(h)]
        s0 = s_ref[b, h]
        from_state = _dot(q.astype(BF16), s0.astype(BF16))
        k_dec_t = (kt_ref[b, h] * kd_ref[h]).astype(BF16)
        st_ref[b, h] = s0 * gc_ref[h] + _dot(k_dec_t, v.astype(BF16))
        scores = [jnp.sum(q * k[i:i + 1, :], axis=-1, keepdims=True) for i in range(n_tok)]
        first.append((from_state, scores))
    second = []
    for (b, h), (from_state, scores) in zip(pairs, first):
        v = v_ref[b, :, vv_of(h)]
        o = from_state * qd_ref[h]
        for i, sc_i in enumerate(scores):
            o = o + (sc_i * dec_ref[h][:, i:i + 1]) * v[i:i + 1, :]
        second.append((o, jnp.mean(o * o, axis=-1, keepdims=True)))
    for (b, h), (o, ms) in zip(pairs, second):
        o_ref[b, :, vv_of(h)] = o * lax.rsqrt(ms + EPS)


def _ret_sample(rq, rk, rv, state, n_tok):
    db = state.shape[0]
    pad = RET_PAD - n_tok
    n_b = 8

    def pad_rows(t):
        t = t.astype(F32).reshape(db, n_tok, t.shape[-1])
        return jnp.pad(t, ((0, 0), (0, pad), (0, 0)))

    q, k, v = pad_rows(rq), pad_rows(rk), pad_rows(rv)
    kt = k.reshape(db, RET_PAD, RET_HEADS, RET_QK_DIM).transpose(0, 2, 3, 1)
    lg = _log_gamma()
    idx = np.arange(RET_PAD, dtype=np.float64)
    diff = idx[:, None] - idx[None, :]
    dec = np.where(diff[None] >= 0, np.exp(lg[:, None, None] * np.maximum(diff, 0.0)[None]), 0.0)
    qd = np.broadcast_to(np.exp(lg[:, None] * (idx[None, :] + 1.0))[:, :, None], (RET_HEADS, RET_PAD, RET_V_DIM))
    kd = np.broadcast_to(np.exp(lg[:, None] * (n_tok - 1.0 - idx)[None, :])[:, None, :],
                         (RET_HEADS, RET_QK_DIM, RET_PAD))
    gc = np.broadcast_to(np.exp(lg * n_tok)[:, None, None], (RET_HEADS, RET_QK_DIM, RET_V_DIM))
    dec, qd, kd, gc = (jnp.asarray(t, F32) for t in (dec, qd, kd, gc))
    b3 = lambda i: (i, 0, 0)
    b4 = lambda i: (i, 0, 0, 0)
    o, st = pl.pallas_call(
        functools.partial(_ret_sample_kernel, n_tok=n_tok, n_b=n_b),
        grid=(db // n_b,),
        in_specs=[pl.BlockSpec((n_b, RET_PAD, RET_QK), b3), pl.BlockSpec((n_b, RET_PAD, RET_QK), b3),
                  pl.BlockSpec((n_b, RET_HEADS, RET_QK_DIM, RET_PAD), b4),
                  pl.BlockSpec((n_b, RET_PAD, RET_V), b3),
                  pl.BlockSpec((n_b, RET_HEADS, RET_QK_DIM, RET_V_DIM), b4),
                  _resident(dec.shape), _resident(qd.shape), _resident(kd.shape), _resident(gc.shape)],
        out_specs=[pl.BlockSpec((n_b, RET_PAD, RET_V), b3),
                   pl.BlockSpec((n_b, RET_HEADS, RET_QK_DIM, RET_V_DIM), b4)],
        out_shape=[jax.ShapeDtypeStruct((db, RET_PAD, RET_V), F32),
                   jax.ShapeDtypeStruct(state.shape, F32)],
        compiler_params=_params(),
        name="ret_sample",
    )(q, k, kt, v, state, dec, qd, kd, gc)
    return o[:, :n_tok].reshape(db * n_tok, RET_V), st


def _attn_kernel(q_ref, k_ref, v_ref, o_ref, l_ref, vt_scr, bias_scr, *staging, dil, nb):
    n_blk = dil * nb
    length = nb * BAND_BLOCK
    blk = BAND_BLOCK
    o_dst, l_dst, tmp = staging if dil > 1 else (o_ref, l_ref, None)

    kk = lax.broadcasted_iota(jnp.int32, (2 * blk, 2 * LANES), 0)
    ql = lax.broadcasted_iota(jnp.int32, (2 * blk, 2 * LANES), 1) & (blk - 1)
    in_span = kk <= ql + blk
    bias_scr[...] = jnp.where(in_span & (kk >= ql), 0.0, NEG)

    def transpose_values(c):
        vt_scr[c] = v_ref[0, c * blk:(c + 1) * blk, :].astype(F32).T.astype(BF16)

    lane = lax.broadcasted_iota(jnp.int32, (blk, LANES), 1)

    def scores(c, p):
        lanes = slice(p * LANES, (p + 1) * LANES)
        k0 = 0 if c % nb == 0 else blk
        q128 = q_ref[0, c * blk:(c + 1) * blk, lanes]
        zero = jnp.zeros_like(q128)
        q_pair = jnp.concatenate([jnp.where(lane < ATT_HEAD_DIM, q128, zero),
                                  jnp.where(lane >= ATT_HEAD_DIM, q128, zero)], axis=0)
        keys = k_ref[0, c * blk - k0:(c + 1) * blk, lanes]
        return _dot_nt(keys, q_pair) + bias_scr[blk - k0:2 * blk, :]

    def weights_values(c, p, s_t):
        lanes = slice(p * LANES, (p + 1) * LANES)
        m = jnp.max(s_t, axis=0, keepdims=True)
        p_t = jnp.exp(s_t - m).astype(BF16)
        v_t = vt_scr[c, lanes, :]
        if c % nb:
            v_t = jnp.concatenate([vt_scr[c - 1, lanes, :], v_t], axis=1)
        v_t = jnp.concatenate([v_t, jnp.ones((2 * SUBLANES, v_t.shape[1]), BF16)], axis=0)
        return _dot(v_t, p_t), m

    def finish(c, p, o_t, m):
        den = o_t[LANES:LANES + 1, :]
        lse = m + jnp.log(den)
        o_pair = jnp.concatenate([o_t[0:ATT_HEAD_DIM, 0:blk] / den[:, 0:blk],
                                  o_t[ATT_HEAD_DIM:LANES, blk:2 * blk] / den[:, blk:2 * blk]], axis=0)
        l_pair = jnp.concatenate([jnp.broadcast_to(lse[:, 0:blk], (ATT_HEAD_DIM, blk)),
                                  jnp.broadcast_to(lse[:, blk:2 * blk], (ATT_HEAD_DIM, blk))], axis=0)
        o_dst[p, c * blk:(c + 1) * blk, :] = o_pair.T
        l_dst[p, c * blk:(c + 1) * blk, :] = l_pair.T

    n_pair = ATT_OUT // LANES
    groups = [[(c, p) for c in range(c0, min(c0 + ATTN_GROUP, n_blk)) for p in range(n_pair)]
              for c0 in range(0, n_blk, ATTN_GROUP)]

    def stage_scores(group):
        for c in sorted({c for c, _ in group}):
            transpose_values(c)
        return [scores(*chain) for chain in group]

    s_next = stage_scores(groups[0])
    pending = []
    for i, group in enumerate(groups):
        s_cur = s_next
        if i + 1 < len(groups):
            s_next = stage_scores(groups[i + 1])
        o_m = [weights_values(*chain, s) for chain, s in zip(group, s_cur)]
        for args in pending:
            finish(*args)
        pending = [chain + om for chain, om in zip(group, o_m)]
    for args in pending:
        finish(*args)

    if dil > 1:
        inner = 4
        outer = dil // inner
        two_pass = dil % SUBLANES == 0 and outer % SUBLANES != 0
        for dst_ref, src in ((o_ref, o_dst), (l_ref, l_dst)):
            for p in range(ATT_OUT // LANES):
                if not two_pass:
                    for r in range(dil):
                        dst_ref[p, pl.ds(r, length, stride=dil), :] = src[p, r * length:(r + 1) * length, :]
                    continue
                sub = outer * length
                for b in range(inner):
                    for a in range(outer):
                        r = inner * a + b
                        tmp[p, pl.ds(b * sub + a, length, stride=outer), :] = src[p, r * length:(r + 1) * length, :]
                for b in range(inner):
                    dst_ref[p, pl.ds(b, sub, stride=inner), :] = tmp[p, b * sub:(b + 1) * sub, :]


def _attn_groups_kernel(*refs, geometry, batch):
    n_g = len(geometry)
    ins, outs, scratch = refs[:3 * n_g], refs[3 * n_g:5 * n_g], refs[5 * n_g:]
    group = pl.program_id(0) // batch
    for g, (dil, nb) in enumerate(geometry):
        @pl.when(group == g)
        def _(g=g, dil=dil, nb=nb):
            staging = scratch[2:] if dil > 1 else ()
            _attn_kernel(*ins[3 * g:3 * g + 3], *outs[2 * g:2 * g + 2], scratch[0], scratch[1], *staging,
                         dil=dil, nb=nb)


def _attn_prompt(qkv, batch, seq):
    n_g = len(ATT_GROUPS)
    geometry = tuple((dil, seq // dil // BAND_BLOCK) for _, dil in ATT_GROUPS)
    for _, nb in geometry:
        _log2(nb)
    n_pair = ATT_OUT // LANES
    in_specs, out_specs = [], []
    for g in range(n_g):
        seq_of = lambda i, g=g: jnp.clip(i - g * batch, 0, batch - 1)
        in_specs += [pl.BlockSpec((1, seq, ATT_OUT), lambda i, s=seq_of: (s(i), 0, 0))] * 3
        out_specs += [pl.BlockSpec((n_pair, seq, LANES), lambda i, s=seq_of: (0, s(i), 0))] * 2
    scratch = [pltpu.VMEM((seq // BAND_BLOCK, ATT_OUT, BAND_BLOCK), BF16),
               pltpu.VMEM((2 * BAND_BLOCK, 2 * LANES), F32)] + [pltpu.VMEM((n_pair, seq, LANES), F32)] * 3
    res = pl.pallas_call(
        functools.partial(_attn_groups_kernel, geometry=geometry, batch=batch),
        grid=(n_g * batch,),
        in_specs=in_specs,
        out_specs=out_specs,
        out_shape=[jax.ShapeDtypeStruct((n_pair, batch * seq, LANES), F32)] * (2 * n_g),
        scratch_shapes=scratch,
        compiler_params=_params(),
        name="attn_prompt",
    )(*(t.reshape(batch, seq, ATT_OUT) for t in qkv))
    return res[0::2], res[1::2]


def _attn_sample_kernel(q_ref, c_ref, n_ref, o_ref, l_ref, nc_ref, *, g, n_tok, wb, n_b):
    for bb in range(n_b):
        _attn_sample_one(q_ref, c_ref, n_ref, o_ref, l_ref, nc_ref, bb, g=g, n_tok=n_tok, wb=wb, n_b=n_b)


def _attn_sample_one(q_ref, c_ref, n_ref, o_ref, l_ref, nc_ref, bb, **static):
    weights, finish = _attn_sample_phases(q_ref, c_ref, n_ref, o_ref, l_ref, nc_ref, bb, **static)
    finish(weights())


def _attn_sample_phases(q_ref, c_ref, n_ref, o_ref, l_ref, nc_ref, bb, *, g, n_tok, wb, n_b):
    win, dil = ATT_GROUPS[g]
    nk = win // dil
    _log2(dil)
    n_col = ATT_HEADS * n_tok
    first_new = LANES - n_tok
    seq_lane = ((pl.program_id(0) * n_b + bb) * n_tok) % LANES
    new = [pltpu.roll(n_ref[kv], (first_new + LANES - seq_lane) % LANES, 1) for kv in range(2)]
    lane_t = lax.broadcasted_iota(jnp.int32, (ATT_OUT, LANES), 1)
    for kv in range(2):
        shifted = pltpu.roll(c_ref[bb, kv], wb - n_tok, 1)
        if wb > LANES:
            nc_ref[bb, kv, :, 0:wb - LANES] = shifted[:, 0:wb - LANES]
        nc_ref[bb, kv, :, wb - LANES:wb] = jnp.where(lane_t < first_new, shifted[:, wb - LANES:wb], new[kv])

    q = q_ref[bb][:, g * ATT_OUT:(g + 1) * ATT_OUT]
    row = lax.broadcasted_iota(jnp.int32, (n_col, ATT_OUT), 0)
    lane = lax.broadcasted_iota(jnp.int32, (n_col, ATT_OUT), 1)
    head_sel = (lane >> _log2(ATT_HEAD_DIM)) == (row >> _log2(n_tok))
    qbd = jnp.where(head_sel, q, jnp.zeros_like(q))
    k_c, v_c = c_ref[bb, 0].astype(BF16), c_ref[bb, 1].astype(BF16)
    k_n, v_n = new[0].astype(BF16), new[1].astype(BF16)

    def weights():
        s_c = _dot(qbd, k_c)
        s_n = _dot(qbd, k_n)
        t_c = lax.broadcasted_iota(jnp.int32, (n_col, wb), 0) & (n_tok - 1)
        d_c = wb + t_c - lax.broadcasted_iota(jnp.int32, (n_col, wb), 1)
        ok_c = ((d_c & (dil - 1)) == 0) & (d_c <= nk * dil)
        t_n = lax.broadcasted_iota(jnp.int32, (n_col, LANES), 0) & (n_tok - 1)
        new_idx = lax.broadcasted_iota(jnp.int32, (n_col, LANES), 1) - first_new
        d_n = t_n - new_idx
        ok_n = (new_idx >= 0) & (d_n >= 0) & ((d_n & (dil - 1)) == 0) & (d_n <= nk * dil)
        s_c = jnp.where(ok_c, s_c, NEG)
        s_n = jnp.where(ok_n, s_n, NEG)
        m = jnp.maximum(jnp.max(s_c, axis=-1, keepdims=True), jnp.max(s_n, axis=-1, keepdims=True))
        p_c = jnp.exp(s_c - m)
        p_n = jnp.exp(s_n - m)
        den = jnp.sum(p_c, axis=-1, keepdims=True) + jnp.sum(p_n, axis=-1, keepdims=True)
        return p_c.astype(BF16), p_n.astype(BF16), m, den

    def finish(state):
        p_c, p_n, m, den = state
        o = (_dot_nt(p_c, v_c) + _dot_nt(p_n, v_n)) / den
        lse = jnp.broadcast_to(m + jnp.log(den), (n_col, ATT_OUT))
        o = jnp.where(head_sel, o, 0.0)
        lse = jnp.where(head_sel, lse, 0.0)
        o_t = o[0:n_tok]
        l_t = lse[0:n_tok]
        for h in range(1, ATT_HEADS):
            o_t = o_t + o[h * n_tok:(h + 1) * n_tok]
            l_t = l_t + lse[h * n_tok:(h + 1) * n_tok]
        o_ref[bb] = o_t
        l_ref[bb] = l_t

    return weights, finish


def _attn_sample(q_tiled, cache, kv_new, g, n_tok):
    db, _, _, wb = cache.shape
    n_b = max(1, min(SUBLANES, ATT_GROUPS[-1][0] // wb))
    assert wb % LANES == 0 and db % n_b == 0 and LANES % (n_b * n_tok) == 0 and kv_new.shape[2] % LANES == 0
    b3 = lambda i: (i, 0, 0)
    b4 = lambda i: (i, 0, 0, 0)
    return pl.pallas_call(
        functools.partial(_attn_sample_kernel, g=g, n_tok=n_tok, wb=wb, n_b=n_b),
        grid=(db // n_b,),
        in_specs=[pl.BlockSpec((n_b, ATT_HEADS * n_tok, ATT_W), b3),
                  pl.BlockSpec((n_b, 2, ATT_OUT, wb), b4),
                  pl.BlockSpec((2, ATT_OUT, LANES), lambda i: (0, 0, i * n_b * n_tok // LANES))],
        out_specs=[pl.BlockSpec((n_b, n_tok, ATT_OUT), b3), pl.BlockSpec((n_b, n_tok, ATT_OUT), b3),
                   pl.BlockSpec((n_b, 2, ATT_OUT, wb), b4)],
        out_shape=[jax.ShapeDtypeStruct((db, n_tok, ATT_OUT), F32), jax.ShapeDtypeStruct((db, n_tok, ATT_OUT), F32),
                   jax.ShapeDtypeStruct(cache.shape, F32)],
        compiler_params=_params(),
        name=f"attn_sample_g{g}",
    )(q_tiled, cache, kv_new)


def _merge_kernel(*refs, tm, n_split, permute, side=None):
    if side is not None:
        side_in, refs = refs[:3], refs[3:]
    (x_ref, on_ref, o0_ref, o1_ref, o2_ref, l0_ref, l1_ref, l2_ref, gpre_ref, wg_ref, wbr_ref, wba_ref, wout_ref,
     gpost_ref, gffn_ref, x1_ref, h2_ref), perm = refs[:17], refs[17:]
    if side is not None:
        side_refs, perm = side_in + perm[:3], perm[3:]
    bounds = [i * tm // n_split for i in range(n_split + 1)]
    rows_of = [slice(a, b) for a, b in zip(bounds[:-1], bounds[1:])]
    n_pair = ATT_OUT // LANES

    def gates(rows):
        x = x_ref[rows, :]
        h = _rms(x, gpre_ref[...]).astype(BF16)
        return [_dot(h, wg_ref[:, c:c + D_MODEL]) for c in (0, RET_V, RET_V + D_MODEL)]

    def branches(rows, rg):
        r = (on_ref[rows, :] * (rg * jax.nn.sigmoid(rg))).astype(BF16)
        atts = []
        for p in range(n_pair):
            l0, l1, l2 = l0_ref[p, rows, :], l1_ref[p, rows, :], l2_ref[p, rows, :]
            mx = jnp.maximum(jnp.maximum(l0, l1), l2)
            e0, e1, e2 = jnp.exp(l0 - mx), jnp.exp(l1 - mx), jnp.exp(l2 - mx)
            att = (e0 * o0_ref[p, rows, :] + e1 * o1_ref[p, rows, :] + e2 * o2_ref[p, rows, :]) / (e0 + e1 + e2)
            atts.append(att.astype(BF16))
        return _dot(r, wbr_ref[...]), _dot(jnp.concatenate(atts, axis=1), wba_ref[...])

    def mixed(gr, ga, br, ba):
        mix = jax.nn.sigmoid(gr) * br + jax.nn.sigmoid(ga) * ba
        return _dot(mix.astype(BF16), wout_ref[...])

    def finish(rows, mo):
        x1 = x_ref[rows, :] + _rms(mo, gpost_ref[...])
        x1_ref[rows, :] = x1
        return _rms(x1, gffn_ref[...])

    g = []
    for i, rows in enumerate(rows_of):
        g.append(gates(rows))
        if i == 0 and side is not None:
            side_weights, side_finish = _attn_sample_phases(*side_refs, 0, g=side[0], n_tok=side[1], wb=side[2],
                                                            n_b=1)
    b = [branches(rows, gi[0]) for rows, gi in zip(rows_of, g)]
    if side is not None:
        side_state = side_weights()
    mo = [mixed(gi[1], gi[2], *bi) for gi, bi in zip(g, b)]
    if side is not None:
        side_finish(side_state)
    h2 = [finish(rows, moi) for rows, moi in zip(rows_of, mo)]
    if not permute:
        for rows, h2i in zip(rows_of, h2):
            h2_ref[rows, :] = h2i.astype(BF16)
        return
    perm_ref, = perm
    nv = tm // SUBLANES
    for lc in range(D_MODEL // LANES):
        lanes = slice(lc * LANES, (lc + 1) * LANES)
        for s in range(SUBLANES):
            tok0 = s * nv
            i = max(j for j, b in enumerate(bounds[:-1]) if b <= tok0)
            off = tok0 - bounds[i]
            assert tok0 + nv <= bounds[i + 1]
            perm_ref[lc, pl.ds(s, nv, stride=SUBLANES), :] = h2[i][off:off + nv, lanes]
        h2_ref[:, lanes] = perm_ref[lc].astype(BF16)


def _side_job(side, n_steps):
    g, n_tok, q_tiled, cache, kv_new = side
    db, _, _, wb = cache.shape
    assert db == n_steps and LANES % n_tok == 0, (db, n_steps)
    b3 = lambda i: (i, 0, 0)
    b4 = lambda i: (i, 0, 0, 0)
    in_specs = [pl.BlockSpec((1, ATT_HEADS * n_tok, ATT_W), b3), pl.BlockSpec((1, 2, ATT_OUT, wb), b4),
                pl.BlockSpec((2, ATT_OUT, LANES), lambda i: (0, 0, i * n_tok // LANES))]
    out_specs = [pl.BlockSpec((1, n_tok, ATT_OUT), b3), pl.BlockSpec((1, n_tok, ATT_OUT), b3),
                 pl.BlockSpec((1, 2, ATT_OUT, wb), b4)]
    out_shape = [jax.ShapeDtypeStruct((db, n_tok, ATT_OUT), F32)] * 2 + [jax.ShapeDtypeStruct(cache.shape, F32)]
    return in_specs, (q_tiled, cache, kv_new), out_specs, out_shape, (g, n_tok, wb)


def _merge(x2d, on, outs, lses, g_pre, w_gates, w_br, w_ba, w_out, g_post, g_ffn, tm, permute, side=None):
    m = x2d.shape[0]
    n_split = 2 if tm >= 2 * MXU_N else 1
    scratch = [pltpu.VMEM((D_MODEL // LANES, tm, LANES), F32)] if permute else []
    row = lambda i: (i, 0)
    vec = _resident((1, D_MODEL))
    att_spec = pl.BlockSpec((ATT_OUT // LANES, tm, LANES), lambda i: (0, i, 0))
    in_specs = [pl.BlockSpec((tm, D_MODEL), row), pl.BlockSpec((tm, RET_V), row)] + [att_spec] * 6 + [
        vec, _resident(w_gates.shape), _resident(w_br.shape), _resident(w_ba.shape), _resident(w_out.shape),
        vec, vec]
    out_specs = [pl.BlockSpec((tm, D_MODEL), row), pl.BlockSpec((tm, D_MODEL), row)]
    out_shape = [jax.ShapeDtypeStruct((m, D_MODEL), F32), jax.ShapeDtypeStruct((m, D_MODEL), BF16)]
    operands = (x2d, on, *outs, *lses, g_pre, w_gates, w_br, w_ba, w_out, g_post, g_ffn)
    side_static = None
    if side is not None:
        s_in, s_ops, s_out, s_shape, side_static = _side_job(side, m // tm)
        in_specs, operands = s_in + in_specs, s_ops + operands
        out_specs, out_shape = out_specs + s_out, out_shape + s_shape
    return pl.pallas_call(
        functools.partial(_merge_kernel, tm=tm, n_split=n_split, permute=permute, side=side_static),
        grid=(m // tm,),
        in_specs=in_specs,
        out_specs=out_specs,
        out_shape=out_shape,
        scratch_shapes=scratch,
        compiler_params=_params(),
        name="merge",
    )(*operands)


def _ffn_kernel(*refs, tm, tiles_per_seq, seq_tok, side=None):
    if side is not None:
        (sq_ref, sc_ref, sn_ref), refs = refs[:3], refs[3:]
        so_ref, sl_ref, snc_ref = refs[9:12]
        refs = refs[:9] + refs[12:]
        side_refs = (sq_ref, sc_ref, sn_ref, so_ref, sl_ref, snc_ref)
    if seq_tok is None:
        h2_ref, x1_ref, wup_ref, wdn_ref, cw_ref, cb_ref, g_ref, y_ref, cs_ref, carry_ref, f_ref, perm_ref = refs
    else:
        h2_ref, x1_ref, p1_ref, p2_ref, wup_ref, wdn_ref, cw_ref, cb_ref, g_ref, y_ref, u_ref, f_ref = refs
    h2 = h2_ref[...]
    if seq_tok is None:
        sub = lax.broadcasted_iota(jnp.int32, (SUBLANES, FF_CHUNK), 0)

        @pl.when(pl.program_id(0) % tiles_per_seq == 0)
        def _():
            carry_ref[...] = jnp.zeros_like(carry_ref)
    else:
        t = lax.broadcasted_iota(jnp.int32, (tm, FF_CHUNK), 0) & (seq_tok - 1)
        _log2(seq_tok)

    def up(j):
        return [_dot(h2, wup_ref[:, c:c + FF_CHUNK]) for c in (j * FF_CHUNK, D_FF + j * FF_CHUNK)]

    def conv(u, col):
        cols = slice(col, col + FF_CHUNK)
        if seq_tok is None:
            prev = carry_ref[:, cols]
            last2 = jnp.where(sub == 0, prev[SUBLANES - 1:SUBLANES], pltpu.roll(u[tm - 2 * SUBLANES:tm - SUBLANES], 1, 0))
            last1 = jnp.where(sub == 0, prev[2 * SUBLANES - 1:2 * SUBLANES], pltpu.roll(u[tm - SUBLANES:tm], 1, 0))
            u1 = jnp.concatenate([last1, u[0:tm - SUBLANES]], axis=0)
            u2 = jnp.concatenate([last2, last1, u[0:tm - 2 * SUBLANES]], axis=0)
            carry_ref[:, cols] = u[tm - 2 * SUBLANES:tm]
        else:
            u1 = jnp.where(t == 0, p1_ref[:, cols], pltpu.roll(u, 1, 0))
            u2 = jnp.where(t <= 1, p2_ref[:, cols], pltpu.roll(u, 2, 0))
            u_ref[:, cols] = u
        cw = cw_ref[:, cols]
        return cb_ref[:, cols] + cw[0:1] * u2 + cw[1:2] * u1 + cw[2:3] * u

    def gate(j, ug, uv):
        cg = conv(ug, j * FF_CHUNK)
        cv = conv(uv, D_FF + j * FF_CHUNK)
        f_ref[:, j * FF_CHUNK:(j + 1) * FF_CHUNK] = (jax.nn.gelu(cg, approximate=True) * cv).astype(BF16)

    n_chunks = D_FF // FF_CHUNK
    u_next = up(0)
    for j in range(n_chunks):
        u_cur = u_next
        if j + 1 < n_chunks:
            u_next = up(j + 1)
        if j == 0 and side is not None:
            side_weights, side_finish = _attn_sample_phases(*side_refs, 0, g=side[0], n_tok=side[1], wb=side[2],
                                                            n_b=1)
        gate(j, *u_cur)
    if side is not None:
        side_state = side_weights()
    down = _dot(f_ref[...], wdn_ref[...])
    if side is not None:
        side_finish(side_state)
    out = _rms(down, g_ref[...])
    if seq_tok is None:
        nv = tm // SUBLANES
        for lc in range(D_MODEL // LANES):
            lanes = slice(lc * LANES, (lc + 1) * LANES)
            perm_ref[lc] = out[:, lanes]
            for s in range(SUBLANES):
                rows = slice(s * nv, (s + 1) * nv)
                y_ref[rows, lanes] = x1_ref[rows, lanes] + perm_ref[lc, pl.ds(s, nv, stride=SUBLANES), :]

        @pl.when(pl.program_id(0) % tiles_per_seq == tiles_per_seq - 1)
        def _():
            cs_ref[0, 0:1, :] = carry_ref[SUBLANES - 1:SUBLANES, :]
            cs_ref[0, 1:2, :] = carry_ref[2 * SUBLANES - 1:2 * SUBLANES, :]
    else:
        y_ref[...] = x1_ref[...] + out


def _ffn_prompt(h2, x1, w_up, w_dn, conv_w, conv_b, g_post, batch, seq, tm, side=None):
    m = h2.shape[0]
    tps = seq // tm
    n_steps = m // tm
    row = lambda i: (i, 0)
    in_specs = [pl.BlockSpec((tm, D_MODEL), row), pl.BlockSpec((tm, D_MODEL), row),
                _resident(w_up.shape), _resident(w_dn.shape), _resident(conv_w.shape), _resident(conv_b.shape),
                _resident((1, D_MODEL))]
    out_specs = [pl.BlockSpec((tm, D_MODEL), row),
                 pl.BlockSpec((1, CONV_W - 1, 2 * D_FF), lambda i: (i // tps, 0, 0))]
    out_shape = [jax.ShapeDtypeStruct((m, D_MODEL), F32),
                 jax.ShapeDtypeStruct((batch, CONV_W - 1, 2 * D_FF), F32)]
    operands = (h2, x1, w_up, w_dn, conv_w, conv_b, g_post)
    side_static = None
    if side is not None:
        s_in, s_ops, s_out, s_shape, side_static = _side_job(side, n_steps)
        in_specs, operands = s_in + in_specs, s_ops + operands
        out_specs, out_shape = out_specs + s_out, out_shape + s_shape
    return pl.pallas_call(
        functools.partial(_ffn_kernel, tm=tm, tiles_per_seq=tps, seq_tok=None, side=side_static),
        grid=(n_steps,),
        in_specs=in_specs,
        out_specs=out_specs,
        out_shape=out_shape,
        scratch_shapes=[pltpu.VMEM((2 * SUBLANES, 2 * D_FF), F32), pltpu.VMEM((tm, D_FF), BF16),
                        pltpu.VMEM((D_MODEL // LANES, tm, LANES), F32)],
        compiler_params=_params(),
        name="ffn_prompt",
    )(*operands)


def _ffn_sample(h2, x1, p1, p2, w_up, w_dn, conv_w, conv_b, g_post, n_tok):
    m = h2.shape[0]
    const = lambda i: (0, 0)
    full = lambda a: _resident(a.shape)
    return pl.pallas_call(
        functools.partial(_ffn_kernel, tm=m, tiles_per_seq=1, seq_tok=n_tok),
        grid=(1,),
        in_specs=[full(h2), full(x1), full(p1), full(p2), full(w_up), full(w_dn), full(conv_w), full(conv_b),
                  _resident((1, D_MODEL))],
        out_specs=[pl.BlockSpec((m, D_MODEL), const), pl.BlockSpec((m, 2 * D_FF), const)],
        out_shape=[jax.ShapeDtypeStruct((m, D_MODEL), F32), jax.ShapeDtypeStruct((m, 2 * D_FF), F32)],
        scratch_shapes=[pltpu.VMEM((m, D_FF), BF16)],
        compiler_params=_params(),
        name="ffn_sample",
    )(h2, x1, p1, p2, w_up, w_dn, conv_w, conv_b, g_post)


def _rot_tables(pos):
    pos = np.asarray(pos, np.float64)
    rf = RET_THETA ** (-np.linspace(0.0, 1.0, RET_QK_DIM // 2))
    ang = pos[:, None] * rf[None, :]
    cos, sin = np.cos(ang), np.sin(ang)
    cr = np.concatenate([cos, cos], axis=-1)
    sr = np.concatenate([-sin, sin], axis=-1)
    k_scale = RET_QK_DIM ** -0.5
    half = ROPE_DIM // 2
    af = ROPE_THETA ** (-np.arange(half) / half)
    ang = pos[:, None] * af[None, :]
    cos, sin = np.cos(ang), np.sin(ang)
    n = pos.shape[0]
    rest = ATT_HEAD_DIM - ROPE_DIM
    zh = np.zeros((n, half))
    ca = np.concatenate([cos, cos, np.ones((n, rest))], axis=-1)
    sa1 = np.concatenate([zh, sin, np.zeros((n, rest))], axis=-1)
    sa2 = np.concatenate([-sin, zh, np.zeros((n, rest))], axis=-1)
    rep = MXU_N // ATT_HEAD_DIM
    ca, sa1, sa2 = (np.tile(t, (1, rep)) for t in (ca, sa1, sa2))
    return [jnp.asarray(t, F32) for t in (cr, sr, cr * k_scale, sr * k_scale, ca, sa1, sa2)]


def kernel(x_prompt, x_sample, cache_kv_g0, cache_kv_g1, cache_kv_g2, state_ret, state_conv, norm_mix_pre, w_in,
           w_branch_ret, w_branch_attn, w_out, norm_mix_post, norm_ffn_pre, w_ffn_up, conv_w, conv_b, w_ffn_down,
           norm_ffn_post):
    batch, seq, _ = x_prompt.shape
    db, n_tok, _ = x_sample.shape
    depth = w_in.shape[0]
    assert depth == 1 and seq % (ATT_GROUPS[-1][1] * BAND_BLOCK) == 0 and n_tok <= SUBLANES
    caches = (cache_kv_g0, cache_kv_g1, cache_kv_g2)

    wi = w_in[0]
    o_rg = 2 * RET_QK + RET_V
    o_aq = o_rg + RET_V
    o_gr = o_aq + 3 * ATT_W
    w_qkv = jnp.concatenate([wi[:, :o_rg], wi[:, o_aq:o_aq + ATT_W] * ATT_HEAD_DIM ** -0.5,
                             wi[:, o_aq + ATT_W:o_gr]], axis=1).astype(BF16)
    w_gates = jnp.concatenate([wi[:, o_rg:o_aq], wi[:, o_gr:]], axis=1).astype(BF16)
    later_weights = (w_branch_ret[0], w_branch_attn[0], w_out[0], w_ffn_up[0], w_ffn_down[0])
    cb = conv_b[0][None, :]
    cw = conv_w[0]
    g_pre, g_post, g_ffn, g_post2 = (t[0][None, :] for t in (norm_mix_pre, norm_mix_post, norm_ffn_pre, norm_ffn_post))

    tm = PROMPT_TILE
    ms = db * n_tok
    xs = x_sample.reshape(ms, D_MODEL)
    pos_s = np.tile(PAST_LEN + np.arange(n_tok), db)
    rq_s, rk_s, rv_s, aq, kv0, kv1, kv2 = _proj(xs, g_pre, w_qkv, _rot_tables(pos_s), ms)
    kvs = (kv0, kv1, kv2)
    q_tiled = jnp.tile(aq.reshape(db, n_tok, ATT_W), (1, ATT_HEADS, 1))
    sides = []
    for g in range(N_GROUPS):
        cache = caches[g][0]
        wb = cache.shape[1]
        cache_t = cache.transpose(0, 2, 3, 4, 1).reshape(db, 2, ATT_OUT, wb)
        kv_new = kvs[g].reshape(ms, 2, ATT_OUT).transpose(1, 2, 0)
        sides.append((g, n_tok, q_tiled, cache_t, kv_new))
    ride = batch * seq // tm == db and N_GROUPS == 3
    sample_attn = {} if ride else {g: _attn_sample(*sides[g][2:], g, n_tok) for g in range(N_GROUPS)}

    xp = x_prompt.reshape(batch * seq, D_MODEL)
    proj_out = _proj(xp, g_pre, w_qkv, _rot_tables(np.arange(seq)), tm, seq=seq, casts=later_weights,
                     side=sides[0] if ride else None)
    rq, rk, rv = proj_out[:3]
    qkv_res = proj_out[3:3 + 3 * N_GROUPS]
    kv_last = proj_out[3 + 3 * N_GROUPS:3 + 4 * N_GROUPS]
    n_main = 3 + 4 * N_GROUPS + len(later_weights)
    w_br, w_ba, w_o, w_up, w_dn = proj_out[3 + 4 * N_GROUPS:n_main]
    if ride:
        sample_attn[0] = proj_out[n_main:]
    on, p_ret = _ret_prompt(rq, rk, rv, batch, seq)
    outs, lses = _attn_prompt(qkv_res, batch, seq)
    p_kv =[t.reshape(batch, 2, ATT_HEADS, ATT_HEAD_DIM, t.shape[3]).transpose(0, 4, 1, 2, 3)[None] for t in kv_last]
    x1, h2, *rest = _merge(xp, on, outs, lses, g_pre, w_gates, w_br, w_ba, w_o, g_post, g_ffn, tm, permute=True,
                           side=sides[1] if ride else None)
    if ride:
        sample_attn[1] = rest
    y_p, p_conv, *rest = _ffn_prompt(h2, x1, w_up, w_dn, cw, cb, g_post2, batch, seq, tm,
                                     side=sides[2] if ride else None)
    if ride:
        sample_attn[2] = rest

    on_s, s_ret = _ret_sample(rq_s, rk_s, rv_s, state_ret[0], n_tok)
    outs, lses, s_kv = [], [], []
    for g in range(N_GROUPS):
        o_g, l_g, nc = sample_attn[g]
        outs.append(o_g.reshape(ms, ATT_OUT // LANES, LANES).transpose(1, 0, 2))
        lses.append(l_g.reshape(ms, ATT_OUT // LANES, LANES).transpose(1, 0, 2))
        s_kv.append(nc.reshape(db, 2, ATT_HEADS, ATT_HEAD_DIM, nc.shape[3]).transpose(0, 4, 1, 2, 3)[None])
    x1, h2 = _merge(xs, on_s, outs, lses, g_pre, w_gates, w_br, w_ba, w_o, g_post, g_ffn, ms, permute=False)
    st = state_conv[0]
    zeros = jnp.zeros((db, n_tok - 1, 2 * D_FF), F32)
    p1 = jnp.concatenate([st[:, 1:2], zeros], axis=1).reshape(ms, 2 * D_FF)
    p2 = jnp.concatenate([st, zeros[:, :n_tok - 2]], axis=1).reshape(ms, 2 * D_FF)
    y_s, u_s = _ffn_sample(h2, x1, p1, p2, w_up, w_dn, cw, cb, g_post2, n_tok)
    s_conv = u_s.reshape(db, n_tok, 2 * D_FF)[:, n_tok - (CONV_W - 1):][None]

    return (y_p.reshape(x_prompt.shape), y_s.reshape(x_sample.shape), p_kv[0], p_kv[1], p_kv[2], p_ret[None],
            p_conv[None], s_kv[0], s_kv[1], s_kv[2], s_ret[None], s_conv)
```
